```python
import jax
import jax.numpy as jnp
from jax import lax
import numpy as np

D_MODEL = 1024
BATCH = 16
SEQ = 256
DEPTH = 2
DEC_BATCH = 8
DEC_SEQ = 2048
PAST_LEN = 512

GRID_W = 64
HEAD_DIM = 64
NA_HEADS = 6
SWA_HEADS = 4
SWA_KV_HEADS = 2
RK_HEADS = 6
NA_WIDTH = NA_HEADS * HEAD_DIM
SWA_WIDTH = SWA_HEADS * HEAD_DIM
SWA_KV_WIDTH = SWA_KV_HEADS * HEAD_DIM
RK_WIDTH = RK_HEADS * HEAD_DIM
MIX_WIDTH = NA_WIDTH + SWA_WIDTH + RK_WIDTH
RK_DECAY_LORA = 64
RK_A_LORA = 64
RK_GATE_LORA = 128
RK_COLS = 3 * RK_WIDTH + RK_DECAY_LORA + RK_A_LORA + RK_GATE_LORA
RK_SPLITS = (RK_WIDTH, 2 * RK_WIDTH, 3 * RK_WIDTH, 3 * RK_WIDTH + RK_DECAY_LORA, 3 * RK_WIDTH + RK_DECAY_LORA + RK_A_LORA)
RK_CONV_W = 3
NA_COLS = 3 * NA_WIDTH
SWA_COLS = SWA_WIDTH + 2 * SWA_KV_WIDTH
IN_COLS = NA_COLS + SWA_COLS + RK_COLS
NA_WIN_R = 8
NA_WIN_C = 16
SWA_WIN = 128
SWA_BLK = 128
CTX_BLK = 128
ROPE_THETA = 10000.0
ATTN_SCALE = HEAD_DIM ** -0.5
N_EXPERTS = 32
TOP_K = 4
D_EXPERT = D_MODEL
SWIGLU_LIMIT = 7.0
SWIGLU_ALPHA = 1.702
MOE_BLK = 256
RMS_EPS = 1e-6
GN_EPS = 64e-5

kernel_name = 'hybrid_na_swa_rwkv7_moe_dit_step'


def rms_norm(x, g):
    xf = x.astype(jnp.float32)
    y = xf * lax.rsqrt(jnp.mean(xf * xf, -1, keepdims=True) + RMS_EPS)
    return y.astype(x.dtype) * g


def heads(z, n_heads):
    return z.reshape(z.shape[:-1] + (n_heads, HEAD_DIM))


def ada_mod(cvec, w, b):
    m = jax.nn.silu(cvec) @ w + b
    return [z[:, None, :] for z in jnp.split(m, 6, axis=-1)]


def axial_rope_tables(n_tok):
    nf = HEAD_DIM // 4
    t = jnp.arange(n_tok)
    inv = ROPE_THETA ** (-jnp.arange(nf, dtype=jnp.float32) / nf)
    ang = jnp.concatenate([(t // GRID_W).astype(jnp.float32)[:, None] * inv,
                           (t % GRID_W).astype(jnp.float32)[:, None] * inv], -1)
    return jnp.cos(ang), jnp.sin(ang)


def apply_axial_rope(x, cos, sin):
    nf = HEAD_DIM // 4
    xr = x.reshape(x.shape[:-1] + (2, 2, nf))
    c = cos.reshape(cos.shape[0], 1, 2, nf).astype(x.dtype)
    s = sin.reshape(sin.shape[0], 1, 2, nf).astype(x.dtype)
    x1, x2 = xr[..., 0, :], xr[..., 1, :]
    return jnp.stack([x1 * c - x2 * s, x2 * c + x1 * s], axis=-2).reshape(x.shape)


def mix_in(h, p):
    proj = h @ p['w_in']
    na, sw, rk = jnp.split(proj, [NA_COLS, NA_COLS + SWA_COLS], axis=-1)
    na_q, na_k, na_v = [heads(z, NA_HEADS) for z in jnp.split(na, 3, axis=-1)]
    sw_q = heads(sw[..., :SWA_WIDTH], SWA_HEADS)
    sw_k = heads(sw[..., SWA_WIDTH:SWA_WIDTH + SWA_KV_WIDTH], SWA_KV_HEADS)
    sw_v = heads(sw[..., SWA_WIDTH + SWA_KV_WIDTH:], SWA_KV_HEADS)
    na_q = rms_norm(na_q, p['na_q_norm'])
    na_k = rms_norm(na_k, p['na_k_norm'])
    sw_q = rms_norm(sw_q, p['swa_q_norm'])
    sw_k = rms_norm(sw_k, p['swa_k_norm'])
    return na_q, na_k, na_v, sw_q, sw_k, sw_v, rk


def ctx_attention(q, k, v, sink):
    b, n, hq, d = q.shape
    hkv = k.shape[2]
    g = hq // hkv
    nb = n // CTX_BLK
    qb = jnp.moveaxis(q.reshape(b, nb, CTX_BLK, hkv, g, d), 1, 0)

    def block(qi):
        s = jnp.einsum('bqhgd,bkhd->bhgqk', qi, k).astype(jnp.float32) * ATTN_SCALE
        if sink is not None:
            snk = jnp.broadcast_to(sink.astype(jnp.float32).reshape(1, hkv, g, 1, 1), s.shape[:-1] + (1,))
            s = jnp.concatenate([s, snk], -1)
        pr = jax.nn.softmax(s, axis=-1)[..., :n].astype(v.dtype)
        return jnp.einsum('bhgqk,bkhd->bqhgd', pr, v).reshape(b, CTX_BLK, hq, d)

    out = lax.map(block, qb)
    return jnp.moveaxis(out, 0, 1).reshape(b, n, hq, d)


def na_latent(q, k, v, kc, vc, rpb):
    b, n, h, d = q.shape
    rows = n // GRID_W
    win_r = min(NA_WIN_R, rows)
    qg = jnp.moveaxis(q.reshape(b, rows, GRID_W, h, d), 1, 0)
    kg = k.reshape(b, rows, GRID_W, h, d)
    vg = v.reshape(b, rows, GRID_W, h, d)
    col = jnp.arange(GRID_W)
    cstart = jnp.clip(col - NA_WIN_C // 2, 0, GRID_W - NA_WIN_C)
    col_mask = (col[None, :] >= cstart[:, None]) & (col[None, :] < cstart[:, None] + NA_WIN_C)
    col_idx = jnp.clip(col[None, :] - col[:, None] + NA_WIN_C - 1, 0, 2 * NA_WIN_C - 2)
    rpb_cols = rpb[:, :, col_idx].astype(jnp.float32)
    n_loc = win_r * GRID_W

    def row_block(args):
        i, qi = args
        start = jnp.clip(i - NA_WIN_R // 2, 0, rows - win_r)
        kw = lax.dynamic_slice_in_dim(kg, start, win_r, axis=1)
        vw = lax.dynamic_slice_in_dim(vg, start, win_r, axis=1)
        roff = start + jnp.arange(win_r) - i + NA_WIN_R - 1
        bias = jnp.transpose(jnp.take(rpb_cols, roff, axis=1), (0, 2, 1, 3))
        s_loc = jnp.einsum('bqhd,brkhd->bhqrk', qi, kw).astype(jnp.float32) * ATTN_SCALE + bias
        s_loc = jnp.where(col_mask[:, None, :], s_loc, -jnp.inf)
        s_ctx = jnp.einsum('bqhd,blhd->bhql', qi, kc).astype(jnp.float32) * ATTN_SCALE
        pr = jax.nn.softmax(jnp.concatenate([s_loc.reshape(b, h, GRID_W, n_loc), s_ctx], -1), axis=-1).astype(v.dtype)
        p_loc = pr[..., :n_loc].reshape(b, h, GRID_W, win_r, GRID_W)
        return (jnp.einsum('bhqrk,brkhd->bqhd', p_loc, vw)
                + jnp.einsum('bhql,blhd->bqhd', pr[..., n_loc:], vc))

    out = lax.map(row_block, (jnp.arange(rows, dtype=jnp.int32), qg))
    return jnp.moveaxis(out, 0, 1).reshape(b, n, h, d)


def swa_latent(q, k, v, kc, vc, sink):
    b, n, hq, d = q.shape
    hkv = k.shape[2]
    g = hq // hkv
    nb = n // SWA_BLK
    span = SWA_BLK + 2 * SWA_WIN
    qb = jnp.moveaxis(q.reshape(b, nb, SWA_BLK, hkv, g, d), 1, 0)
    pad = ((0, 0), (SWA_WIN, SWA_WIN), (0, 0), (0, 0))
    kp = jnp.pad(k, pad)
    vp = jnp.pad(v, pad)
    sink_logit = sink.astype(jnp.float32).reshape(1, hkv, g, 1, 1)

    def block(args):
        j, qi = args
        kw = lax.dynamic_slice_in_dim(kp, j * SWA_BLK, span, axis=1)
        vw = lax.dynamic_slice_in_dim(vp, j * SWA_BLK, span, axis=1)
        qpos = j * SWA_BLK + jnp.arange(SWA_BLK)
        kpos = j * SWA_BLK - SWA_WIN + jnp.arange(span)
        valid = (jnp.abs(qpos[:, None] - kpos[None, :]) <= SWA_WIN) & ((kpos >= 0) & (kpos < n))[None, :]
        s_loc = jnp.where(valid, jnp.einsum('bqhgd,bkhd->bhgqk', qi, kw).astype(jnp.float32) * ATTN_SCALE, -jnp.inf)
        s_ctx = jnp.einsum('bqhgd,blhd->bhgql', qi, kc).astype(jnp.float32) * ATTN_SCALE
        s_snk = jnp.broadcast_to(sink_logit, s_ctx.shape[:-1] + (1,))
        pr = jax.nn.softmax(jnp.concatenate([s_loc, s_ctx, s_snk], -1), axis=-1).astype(v.dtype)
        o = (jnp.einsum('bhgqk,bkhd->bqhgd', pr[..., :span], vw)
             + jnp.einsum('bhgql,blhd->bqhgd', pr[..., span:-1], vc))
        return o.reshape(b, SWA_BLK, hq, d)

    out = lax.map(block, (jnp.arange(nb, dtype=jnp.int32), qb))
    return jnp.moveaxis(out, 0, 1).reshape(b, n, hq, d)


def rwkv_scan(r, w, k, v, a, bb, s0):
    def both(z):
        return jnp.moveaxis(jnp.stack([z, jnp.flip(z, 1)]), 2, 0).astype(jnp.float32)

    def per_dir(z):
        return jnp.moveaxis(jnp.stack([z[0], jnp.flip(z[1], 1)]), 2, 0).astype(jnp.float32)

    xs = (both(r), per_dir(w), per_dir(k), both(v), both(a), per_dir(bb))

    def step(S, inp):
        rt, wt, kt, vt, at, bt = inp
        sa = jnp.einsum('dbhvk,dbhk->dbhv', S, at)
        S = S * wt[..., None, :] + sa[..., None] * bt[..., None, :] + vt[..., None] * kt[..., None, :]
        return S, jnp.einsum('dbhvk,dbhk->dbhv', S, rt)

    s_fin, ys = lax.scan(step, s0, xs)
    ys = jnp.moveaxis(ys, 0, 2)
    return ys[0] + jnp.flip(ys[1], 1), s_fin


def rwkv_mix(u, s0, p):
    bsz, n, _ = u.shape
    cw = p['rk_conv']
    up = jnp.pad(u, ((0, 0), (1, 1), (0, 0)))
    u = up[:, :-2] * cw[0] + up[:, 1:-1] * cw[1] + up[:, 2:] * cw[2]
    r, k, v, wl, al, gl = jnp.split(u, RK_SPLITS, axis=-1)
    w = -jax.nn.softplus(-(p['rk_w0'][:, None, None, :] + jnp.einsum('btr,drc->dbtc', jnp.tanh(wl), p['rk_w2']))) - 0.5
    decay = jnp.exp(-jnp.exp(w.astype(jnp.float32)))
    a = jax.nn.sigmoid(p['rk_a0'][:, None, None, :] + jnp.einsum('btr,drc->dbtc', al, p['rk_a2']))
    g = jax.nn.sigmoid(gl) @ p['rk_g2']
    kk = heads(k * p['rk_k_k'], RK_HEADS).astype(jnp.float32)
    kk = kk * lax.rsqrt(jnp.maximum(jnp.sum(kk * kk, -1, keepdims=True), 1e-24))
    kd = heads(k[None] * (1.0 + (a - 1.0) * p['rk_k_a']), RK_HEADS)
    rh = heads(r, RK_HEADS)
    vh = heads(v, RK_HEADS)
    y, s_fin = rwkv_scan(rh, heads(decay, RK_HEADS), kd, vh, -kk, kk * heads(a, RK_HEADS), s0)
    mu = jnp.mean(y, -1, keepdims=True)
    var = jnp.mean(jnp.square(y - mu), -1, keepdims=True)
    yn = ((y - mu) * lax.rsqrt(var + GN_EPS)).reshape(bsz, n, RK_WIDTH).astype(u.dtype) * p['rk_ln_g'] + p['rk_ln_b']
    bonus = (jnp.sum(rh * (kd[0] + kd[1]) * p['rk_r_k'], -1, keepdims=True) * vh).reshape(bsz, n, RK_WIDTH)
    return (yn + bonus) * g, s_fin


def moe_ffn(h, p):
    shp = h.shape
    dm = shp[-1]
    xt = h.reshape(-1, dm)
    n_tok = xt.shape[0]
    logits = (xt @ p['moe_router_w'] + p['moe_router_b']).astype(jnp.float32)
    top_v, top_i = lax.top_k(logits, TOP_K)
    gates = jax.nn.softmax(top_v, axis=-1).astype(h.dtype).reshape(-1)
    e_flat = top_i.reshape(-1).astype(jnp.int32)
    n_rows = n_tok * TOP_K
    tok_flat = jnp.arange(n_rows, dtype=jnp.int32) // TOP_K
    order = jnp.argsort(e_flat)
    e_sorted = e_flat[order]
    counts = jnp.zeros((N_EXPERTS,), jnp.int32).at[e_flat].add(1)
    starts = jnp.cumsum(counts) - counts
    pcounts = (counts + MOE_BLK - 1) // MOE_BLK * MOE_BLK
    pends = jnp.cumsum(pcounts)
    pstarts = pends - pcounts
    dest = pstarts[e_sorted] + jnp.arange(n_rows, dtype=jnp.int32) - starts[e_sorted]
    n_blk = -(-n_rows // MOE_BLK) + N_EXPERTS
    row_tok = jnp.full((n_blk * MOE_BLK,), n_tok, jnp.int32).at[dest].set(tok_flat[order])
    row_gate = jnp.zeros((n_blk * MOE_BLK,), h.dtype).at[dest].set(gates[order])
    blk_exp = jnp.minimum(jnp.searchsorted(pends, jnp.arange(n_blk, dtype=jnp.int32) * MOE_BLK, side='right'), N_EXPERTS - 1).astype(jnp.int32)
    x_pad = jnp.concatenate([xt, jnp.zeros((1, dm), xt.dtype)], 0)
    xb = x_pad[row_tok].reshape(n_blk, MOE_BLK, dm)
    w1, b1, w2, b2 = p['moe_w1'], p['moe_b1'], p['moe_w2'], p['moe_b2']

    def expert_block(args):
        xi, e = args
        uu = xi @ w1[e] + b1[e]
        glu = jnp.minimum(uu[..., 0::2], SWIGLU_LIMIT)
        lin = jnp.clip(uu[..., 1::2], -SWIGLU_LIMIT, SWIGLU_LIMIT)
        return (glu * jax.nn.sigmoid(SWIGLU_ALPHA * glu) * (lin + 1.0)) @ w2[e] + b2[e]

    yb = lax.map(expert_block, (xb, blk_exp)).reshape(-1, dm)
    y = jnp.zeros((n_tok + 1, dm), h.dtype).at[row_tok].add(yb * row_gate[:, None])
    return y[:n_tok].reshape(shp)


def residual_tail(x, o_na, o_sw, o_rk, g1, sh2, sc2, g2, p):
    b, n, _ = x.shape
    o = jnp.concatenate([o_na.reshape(b, n, NA_WIDTH), o_sw.reshape(b, n, SWA_WIDTH), o_rk], -1) @ p['w_out']
    x = x + g1 * o
    h = rms_norm(x, p['norm2_g']) * (1.0 + sc2) + sh2
    return x + g2 * moe_ffn(h, p)


def context_layer(x, c_ctx, p):
    sh1, sc1, g1, sh2, sc2, g2 = ada_mod(c_ctx[None, :], p['w_ada'], p['b_ada'])
    h = rms_norm(x, p['norm1_g']) * (1.0 + sc1) + sh1
    na_q, na_k, na_v, sw_q, sw_k, sw_v, u = mix_in(h, p)
    o_na = ctx_attention(na_q, na_k, na_v, None)
    o_sw = ctx_attention(sw_q, sw_k, sw_v, p['swa_sink'])
    s0 = jnp.zeros((2, x.shape[0], RK_HEADS, HEAD_DIM, HEAD_DIM), jnp.float32)
    o_rk, s_fin = rwkv_mix(u, s0, p)
    x = residual_tail(x, o_na, o_sw, o_rk, g1, sh2, sc2, g2, p)
    return x, na_k, na_v, sw_k, sw_v, jnp.moveaxis(s_fin, 0, 1).astype(x.dtype)


def latent_layer(x, c, na_kc, na_vc, sw_kc, sw_vc, s0, p):
    sh1, sc1, g1, sh2, sc2, g2 = ada_mod(c, p['w_ada'], p['b_ada'])
    h = rms_norm(x, p['norm1_g']) * (1.0 + sc1) + sh1
    na_q, na_k, na_v, sw_q, sw_k, sw_v, u = mix_in(h, p)
    cos, sin = axial_rope_tables(x.shape[1])
    sw_q = apply_axial_rope(sw_q, cos, sin)
    sw_k = apply_axial_rope(sw_k, cos, sin)
    o_na = na_latent(na_q, na_k, na_v, na_kc, na_vc, p['na_rpb'])
    o_sw = swa_latent(sw_q, sw_k, sw_v, sw_kc, sw_vc, p['swa_sink'])
    o_rk, _ = rwkv_mix(u, jnp.moveaxis(s0, 1, 0).astype(jnp.float32), p)
    return residual_tail(x, o_na, o_sw, o_rk, g1, sh2, sc2, g2, p)


def setup_inputs(seed: int = 0) -> dict:
    key = jax.random.key(seed)
    kit = iter(jax.random.split(key, 48))

    def nrm(shape, s):
        return jax.random.normal(next(kit), shape, jnp.float32) * s

    L = DEPTH
    d = {}
    d['x_prompt'] = nrm((BATCH, SEQ, D_MODEL), 1.0)
    d['x_sample'] = nrm((DEC_BATCH, DEC_SEQ, D_MODEL), 1.0)
    d['c'] = nrm((DEC_BATCH, D_MODEL), 1.0)
    d['cache_na_k'] = nrm((DEC_BATCH, L, PAST_LEN, NA_HEADS, HEAD_DIM), 1.0)
    d['cache_na_v'] = nrm((DEC_BATCH, L, PAST_LEN, NA_HEADS, HEAD_DIM), 1.0)
    d['cache_swa_k'] = nrm((DEC_BATCH, L, PAST_LEN, SWA_KV_HEADS, HEAD_DIM), 1.0)
    d['cache_swa_v'] = nrm((DEC_BATCH, L, PAST_LEN, SWA_KV_HEADS, HEAD_DIM), 1.0)
    d['state_rwkv'] = nrm((DEC_BATCH, L, 2, RK_HEADS, HEAD_DIM, HEAD_DIM), 0.3)
    d['c_ctx'] = nrm((D_MODEL,), 1.0)
    d['w_ada'] = nrm((L, D_MODEL, 6 * D_MODEL), 0.5 * D_MODEL ** -0.5)
    d['b_ada'] = nrm((L, 6 * D_MODEL), 0.02)
    d['norm1_g'] = 1.0 + nrm((L, D_MODEL), 0.05)
    d['norm2_g'] = 1.0 + nrm((L, D_MODEL), 0.05)
    d['w_in'] = nrm((L, D_MODEL, IN_COLS), D_MODEL ** -0.5)
    d['w_out'] = nrm((L, MIX_WIDTH, D_MODEL), MIX_WIDTH ** -0.5)
    d['na_q_norm'] = 1.0 + nrm((L, HEAD_DIM), 0.05)
    d['na_k_norm'] = 1.0 + nrm((L, HEAD_DIM), 0.05)
    d['na_rpb'] = nrm((L, NA_HEADS, 2 * NA_WIN_R - 1, 2 * NA_WIN_C - 1), 0.1)
    d['swa_q_norm'] = 1.0 + nrm((L, HEAD_DIM), 0.05)
    d['swa_k_norm'] = 1.0 + nrm((L, HEAD_DIM), 0.05)
    d['swa_sink'] = nrm((L, SWA_HEADS), 0.5)
    d['rk_conv'] = nrm((L, RK_CONV_W, RK_COLS), 0.1).at[:, RK_CONV_W // 2].add(1.0)
    d['rk_w0'] = nrm((L, 2, RK_WIDTH), 0.5)
    d['rk_w2'] = nrm((L, 2, RK_DECAY_LORA, RK_WIDTH), 0.1)
    d['rk_a0'] = nrm((L, 2, RK_WIDTH), 0.3)
    d['rk_a2'] = nrm((L, 2, RK_A_LORA, RK_WIDTH), 0.1)
    d['rk_g2'] = nrm((L, RK_GATE_LORA, RK_WIDTH), RK_GATE_LORA ** -0.5)
    d['rk_k_k'] = 0.85 + nrm((L, RK_WIDTH), 0.1)
    d['rk_k_a'] = 1.0 + nrm((L, RK_WIDTH), 0.1)
    d['rk_r_k'] = nrm((L, RK_HEADS, HEAD_DIM), 0.1)
    d['rk_ln_g'] = 1.0 + nrm((L, RK_WIDTH), 0.05)
    d['rk_ln_b'] = nrm((L, RK_WIDTH), 0.02)
    d['moe_router_w'] = nrm((L, D_MODEL, N_EXPERTS), D_MODEL ** -0.5)
    d['moe_router_b'] = nrm((L, N_EXPERTS), 0.01)
    d['moe_w1'] = nrm((L, N_EXPERTS, D_MODEL, 2 * D_EXPERT), D_MODEL ** -0.5)
    d['moe_b1'] = nrm((L, N_EXPERTS, 2 * D_EXPERT), 0.01)
    d['moe_w2'] = nrm((L, N_EXPERTS, D_EXPERT, D_MODEL), D_EXPERT ** -0.5)
    d['moe_b2'] = nrm((L, N_EXPERTS, D_MODEL), 0.01)
    return d


def reference(x_prompt, x_sample, c, cache_na_k, cache_na_v, cache_swa_k, cache_swa_v, state_rwkv,
              c_ctx, w_ada, b_ada, norm1_g, norm2_g, w_in, w_out, na_q_norm, na_k_norm, na_rpb,
              swa_q_norm, swa_k_norm, swa_sink, rk_conv, rk_w0, rk_w2, rk_a0, rk_a2, rk_g2,
              rk_k_k, rk_k_a, rk_r_k, rk_ln_g, rk_ln_b, moe_router_w, moe_router_b,
              moe_w1, moe_b1, moe_w2, moe_b2):
    y_p = x_prompt
    y_s = x_sample
    na_k_l, na_v_l, sw_k_l, sw_v_l, st_l = [], [], [], [], []
    for l in range(DEPTH):
        p = {'w_ada': w_ada[l], 'b_ada': b_ada[l], 'norm1_g': norm1_g[l], 'norm2_g': norm2_g[l],
             'w_in': w_in[l], 'w_out': w_out[l], 'na_q_norm': na_q_norm[l], 'na_k_norm': na_k_norm[l],
             'na_rpb': na_rpb[l], 'swa_q_norm': swa_q_norm[l], 'swa_k_norm': swa_k_norm[l],
             'swa_sink': swa_sink[l], 'rk_conv': rk_conv[l], 'rk_w0': rk_w0[l], 'rk_w2': rk_w2[l],
             'rk_a0': rk_a0[l], 'rk_a2': rk_a2[l], 'rk_g2': rk_g2[l], 'rk_k_k': rk_k_k[l],
             'rk_k_a': rk_k_a[l], 'rk_r_k': rk_r_k[l], 'rk_ln_g': rk_ln_g[l], 'rk_ln_b': rk_ln_b[l],
             'moe_router_w': moe_router_w[l], 'moe_router_b': moe_router_b[l],
             'moe_w1': moe_w1[l], 'moe_b1': moe_b1[l], 'moe_w2': moe_w2[l], 'moe_b2': moe_b2[l]}
        y_p, nk, nv, sk, sv, st = context_layer(y_p, c_ctx, p)
        na_k_l.append(nk)
        na_v_l.append(nv)
        sw_k_l.append(sk)
        sw_v_l.append(sv)
        st_l.append(st)
        y_s = latent_layer(y_s, c, cache_na_k[:, l], cache_na_v[:, l], cache_swa_k[:, l],
                           cache_swa_v[:, l], state_rwkv[:, l], p)
    new_na_k = jnp.stack(na_k_l, axis=1)
    new_na_v = jnp.stack(na_v_l, axis=1)
    new_swa_k = jnp.stack(sw_k_l, axis=1)
    new_swa_v = jnp.stack(sw_v_l, axis=1)
    new_state_rwkv = jnp.stack(st_l, axis=1)
    return (y_p, y_s, new_na_k, new_na_v, new_swa_k, new_swa_v, new_state_rwkv)
```

```python
import functools

import jax
import jax.numpy as jnp
from jax import lax
from jax.experimental import pallas as pl
from jax.experimental.pallas import tpu as pltpu

F32 = jnp.float32
BF16 = jnp.bfloat16

D_MODEL = 1024
HEAD_DIM = 64
LANES = 128
GRID_W = 64
NA_HEADS = 6
SWA_HEADS = 4
SWA_KV_HEADS = 2
RK_HEADS = 6
NA_WIDTH = NA_HEADS * HEAD_DIM
SWA_WIDTH = SWA_HEADS * HEAD_DIM
SWA_KV_WIDTH = SWA_KV_HEADS * HEAD_DIM
RK_WIDTH = RK_HEADS * HEAD_DIM
RK_DECAY_LORA = 64
RK_A_LORA = 64
RK_GATE_LORA = 128
RK_COLS = 3 * RK_WIDTH + RK_DECAY_LORA + RK_A_LORA + RK_GATE_LORA
NA_COLS = 3 * NA_WIDTH
SWA_COLS = SWA_WIDTH + 2 * SWA_KV_WIDTH
ATT_COLS = NA_COLS + SWA_COLS
IN_COLS = ATT_COLS + RK_COLS
NA_WIN_R = 8
NA_WIN_C = 16
SWA_WIN = 128
ROPE_THETA = 10000.0
ATTN_SCALE = HEAD_DIM ** -0.5
N_EXPERTS = 32
TOP_K = 4
SWIGLU_LIMIT = 7.0
SWIGLU_ALPHA = 1.702
MOE_BLK = 256
RMS_EPS = 1e-6
GN_EPS = 64e-5
NEG_BIG = -1e30

TOK_TILE = 256
VMEM_LIMIT = 48 * 1024 * 1024


def _cparams(sem):
    return pltpu.CompilerParams(dimension_semantics=sem, vmem_limit_bytes=VMEM_LIMIT)


def _dot(a, b):
    return jnp.dot(a, b, preferred_element_type=F32)


def _dot_nt(a, b):
    return lax.dot_general(a, b, (((1,), (1,)), ((), ())), preferred_element_type=F32)


def _split_bf16(x):
    hi = x.astype(BF16)
    lo = (x - hi.astype(F32)).astype(BF16)
    return hi, lo


def _dot3(a, b):
    ah, al = _split_bf16(a)
    bh, bl = _split_bf16(b)
    return _dot(ah, bh) + (_dot(ah, bl) + _dot(al, bh))


def _lane_lo(shape):
    return lax.broadcasted_iota(jnp.int32, shape, len(shape) - 1) < HEAD_DIM


def _pair_sum(x):
    lo = _lane_lo(x.shape)
    s_lo = jnp.sum(jnp.where(lo, x, 0.0), axis=-1, keepdims=True)
    s_hi = jnp.sum(jnp.where(lo, 0.0, x), axis=-1, keepdims=True)
    return jnp.where(lo, s_lo, s_hi)


def _stack_heads(q):
    lo = _lane_lo(q.shape)
    return jnp.concatenate([jnp.where(lo, q, 0.0), jnp.where(lo, 0.0, q)], axis=0)


def _unstack_heads(o2):
    n = o2.shape[0] // 2
    return jnp.where(_lane_lo((n, LANES)), o2[:n], o2[n:])


def _dup_head(x, j):
    keep = _lane_lo(x.shape) == (j == 0)
    return jnp.where(keep, x, pltpu.roll(x, HEAD_DIM, 1))


def _ada_kernel(c_ref, w_ref, b_ref, o_ref):
    cv = c_ref[...]
    s = cv * jax.nn.sigmoid(cv)
    o_ref[0] = _dot3(s, w_ref[0]) + b_ref[0]


def _ada_mod(cvecs, w_ada, b_ada):
    depth, _, n_out = w_ada.shape
    rows = cvecs.shape[0]
    tn = 1024
    return pl.pallas_call(
        _ada_kernel,
        grid=(depth, n_out // tn),
        in_specs=[
            pl.BlockSpec((rows, D_MODEL), lambda l, j: (0, 0)),
            pl.BlockSpec((1, D_MODEL, tn), lambda l, j: (l, 0, j)),
            pl.BlockSpec((1, 1, tn), lambda l, j: (l, 0, j)),
        ],
        out_specs=pl.BlockSpec((1, rows, tn), lambda l, j: (l, 0, j)),
        out_shape=jax.ShapeDtypeStruct((depth, rows, n_out), F32),
        compiler_params=_cparams(("parallel", "parallel")),
        name="ada_mod",
    )(cvecs, w_ada, b_ada.reshape(depth, 1, n_out))


NA_QK_BLOCKS = 2 * NA_WIDTH // LANES
SWA_Q_BLOCK0 = NA_COLS // LANES
SWA_QK_BLOCKS = (SWA_WIDTH + SWA_KV_WIDTH) // LANES


def _in_proj_kernel(x_ref, g_ref, mod_ref, w_ref, qkg_ref, cos_ref, sin_ref, att_ref, u_ref):
    x = x_ref[...]
    y = x * lax.rsqrt(jnp.mean(x * x, axis=-1, keepdims=True) + RMS_EPS)
    h = (y * g_ref[...]) * (1.0 + mod_ref[0, 1:2, :]) + mod_ref[0, 0:1, :]
    proj = _dot(h.astype(BF16), w_ref[...])
    u_ref[...] = proj[:, ATT_COLS:]

    def qk_norm(blk, gain):
        ms = _pair_sum(blk * blk) * (1.0 / HEAD_DIM)
        return blk * lax.rsqrt(ms + RMS_EPS) * gain

    lane = lax.broadcasted_iota(jnp.int32, (x.shape[0], LANES), 1)
    first = (lane % (HEAD_DIM // 2)) < (HEAD_DIM // 4)
    for cb in range(ATT_COLS // LANES):
        blk = proj[:, cb * LANES:(cb + 1) * LANES]
        if cb < NA_QK_BLOCKS:
            gi = 0 if cb < NA_QK_BLOCKS // 2 else 1
            blk = qk_norm(blk, qkg_ref[gi:gi + 1, :])
        elif SWA_Q_BLOCK0 <= cb < SWA_Q_BLOCK0 + SWA_QK_BLOCKS:
            gi = 2 if cb < SWA_Q_BLOCK0 + SWA_WIDTH // LANES else 3
            blk = qk_norm(blk, qkg_ref[gi:gi + 1, :])
            partner = jnp.where(first, pltpu.roll(blk, LANES - HEAD_DIM // 4, 1),
                                pltpu.roll(blk, HEAD_DIM // 4, 1))
            blk = blk * cos_ref[...] + partner * sin_ref[...]
        att_ref[:, cb * LANES:(cb + 1) * LANES] = blk


def _in_proj(x, norm_g, mods, w_in_bf16, qk_gains, cos_tab, sin_tab, tile_mod, tile_rope):
    n_tok = x.shape[0]
    return pl.pallas_call(
        _in_proj_kernel,
        grid=(n_tok // TOK_TILE,),
        in_specs=[
            pl.BlockSpec((TOK_TILE, D_MODEL), lambda i: (i, 0)),
            pl.BlockSpec((1, D_MODEL), lambda i: (0, 0)),
            pl.BlockSpec((1, 6, D_MODEL), lambda i: (tile_mod(i), 0, 0)),
            pl.BlockSpec((D_MODEL, IN_COLS), lambda i: (0, 0)),
            pl.BlockSpec((4, LANES), lambda i: (0, 0)),
            pl.BlockSpec((TOK_TILE, LANES), lambda i: (tile_rope(i), 0)),
            pl.BlockSpec((TOK_TILE, LANES), lambda i: (tile_rope(i), 0)),
        ],
        out_specs=[
            pl.BlockSpec((TOK_TILE, ATT_COLS), lambda i: (i, 0)),
            pl.BlockSpec((TOK_TILE, RK_COLS), lambda i: (i, 0)),
        ],
        out_shape=[
            jax.ShapeDtypeStruct((n_tok, ATT_COLS), F32),
            jax.ShapeDtypeStruct((n_tok, RK_COLS), F32),
        ],
        compiler_params=_cparams(("parallel",)),
        name="in_proj",
    )(x, norm_g, mods, w_in_bf16, qk_gains, cos_tab, sin_tab)


def _rope_tables(n_lat):
    nf = HEAD_DIM // 4
    t = jnp.arange(n_lat)
    lane = jnp.arange(LANES)
    d = lane % HEAD_DIM
    inv = ROPE_THETA ** (-(d % nf).astype(F32) / nf)
    pos = jnp.where((d // (2 * nf))[None, :] == 0, (t // GRID_W)[:, None], (t % GRID_W)[:, None]).astype(F32)
    ang = pos * inv[None, :]
    sign = jnp.where((d % (2 * nf)) < nf, -1.0, 1.0).astype(F32)
    cos = jnp.concatenate([jnp.cos(ang), jnp.ones((TOK_TILE, LANES), F32)], 0)
    sin = jnp.concatenate([jnp.sin(ang) * sign[None, :], jnp.zeros((TOK_TILE, LANES), F32)], 0)
    return cos, sin


def _ctx_attn_kernel(sink_ref, q_ref, k_ref, v_ref, o_ref, *, gqa):
    j = pl.program_id(1)
    k = k_ref[0]
    v = v_ref[0]
    if gqa:
        k = _dup_head(k, j)
        v = _dup_head(v, j)
    n = k.shape[0]
    q2 = _stack_heads(q_ref[0]).astype(BF16)
    s = _dot_nt(q2, k.astype(BF16)) * ATTN_SCALE
    m = jnp.max(s, axis=-1, keepdims=True)
    if gqa:
        row = lax.broadcasted_iota(jnp.int32, (2 * n, 1), 0)
        snk = jnp.where(row < n, sink_ref[2 * j], sink_ref[2 * j + 1])
        m = jnp.maximum(m, snk)
    p = jnp.exp(s - m)
    den = jnp.sum(p, axis=-1, keepdims=True)
    if gqa:
        den = den + jnp.exp(snk - m)
    o2 = _dot(p.astype(BF16), v.astype(BF16)) / den
    o_ref[0] = _unstack_heads(o2)


def _ctx_attn(att, sink, *, gqa):
    b, t, _ = att.shape
    if gqa:
        nq = SWA_WIDTH // LANES
        qb, kb, vb = SWA_Q_BLOCK0, SWA_Q_BLOCK0 + nq, SWA_Q_BLOCK0 + nq + 1
        kmap = lambda bi, j: (bi, 0, kb)
        vmap = lambda bi, j: (bi, 0, vb)
    else:
        nq = NA_WIDTH // LANES
        qb, kb, vb = 0, nq, 2 * nq
        kmap = lambda bi, j: (bi, 0, kb + j)
        vmap = lambda bi, j: (bi, 0, vb + j)
    return pl.pallas_call(
        functools.partial(_ctx_attn_kernel, gqa=gqa),
        grid=(b, nq),
        in_specs=[
            pl.BlockSpec(memory_space=pltpu.SMEM),
            pl.BlockSpec((1, t, LANES), lambda bi, j: (bi, 0, qb + j)),
            pl.BlockSpec((1, t, LANES), kmap),
            pl.BlockSpec((1, t, LANES), vmap),
        ],
        out_specs=pl.BlockSpec((1, t, LANES), lambda bi, j: (bi, 0, j)),
        out_shape=jax.ShapeDtypeStruct((b, t, nq * LANES), F32),
        compiler_params=_cparams(("parallel", "parallel")),
        name="ctx_attn_swa" if gqa else "ctx_attn_na",
    )(sink, att, att, att)


def _na_lat_kernel(q_ref, k_ref, v_ref, kc_ref, vc_ref, tab_ref, o_ref, kb_ref, vb_ref):
    n = q_ref.shape[1]
    rows = n // GRID_W
    win = NA_WIN_R * GRID_W
    kb_ref[...] = k_ref[0].astype(BF16)
    vb_ref[...] = v_ref[0].astype(BF16)
    kc = kc_ref[0].astype(BF16)
    vc = vc_ref[0].astype(BF16)

    def row_block(i, carry):
        start = jnp.clip(i - NA_WIN_R // 2, 0, rows - NA_WIN_R)
        rb = start - i + (NA_WIN_R - 1)
        q0 = pl.multiple_of(i * GRID_W, GRID_W)
        k0 = pl.multiple_of(start * GRID_W, GRID_W)
        q2 = _stack_heads(q_ref[0, pl.ds(q0, GRID_W), :]).astype(BF16)
        kw = kb_ref[pl.ds(k0, win), :]
        vw = vb_ref[pl.ds(k0, win), :]
        s_loc = _dot_nt(q2, kw) * ATTN_SCALE + tab_ref[0, rb]
        s_ctx = _dot_nt(q2, kc) * ATTN_SCALE
        m = jnp.maximum(jnp.max(s_loc, axis=-1, keepdims=True), jnp.max(s_ctx, axis=-1, keepdims=True))
        p_loc = jnp.exp(s_loc - m)
        p_ctx = jnp.exp(s_ctx - m)
        den = jnp.sum(p_loc, axis=-1, keepdims=True) + jnp.sum(p_ctx, axis=-1, keepdims=True)
        o2 = (_dot(p_loc.astype(BF16), vw) + _dot(p_ctx.astype(BF16), vc)) / den
        o_ref[0, pl.ds(q0, GRID_W), :] = _unstack_heads(o2)
        return carry

    lax.fori_loop(0, rows, row_block, 0)


def _na_bias_tables(rpb):
    col = jnp.arange(GRID_W)
    cstart = jnp.clip(col - NA_WIN_C // 2, 0, GRID_W - NA_WIN_C)
    col_mask = (col[None, :] >= cstart[:, None]) & (col[None, :] < cstart[:, None] + NA_WIN_C)
    col_idx = jnp.clip(col[None, :] - col[:, None] + NA_WIN_C - 1, 0, 2 * NA_WIN_C - 2)
    rpb_cols = jnp.where(col_mask[None, None], rpb[:, :, col_idx], NEG_BIG)
    roff = jnp.arange(NA_WIN_R)[:, None] + jnp.arange(NA_WIN_R)[None, :]
    t = rpb_cols[:, roff]
    t = jnp.transpose(t, (0, 1, 3, 2, 4)).reshape(NA_HEADS // 2, 2, NA_WIN_R, GRID_W, NA_WIN_R * GRID_W)
    return jnp.transpose(t, (0, 2, 1, 3, 4)).reshape(NA_HEADS // 2, NA_WIN_R, 2 * GRID_W, NA_WIN_R * GRID_W)


def _na_latent(att, kc, vc, tab):
    b, n, _ = att.shape
    p = kc.shape[1]
    nq = NA_WIDTH // LANES
    return pl.pallas_call(
        _na_lat_kernel,
        grid=(b, nq),
        in_specs=[
            pl.BlockSpec((1, n, LANES), lambda bi, j: (bi, 0, j)),
            pl.BlockSpec((1, n, LANES), lambda bi, j: (bi, 0, nq + j)),
            pl.BlockSpec((1, n, LANES), lambda bi, j: (bi, 0, 2 * nq + j)),
            pl.BlockSpec((1, p, LANES), lambda bi, j: (bi, 0, j)),
            pl.BlockSpec((1, p, LANES), lambda bi, j: (bi, 0, j)),
            pl.BlockSpec((1, NA_WIN_R, 2 * GRID_W, NA_WIN_R * GRID_W), lambda bi, j: (j, 0, 0, 0)),
        ],
        out_specs=pl.BlockSpec((1, n, LANES), lambda bi, j: (bi, 0, j)),
        out_shape=jax.ShapeDtypeStruct((b, n, NA_WIDTH), F32),
        scratch_shapes=[pltpu.VMEM((n, LANES), BF16), pltpu.VMEM((n, LANES), BF16)],
        compiler_params=_cparams(("parallel", "parallel")),
        name="na_latent",
    )(att, att, att, kc, vc, tab)


def _swa_lat_kernel(sink_ref, q_ref, k_ref, v_ref, kc_ref, vc_ref, o_ref, kb_ref, vb_ref):
    j = pl.program_id(1)
    n = q_ref.shape[1]
    blk = SWA_WIN
    span = 3 * blk
    kb_ref[...] = _dup_head(k_ref[0], j).astype(BF16)
    vb_ref[...] = _dup_head(v_ref[0], j).astype(BF16)
    kc = _dup_head(kc_ref[0], j).astype(BF16)
    vc = _dup_head(vc_ref[0], j).astype(BF16)
    row = lax.broadcasted_iota(jnp.int32, (2 * blk, 1), 0)
    snk = jnp.where(row < blk, sink_ref[2 * j], sink_ref[2 * j + 1])
    qoff = lax.broadcasted_iota(jnp.int32, (2 * blk, span), 0) % blk
    koff = lax.broadcasted_iota(jnp.int32, (2 * blk, span), 1)

    def q_block(qi, carry):
        q0 = pl.multiple_of(qi * blk, blk)
        w0 = pl.multiple_of(jnp.clip(q0 - blk, 0, n - span), blk)
        q2 = _stack_heads(q_ref[0, pl.ds(q0, blk), :]).astype(BF16)
        kw = kb_ref[pl.ds(w0, span), :]
        vw = vb_ref[pl.ds(w0, span), :]
        valid = jnp.abs((q0 + qoff) - (w0 + koff)) <= SWA_WIN
        s_loc = jnp.where(valid, _dot_nt(q2, kw) * ATTN_SCALE, NEG_BIG)
        s_ctx = _dot_nt(q2, kc) * ATTN_SCALE
        m = jnp.maximum(jnp.max(s_loc, axis=-1, keepdims=True), jnp.max(s_ctx, axis=-1, keepdims=True))
        m = jnp.maximum(m, snk)
        p_loc = jnp.exp(s_loc - m)
        p_ctx = jnp.exp(s_ctx - m)
        den = (jnp.sum(p_loc, axis=-1, keepdims=True) + jnp.sum(p_ctx, axis=-1, keepdims=True)
               + jnp.exp(snk - m))
        o2 = (_dot(p_loc.astype(BF16), vw) + _dot(p_ctx.astype(BF16), vc)) / den
        o_ref[0, pl.ds(q0, blk), :] = _unstack_heads(o2)
        return carry

    lax.fori_loop(0, n // blk, q_block, 0)


def _swa_latent(att, kc, vc, sink):
    b, n, _ = att.shape
    p = kc.shape[1]
    nq = SWA_WIDTH // LANES
    qb, kb, vb = SWA_Q_BLOCK0, SWA_Q_BLOCK0 + nq, SWA_Q_BLOCK0 + nq + 1
    return pl.pallas_call(
        _swa_lat_kernel,
        grid=(b, nq),
        in_specs=[
            pl.BlockSpec(memory_space=pltpu.SMEM),
            pl.BlockSpec((1, n, LANES), lambda bi, j: (bi, 0, qb + j)),
            pl.BlockSpec((1, n, LANES), lambda bi, j: (bi, 0, kb)),
            pl.BlockSpec((1, n, LANES), lambda bi, j: (bi, 0, vb)),
            pl.BlockSpec((1, p, LANES), lambda bi, j: (bi, 0, 0)),
            pl.BlockSpec((1, p, LANES), lambda bi, j: (bi, 0, 0)),
        ],
        out_specs=pl.BlockSpec((1, n, LANES), lambda bi, j: (bi, 0, j)),
        out_shape=jax.ShapeDtypeStruct((b, n, SWA_WIDTH), F32),
        scratch_shapes=[pltpu.VMEM((n, LANES), BF16), pltpu.VMEM((n, LANES), BF16)],
        compiler_params=_cparams(("parallel", "parallel")),
        name="swa_latent",
    )(sink, att, att, att, kc, vc)


RK_NB = RK_WIDTH // LANES
LORA_BLOCK = 3 * RK_WIDTH // LANES
GATE_BLOCK = LORA_BLOCK + 1


def _softplus(x):
    return jnp.maximum(x, 0.0) + jnp.log(1.0 + jnp.exp(-jnp.abs(x)))


def _rk_prep_kernel(u_ref, up_ref, un_ref, cw_ref, w0_ref, w2_ref, a0_ref, a2_ref, g2_ref, kk_ref, ka_ref,
                    rk_ref, r_ref, v_ref, a_ref, w_ref, kd_ref, b_ref, g_ref, bonus_ref,
                    *, n_ctx_tiles, tiles_per_seq):
    i = pl.program_id(0)
    li = i - n_ctx_tiles
    is_lat = i >= n_ctx_tiles
    has_prev = jnp.logical_and(is_lat, li % tiles_per_seq != 0)
    has_next = jnp.logical_and(is_lat, li % tiles_per_seq != tiles_per_seq - 1)
    u = u_ref[...]
    tm = u.shape[0]
    prev_row = jnp.where(has_prev, up_ref[7:8, :], 0.0)
    next_row = jnp.where(has_next, un_ref[0:1, :], 0.0)
    row = lax.broadcasted_iota(jnp.int32, u.shape, 0)
    um = jnp.where(row == 0, prev_row, pltpu.roll(u, 1, 0))
    up = jnp.where(row == tm - 1, next_row, pltpu.roll(u, tm - 1, 0))
    u = um * cw_ref[0:1, :] + u * cw_ref[1:2, :] + up * cw_ref[2:3, :]

    r = u[:, 0:RK_WIDTH]
    k = u[:, RK_WIDTH:2 * RK_WIDTH]
    v = u[:, 2 * RK_WIDTH:3 * RK_WIDTH]
    lora = u[:, LORA_BLOCK * LANES:(LORA_BLOCK + 1) * LANES]
    gl = u[:, GATE_BLOCK * LANES:(GATE_BLOCK + 1) * LANES]
    r_ref[...] = r
    v_ref[...] = v
    g_ref[...] = _dot3(jax.nn.sigmoid(gl), g2_ref[...])

    kn = k * kk_ref[...]
    kk = jnp.concatenate(
        [kn[:, c * LANES:(c + 1) * LANES]
         * lax.rsqrt(jnp.maximum(_pair_sum(jnp.square(kn[:, c * LANES:(c + 1) * LANES])), 1e-24))
         for c in range(RK_NB)], axis=1)
    a_ref[...] = -kk

    lora_t = jnp.tanh(lora)
    kd_sum = None
    for d in range(2):
        w = -_softplus(-(w0_ref[d:d + 1, :] + _dot3(lora_t, w2_ref[d]))) - 0.5
        w_ref[d] = jnp.exp(-jnp.exp(w))
        a = jax.nn.sigmoid(a0_ref[d:d + 1, :] + _dot3(lora, a2_ref[d]))
        kd = k * (1.0 + (a - 1.0) * ka_ref[...])
        kd_ref[d] = kd
        b_ref[d] = kk * a
        kd_sum = kd if kd_sum is None else kd_sum + kd

    t = r * kd_sum * rk_ref[...]
    bonus_ref[...] = jnp.concatenate(
        [_pair_sum(t[:, c * LANES:(c + 1) * LANES]) for c in range(RK_NB)], axis=1) * v


def _rk_prep(u, p, n_ctx_tiles, tiles_per_seq):
    n_tok = u.shape[0]
    n_tiles = n_tok // TOK_TILE
    sub = TOK_TILE // 8
    last8 = n_tok // 8 - 1
    tok = lambda i: (i, 0)
    const2 = lambda i: (0, 0)
    const3 = lambda i: (0, 0, 0)
    one = jax.ShapeDtypeStruct((n_tok, RK_WIDTH), F32)
    two = jax.ShapeDtypeStruct((2, n_tok, RK_WIDTH), F32)
    tok_spec = pl.BlockSpec((TOK_TILE, RK_WIDTH), tok)
    dir_spec = pl.BlockSpec((2, TOK_TILE, RK_WIDTH), lambda i: (0, i, 0))
    return pl.pallas_call(
        functools.partial(_rk_prep_kernel, n_ctx_tiles=n_ctx_tiles, tiles_per_seq=tiles_per_seq),
        grid=(n_tiles,),
        in_specs=[
            pl.BlockSpec((TOK_TILE, RK_COLS), tok),
            pl.BlockSpec((8, RK_COLS), lambda i: (jnp.maximum(i * sub - 1, 0), 0)),
            pl.BlockSpec((8, RK_COLS), lambda i: (jnp.minimum((i + 1) * sub, last8), 0)),
            pl.BlockSpec((3, RK_COLS), const2),
            pl.BlockSpec((2, RK_WIDTH), const2),
            pl.BlockSpec((2, LANES, RK_WIDTH), const3),
            pl.BlockSpec((2, RK_WIDTH), const2),
            pl.BlockSpec((2, LANES, RK_WIDTH), const3),
            pl.BlockSpec((RK_GATE_LORA, RK_WIDTH), const2),
            pl.BlockSpec((1, RK_WIDTH), const2),
            pl.BlockSpec((1, RK_WIDTH), const2),
            pl.BlockSpec((1, RK_WIDTH), const2),
        ],
        out_specs=[tok_spec, tok_spec, tok_spec, dir_spec, dir_spec, dir_spec, tok_spec, tok_spec],
        out_shape=[one, one, one, two, two, two, one, one],
        compiler_params=_cparams(("parallel",)),
        name="rk_prep",
    )(u, u, u, p["rk_conv"], p["rk_w0"], p["rk_w2_pad"], p["rk_a0"], p["rk_a2_pad"], p["rk_g2"],
      p["rk_k_k"], p["rk_k_a"], p["rk_r_k"])


SCAN_CHUNK = 16
SCAN_UNROLL = 8


def _rk_scan_kernel(r_ref, w_ref, k_ref, v_ref, a_ref, b_ref, s0_ref, y_ref, s_ref):
    @pl.when(pl.program_id(1) == 0)
    def _():
        s_ref[...] = s0_ref[...]

    nk = HEAD_DIM
    shape = (HEAD_DIM, LANES)

    def step(t, carry):
        def sa_body(kq, acc):
            for uu in range(SCAN_UNROLL):
                kx = kq * SCAN_UNROLL + uu
                acc = acc + s_ref[kx] * jnp.broadcast_to(a_ref[t, pl.ds(kx, 1), :], shape)
            return acc

        sa = lax.fori_loop(0, nk // SCAN_UNROLL, sa_body, jnp.zeros(shape, F32))
        vt = v_ref[t]

        def upd_body(kq, acc):
            for uu in range(SCAN_UNROLL):
                kx = kq * SCAN_UNROLL + uu
                row = lambda ref: jnp.broadcast_to(ref[t, pl.ds(kx, 1), :], shape)
                s_new = s_ref[kx] * row(w_ref) + sa * row(b_ref) + vt * row(k_ref)
                s_ref[kx] = s_new
                acc = acc + s_new * row(r_ref)
            return acc

        y_ref[t] = lax.fori_loop(0, nk // SCAN_UNROLL, upd_body, jnp.zeros(shape, F32))
        return carry

    lax.fori_loop(0, r_ref.shape[0], step, 0)


def _rk_scan(r, w, k, v, a, b, s0):
    t, _, nl = r.shape
    seq = pl.BlockSpec((SCAN_CHUNK, HEAD_DIM, LANES), lambda j, c: (c, 0, j))
    st = pl.BlockSpec((HEAD_DIM, HEAD_DIM, LANES), lambda j, c: (0, 0, j))
    return pl.pallas_call(
        _rk_scan_kernel,
        grid=(nl // LANES, t // SCAN_CHUNK),
        in_specs=[seq, seq, seq, seq, seq, seq, st],
        out_specs=[seq, st],
        out_shape=[jax.ShapeDtypeStruct((t, HEAD_DIM, nl), F32),
                   jax.ShapeDtypeStruct((HEAD_DIM, HEAD_DIM, nl), F32)],
        compiler_params=_cparams(("parallel", "arbitrary")),
        name="rk_scan",
    )(r, w, k, v, a, b, s0)


def _to_scan_layout(z_fwd, z_bwd, bsz, t, nl):
    z0 = z_fwd.reshape(bsz, t, RK_HEADS, HEAD_DIM)
    z1 = jnp.flip(z_bwd.reshape(bsz, t, RK_HEADS, HEAD_DIM), axis=1)
    z = jnp.stack([z0, z1])
    z = jnp.transpose(z, (2, 4, 0, 1, 3)).reshape(t, HEAD_DIM, 2 * bsz * RK_HEADS)
    return jnp.pad(z, ((0, 0), (0, 0), (0, nl - z.shape[-1])))


def _rwkv_scan_group(prep, lo, bsz, t, s0):
    r, v, a, w, kd, b = prep
    n_inst = 2 * bsz * RK_HEADS
    nl = -(-n_inst // LANES) * LANES
    sl = slice(lo, lo + bsz * t)
    lay = functools.partial(_to_scan_layout, bsz=bsz, t=t, nl=nl)
    if s0 is None:
        s0l = jnp.zeros((HEAD_DIM, HEAD_DIM, nl), F32)
    else:
        s0l = jnp.transpose(s0, (4, 3, 0, 1, 2)).reshape(HEAD_DIM, HEAD_DIM, n_inst)
        s0l = jnp.pad(s0l, ((0, 0), (0, 0), (0, nl - n_inst)))
    y, s_fin = _rk_scan(lay(r[sl], r[sl]), lay(w[0, sl], w[1, sl]), lay(kd[0, sl], kd[1, sl]),
                        lay(v[sl], v[sl]), lay(a[sl], a[sl]), lay(b[0, sl], b[1, sl]), s0l)
    y = y[:, :, :n_inst].reshape(t, HEAD_DIM, 2, bsz, RK_HEADS)
    y = jnp.transpose(y, (2, 3, 0, 4, 1))
    y = (y[0] + jnp.flip(y[1], axis=1)).reshape(bsz * t, RK_WIDTH)
    s_fin = s_fin[:, :, :n_inst].reshape(HEAD_DIM, HEAD_DIM, 2, bsz, RK_HEADS)
    return y, jnp.transpose(s_fin, (2, 3, 4, 1, 0))


def _out_proj_kernel(x_ref, ona_ref, osw_ref, y_ref, bonus_ref, g_ref, lng_ref, lnb_ref, w_ref, mod_ref,
                     n2_ref, rw_ref, rb_ref, x1_ref, h2_ref, lg_ref):
    y = y_ref[...]
    parts = []
    for c in range(RK_NB):
        yc = y[:, c * LANES:(c + 1) * LANES]
        dc = yc - _pair_sum(yc) * (1.0 / HEAD_DIM)
        var = _pair_sum(dc * dc) * (1.0 / HEAD_DIM)
        parts.append(dc * lax.rsqrt(var + GN_EPS))
    yn = jnp.concatenate(parts, axis=1) * lng_ref[...] + lnb_ref[...]
    o_rk = (yn + bonus_ref[...]) * g_ref[...]
    o = (_dot(ona_ref[...].astype(BF16), w_ref[0:NA_WIDTH, :])
         + _dot(osw_ref[...].astype(BF16), w_ref[NA_WIDTH:NA_WIDTH + SWA_WIDTH, :])
         + _dot(o_rk.astype(BF16), w_ref[NA_WIDTH + SWA_WIDTH:, :]))
    x1 = x_ref[...] + mod_ref[0, 2:3, :] * o
    x1_ref[...] = x1
    yn2 = x1 * lax.rsqrt(jnp.mean(x1 * x1, axis=-1, keepdims=True) + RMS_EPS)
    h2 = (yn2 * n2_ref[...]) * (1.0 + mod_ref[0, 4:5, :]) + mod_ref[0, 3:4, :]
    h2_ref[...] = h2.astype(BF16)
    lg_ref[...] = (_dot3(h2, rw_ref[...]) + rb_ref[...])[:, :N_EXPERTS]


def _out_proj(x, o_na, o_sw, y, bonus, g, p, mods, tile_mod):
    n_tok = x.shape[0]
    tok = lambda i: (i, 0)
    const = lambda i: (0, 0)
    return pl.pallas_call(
        _out_proj_kernel,
        grid=(n_tok // TOK_TILE,),
        in_specs=[
            pl.BlockSpec((TOK_TILE, D_MODEL), tok),
            pl.BlockSpec((TOK_TILE, NA_WIDTH), tok),
            pl.BlockSpec((TOK_TILE, SWA_WIDTH), tok),
            pl.BlockSpec((TOK_TILE, RK_WIDTH), tok),
            pl.BlockSpec((TOK_TILE, RK_WIDTH), tok),
            pl.BlockSpec((TOK_TILE, RK_WIDTH), tok),
            pl.BlockSpec((1, RK_WIDTH), const),
            pl.BlockSpec((1, RK_WIDTH), const),
            pl.BlockSpec((D_MODEL, D_MODEL), const),
            pl.BlockSpec((1, 6, D_MODEL), lambda i: (tile_mod(i), 0, 0)),
            pl.BlockSpec((1, D_MODEL), const),
            pl.BlockSpec((D_MODEL, LANES), const),
            pl.BlockSpec((1, LANES), const),
        ],
        out_specs=[
            pl.BlockSpec((TOK_TILE, D_MODEL), tok),
            pl.BlockSpec((TOK_TILE, D_MODEL), tok),
            pl.BlockSpec((TOK_TILE, N_EXPERTS), tok),
        ],
        out_shape=[
            jax.ShapeDtypeStruct((n_tok, D_MODEL), F32),
            jax.ShapeDtypeStruct((n_tok, D_MODEL), BF16),
            jax.ShapeDtypeStruct((n_tok, N_EXPERTS), F32),
        ],
        compiler_params=_cparams(("parallel",)),
        name="out_proj",
    )(x, o_na, o_sw, y, bonus, g, p["rk_ln_g"], p["rk_ln_b"], p["w_out_bf16"], mods, p["norm2_g"],
      p["router_w_pad"], p["router_b_pad"])


def _moe_kernel(meta_ref, x_ref, gate_ref, w1g_ref, w1l_ref, b1g_ref, b1l_ref, w2_ref, b2_ref, o_ref):
    i = pl.program_id(0)
    n_used = meta_ref[meta_ref.shape[0] - 1]

    @pl.when(i < n_used)
    def _():
        x = x_ref[...]
        glu = jnp.minimum(_dot(x, w1g_ref[0]) + b1g_ref[0], SWIGLU_LIMIT)
        lin = jnp.clip(_dot(x, w1l_ref[0]) + b1l_ref[0], -SWIGLU_LIMIT, SWIGLU_LIMIT)
        act = glu * jax.nn.sigmoid(SWIGLU_ALPHA * glu) * (lin + 1.0)
        o_ref[...] = (_dot(act.astype(BF16), w2_ref[0]) + b2_ref[0]) * gate_ref[...]

    @pl.when(i >= n_used)
    def _():
        o_ref[...] = jnp.zeros_like(o_ref)


def _moe_blocks(meta, xb, row_gate, p):
    n_rows = xb.shape[0]
    n_blk = n_rows // MOE_BLK
    d_e = p["w1g"].shape[-1]
    row = lambda i, m: (i, 0)
    exp3 = lambda i, m: (m[i], 0, 0)
    grid_spec = pltpu.PrefetchScalarGridSpec(
        num_scalar_prefetch=1,
        grid=(n_blk,),
        in_specs=[
            pl.BlockSpec((MOE_BLK, D_MODEL), row),
            pl.BlockSpec((MOE_BLK, 1), row),
            pl.BlockSpec((1, D_MODEL, d_e), exp3),
            pl.BlockSpec((1, D_MODEL, d_e), exp3),
            pl.BlockSpec((1, 1, d_e), exp3),
            pl.BlockSpec((1, 1, d_e), exp3),
            pl.BlockSpec((1, d_e, D_MODEL), exp3),
            pl.BlockSpec((1, 1, D_MODEL), exp3),
        ],
        out_specs=pl.BlockSpec((MOE_BLK, D_MODEL), row),
    )
    return pl.pallas_call(
        _moe_kernel,
        grid_spec=grid_spec,
        out_shape=jax.ShapeDtypeStruct((n_rows, D_MODEL), F32),
        compiler_params=_cparams(("arbitrary",)),
        name="moe_blocks",
    )(meta, xb, row_gate, p["w1g"], p["w1l"], p["b1g"], p["b1l"], p["w2"], p["b2"])


def _route(logits):
    n_tok = logits.shape[0]
    top_v, top_i = lax.top_k(logits, TOP_K)
    gates = jax.nn.softmax(top_v, axis=-1).reshape(-1)
    e_flat = top_i.reshape(-1).astype(jnp.int32)
    n_rows = n_tok * TOP_K
    onehot = (e_flat[:, None] == jnp.arange(N_EXPERTS, dtype=jnp.int32)[None, :]).astype(jnp.int32)
    csum = jnp.cumsum(onehot, axis=0)
    rank = jnp.take_along_axis(csum, e_flat[:, None], axis=1)[:, 0] - 1
    counts = csum[-1]
    pcounts = (counts + MOE_BLK - 1) // MOE_BLK * MOE_BLK
    pends = jnp.cumsum(pcounts)
    pstarts = pends - pcounts
    dest = pstarts[e_flat] + rank
    n_blk = n_rows // MOE_BLK + N_EXPERTS
    row_tok = jnp.zeros((n_blk * MOE_BLK,), jnp.int32).at[dest].set(
        jnp.arange(n_rows, dtype=jnp.int32) // TOP_K)
    row_gate = jnp.zeros((n_blk * MOE_BLK,), F32).at[dest].set(gates)
    blk_exp = jnp.minimum(
        jnp.searchsorted(pends, jnp.arange(n_blk, dtype=jnp.int32) * MOE_BLK, side="right"),
        N_EXPERTS - 1).astype(jnp.int32)
    meta = jnp.concatenate([blk_exp, (pends[-1:] // MOE_BLK).astype(jnp.int32)])
    return meta, row_tok, row_gate.reshape(-1, 1), dest


def _combine_kernel(x_ref, yg_ref, mod_ref, o_ref):
    acc = yg_ref[:, 0:D_MODEL]
    for j in range(1, TOP_K):
        acc = acc + yg_ref[:, j * D_MODEL:(j + 1) * D_MODEL]
    o_ref[...] = x_ref[...] + mod_ref[0, 5:6, :] * acc


def _combine(x1, yg, mods, tile_mod):
    n_tok = x1.shape[0]
    return pl.pallas_call(
        _combine_kernel,
        grid=(n_tok // TOK_TILE,),
        in_specs=[
            pl.BlockSpec((TOK_TILE, D_MODEL), lambda i: (i, 0)),
            pl.BlockSpec((TOK_TILE, TOP_K * D_MODEL), lambda i: (i, 0)),
            pl.BlockSpec((1, 6, D_MODEL), lambda i: (tile_mod(i), 0, 0)),
        ],
        out_specs=pl.BlockSpec((TOK_TILE, D_MODEL), lambda i: (i, 0)),
        out_shape=jax.ShapeDtypeStruct((n_tok, D_MODEL), F32),
        compiler_params=_cparams(("parallel",)),
        name="moe_combine",
    )(x1, yg, mods)


def kernel(x_prompt, x_sample, c, cache_na_k, cache_na_v, cache_swa_k, cache_swa_v, state_rwkv, c_ctx, w_ada, b_ada, norm1_g, norm2_g, w_in, w_out, na_q_norm, na_k_norm, na_rpb, swa_q_norm, swa_k_norm, swa_sink, rk_conv, rk_w0, rk_w2, rk_a0, rk_a2, rk_g2, rk_k_k, rk_k_a, rk_r_k, rk_ln_g, rk_ln_b, moe_router_w, moe_router_b, moe_w1, moe_b1, moe_w2, moe_b2):
    bc, tc, _ = x_prompt.shape
    bl, tl, _ = x_sample.shape
    depth = w_in.shape[0]
    n_ctx = bc * tc
    n_lat = bl * tl
    assert tc == TOK_TILE and tl % TOK_TILE == 0
    n_ctx_tiles = n_ctx // TOK_TILE
    tiles_per_seq = tl // TOK_TILE
    past = cache_na_k.shape[2]

    def tile_mod(i):
        return jnp.where(i < n_ctx_tiles, 0, 1 + (i - n_ctx_tiles) // tiles_per_seq)

    def tile_rope(i):
        return jnp.where(i < n_ctx_tiles, tiles_per_seq, (i - n_ctx_tiles) % tiles_per_seq)

    x = jnp.concatenate([x_prompt.reshape(n_ctx, D_MODEL), x_sample.reshape(n_lat, D_MODEL)], axis=0)

    n_mod = 1 + bl
    mod_rows = -(-n_mod // 8) * 8
    cvecs = jnp.concatenate([c_ctx[None, :], c, jnp.zeros((mod_rows - n_mod, D_MODEL), F32)], axis=0)
    mods_all = _ada_mod(cvecs, w_ada, b_ada).reshape(depth, mod_rows, 6, D_MODEL)
    cos_tab, sin_tab = _rope_tables(tl)
    tile2 = lambda g: jnp.concatenate([g, g])[None, :]
    pad_lanes = lambda z: jnp.pad(z, ((0, 0), (0, LANES - z.shape[1])))
    zeros_lora = jnp.zeros((2, RK_DECAY_LORA, RK_WIDTH), F32)

    na_k_l, na_v_l, sw_k_l, sw_v_l, st_l = [], [], [], [], []
    for l in range(depth):
        mods = mods_all[l]
        qk_gains = jnp.concatenate(
            [tile2(na_q_norm[l]), tile2(na_k_norm[l]), tile2(swa_q_norm[l]), tile2(swa_k_norm[l])], axis=0)
        p = {
            "rk_conv": rk_conv[l], "rk_w0": rk_w0[l], "rk_a0": rk_a0[l], "rk_g2": rk_g2[l],
            "rk_w2_pad": jnp.concatenate([rk_w2[l], zeros_lora], axis=1),
            "rk_a2_pad": jnp.concatenate([zeros_lora, rk_a2[l]], axis=1),
            "rk_k_k": rk_k_k[l][None, :], "rk_k_a": rk_k_a[l][None, :],
            "rk_r_k": rk_r_k[l].reshape(1, RK_WIDTH),
            "rk_ln_g": rk_ln_g[l][None, :], "rk_ln_b": rk_ln_b[l][None, :],
            "w_out_bf16": w_out[l].astype(BF16), "norm2_g": norm2_g[l][None, :],
            "router_w_pad": pad_lanes(moe_router_w[l]), "router_b_pad": pad_lanes(moe_router_b[l][None, :]),
            "w1g": moe_w1[l][:, :, 0::2].astype(BF16), "w1l": moe_w1[l][:, :, 1::2].astype(BF16),
            "b1g": moe_b1[l][:, None, 0::2], "b1l": moe_b1[l][:, None, 1::2],
            "w2": moe_w2[l].astype(BF16), "b2": moe_b2[l][:, None, :],
        }

        att, u = _in_proj(x, norm1_g[l][None, :], mods, w_in[l].astype(BF16), qk_gains, cos_tab, sin_tab,
                          tile_mod, tile_rope)
        att_c = att[:n_ctx].reshape(bc, tc, ATT_COLS)
        att_l = att[n_ctx:].reshape(bl, tl, ATT_COLS)
        na_k_l.append(att_c[:, :, NA_WIDTH:2 * NA_WIDTH].reshape(bc, tc, NA_HEADS, HEAD_DIM))
        na_v_l.append(att_c[:, :, 2 * NA_WIDTH:NA_COLS].reshape(bc, tc, NA_HEADS, HEAD_DIM))
        sw_k_l.append(att_c[:, :, NA_COLS + SWA_WIDTH:NA_COLS + SWA_WIDTH + SWA_KV_WIDTH]
                      .reshape(bc, tc, SWA_KV_HEADS, HEAD_DIM))
        sw_v_l.append(att_c[:, :, NA_COLS + SWA_WIDTH + SWA_KV_WIDTH:].reshape(bc, tc, SWA_KV_HEADS, HEAD_DIM))

        sink = swa_sink[l]
        o_na = jnp.concatenate([
            _ctx_attn(att_c, sink, gqa=False).reshape(n_ctx, NA_WIDTH),
            _na_latent(att_l, cache_na_k[:, l].reshape(bl, past, NA_WIDTH),
                       cache_na_v[:, l].reshape(bl, past, NA_WIDTH),
                       _na_bias_tables(na_rpb[l])).reshape(n_lat, NA_WIDTH)], axis=0)
        o_sw = jnp.concatenate([
            _ctx_attn(att_c, sink, gqa=True).reshape(n_ctx, SWA_WIDTH),
            _swa_latent(att_l, cache_swa_k[:, l].reshape(bl, past, SWA_KV_WIDTH),
                        cache_swa_v[:, l].reshape(bl, past, SWA_KV_WIDTH), sink).reshape(n_lat, SWA_WIDTH)],
            axis=0)

        r, v, a, w, kd, b, g, bonus = _rk_prep(u, p, n_ctx_tiles, tiles_per_seq)
        prep = (r, v, a, w, kd, b)
        y_c, s_fin = _rwkv_scan_group(prep, 0, bc, tc, None)
        y_l, _ = _rwkv_scan_group(prep, n_ctx, bl, tl, jnp.moveaxis(state_rwkv[:, l], 1, 0))
        st_l.append(jnp.moveaxis(s_fin, 0, 1))
        y = jnp.concatenate([y_c, y_l], axis=0)

        x1, h2, logits = _out_proj(x, o_na, o_sw, y, bonus, g, p, mods, tile_mod)
        meta, row_tok, row_gate, dest = _route(logits)
        yb = _moe_blocks(meta, h2[row_tok], row_gate, p)
        x = _combine(x1, yb[dest].reshape(n_ctx + n_lat, TOP_K * D_MODEL), mods, tile_mod)

    y_p = x[:n_ctx].reshape(bc, tc, D_MODEL)
    y_s = x[n_ctx:].reshape(bl, tl, D_MODEL)
    return (y_p, y_s, jnp.stack(na_k_l, axis=1), jnp.stack(na_v_l, axis=1), jnp.stack(sw_k_l, axis=1),
            jnp.stack(sw_v_l, axis=1), jnp.stack(st_l, axis=1))
```

```python
import functools

import jax
import jax.numpy as jnp
from jax import lax
from jax.experimental import pallas as pl
from jax.experimental.pallas import tpu as pltpu

F32 = jnp.float32
BF16 = jnp.bfloat16

D_MODEL = 1024
HEAD_DIM = 64
LANES = 128
GRID_W = 64
NA_HEADS = 6
SWA_HEADS = 4
SWA_KV_HEADS = 2
RK_HEADS = 6
NA_WIDTH = NA_HEADS * HEAD_DIM
SWA_WIDTH = SWA_HEADS * HEAD_DIM
SWA_KV_WIDTH = SWA_KV_HEADS * HEAD_DIM
RK_WIDTH = RK_HEADS * HEAD_DIM
RK_DECAY_LORA = 64
RK_A_LORA = 64
RK_GATE_LORA = 128
RK_COLS = 3 * RK_WIDTH + RK_DECAY_LORA + RK_A_LORA + RK_GATE_LORA
NA_COLS = 3 * NA_WIDTH
SWA_COLS = SWA_WIDTH + 2 * SWA_KV_WIDTH
ATT_COLS = NA_COLS + SWA_COLS
IN_COLS = ATT_COLS + RK_COLS
NA_WIN_R = 8
NA_WIN_C = 16
SWA_WIN = 128
ROPE_THETA = 10000.0
ATTN_SCALE = HEAD_DIM ** -0.5
N_EXPERTS = 32
TOP_K = 4
SWIGLU_LIMIT = 7.0
SWIGLU_ALPHA = 1.702
MOE_BLK = 256
RMS_EPS = 1e-6
GN_EPS = 64e-5
NEG_BIG = -1e30

TOK_TILE = 256
VMEM_LIMIT = 48 * 1024 * 1024


def _cparams(sem):
    return pltpu.CompilerParams(dimension_semantics=sem, vmem_limit_bytes=VMEM_LIMIT)


def _dot(a, b):
    return jnp.dot(a, b, preferred_element_type=F32)


def _dot_nt(a, b):
    return lax.dot_general(a, b, (((1,), (1,)), ((), ())), preferred_element_type=F32)


def _split_bf16(x):
    hi = x.astype(BF16)
    lo = (x - hi.astype(F32)).astype(BF16)
    return hi, lo


def _dot3(a, b):
    ah, al = _split_bf16(a)
    bh, bl = _split_bf16(b)
    return _dot(ah, bh) + (_dot(ah, bl) + _dot(al, bh))


def _lane_lo(shape):
    return lax.broadcasted_iota(jnp.int32, shape, len(shape) - 1) < HEAD_DIM


def _pair_sum(x):
    lo = _lane_lo(x.shape)
    s_lo = jnp.sum(jnp.where(lo, x, 0.0), axis=-1, keepdims=True)
    s_hi = jnp.sum(jnp.where(lo, 0.0, x), axis=-1, keepdims=True)
    return jnp.where(lo, s_lo, s_hi)


def _stack_heads(q):
    lo = _lane_lo(q.shape)
    return jnp.concatenate([jnp.where(lo, q, 0.0), jnp.where(lo, 0.0, q)], axis=0)


def _unstack_heads(o2):
    n = o2.shape[0] // 2
    return jnp.where(_lane_lo((n, LANES)), o2[:n], o2[n:])


def _dup_head(x, j):
    keep = _lane_lo(x.shape) == (j == 0)
    return jnp.where(keep, x, pltpu.roll(x, HEAD_DIM, 1))


def _ada_kernel(c_ref, w_ref, b_ref, o_ref):
    cv = c_ref[...]
    s = cv * jax.nn.sigmoid(cv)
    o_ref[0] = _dot3(s, w_ref[0]) + b_ref[0]


def _ada_mod(cvecs, w_ada, b_ada):
    depth, _, n_out = w_ada.shape
    rows = cvecs.shape[0]
    tn = 1024
    return pl.pallas_call(
        _ada_kernel,
        grid=(depth, n_out // tn),
        in_specs=[
            pl.BlockSpec((rows, D_MODEL), lambda l, j: (0, 0)),
            pl.BlockSpec((1, D_MODEL, tn), lambda l, j: (l, 0, j)),
            pl.BlockSpec((1, 1, tn), lambda l, j: (l, 0, j)),
        ],
        out_specs=pl.BlockSpec((1, rows, tn), lambda l, j: (l, 0, j)),
        out_shape=jax.ShapeDtypeStruct((depth, rows, n_out), F32),
        compiler_params=_cparams(("parallel", "parallel")),
        name="ada_mod",
    )(cvecs, w_ada, b_ada.reshape(depth, 1, n_out))


NA_QK_BLOCKS = 2 * NA_WIDTH // LANES
SWA_Q_BLOCK0 = NA_COLS // LANES
SWA_QK_BLOCKS = (SWA_WIDTH + SWA_KV_WIDTH) // LANES


def _in_proj_kernel(x_ref, g_ref, mod_ref, w_ref, qkg_ref, cos_ref, sin_ref, att_ref, u_ref):
    x = x_ref[...]
    y = x * lax.rsqrt(jnp.mean(x * x, axis=-1, keepdims=True) + RMS_EPS)
    h = (y * g_ref[...]) * (1.0 + mod_ref[0, 1:2, :]) + mod_ref[0, 0:1, :]
    proj = _dot(h.astype(BF16), w_ref[...])
    u_ref[...] = proj[:, ATT_COLS:]

    def qk_norm(blk, gain):
        ms = _pair_sum(blk * blk) * (1.0 / HEAD_DIM)
        return blk * lax.rsqrt(ms + RMS_EPS) * gain

    lane = lax.broadcasted_iota(jnp.int32, (x.shape[0], LANES), 1)
    first = (lane % (HEAD_DIM // 2)) < (HEAD_DIM // 4)
    for cb in range(ATT_COLS // LANES):
        blk = proj[:, cb * LANES:(cb + 1) * LANES]
        if cb < NA_QK_BLOCKS:
            gi = 0 if cb < NA_QK_BLOCKS // 2 else 1
            blk = qk_norm(blk, qkg_ref[gi:gi + 1, :])
        elif SWA_Q_BLOCK0 <= cb < SWA_Q_BLOCK0 + SWA_QK_BLOCKS:
            gi = 2 if cb < SWA_Q_BLOCK0 + SWA_WIDTH // LANES else 3
            blk = qk_norm(blk, qkg_ref[gi:gi + 1, :])
            partner = jnp.where(first, pltpu.roll(blk, LANES - HEAD_DIM // 4, 1),
                                pltpu.roll(blk, HEAD_DIM // 4, 1))
            blk = blk * cos_ref[...] + partner * sin_ref[...]
        att_ref[:, cb * LANES:(cb + 1) * LANES] = blk


def _in_proj(x, norm_g, mods, w_in_bf16, qk_gains, cos_tab, sin_tab, tile_mod, tile_rope):
    n_tok = x.shape[0]
    return pl.pallas_call(
        _in_proj_kernel,
        grid=(n_tok // TOK_TILE,),
        in_specs=[
            pl.BlockSpec((TOK_TILE, D_MODEL), lambda i: (i, 0)),
            pl.BlockSpec((1, D_MODEL), lambda i: (0, 0)),
            pl.BlockSpec((1, 6, D_MODEL), lambda i: (tile_mod(i), 0, 0)),
            pl.BlockSpec((D_MODEL, IN_COLS), lambda i: (0, 0)),
            pl.BlockSpec((4, LANES), lambda i: (0, 0)),
            pl.BlockSpec((TOK_TILE, LANES), lambda i: (tile_rope(i), 0)),
            pl.BlockSpec((TOK_TILE, LANES), lambda i: (tile_rope(i), 0)),
        ],
        out_specs=[
            pl.BlockSpec((TOK_TILE, ATT_COLS), lambda i: (i, 0)),
            pl.BlockSpec((TOK_TILE, RK_COLS), lambda i: (i, 0)),
        ],
        out_shape=[
            jax.ShapeDtypeStruct((n_tok, ATT_COLS), F32),
            jax.ShapeDtypeStruct((n_tok, RK_COLS), F32),
        ],
        compiler_params=_cparams(("parallel",)),
        name="in_proj",
    )(x, norm_g, mods, w_in_bf16, qk_gains, cos_tab, sin_tab)


def _rope_tables(n_lat):
    nf = HEAD_DIM // 4
    t = jnp.arange(n_lat)
    lane = jnp.arange(LANES)
    d = lane % HEAD_DIM
    inv = ROPE_THETA ** (-(d % nf).astype(F32) / nf)
    pos = jnp.where((d // (2 * nf))[None, :] == 0, (t // GRID_W)[:, None], (t % GRID_W)[:, None]).astype(F32)
    ang = pos * inv[None, :]
    sign = jnp.where((d % (2 * nf)) < nf, -1.0, 1.0).astype(F32)
    cos = jnp.concatenate([jnp.cos(ang), jnp.ones((TOK_TILE, LANES), F32)], 0)
    sin = jnp.concatenate([jnp.sin(ang) * sign[None, :], jnp.zeros((TOK_TILE, LANES), F32)], 0)
    return cos, sin


def _ctx_attn_kernel(sink_ref, q_ref, k_ref, v_ref, o_ref, *, gqa):
    j = pl.program_id(1)
    k = k_ref[0]
    v = v_ref[0]
    if gqa:
        k = _dup_head(k, j)
        v = _dup_head(v, j)
    n = k.shape[0]
    q2 = _stack_heads(q_ref[0]).astype(BF16)
    s = _dot_nt(q2, k.astype(BF16)) * ATTN_SCALE
    m = jnp.max(s, axis=-1, keepdims=True)
    if gqa:
        row = lax.broadcasted_iota(jnp.int32, (2 * n, 1), 0)
        snk = jnp.where(row < n, sink_ref[2 * j], sink_ref[2 * j + 1])
        m = jnp.maximum(m, snk)
    p = jnp.exp(s - m)
    den = jnp.sum(p, axis=-1, keepdims=True)
    if gqa:
        den = den + jnp.exp(snk - m)
    o2 = _dot(p.astype(BF16), v.astype(BF16)) / den
    o_ref[0] = _unstack_heads(o2)


def _ctx_attn(att, sink, *, gqa):
    b, t, _ = att.shape
    if gqa:
        nq = SWA_WIDTH // LANES
        qb, kb, vb = SWA_Q_BLOCK0, SWA_Q_BLOCK0 + nq, SWA_Q_BLOCK0 + nq + 1
        kmap = lambda bi, j: (bi, 0, kb)
        vmap = lambda bi, j: (bi, 0, vb)
    else:
        nq = NA_WIDTH // LANES
        qb, kb, vb = 0, nq, 2 * nq
        kmap = lambda bi, j: (bi, 0, kb + j)
        vmap = lambda bi, j: (bi, 0, vb + j)
    return pl.pallas_call(
        functools.partial(_ctx_attn_kernel, gqa=gqa),
        grid=(b, nq),
        in_specs=[
            pl.BlockSpec(memory_space=pltpu.SMEM),
            pl.BlockSpec((1, t, LANES), lambda bi, j: (bi, 0, qb + j)),
            pl.BlockSpec((1, t, LANES), kmap),
            pl.BlockSpec((1, t, LANES), vmap),
        ],
        out_specs=pl.BlockSpec((1, t, LANES), lambda bi, j: (bi, 0, j)),
        out_shape=jax.ShapeDtypeStruct((b, t, nq * LANES), F32),
        compiler_params=_cparams(("parallel", "parallel")),
        name="ctx_attn_swa" if gqa else "ctx_attn_na",
    )(sink, att, att, att)


def _na_lat_kernel(q_ref, k_ref, v_ref, kc_ref, vc_ref, tab_ref, o_ref, kb_ref, vb_ref):
    n = q_ref.shape[1]
    rows = n // GRID_W
    win = NA_WIN_R * GRID_W
    kb_ref[...] = k_ref[0].astype(BF16)
    vb_ref[...] = v_ref[0].astype(BF16)
    kc = kc_ref[0].astype(BF16)
    vc = vc_ref[0].astype(BF16)

    def row_block(i, carry):
        start = jnp.clip(i - NA_WIN_R // 2, 0, rows - NA_WIN_R)
        rb = start - i + (NA_WIN_R - 1)
        q0 = pl.multiple_of(i * GRID_W, GRID_W)
        k0 = pl.multiple_of(start * GRID_W, GRID_W)
        q2 = _stack_heads(q_ref[0, pl.ds(q0, GRID_W), :]).astype(BF16)
        kw = kb_ref[pl.ds(k0, win), :]
        vw = vb_ref[pl.ds(k0, win), :]
        s_loc = _dot_nt(q2, kw) * ATTN_SCALE + tab_ref[0, rb]
        s_ctx = _dot_nt(q2, kc) * ATTN_SCALE
        m = jnp.maximum(jnp.max(s_loc, axis=-1, keepdims=True), jnp.max(s_ctx, axis=-1, keepdims=True))
        p_loc = jnp.exp(s_loc - m)
        p_ctx = jnp.exp(s_ctx - m)
        den = jnp.sum(p_loc, axis=-1, keepdims=True) + jnp.sum(p_ctx, axis=-1, keepdims=True)
        o2 = (_dot(p_loc.astype(BF16), vw) + _dot(p_ctx.astype(BF16), vc)) / den
        o_ref[0, pl.ds(q0, GRID_W), :] = _unstack_heads(o2)
        return carry

    lax.fori_loop(0, rows, row_block, 0)


def _na_bias_tables(rpb):
    col = jnp.arange(GRID_W)
    cstart = jnp.clip(col - NA_WIN_C // 2, 0, GRID_W - NA_WIN_C)
    col_mask = (col[None, :] >= cstart[:, None]) & (col[None, :] < cstart[:, None] + NA_WIN_C)
    col_idx = jnp.clip(col[None, :] - col[:, None] + NA_WIN_C - 1, 0, 2 * NA_WIN_C - 2)
    rpb_cols = jnp.where(col_mask[None, None], rpb[:, :, col_idx], NEG_BIG)
    roff = jnp.arange(NA_WIN_R)[:, None] + jnp.arange(NA_WIN_R)[None, :]
    t = rpb_cols[:, roff]
    t = jnp.transpose(t, (0, 1, 3, 2, 4)).reshape(NA_HEADS // 2, 2, NA_WIN_R, GRID_W, NA_WIN_R * GRID_W)
    return jnp.transpose(t, (0, 2, 1, 3, 4)).reshape(NA_HEADS // 2, NA_WIN_R, 2 * GRID_W, NA_WIN_R * GRID_W)


def _na_latent(att, kc, vc, tab):
    b, n, _ = att.shape
    p = kc.shape[1]
    nq = NA_WIDTH // LANES
    return pl.pallas_call(
        _na_lat_kernel,
        grid=(b, nq),
        in_specs=[
            pl.BlockSpec((1, n, LANES), lambda bi, j: (bi, 0, j)),
            pl.BlockSpec((1, n, LANES), lambda bi, j: (bi, 0, nq + j)),
            pl.BlockSpec((1, n, LANES), lambda bi, j: (bi, 0, 2 * nq + j)),
            pl.BlockSpec((1, p, LANES), lambda bi, j: (bi, 0, j)),
            pl.BlockSpec((1, p, LANES), lambda bi, j: (bi, 0, j)),
            pl.BlockSpec((1, NA_WIN_R, 2 * GRID_W, NA_WIN_R * GRID_W), lambda bi, j: (j, 0, 0, 0)),
        ],
        out_specs=pl.BlockSpec((1, n, LANES), lambda bi, j: (bi, 0, j)),
        out_shape=jax.ShapeDtypeStruct((b, n, NA_WIDTH), F32),
        scratch_shapes=[pltpu.VMEM((n, LANES), BF16), pltpu.VMEM((n, LANES), BF16)],
        compiler_params=_cparams(("parallel", "parallel")),
        name="na_latent",
    )(att, att, att, kc, vc, tab)


def _swa_lat_kernel(sink_ref, q_ref, k_ref, v_ref, kc_ref, vc_ref, o_ref, kb_ref, vb_ref):
    j = pl.program_id(1)
    n = q_ref.shape[1]
    blk = SWA_WIN
    span = 3 * blk
    kb_ref[...] = _dup_head(k_ref[0], j).astype(BF16)
    vb_ref[...] = _dup_head(v_ref[0], j).astype(BF16)
    kc = _dup_head(kc_ref[0], j).astype(BF16)
    vc = _dup_head(vc_ref[0], j).astype(BF16)
    row = lax.broadcasted_iota(jnp.int32, (2 * blk, 1), 0)
    snk = jnp.where(row < blk, sink_ref[2 * j], sink_ref[2 * j + 1])
    qoff = lax.broadcasted_iota(jnp.int32, (2 * blk, span), 0) % blk
    koff = lax.broadcasted_iota(jnp.int32, (2 * blk, span), 1)

    def q_block(qi, carry):
        q0 = pl.multiple_of(qi * blk, blk)
        w0 = pl.multiple_of(jnp.clip(q0 - blk, 0, n - span), blk)
        q2 = _stack_heads(q_ref[0, pl.ds(q0, blk), :]).astype(BF16)
        kw = kb_ref[pl.ds(w0, span), :]
        vw = vb_ref[pl.ds(w0, span), :]
        valid = jnp.abs((q0 + qoff) - (w0 + koff)) <= SWA_WIN
        s_loc = jnp.where(valid, _dot_nt(q2, kw) * ATTN_SCALE, NEG_BIG)
        s_ctx = _dot_nt(q2, kc) * ATTN_SCALE
        m = jnp.maximum(jnp.max(s_loc, axis=-1, keepdims=True), jnp.max(s_ctx, axis=-1, keepdims=True))
        m = jnp.maximum(m, snk)
        p_loc = jnp.exp(s_loc - m)
        p_ctx = jnp.exp(s_ctx - m)
        den = (jnp.sum(p_loc, axis=-1, keepdims=True) + jnp.sum(p_ctx, axis=-1, keepdims=True)
               + jnp.exp(snk - m))
        o2 = (_dot(p_loc.astype(BF16), vw) + _dot(p_ctx.astype(BF16), vc)) / den
        o_ref[0, pl.ds(q0, blk), :] = _unstack_heads(o2)
        return carry

    lax.fori_loop(0, n // blk, q_block, 0)


def _swa_latent(att, kc, vc, sink):
    b, n, _ = att.shape
    p = kc.shape[1]
    nq = SWA_WIDTH // LANES
    qb, kb, vb = SWA_Q_BLOCK0, SWA_Q_BLOCK0 + nq, SWA_Q_BLOCK0 + nq + 1
    return pl.pallas_call(
        _swa_lat_kernel,
        grid=(b, nq),
        in_specs=[
            pl.BlockSpec(memory_space=pltpu.SMEM),
            pl.BlockSpec((1, n, LANES), lambda bi, j: (bi, 0, qb + j)),
            pl.BlockSpec((1, n, LANES), lambda bi, j: (bi, 0, kb)),
            pl.BlockSpec((1, n, LANES), lambda bi, j: (bi, 0, vb)),
            pl.BlockSpec((1, p, LANES), lambda bi, j: (bi, 0, 0)),
            pl.BlockSpec((1, p, LANES), lambda bi, j: (bi, 0, 0)),
        ],
        out_specs=pl.BlockSpec((1, n, LANES), lambda bi, j: (bi, 0, j)),
        out_shape=jax.ShapeDtypeStruct((b, n, SWA_WIDTH), F32),
        scratch_shapes=[pltpu.VMEM((n, LANES), BF16), pltpu.VMEM((n, LANES), BF16)],
        compiler_params=_cparams(("parallel", "parallel")),
        name="swa_latent",
    )(sink, att, att, att, kc, vc)


RK_NB = RK_WIDTH // LANES
LORA_BLOCK = 3 * RK_WIDTH // LANES
GATE_BLOCK = LORA_BLOCK + 1


def _softplus(x):
    return jnp.maximum(x, 0.0) + jnp.log(1.0 + jnp.exp(-jnp.abs(x)))


def _rk_prep_kernel(u_ref, up_ref, un_ref, cw_ref, w0_ref, w2_ref, a0_ref, a2_ref, g2_ref, kk_ref, ka_ref,
                    rk_ref, r_ref, v_ref, a_ref, w_ref, kd_ref, b_ref, g_ref, bonus_ref,
                    *, n_ctx_tiles, tiles_per_seq):
    i = pl.program_id(0)
    li = i - n_ctx_tiles
    is_lat = i >= n_ctx_tiles
    has_prev = jnp.logical_and(is_lat, li % tiles_per_seq != 0)
    has_next = jnp.logical_and(is_lat, li % tiles_per_seq != tiles_per_seq - 1)
    u = u_ref[...]
    tm = u.shape[0]
    prev_row = jnp.where(has_prev, up_ref[7:8, :], 0.0)
    next_row = jnp.where(has_next, un_ref[0:1, :], 0.0)
    row = lax.broadcasted_iota(jnp.int32, u.shape, 0)
    um = jnp.where(row == 0, prev_row, pltpu.roll(u, 1, 0))
    up = jnp.where(row == tm - 1, next_row, pltpu.roll(u, tm - 1, 0))
    u = um * cw_ref[0:1, :] + u * cw_ref[1:2, :] + up * cw_ref[2:3, :]

    r = u[:, 0:RK_WIDTH]
    k = u[:, RK_WIDTH:2 * RK_WIDTH]
    v = u[:, 2 * RK_WIDTH:3 * RK_WIDTH]
    lora = u[:, LORA_BLOCK * LANES:(LORA_BLOCK + 1) * LANES]
    gl = u[:, GATE_BLOCK * LANES:(GATE_BLOCK + 1) * LANES]
    r_ref[...] = r
    v_ref[...] = v
    g_ref[...] = _dot3(jax.nn.sigmoid(gl), g2_ref[...])

    kn = k * kk_ref[...]
    kk = jnp.concatenate(
        [kn[:, c * LANES:(c + 1) * LANES]
         * lax.rsqrt(jnp.maximum(_pair_sum(jnp.square(kn[:, c * LANES:(c + 1) * LANES])), 1e-24))
         for c in range(RK_NB)], axis=1)
    a_ref[...] = -kk

    lora_t = jnp.tanh(lora)
    kd_sum = None
    for d in range(2):
        w = -_softplus(-(w0_ref[d:d + 1, :] + _dot3(lora_t, w2_ref[d]))) - 0.5
        w_ref[d] = jnp.exp(-jnp.exp(w))
        a = jax.nn.sigmoid(a0_ref[d:d + 1, :] + _dot3(lora, a2_ref[d]))
        kd = k * (1.0 + (a - 1.0) * ka_ref[...])
        kd_ref[d] = kd
        b_ref[d] = kk * a
        kd_sum = kd if kd_sum is None else kd_sum + kd

    t = r * kd_sum * rk_ref[...]
    bonus_ref[...] = jnp.concatenate(
        [_pair_sum(t[:, c * LANES:(c + 1) * LANES]) for c in range(RK_NB)], axis=1) * v


def _rk_prep(u, p, n_ctx_tiles, tiles_per_seq):
    n_tok = u.shape[0]
    n_tiles = n_tok // TOK_TILE
    sub = TOK_TILE // 8
    last8 = n_tok // 8 - 1
    tok = lambda i: (i, 0)
    const2 = lambda i: (0, 0)
    const3 = lambda i: (0, 0, 0)
    one = jax.ShapeDtypeStruct((n_tok, RK_WIDTH), F32)
    two = jax.ShapeDtypeStruct((2, n_tok, RK_WIDTH), F32)
    tok_spec = pl.BlockSpec((TOK_TILE, RK_WIDTH), tok)
    dir_spec = pl.BlockSpec((2, TOK_TILE, RK_WIDTH), lambda i: (0, i, 0))
    return pl.pallas_call(
        functools.partial(_rk_prep_kernel, n_ctx_tiles=n_ctx_tiles, tiles_per_seq=tiles_per_seq),
        grid=(n_tiles,),
        in_specs=[
            pl.BlockSpec((TOK_TILE, RK_COLS), tok),
            pl.BlockSpec((8, RK_COLS), lambda i: (jnp.maximum(i * sub - 1, 0), 0)),
            pl.BlockSpec((8, RK_COLS), lambda i: (jnp.minimum((i + 1) * sub, last8), 0)),
            pl.BlockSpec((3, RK_COLS), const2),
            pl.BlockSpec((2, RK_WIDTH), const2),
            pl.BlockSpec((2, LANES, RK_WIDTH), const3),
            pl.BlockSpec((2, RK_WIDTH), const2),
            pl.BlockSpec((2, LANES, RK_WIDTH), const3),
            pl.BlockSpec((RK_GATE_LORA, RK_WIDTH), const2),
            pl.BlockSpec((1, RK_WIDTH), const2),
            pl.BlockSpec((1, RK_WIDTH), const2),
            pl.BlockSpec((1, RK_WIDTH), const2),
        ],
        out_specs=[tok_spec, tok_spec, tok_spec, dir_spec, dir_spec, dir_spec, tok_spec, tok_spec],
        out_shape=[one, one, one, two, two, two, one, one],
        compiler_params=_cparams(("parallel",)),
        name="rk_prep",
    )(u, u, u, p["rk_conv"], p["rk_w0"], p["rk_w2_pad"], p["rk_a0"], p["rk_a2_pad"], p["rk_g2"],
      p["rk_k_k"], p["rk_k_a"], p["rk_r_k"])


SCAN_CHUNK = 16
SCAN_UNROLL = 8


def _rk_scan_kernel(r_ref, w_ref, k_ref, v_ref, a_ref, b_ref, s0_ref, y_ref, s_ref):
    @pl.when(pl.program_id(1) == 0)
    def _():
        s_ref[...] = s0_ref[...]

    nk = HEAD_DIM
    shape = (HEAD_DIM, LANES)

    def step(t, carry):
        def sa_body(kq, acc):
            for uu in range(SCAN_UNROLL):
                kx = kq * SCAN_UNROLL + uu
                acc = acc + s_ref[kx] * jnp.broadcast_to(a_ref[t, pl.ds(kx, 1), :], shape)
            return acc

        sa = lax.fori_loop(0, nk // SCAN_UNROLL, sa_body, jnp.zeros(shape, F32))
        vt = v_ref[t]

        def upd_body(kq, acc):
            for uu in range(SCAN_UNROLL):
                kx = kq * SCAN_UNROLL + uu
                row = lambda ref: jnp.broadcast_to(ref[t, pl.ds(kx, 1), :], shape)
                s_new = s_ref[kx] * row(w_ref) + sa * row(b_ref) + vt * row(k_ref)
                s_ref[kx] = s_new
                acc = acc + s_new * row(r_ref)
            return acc

        y_ref[t] = lax.fori_loop(0, nk // SCAN_UNROLL, upd_body, jnp.zeros(shape, F32))
        return carry

    lax.fori_loop(0, r_ref.shape[0], step, 0)


def _rk_scan(r, w, k, v, a, b, s0):
    t, _, nl = r.shape
    seq = pl.BlockSpec((SCAN_CHUNK, HEAD_DIM, LANES), lambda j, c: (c, 0, j))
    st = pl.BlockSpec((HEAD_DIM, HEAD_DIM, LANES), lambda j, c: (0, 0, j))
    return pl.pallas_call(
        _rk_scan_kernel,
        grid=(nl // LANES, t // SCAN_CHUNK),
        in_specs=[seq, seq, seq, seq, seq, seq, st],
        out_specs=[seq, st],
        out_shape=[jax.ShapeDtypeStruct((t, HEAD_DIM, nl), F32),
                   jax.ShapeDtypeStruct((HEAD_DIM, HEAD_DIM, nl), F32)],
        compiler_params=_cparams(("parallel", "arbitrary")),
        name="rk_scan",
    )(r, w, k, v, a, b, s0)


def _to_scan_layout(z_fwd, z_bwd, bsz, t, nl):
    z0 = z_fwd.reshape(bsz, t, RK_HEADS, HEAD_DIM)
    z1 = jnp.flip(z_bwd.reshape(bsz, t, RK_HEADS, HEAD_DIM), axis=1)
    z = jnp.stack([z0, z1])
    z = jnp.transpose(z, (2, 4, 0, 1, 3)).reshape(t, HEAD_DIM, 2 * bsz * RK_HEADS)
    return jnp.pad(z, ((0, 0), (0, 0), (0, nl - z.shape[-1])))


def _rwkv_scan_group(prep, lo, bsz, t, s0):
    r, v, a, w, kd, b = prep
    n_inst = 2 * bsz * RK_HEADS
    nl = -(-n_inst // LANES) * LANES
    sl = slice(lo, lo + bsz * t)
    lay = functools.partial(_to_scan_layout, bsz=bsz, t=t, nl=nl)
    if s0 is None:
        s0l = jnp.zeros((HEAD_DIM, HEAD_DIM, nl), F32)
    else:
        s0l = jnp.transpose(s0, (4, 3, 0, 1, 2)).reshape(HEAD_DIM, HEAD_DIM, n_inst)
        s0l = jnp.pad(s0l, ((0, 0), (0, 0), (0, nl - n_inst)))
    y, s_fin = _rk_scan(lay(r[sl], r[sl]), lay(w[0, sl], w[1, sl]), lay(kd[0, sl], kd[1, sl]),
                        lay(v[sl], v[sl]), lay(a[sl], a[sl]), lay(b[0, sl], b[1, sl]), s0l)
    y = y[:, :, :n_inst].reshape(t, HEAD_DIM, 2, bsz, RK_HEADS)
    y = jnp.transpose(y, (2, 3, 0, 4, 1))
    y = (y[0] + jnp.flip(y[1], axis=1)).reshape(bsz * t, RK_WIDTH)
    s_fin = s_fin[:, :, :n_inst].reshape(HEAD_DIM, HEAD_DIM, 2, bsz, RK_HEADS)
    return y, jnp.transpose(s_fin, (2, 3, 4, 1, 0))


def _out_proj_kernel(x_ref, ona_ref, osw_ref, y_ref, bonus_ref, g_ref, lng_ref, lnb_ref, w_ref, mod_ref,
                     n2_ref, rw_ref, rb_ref, x1_ref, h2_ref, lg_ref):
    y = y_ref[...]
    parts = []
    for c in range(RK_NB):
        yc = y[:, c * LANES:(c + 1) * LANES]
        dc = yc - _pair_sum(yc) * (1.0 / HEAD_DIM)
        var = _pair_sum(dc * dc) * (1.0 / HEAD_DIM)
        parts.append(dc * lax.rsqrt(var + GN_EPS))
    yn = jnp.concatenate(parts, axis=1) * lng_ref[...] + lnb_ref[...]
    o_rk = (yn + bonus_ref[...]) * g_ref[...]
    o = (_dot(ona_ref[...].astype(BF16), w_ref[0:NA_WIDTH, :])
         + _dot(osw_ref[...].astype(BF16), w_ref[NA_WIDTH:NA_WIDTH + SWA_WIDTH, :])
         + _dot(o_rk.astype(BF16), w_ref[NA_WIDTH + SWA_WIDTH:, :]))
    x1 = x_ref[...] + mod_ref[0, 2:3, :] * o
    x1_ref[...] = x1
    yn2 = x1 * lax.rsqrt(jnp.mean(x1 * x1, axis=-1, keepdims=True) + RMS_EPS)
    h2 = (yn2 * n2_ref[...]) * (1.0 + mod_ref[0, 4:5, :]) + mod_ref[0, 3:4, :]
    h2_ref[...] = h2.astype(BF16)
    lg_ref[...] = (_dot3(h2, rw_ref[...]) + rb_ref[...])[:, :N_EXPERTS]


def _out_proj(x, o_na, o_sw, y, bonus, g, p, mods, tile_mod):
    n_tok = x.shape[0]
    tok = lambda i: (i, 0)
    const = lambda i: (0, 0)
    return pl.pallas_call(
        _out_proj_kernel,
        grid=(n_tok // TOK_TILE,),
        in_specs=[
            pl.BlockSpec((TOK_TILE, D_MODEL), tok),
            pl.BlockSpec((TOK_TILE, NA_WIDTH), tok),
            pl.BlockSpec((TOK_TILE, SWA_WIDTH), tok),
            pl.BlockSpec((TOK_TILE, RK_WIDTH), tok),
            pl.BlockSpec((TOK_TILE, RK_WIDTH), tok),
            pl.BlockSpec((TOK_TILE, RK_WIDTH), tok),
            pl.BlockSpec((1, RK_WIDTH), const),
            pl.BlockSpec((1, RK_WIDTH), const),
            pl.BlockSpec((D_MODEL, D_MODEL), const),
            pl.BlockSpec((1, 6, D_MODEL), lambda i: (tile_mod(i), 0, 0)),
            pl.BlockSpec((1, D_MODEL), const),
            pl.BlockSpec((D_MODEL, LANES), const),
            pl.BlockSpec((1, LANES), const),
        ],
        out_specs=[
            pl.BlockSpec((TOK_TILE, D_MODEL), tok),
            pl.BlockSpec((TOK_TILE, D_MODEL), tok),
            pl.BlockSpec((TOK_TILE, N_EXPERTS), tok),
        ],
        out_shape=[
            jax.ShapeDtypeStruct((n_tok, D_MODEL), F32),
            jax.ShapeDtypeStruct((n_tok, D_MODEL), BF16),
            jax.ShapeDtypeStruct((n_tok, N_EXPERTS), F32),
        ],
        compiler_params=_cparams(("parallel",)),
        name="out_proj",
    )(x, o_na, o_sw, y, bonus, g, p["rk_ln_g"], p["rk_ln_b"], p["w_out_bf16"], mods, p["norm2_g"],
      p["router_w_pad"], p["router_b_pad"])


W2_STAGE_ROWS = 128
MOE_VMEM_LIMIT = 56 * 1024 * 1024


def _moe_kernel(meta_ref, x_ref, w1_ref, b1_ref, w2_ref, b2_ref, o_ref, w1b_ref, w2e_ref, stage_ref):
    i = pl.program_id(0)
    n_blk = meta_ref.shape[0] - 1
    n_used = meta_ref[n_blk]
    d_e = w2_ref.shape[1]
    new_expert = jnp.logical_or(i == 0, meta_ref[i] != meta_ref[jnp.maximum(i - 1, 0)])

    @pl.when(i == 0)
    def _():
        stage_ref[...] = jnp.zeros_like(stage_ref)

    @pl.when(jnp.logical_and(i < n_used, new_expert))
    def _():
        w1b_ref[...] = w1_ref[0].astype(BF16)
        for c in range(d_e // W2_STAGE_ROWS):
            rows = slice(c * W2_STAGE_ROWS, (c + 1) * W2_STAGE_ROWS)
            for cb in range(D_MODEL // LANES):
                cols = slice(cb * LANES, (cb + 1) * LANES)
                stage_ref[cb, pl.ds(0, W2_STAGE_ROWS, stride=2), :] = w2_ref[0, rows, cols]
                w2e_ref[2 * c * W2_STAGE_ROWS:2 * (c + 1) * W2_STAGE_ROWS, cols] = stage_ref[cb].astype(BF16)

    @pl.when(i < n_used)
    def _():
        uu = _dot(x_ref[...], w1b_ref[...]) + b1_ref[0]
        acts = []
        for c in range(uu.shape[1] // LANES):
            blk = uu[:, c * LANES:(c + 1) * LANES]
            glu = jnp.minimum(blk, SWIGLU_LIMIT)
            lin = jnp.clip(pltpu.roll(blk, LANES - 1, 1), -SWIGLU_LIMIT, SWIGLU_LIMIT)
            acts.append((glu * jax.nn.sigmoid(SWIGLU_ALPHA * glu) * (lin + 1.0)).astype(BF16))
        act = jnp.concatenate(acts, axis=1)
        o_ref[...] = (_dot(act, w2e_ref[...]) + b2_ref[0]).astype(BF16)

    @pl.when(i >= n_used)
    def _():
        o_ref[...] = jnp.zeros_like(o_ref)


def _moe_blocks(meta, xb, w1, b1, w2, b2, layer):
    n_rows = xb.shape[0]
    n_blk = n_rows // MOE_BLK
    d_e = w2.shape[2]
    row = lambda i, m: (i, 0)
    exp3 = lambda i, m: (layer, m[i], 0, 0)
    grid_spec = pltpu.PrefetchScalarGridSpec(
        num_scalar_prefetch=1,
        grid=(n_blk,),
        in_specs=[
            pl.BlockSpec((MOE_BLK, D_MODEL), row),
            pl.BlockSpec((None, 1, D_MODEL, 2 * d_e), exp3),
            pl.BlockSpec((None, 1, 1, 2 * d_e), exp3),
            pl.BlockSpec((None, 1, d_e, D_MODEL), exp3),
            pl.BlockSpec((None, 1, 1, D_MODEL), exp3),
        ],
        out_specs=pl.BlockSpec((MOE_BLK, D_MODEL), row),
        scratch_shapes=[
            pltpu.VMEM((D_MODEL, 2 * d_e), BF16),
            pltpu.VMEM((2 * d_e, D_MODEL), BF16),
            pltpu.VMEM((D_MODEL // LANES, 2 * W2_STAGE_ROWS, LANES), F32),
        ],
    )
    return pl.pallas_call(
        _moe_kernel,
        grid_spec=grid_spec,
        out_shape=jax.ShapeDtypeStruct((n_rows, D_MODEL), BF16),
        compiler_params=pltpu.CompilerParams(dimension_semantics=("arbitrary",),
                                             vmem_limit_bytes=MOE_VMEM_LIMIT),
        name="moe_blocks",
    )(meta, xb, w1, b1, w2, b2)


def _route(logits):
    n_tok = logits.shape[0]
    top_v, top_i = lax.top_k(logits, TOP_K)
    gates = jax.nn.softmax(top_v, axis=-1)
    e_flat = top_i.reshape(-1).astype(jnp.int32)
    n_rows = n_tok * TOP_K
    onehot = (e_flat[:, None] == jnp.arange(N_EXPERTS, dtype=jnp.int32)[None, :]).astype(jnp.int32)
    csum = jnp.cumsum(onehot, axis=0)
    rank = jnp.take_along_axis(csum, e_flat[:, None], axis=1)[:, 0] - 1
    counts = csum[-1]
    starts = jnp.cumsum(counts) - counts
    pcounts = (counts + MOE_BLK - 1) // MOE_BLK * MOE_BLK
    pends = jnp.cumsum(pcounts)
    pstarts = pends - pcounts
    dest = pstarts[e_flat] + rank
    n_blk = n_rows // MOE_BLK + N_EXPERTS
    blk_exp = jnp.minimum(
        jnp.searchsorted(pends, jnp.arange(n_blk, dtype=jnp.int32) * MOE_BLK, side="right"),
        N_EXPERTS - 1).astype(jnp.int32)
    order = jnp.argsort(e_flat)
    pos = jnp.arange(n_blk * MOE_BLK, dtype=jnp.int32)
    e_pos = jnp.repeat(blk_exp, MOE_BLK)
    src = jnp.clip(pos - pstarts[e_pos] + starts[e_pos], 0, n_rows - 1)
    row_tok = order[src].astype(jnp.int32) // TOP_K
    meta = jnp.concatenate([blk_exp, (pends[-1:] // MOE_BLK).astype(jnp.int32)])
    return meta, row_tok, gates, dest.reshape(n_tok, TOP_K).T.reshape(-1)


def _combine_kernel(x_ref, yg_ref, gate_ref, mod_ref, o_ref):
    gate = gate_ref[...]
    acc = gate[:, 0:1] * yg_ref[0].astype(F32)
    for j in range(1, TOP_K):
        acc = acc + gate[:, j:j + 1] * yg_ref[j].astype(F32)
    o_ref[...] = x_ref[...] + mod_ref[0, 5:6, :] * acc


def _combine(x1, yg, gates, mods, tile_mod):
    n_tok = x1.shape[0]
    return pl.pallas_call(
        _combine_kernel,
        grid=(n_tok // TOK_TILE,),
        in_specs=[
            pl.BlockSpec((TOK_TILE, D_MODEL), lambda i: (i, 0)),
            pl.BlockSpec((TOP_K, TOK_TILE, D_MODEL), lambda i: (0, i, 0)),
            pl.BlockSpec((TOK_TILE, TOP_K), lambda i: (i, 0)),
            pl.BlockSpec((1, 6, D_MODEL), lambda i: (tile_mod(i), 0, 0)),
        ],
        out_specs=pl.BlockSpec((TOK_TILE, D_MODEL), lambda i: (i, 0)),
        out_shape=jax.ShapeDtypeStruct((n_tok, D_MODEL), F32),
        compiler_params=_cparams(("parallel",)),
        name="moe_combine",
    )(x1, yg, gates, mods)


def kernel(x_prompt, x_sample, c, cache_na_k, cache_na_v, cache_swa_k, cache_swa_v, state_rwkv, c_ctx, w_ada, b_ada, norm1_g, norm2_g, w_in, w_out, na_q_norm, na_k_norm, na_rpb, swa_q_norm, swa_k_norm, swa_sink, rk_conv, rk_w0, rk_w2, rk_a0, rk_a2, rk_g2, rk_k_k, rk_k_a, rk_r_k, rk_ln_g, rk_ln_b, moe_router_w, moe_router_b, moe_w1, moe_b1, moe_w2, moe_b2):
    bc, tc, _ = x_prompt.shape
    bl, tl, _ = x_sample.shape
    depth = w_in.shape[0]
    n_ctx = bc * tc
    n_lat = bl * tl
    assert tc == TOK_TILE and tl % TOK_TILE == 0
    n_ctx_tiles = n_ctx // TOK_TILE
    tiles_per_seq = tl // TOK_TILE
    past = cache_na_k.shape[2]

    def tile_mod(i):
        return jnp.where(i < n_ctx_tiles, 0, 1 + (i - n_ctx_tiles) // tiles_per_seq)

    def tile_rope(i):
        return jnp.where(i < n_ctx_tiles, tiles_per_seq, (i - n_ctx_tiles) % tiles_per_seq)

    x = jnp.concatenate([x_prompt.reshape(n_ctx, D_MODEL), x_sample.reshape(n_lat, D_MODEL)], axis=0)

    n_mod = 1 + bl
    mod_rows = -(-n_mod // 8) * 8
    cvecs = jnp.concatenate([c_ctx[None, :], c, jnp.zeros((mod_rows - n_mod, D_MODEL), F32)], axis=0)
    mods_all = _ada_mod(cvecs, w_ada, b_ada).reshape(depth, mod_rows, 6, D_MODEL)
    cos_tab, sin_tab = _rope_tables(tl)
    tile2 = lambda g: jnp.concatenate([g, g])[None, :]
    pad_lanes = lambda z: jnp.pad(z, ((0, 0), (0, LANES - z.shape[1])))
    zeros_lora = jnp.zeros((2, RK_DECAY_LORA, RK_WIDTH), F32)

    na_k_l, na_v_l, sw_k_l, sw_v_l, st_l = [], [], [], [], []
    for l in range(depth):
        mods = mods_all[l]
        qk_gains = jnp.concatenate(
            [tile2(na_q_norm[l]), tile2(na_k_norm[l]), tile2(swa_q_norm[l]), tile2(swa_k_norm[l])], axis=0)
        p = {
            "rk_conv": rk_conv[l], "rk_w0": rk_w0[l], "rk_a0": rk_a0[l], "rk_g2": rk_g2[l],
            "rk_w2_pad": jnp.concatenate([rk_w2[l], zeros_lora], axis=1),
            "rk_a2_pad": jnp.concatenate([zeros_lora, rk_a2[l]], axis=1),
            "rk_k_k": rk_k_k[l][None, :], "rk_k_a": rk_k_a[l][None, :],
            "rk_r_k": rk_r_k[l].reshape(1, RK_WIDTH),
            "rk_ln_g": rk_ln_g[l][None, :], "rk_ln_b": rk_ln_b[l][None, :],
            "w_out_bf16": w_out[l].astype(BF16), "norm2_g": norm2_g[l][None, :],
            "router_w_pad": pad_lanes(moe_router_w[l]), "router_b_pad": pad_lanes(moe_router_b[l][None, :]),
        }

        att, u = _in_proj(x, norm1_g[l][None, :], mods, w_in[l].astype(BF16), qk_gains, cos_tab, sin_tab,
                          tile_mod, tile_rope)
        att_c = att[:n_ctx].reshape(bc, tc, ATT_COLS)
        att_l = att[n_ctx:].reshape(bl, tl, ATT_COLS)
        na_k_l.append(att_c[:, :, NA_WIDTH:2 * NA_WIDTH].reshape(bc, tc, NA_HEADS, HEAD_DIM))
        na_v_l.append(att_c[:, :, 2 * NA_WIDTH:NA_COLS].reshape(bc, tc, NA_HEADS, HEAD_DIM))
        sw_k_l.append(att_c[:, :, NA_COLS + SWA_WIDTH:NA_COLS + SWA_WIDTH + SWA_KV_WIDTH]
                      .reshape(bc, tc, SWA_KV_HEADS, HEAD_DIM))
        sw_v_l.append(att_c[:, :, NA_COLS + SWA_WIDTH + SWA_KV_WIDTH:].reshape(bc, tc, SWA_KV_HEADS, HEAD_DIM))

        sink = swa_sink[l]
        o_na = jnp.concatenate([
            _ctx_attn(att_c, sink, gqa=False).reshape(n_ctx, NA_WIDTH),
            _na_latent(att_l, cache_na_k[:, l].reshape(bl, past, NA_WIDTH),
                       cache_na_v[:, l].reshape(bl, past, NA_WIDTH),
                       _na_bias_tables(na_rpb[l])).reshape(n_lat, NA_WIDTH)], axis=0)
        o_sw = jnp.concatenate([
            _ctx_attn(att_c, sink, gqa=True).reshape(n_ctx, SWA_WIDTH),
            _swa_latent(att_l, cache_swa_k[:, l].reshape(bl, past, SWA_KV_WIDTH),
                        cache_swa_v[:, l].reshape(bl, past, SWA_KV_WIDTH), sink).reshape(n_lat, SWA_WIDTH)],
            axis=0)

        r, v, a, w, kd, b, g, bonus = _rk_prep(u, p, n_ctx_tiles, tiles_per_seq)
        prep = (r, v, a, w, kd, b)
        y_c, s_fin = _rwkv_scan_group(prep, 0, bc, tc, None)
        y_l, _ = _rwkv_scan_group(prep, n_ctx, bl, tl, jnp.moveaxis(state_rwkv[:, l], 1, 0))
        st_l.append(jnp.moveaxis(s_fin, 0, 1))
        y = jnp.concatenate([y_c, y_l], axis=0)

        x1, h2, logits = _out_proj(x, o_na, o_sw, y, bonus, g, p, mods, tile_mod)
        meta, row_tok, gates, dest = _route(logits)
        yb = _moe_blocks(meta, h2[row_tok], moe_w1, moe_b1[:, :, None, :], moe_w2, moe_b2[:, :, None, :], l)
        x = _combine(x1, yb[dest].reshape(TOP_K, n_ctx + n_lat, D_MODEL), gates, mods, tile_mod)

    y_p = x[:n_ctx].reshape(bc, tc, D_MODEL)
    y_s = x[n_ctx:].reshape(bl, tl, D_MODEL)
    return (y_p, y_s, jnp.stack(na_k_l, axis=1), jnp.stack(na_v_l, axis=1), jnp.stack(sw_k_l, axis=1),
            jnp.stack(sw_v_l, axis=1), jnp.stack(st_l, axis=1))
```

```python
import functools

import jax
import jax.numpy as jnp
from jax import lax
from jax.experimental import pallas as pl
from jax.experimental.pallas import tpu as pltpu

F32 = jnp.float32
BF16 = jnp.bfloat16

D_MODEL = 1024
HEAD_DIM = 64
LANES = 128
GRID_W = 64
NA_HEADS = 6
SWA_HEADS = 4
SWA_KV_HEADS = 2
RK_HEADS = 6
NA_WIDTH = NA_HEADS * HEAD_DIM
SWA_WIDTH = SWA_HEADS * HEAD_DIM
SWA_KV_WIDTH = SWA_KV_HEADS * HEAD_DIM
RK_WIDTH = RK_HEADS * HEAD_DIM
RK_DECAY_LORA = 64
RK_A_LORA = 64
RK_GATE_LORA = 128
RK_COLS = 3 * RK_WIDTH + RK_DECAY_LORA + RK_A_LORA + RK_GATE_LORA
NA_COLS = 3 * NA_WIDTH
SWA_COLS = SWA_WIDTH + 2 * SWA_KV_WIDTH
ATT_COLS = NA_COLS + SWA_COLS
IN_COLS = ATT_COLS + RK_COLS
NA_WIN_R = 8
NA_WIN_C = 16
SWA_WIN = 128
ROPE_THETA = 10000.0
ATTN_SCALE = HEAD_DIM ** -0.5
N_EXPERTS = 32
TOP_K = 4
SWIGLU_LIMIT = 7.0
SWIGLU_ALPHA = 1.702
MOE_BLK = 256
RMS_EPS = 1e-6
GN_EPS = 64e-5
NEG_BIG = -1e30

TOK_TILE = 256
VMEM_LIMIT = 48 * 1024 * 1024


def _cparams(sem):
    return pltpu.CompilerParams(dimension_semantics=sem, vmem_limit_bytes=VMEM_LIMIT)


def _dot(a, b):
    return jnp.dot(a, b, preferred_element_type=F32)


def _dot_nt(a, b):
    return lax.dot_general(a, b, (((1,), (1,)), ((), ())), preferred_element_type=F32)


def _split_bf16(x):
    hi = x.astype(BF16)
    lo = (x - hi.astype(F32)).astype(BF16)
    return hi, lo


def _dot3(a, b):
    ah, al = _split_bf16(a)
    bh, bl = _split_bf16(b)
    return _dot(ah, bh) + (_dot(ah, bl) + _dot(al, bh))


def _lane_lo(shape):
    return lax.broadcasted_iota(jnp.int32, shape, len(shape) - 1) < HEAD_DIM


def _pair_sum(x):
    lo = _lane_lo(x.shape)
    s_lo = jnp.sum(jnp.where(lo, x, 0.0), axis=-1, keepdims=True)
    s_hi = jnp.sum(jnp.where(lo, 0.0, x), axis=-1, keepdims=True)
    return jnp.where(lo, s_lo, s_hi)


def _stack_heads(q):
    lo = _lane_lo(q.shape)
    return jnp.concatenate([jnp.where(lo, q, 0.0), jnp.where(lo, 0.0, q)], axis=0)


def _unstack_heads(o2):
    n = o2.shape[0] // 2
    return jnp.where(_lane_lo((n, LANES)), o2[:n], o2[n:])


def _dup_head(x, j):
    keep = _lane_lo(x.shape) == (j == 0)
    return jnp.where(keep, x, pltpu.roll(x, HEAD_DIM, 1))


def _ada_kernel(c_ref, w_ref, b_ref, o_ref):
    cv = c_ref[...]
    s = cv * jax.nn.sigmoid(cv)
    o_ref[0] = _dot3(s, w_ref[0]) + b_ref[0]


def _ada_mod(cvecs, w_ada, b_ada):
    depth, _, n_out = w_ada.shape
    rows = cvecs.shape[0]
    tn = 1024
    return pl.pallas_call(
        _ada_kernel,
        grid=(depth, n_out // tn),
        in_specs=[
            pl.BlockSpec((rows, D_MODEL), lambda l, j: (0, 0)),
            pl.BlockSpec((1, D_MODEL, tn), lambda l, j: (l, 0, j)),
            pl.BlockSpec((1, 1, tn), lambda l, j: (l, 0, j)),
        ],
        out_specs=pl.BlockSpec((1, rows, tn), lambda l, j: (l, 0, j)),
        out_shape=jax.ShapeDtypeStruct((depth, rows, n_out), F32),
        compiler_params=_cparams(("parallel", "parallel")),
        name="ada_mod",
    )(cvecs, w_ada, b_ada.reshape(depth, 1, n_out))


NA_QK_BLOCKS = 2 * NA_WIDTH // LANES
SWA_Q_BLOCK0 = NA_COLS // LANES
SWA_QK_BLOCKS = (SWA_WIDTH + SWA_KV_WIDTH) // LANES


def _in_proj_kernel(x_ref, g_ref, mod_ref, w_ref, qkg_ref, cos_ref, sin_ref, att_ref, u_ref):
    x = x_ref[...]
    y = x * lax.rsqrt(jnp.mean(x * x, axis=-1, keepdims=True) + RMS_EPS)
    h = (y * g_ref[...]) * (1.0 + mod_ref[0, 1:2, :]) + mod_ref[0, 0:1, :]
    proj = _dot(h.astype(BF16), w_ref[...])
    u_ref[...] = proj[:, ATT_COLS:]

    def qk_norm(blk, gain):
        ms = _pair_sum(blk * blk) * (1.0 / HEAD_DIM)
        return blk * lax.rsqrt(ms + RMS_EPS) * gain

    lane = lax.broadcasted_iota(jnp.int32, (x.shape[0], LANES), 1)
    first = (lane % (HEAD_DIM // 2)) < (HEAD_DIM // 4)
    for cb in range(ATT_COLS // LANES):
        blk = proj[:, cb * LANES:(cb + 1) * LANES]
        if cb < NA_QK_BLOCKS:
            gi = 0 if cb < NA_QK_BLOCKS // 2 else 1
            blk = qk_norm(blk, qkg_ref[gi:gi + 1, :])
        elif SWA_Q_BLOCK0 <= cb < SWA_Q_BLOCK0 + SWA_QK_BLOCKS:
            gi = 2 if cb < SWA_Q_BLOCK0 + SWA_WIDTH // LANES else 3
            blk = qk_norm(blk, qkg_ref[gi:gi + 1, :])
            partner = jnp.where(first, pltpu.roll(blk, LANES - HEAD_DIM // 4, 1),
                                pltpu.roll(blk, HEAD_DIM // 4, 1))
            blk = blk * cos_ref[...] + partner * sin_ref[...]
        att_ref[:, cb * LANES:(cb + 1) * LANES] = blk


def _in_proj(x, norm_g, mods, w_in_bf16, qk_gains, cos_tab, sin_tab, tile_mod, tile_rope):
    n_tok = x.shape[0]
    return pl.pallas_call(
        _in_proj_kernel,
        grid=(n_tok // TOK_TILE,),
        in_specs=[
            pl.BlockSpec((TOK_TILE, D_MODEL), lambda i: (i, 0)),
            pl.BlockSpec((1, D_MODEL), lambda i: (0, 0)),
            pl.BlockSpec((1, 6, D_MODEL), lambda i: (tile_mod(i), 0, 0)),
            pl.BlockSpec((D_MODEL, IN_COLS), lambda i: (0, 0)),
            pl.BlockSpec((4, LANES), lambda i: (0, 0)),
            pl.BlockSpec((TOK_TILE, LANES), lambda i: (tile_rope(i), 0)),
            pl.BlockSpec((TOK_TILE, LANES), lambda i: (tile_rope(i), 0)),
        ],
        out_specs=[
            pl.BlockSpec((TOK_TILE, ATT_COLS), lambda i: (i, 0)),
            pl.BlockSpec((TOK_TILE, RK_COLS), lambda i: (i, 0)),
        ],
        out_shape=[
            jax.ShapeDtypeStruct((n_tok, ATT_COLS), F32),
            jax.ShapeDtypeStruct((n_tok, RK_COLS), F32),
        ],
        compiler_params=_cparams(("parallel",)),
        name="in_proj",
    )(x, norm_g, mods, w_in_bf16, qk_gains, cos_tab, sin_tab)


def _rope_tables(n_lat):
    nf = HEAD_DIM // 4
    t = jnp.arange(n_lat)
    lane = jnp.arange(LANES)
    d = lane % HEAD_DIM
    inv = ROPE_THETA ** (-(d % nf).astype(F32) / nf)
    pos = jnp.where((d // (2 * nf))[None, :] == 0, (t // GRID_W)[:, None], (t % GRID_W)[:, None]).astype(F32)
    ang = pos * inv[None, :]
    sign = jnp.where((d % (2 * nf)) < nf, -1.0, 1.0).astype(F32)
    cos = jnp.concatenate([jnp.cos(ang), jnp.ones((TOK_TILE, LANES), F32)], 0)
    sin = jnp.concatenate([jnp.sin(ang) * sign[None, :], jnp.zeros((TOK_TILE, LANES), F32)], 0)
    return cos, sin


def _ctx_attn_kernel(sink_ref, q_ref, k_ref, v_ref, o_ref, *, gqa):
    j = pl.program_id(1)
    k = k_ref[0]
    v = v_ref[0]
    if gqa:
        k = _dup_head(k, j)
        v = _dup_head(v, j)
    n = k.shape[0]
    q2 = _stack_heads(q_ref[0]).astype(BF16)
    s = _dot_nt(q2, k.astype(BF16)) * ATTN_SCALE
    m = jnp.max(s, axis=-1, keepdims=True)
    if gqa:
        row = lax.broadcasted_iota(jnp.int32, (2 * n, 1), 0)
        snk = jnp.where(row < n, sink_ref[2 * j], sink_ref[2 * j + 1])
        m = jnp.maximum(m, snk)
    p = jnp.exp(s - m)
    den = jnp.sum(p, axis=-1, keepdims=True)
    if gqa:
        den = den + jnp.exp(snk - m)
    o2 = _dot(p.astype(BF16), v.astype(BF16)) / den
    o_ref[0] = _unstack_heads(o2)


def _ctx_attn(att, sink, *, gqa):
    b, t, _ = att.shape
    if gqa:
        nq = SWA_WIDTH // LANES
        qb, kb, vb = SWA_Q_BLOCK0, SWA_Q_BLOCK0 + nq, SWA_Q_BLOCK0 + nq + 1
        kmap = lambda bi, j: (bi, 0, kb)
        vmap = lambda bi, j: (bi, 0, vb)
    else:
        nq = NA_WIDTH // LANES
        qb, kb, vb = 0, nq, 2 * nq
        kmap = lambda bi, j: (bi, 0, kb + j)
        vmap = lambda bi, j: (bi, 0, vb + j)
    return pl.pallas_call(
        functools.partial(_ctx_attn_kernel, gqa=gqa),
        grid=(b, nq),
        in_specs=[
            pl.BlockSpec(memory_space=pltpu.SMEM),
            pl.BlockSpec((1, t, LANES), lambda bi, j: (bi, 0, qb + j)),
            pl.BlockSpec((1, t, LANES), kmap),
            pl.BlockSpec((1, t, LANES), vmap),
        ],
        out_specs=pl.BlockSpec((1, t, LANES), lambda bi, j: (bi, 0, j)),
        out_shape=jax.ShapeDtypeStruct((b, t, nq * LANES), F32),
        compiler_params=_cparams(("parallel", "parallel")),
        name="ctx_attn_swa" if gqa else "ctx_attn_na",
    )(sink, att, att, att)


def _na_lat_kernel(q_ref, k_ref, v_ref, kc_ref, vc_ref, tab_ref, o_ref, kb_ref, vb_ref):
    n = q_ref.shape[1]
    rows = n // GRID_W
    win = NA_WIN_R * GRID_W
    kb_ref[...] = k_ref[0].astype(BF16)
    vb_ref[...] = v_ref[0].astype(BF16)
    kc = kc_ref[0].astype(BF16)
    vc = vc_ref[0].astype(BF16)

    def row_block(i, carry):
        start = jnp.clip(i - NA_WIN_R // 2, 0, rows - NA_WIN_R)
        rb = start - i + (NA_WIN_R - 1)
        q0 = pl.multiple_of(i * GRID_W, GRID_W)
        k0 = pl.multiple_of(start * GRID_W, GRID_W)
        q2 = _stack_heads(q_ref[0, pl.ds(q0, GRID_W), :]).astype(BF16)
        kw = kb_ref[pl.ds(k0, win), :]
        vw = vb_ref[pl.ds(k0, win), :]
        s_loc = _dot_nt(q2, kw) * ATTN_SCALE + tab_ref[0, rb]
        s_ctx = _dot_nt(q2, kc) * ATTN_SCALE
        m = jnp.maximum(jnp.max(s_loc, axis=-1, keepdims=True), jnp.max(s_ctx, axis=-1, keepdims=True))
        p_loc = jnp.exp(s_loc - m)
        p_ctx = jnp.exp(s_ctx - m)
        den = jnp.sum(p_loc, axis=-1, keepdims=True) + jnp.sum(p_ctx, axis=-1, keepdims=True)
        o2 = (_dot(p_loc.astype(BF16), vw) + _dot(p_ctx.astype(BF16), vc)) / den
        o_ref[0, pl.ds(q0, GRID_W), :] = _unstack_heads(o2)
        return carry

    lax.fori_loop(0, rows, row_block, 0)


def _na_bias_tables(rpb):
    col = jnp.arange(GRID_W)
    cstart = jnp.clip(col - NA_WIN_C // 2, 0, GRID_W - NA_WIN_C)
    col_mask = (col[None, :] >= cstart[:, None]) & (col[None, :] < cstart[:, None] + NA_WIN_C)
    col_idx = jnp.clip(col[None, :] - col[:, None] + NA_WIN_C - 1, 0, 2 * NA_WIN_C - 2)
    rpb_cols = jnp.where(col_mask[None, None], rpb[:, :, col_idx], NEG_BIG)
    roff = jnp.arange(NA_WIN_R)[:, None] + jnp.arange(NA_WIN_R)[None, :]
    t = rpb_cols[:, roff]
    t = jnp.transpose(t, (0, 1, 3, 2, 4)).reshape(NA_HEADS // 2, 2, NA_WIN_R, GRID_W, NA_WIN_R * GRID_W)
    return jnp.transpose(t, (0, 2, 1, 3, 4)).reshape(NA_HEADS // 2, NA_WIN_R, 2 * GRID_W, NA_WIN_R * GRID_W)


def _na_latent(att, kc, vc, tab):
    b, n, _ = att.shape
    p = kc.shape[1]
    nq = NA_WIDTH // LANES
    return pl.pallas_call(
        _na_lat_kernel,
        grid=(b, nq),
        in_specs=[
            pl.BlockSpec((1, n, LANES), lambda bi, j: (bi, 0, j)),
            pl.BlockSpec((1, n, LANES), lambda bi, j: (bi, 0, nq + j)),
            pl.BlockSpec((1, n, LANES), lambda bi, j: (bi, 0, 2 * nq + j)),
            pl.BlockSpec((1, p, LANES), lambda bi, j: (bi, 0, j)),
            pl.BlockSpec((1, p, LANES), lambda bi, j: (bi, 0, j)),
            pl.BlockSpec((1, NA_WIN_R, 2 * GRID_W, NA_WIN_R * GRID_W), lambda bi, j: (j, 0, 0, 0)),
        ],
        out_specs=pl.BlockSpec((1, n, LANES), lambda bi, j: (bi, 0, j)),
        out_shape=jax.ShapeDtypeStruct((b, n, NA_WIDTH), F32),
        scratch_shapes=[pltpu.VMEM((n, LANES), BF16), pltpu.VMEM((n, LANES), BF16)],
        compiler_params=_cparams(("parallel", "parallel")),
        name="na_latent",
    )(att, att, att, kc, vc, tab)


def _swa_lat_kernel(sink_ref, q_ref, k_ref, v_ref, kc_ref, vc_ref, o_ref, kb_ref, vb_ref):
    j = pl.program_id(1)
    n = q_ref.shape[1]
    blk = SWA_WIN
    span = 3 * blk
    kb_ref[...] = _dup_head(k_ref[0], j).astype(BF16)
    vb_ref[...] = _dup_head(v_ref[0], j).astype(BF16)
    kc = _dup_head(kc_ref[0], j).astype(BF16)
    vc = _dup_head(vc_ref[0], j).astype(BF16)
    row = lax.broadcasted_iota(jnp.int32, (2 * blk, 1), 0)
    snk = jnp.where(row < blk, sink_ref[2 * j], sink_ref[2 * j + 1])
    qoff = lax.broadcasted_iota(jnp.int32, (2 * blk, span), 0) % blk
    koff = lax.broadcasted_iota(jnp.int32, (2 * blk, span), 1)

    def q_block(qi, carry):
        q0 = pl.multiple_of(qi * blk, blk)
        w0 = pl.multiple_of(jnp.clip(q0 - blk, 0, n - span), blk)
        q2 = _stack_heads(q_ref[0, pl.ds(q0, blk), :]).astype(BF16)
        kw = kb_ref[pl.ds(w0, span), :]
        vw = vb_ref[pl.ds(w0, span), :]
        valid = jnp.abs((q0 + qoff) - (w0 + koff)) <= SWA_WIN
        s_loc = jnp.where(valid, _dot_nt(q2, kw) * ATTN_SCALE, NEG_BIG)
        s_ctx = _dot_nt(q2, kc) * ATTN_SCALE
        m = jnp.maximum(jnp.max(s_loc, axis=-1, keepdims=True), jnp.max(s_ctx, axis=-1, keepdims=True))
        m = jnp.maximum(m, snk)
        p_loc = jnp.exp(s_loc - m)
        p_ctx = jnp.exp(s_ctx - m)
        den = (jnp.sum(p_loc, axis=-1, keepdims=True) + jnp.sum(p_ctx, axis=-1, keepdims=True)
               + jnp.exp(snk - m))
        o2 = (_dot(p_loc.astype(BF16), vw) + _dot(p_ctx.astype(BF16), vc)) / den
        o_ref[0, pl.ds(q0, blk), :] = _unstack_heads(o2)
        return carry

    lax.fori_loop(0, n // blk, q_block, 0)


def _swa_latent(att, kc, vc, sink):
    b, n, _ = att.shape
    p = kc.shape[1]
    nq = SWA_WIDTH // LANES
    qb, kb, vb = SWA_Q_BLOCK0, SWA_Q_BLOCK0 + nq, SWA_Q_BLOCK0 + nq + 1
    return pl.pallas_call(
        _swa_lat_kernel,
        grid=(b, nq),
        in_specs=[
            pl.BlockSpec(memory_space=pltpu.SMEM),
            pl.BlockSpec((1, n, LANES), lambda bi, j: (bi, 0, qb + j)),
            pl.BlockSpec((1, n, LANES), lambda bi, j: (bi, 0, kb)),
            pl.BlockSpec((1, n, LANES), lambda bi, j: (bi, 0, vb)),
            pl.BlockSpec((1, p, LANES), lambda bi, j: (bi, 0, 0)),
            pl.BlockSpec((1, p, LANES), lambda bi, j: (bi, 0, 0)),
        ],
        out_specs=pl.BlockSpec((1, n, LANES), lambda bi, j: (bi, 0, j)),
        out_shape=jax.ShapeDtypeStruct((b, n, SWA_WIDTH), F32),
        scratch_shapes=[pltpu.VMEM((n, LANES), BF16), pltpu.VMEM((n, LANES), BF16)],
        compiler_params=_cparams(("parallel", "parallel")),
        name="swa_latent",
    )(sink, att, att, att, kc, vc)


RK_NB = RK_WIDTH // LANES
LORA_BLOCK = 3 * RK_WIDTH // LANES
GATE_BLOCK = LORA_BLOCK + 1
Q_R, Q_V, Q_A, Q_W, Q_K, Q_B = range(6)
Q_STEP = 6
Q_DIR = 3
Q_SLOTS = Q_STEP + Q_DIR


def _softplus(x):
    return jnp.maximum(x, 0.0) + jnp.log(1.0 + jnp.exp(-jnp.abs(x)))


def _rk_prep_kernel(u_ref, up_ref, un_ref, cw_ref, w0_ref, w2_ref, a0_ref, a2_ref, g2_ref, kk_ref, ka_ref,
                    rk_ref, q_ref, g_ref, bonus_ref, *, n_ctx_tiles, tiles_per_seq):
    def put(slot, val):
        q_ref[:, slot * RK_WIDTH:(slot + 1) * RK_WIDTH] = val

    i = pl.program_id(0)
    li = i - n_ctx_tiles
    is_lat = i >= n_ctx_tiles
    has_prev = jnp.logical_and(is_lat, li % tiles_per_seq != 0)
    has_next = jnp.logical_and(is_lat, li % tiles_per_seq != tiles_per_seq - 1)
    u = u_ref[...]
    tm = u.shape[0]
    prev_row = jnp.where(has_prev, up_ref[7:8, :], 0.0)
    next_row = jnp.where(has_next, un_ref[0:1, :], 0.0)
    row = lax.broadcasted_iota(jnp.int32, u.shape, 0)
    um = jnp.where(row == 0, prev_row, pltpu.roll(u, 1, 0))
    up = jnp.where(row == tm - 1, next_row, pltpu.roll(u, tm - 1, 0))
    u = um * cw_ref[0:1, :] + u * cw_ref[1:2, :] + up * cw_ref[2:3, :]

    r = u[:, 0:RK_WIDTH]
    k = u[:, RK_WIDTH:2 * RK_WIDTH]
    v = u[:, 2 * RK_WIDTH:3 * RK_WIDTH]
    lora = u[:, LORA_BLOCK * LANES:(LORA_BLOCK + 1) * LANES]
    gl = u[:, GATE_BLOCK * LANES:(GATE_BLOCK + 1) * LANES]
    put(Q_R, r)
    put(Q_V, v)
    g_ref[...] = _dot3(jax.nn.sigmoid(gl), g2_ref[...])

    kn = k * kk_ref[...]
    kk = jnp.concatenate(
        [kn[:, c * LANES:(c + 1) * LANES]
         * lax.rsqrt(jnp.maximum(_pair_sum(jnp.square(kn[:, c * LANES:(c + 1) * LANES])), 1e-24))
         for c in range(RK_NB)], axis=1)
    put(Q_A, -kk)

    lora_t = jnp.tanh(lora)
    kd_sum = None
    for d in range(2):
        w = -_softplus(-(w0_ref[d:d + 1, :] + _dot3(lora_t, w2_ref[d]))) - 0.5
        put(Q_W + Q_DIR * d, jnp.exp(-jnp.exp(w)))
        a = jax.nn.sigmoid(a0_ref[d:d + 1, :] + _dot3(lora, a2_ref[d]))
        kd = k * (1.0 + (a - 1.0) * ka_ref[...])
        put(Q_K + Q_DIR * d, kd)
        put(Q_B + Q_DIR * d, kk * a)
        kd_sum = kd if kd_sum is None else kd_sum + kd

    t = r * kd_sum * rk_ref[...]
    bonus_ref[...] = jnp.concatenate(
        [_pair_sum(t[:, c * LANES:(c + 1) * LANES]) for c in range(RK_NB)], axis=1) * v


def _rk_prep(u, p, n_ctx_tiles, tiles_per_seq):
    n_tok = u.shape[0]
    n_tiles = n_tok // TOK_TILE
    sub = TOK_TILE // 8
    last8 = n_tok // 8 - 1
    tok = lambda i: (i, 0)
    const2 = lambda i: (0, 0)
    const3 = lambda i: (0, 0, 0)
    one = jax.ShapeDtypeStruct((n_tok, RK_WIDTH), F32)
    tok_spec = pl.BlockSpec((TOK_TILE, RK_WIDTH), tok)
    return pl.pallas_call(
        functools.partial(_rk_prep_kernel, n_ctx_tiles=n_ctx_tiles, tiles_per_seq=tiles_per_seq),
        grid=(n_tiles,),
        in_specs=[
            pl.BlockSpec((TOK_TILE, RK_COLS), tok),
            pl.BlockSpec((8, RK_COLS), lambda i: (jnp.maximum(i * sub - 1, 0), 0)),
            pl.BlockSpec((8, RK_COLS), lambda i: (jnp.minimum((i + 1) * sub, last8), 0)),
            pl.BlockSpec((3, RK_COLS), const2),
            pl.BlockSpec((2, RK_WIDTH), const2),
            pl.BlockSpec((2, LANES, RK_WIDTH), const3),
            pl.BlockSpec((2, RK_WIDTH), const2),
            pl.BlockSpec((2, LANES, RK_WIDTH), const3),
            pl.BlockSpec((RK_GATE_LORA, RK_WIDTH), const2),
            pl.BlockSpec((1, RK_WIDTH), const2),
            pl.BlockSpec((1, RK_WIDTH), const2),
            pl.BlockSpec((1, RK_WIDTH), const2),
        ],
        out_specs=[pl.BlockSpec((TOK_TILE, Q_SLOTS * RK_WIDTH), tok), tok_spec, tok_spec],
        out_shape=[jax.ShapeDtypeStruct((n_tok, Q_SLOTS * RK_WIDTH), F32), one, one],
        compiler_params=_cparams(("parallel",)),
        name="rk_prep",
    )(u, u, u, p["rk_conv"], p["rk_w0"], p["rk_w2_pad"], p["rk_a0"], p["rk_a2_pad"], p["rk_g2"],
      p["rk_k_k"], p["rk_k_a"], p["rk_r_k"])


SCAN_CHUNK = 16
SCAN_UNROLL = 8


SCAN_GROUP = 8
DIR_LANES = SCAN_GROUP * RK_HEADS


def _rk_scan_kernel(xf_ref, xb_ref, s0_ref, yf_ref, yb_ref, s_ref, m_ref):
    @pl.when(pl.program_id(1) == 0)
    def _():
        s_ref[...] = s0_ref[...]

    tc = xf_ref.shape[0]
    shape = (HEAD_DIM, LANES)
    is_fwd = lax.broadcasted_iota(jnp.int32, shape, 1) < DIR_LANES

    def step(t, carry):
        tb = tc - 1 - t
        for q in range(Q_STEP):
            m_ref[q] = jnp.where(is_fwd, xf_ref[t, q], xb_ref[tb, q])

        def row(q, kx):
            return jnp.broadcast_to(m_ref[q, pl.ds(kx, 1), :], shape)

        def sa_body(kq, acc):
            for uu in range(SCAN_UNROLL):
                kx = kq * SCAN_UNROLL + uu
                acc = acc + s_ref[kx] * row(Q_A, kx)
            return acc

        sa = lax.fori_loop(0, HEAD_DIM // SCAN_UNROLL, sa_body, jnp.zeros(shape, F32))
        vt = m_ref[Q_V]

        def upd_body(kq, acc):
            for uu in range(SCAN_UNROLL):
                kx = kq * SCAN_UNROLL + uu
                s_new = s_ref[kx] * row(Q_W, kx) + sa * row(Q_B, kx) + vt * row(Q_K, kx)
                s_ref[kx] = s_new
                acc = acc + s_new * row(Q_R, kx)
            return acc

        y = lax.fori_loop(0, HEAD_DIM // SCAN_UNROLL, upd_body, jnp.zeros(shape, F32))
        yf_ref[t] = y
        yb_ref[tb] = y
        return carry

    lax.fori_loop(0, tc, step, 0)


def _rk_scan(x, s0):
    t, _, _, nl = x.shape
    nc = t // SCAN_CHUNK
    xblk = (SCAN_CHUNK, Q_STEP, HEAD_DIM, LANES)
    yblk = (SCAN_CHUNK, HEAD_DIM, LANES)
    st = pl.BlockSpec((HEAD_DIM, HEAD_DIM, LANES), lambda j, c: (0, 0, j))
    ysh = jax.ShapeDtypeStruct((t, HEAD_DIM, nl), F32)
    return pl.pallas_call(
        _rk_scan_kernel,
        grid=(nl // LANES, nc),
        in_specs=[pl.BlockSpec(xblk, lambda j, c: (c, 0, 0, j)),
                  pl.BlockSpec(xblk, lambda j, c: (nc - 1 - c, 0, 0, j)), st],
        out_specs=[pl.BlockSpec(yblk, lambda j, c: (c, 0, j)),
                   pl.BlockSpec(yblk, lambda j, c: (nc - 1 - c, 0, j)), st],
        out_shape=[ysh, ysh, jax.ShapeDtypeStruct((HEAD_DIM, HEAD_DIM, nl), F32)],
        scratch_shapes=[pltpu.VMEM((Q_STEP, HEAD_DIM, LANES), F32)],
        compiler_params=_cparams(("parallel", "arbitrary")),
        name="rk_scan",
    )(x, x, s0)


def _rwkv_scan_group(q, bsz, t, s0):
    ng = bsz // SCAN_GROUP
    pad = LANES - 2 * DIR_LANES
    x = q.reshape(ng, SCAN_GROUP, t, Q_SLOTS, RK_HEADS, HEAD_DIM)
    x = jnp.transpose(x, (2, 3, 5, 0, 1, 4)).reshape(t, Q_SLOTS, HEAD_DIM, ng, DIR_LANES)
    x = jnp.concatenate([x[:, :Q_STEP], jnp.concatenate([x[:, :Q_W], x[:, Q_STEP:]], axis=1)], axis=-1)
    x = jnp.pad(x, ((0, 0), (0, 0), (0, 0), (0, 0), (0, pad))).reshape(t, Q_STEP, HEAD_DIM, ng * LANES)
    if s0 is None:
        s0l = jnp.zeros((HEAD_DIM, HEAD_DIM, ng * LANES), F32)
    else:
        s0l = jnp.transpose(s0.reshape(2, ng, SCAN_GROUP, RK_HEADS, HEAD_DIM, HEAD_DIM), (5, 4, 1, 0, 2, 3))
        s0l = jnp.pad(s0l.reshape(HEAD_DIM, HEAD_DIM, ng, 2 * DIR_LANES),
                      ((0, 0), (0, 0), (0, 0), (0, LANES - 2 * DIR_LANES))).reshape(HEAD_DIM, HEAD_DIM, ng * LANES)
    yf, yb, s_fin = _rk_scan(x, s0l)

    def tokens(y, lo):
        y = y.reshape(t, HEAD_DIM, ng, LANES)[..., lo:lo + DIR_LANES].reshape(t, HEAD_DIM, ng, SCAN_GROUP, RK_HEADS)
        return jnp.transpose(y, (2, 3, 0, 4, 1)).reshape(bsz * t, RK_WIDTH)

    y = tokens(yf, 0) + tokens(yb, DIR_LANES)
    s_fin = s_fin.reshape(HEAD_DIM, HEAD_DIM, ng, LANES)[..., :2 * DIR_LANES]
    s_fin = s_fin.reshape(HEAD_DIM, HEAD_DIM, ng, 2, SCAN_GROUP, RK_HEADS)
    return y, jnp.transpose(s_fin, (3, 2, 4, 5, 1, 0)).reshape(2, bsz, RK_HEADS, HEAD_DIM, HEAD_DIM)


def _out_proj_kernel(x_ref, ona_ref, osw_ref, y_ref, bonus_ref, g_ref, lng_ref, lnb_ref, w_ref, mod_ref,
                     n2_ref, rw_ref, rb_ref, x1_ref, h2_ref, lg_ref):
    y = y_ref[...]
    parts = []
    for c in range(RK_NB):
        yc = y[:, c * LANES:(c + 1) * LANES]
        dc = yc - _pair_sum(yc) * (1.0 / HEAD_DIM)
        var = _pair_sum(dc * dc) * (1.0 / HEAD_DIM)
        parts.append(dc * lax.rsqrt(var + GN_EPS))
    yn = jnp.concatenate(parts, axis=1) * lng_ref[...] + lnb_ref[...]
    o_rk = (yn + bonus_ref[...]) * g_ref[...]
    o = (_dot(ona_ref[...].astype(BF16), w_ref[0:NA_WIDTH, :])
         + _dot(osw_ref[...].astype(BF16), w_ref[NA_WIDTH:NA_WIDTH + SWA_WIDTH, :])
         + _dot(o_rk.astype(BF16), w_ref[NA_WIDTH + SWA_WIDTH:, :]))
    x1 = x_ref[...] + mod_ref[0, 2:3, :] * o
    x1_ref[...] = x1
    yn2 = x1 * lax.rsqrt(jnp.mean(x1 * x1, axis=-1, keepdims=True) + RMS_EPS)
    h2 = (yn2 * n2_ref[...]) * (1.0 + mod_ref[0, 4:5, :]) + mod_ref[0, 3:4, :]
    h2_ref[...] = h2.astype(BF16)
    lg_ref[...] = (_dot3(h2, rw_ref[...]) + rb_ref[...])[:, :N_EXPERTS]


def _out_proj(x, o_na, o_sw, y, bonus, g, p, mods, tile_mod):
    n_tok = x.shape[0]
    tok = lambda i: (i, 0)
    const = lambda i: (0, 0)
    return pl.pallas_call(
        _out_proj_kernel,
        grid=(n_tok // TOK_TILE,),
        in_specs=[
            pl.BlockSpec((TOK_TILE, D_MODEL), tok),
            pl.BlockSpec((TOK_TILE, NA_WIDTH), tok),
            pl.BlockSpec((TOK_TILE, SWA_WIDTH), tok),
            pl.BlockSpec((TOK_TILE, RK_WIDTH), tok),
            pl.BlockSpec((TOK_TILE, RK_WIDTH), tok),
            pl.BlockSpec((TOK_TILE, RK_WIDTH), tok),
            pl.BlockSpec((1, RK_WIDTH), const),
            pl.BlockSpec((1, RK_WIDTH), const),
            pl.BlockSpec((D_MODEL, D_MODEL), const),
            pl.BlockSpec((1, 6, D_MODEL), lambda i: (tile_mod(i), 0, 0)),
            pl.BlockSpec((1, D_MODEL), const),
            pl.BlockSpec((D_MODEL, LANES), const),
            pl.BlockSpec((1, LANES), const),
        ],
        out_specs=[
            pl.BlockSpec((TOK_TILE, D_MODEL), tok),
            pl.BlockSpec((TOK_TILE, D_MODEL), tok),
            pl.BlockSpec((TOK_TILE, N_EXPERTS), tok),
        ],
        out_shape=[
            jax.ShapeDtypeStruct((n_tok, D_MODEL), F32),
            jax.ShapeDtypeStruct((n_tok, D_MODEL), BF16),
            jax.ShapeDtypeStruct((n_tok, N_EXPERTS), F32),
        ],
        compiler_params=_cparams(("parallel",)),
        name="out_proj",
    )(x, o_na, o_sw, y, bonus, g, p["rk_ln_g"], p["rk_ln_b"], p["w_out_bf16"], mods, p["norm2_g"],
      p["router_w_pad"], p["router_b_pad"])


H2_PAD_ROWS = 32768
W2_STAGE_ROWS = 128
MOE_VMEM_LIMIT = 56 * 1024 * 1024


def _moe_kernel(meta_ref, x_ref, w1_ref, b1_ref, w2_ref, b2_ref, o_ref, w1b_ref, w2e_ref, stage_ref):
    i = pl.program_id(0)
    n_blk = meta_ref.shape[0] - 1
    n_used = meta_ref[n_blk]
    d_e = w2_ref.shape[1]
    new_expert = jnp.logical_or(i == 0, meta_ref[i] != meta_ref[jnp.maximum(i - 1, 0)])

    @pl.when(i == 0)
    def _():
        stage_ref[...] = jnp.zeros_like(stage_ref)

    @pl.when(jnp.logical_and(i < n_used, new_expert))
    def _():
        w1b_ref[...] = w1_ref[0].astype(BF16)
        for c in range(d_e // W2_STAGE_ROWS):
            rows = slice(c * W2_STAGE_ROWS, (c + 1) * W2_STAGE_ROWS)
            for cb in range(D_MODEL // LANES):
                cols = slice(cb * LANES, (cb + 1) * LANES)
                stage_ref[cb, pl.ds(0, W2_STAGE_ROWS, stride=2), :] = w2_ref[0, rows, cols]
                w2e_ref[2 * c * W2_STAGE_ROWS:2 * (c + 1) * W2_STAGE_ROWS, cols] = stage_ref[cb].astype(BF16)

    @pl.when(i < n_used)
    def _():
        uu = _dot(x_ref[...], w1b_ref[...]) + b1_ref[0]
        acts = []
        for c in range(uu.shape[1] // LANES):
            blk = uu[:, c * LANES:(c + 1) * LANES]
            glu = jnp.minimum(blk, SWIGLU_LIMIT)
            lin = jnp.clip(pltpu.roll(blk, LANES - 1, 1), -SWIGLU_LIMIT, SWIGLU_LIMIT)
            acts.append((glu * jax.nn.sigmoid(SWIGLU_ALPHA * glu) * (lin + 1.0)).astype(BF16))
        act = jnp.concatenate(acts, axis=1)
        o_ref[...] = (_dot(act, w2e_ref[...]) + b2_ref[0]).astype(BF16)

    @pl.when(i >= n_used)
    def _():
        o_ref[...] = jnp.zeros_like(o_ref)


def _moe_blocks(meta, xb, w1, b1, w2, b2, layer):
    n_rows = xb.shape[0]
    n_blk = n_rows // MOE_BLK
    d_e = w2.shape[2]
    row = lambda i, m: (i, 0)
    exp3 = lambda i, m: (layer, m[i], 0, 0)
    grid_spec = pltpu.PrefetchScalarGridSpec(
        num_scalar_prefetch=1,
        grid=(n_blk,),
        in_specs=[
            pl.BlockSpec((MOE_BLK, D_MODEL), row),
            pl.BlockSpec((None, 1, D_MODEL, 2 * d_e), exp3),
            pl.BlockSpec((None, 1, 1, 2 * d_e), exp3),
            pl.BlockSpec((None, 1, d_e, D_MODEL), exp3),
            pl.BlockSpec((None, 1, 1, D_MODEL), exp3),
        ],
        out_specs=pl.BlockSpec((MOE_BLK, D_MODEL), row),
        scratch_shapes=[
            pltpu.VMEM((D_MODEL, 2 * d_e), BF16),
            pltpu.VMEM((2 * d_e, D_MODEL), BF16),
            pltpu.VMEM((D_MODEL // LANES, 2 * W2_STAGE_ROWS, LANES), F32),
        ],
    )
    return pl.pallas_call(
        _moe_kernel,
        grid_spec=grid_spec,
        out_shape=jax.ShapeDtypeStruct((n_rows, D_MODEL), BF16),
        compiler_params=pltpu.CompilerParams(dimension_semantics=("arbitrary",),
                                             vmem_limit_bytes=MOE_VMEM_LIMIT),
        name="moe_blocks",
    )(meta, xb, w1, b1, w2, b2)


def _route(logits):
    n_tok = logits.shape[0]
    top_v, top_i = lax.top_k(logits, TOP_K)
    gates = jax.nn.softmax(top_v, axis=-1)
    e_flat = top_i.reshape(-1).astype(jnp.int32)
    n_rows = n_tok * TOP_K
    onehot = (e_flat[:, None] == jnp.arange(N_EXPERTS, dtype=jnp.int32)[None, :]).astype(jnp.int32)
    csum = jnp.cumsum(onehot, axis=0)
    rank = jnp.take_along_axis(csum, e_flat[:, None], axis=1)[:, 0] - 1
    counts = csum[-1]
    starts = jnp.cumsum(counts) - counts
    pcounts = (counts + MOE_BLK - 1) // MOE_BLK * MOE_BLK
    pends = jnp.cumsum(pcounts)
    pstarts = pends - pcounts
    dest = pstarts[e_flat] + rank
    n_blk = n_rows // MOE_BLK + N_EXPERTS
    blk_exp = jnp.minimum(
        jnp.searchsorted(pends, jnp.arange(n_blk, dtype=jnp.int32) * MOE_BLK, side="right"),
        N_EXPERTS - 1).astype(jnp.int32)
    order = jnp.argsort(e_flat)
    pos = jnp.arange(n_blk * MOE_BLK, dtype=jnp.int32)
    e_pos = jnp.repeat(blk_exp, MOE_BLK)
    src = jnp.clip(pos - pstarts[e_pos] + starts[e_pos], 0, n_rows - 1)
    row_tok = order[src].astype(jnp.int32) // TOP_K
    meta = jnp.concatenate([blk_exp, (pends[-1:] // MOE_BLK).astype(jnp.int32)])
    return meta, row_tok, gates, dest.reshape(n_tok, TOP_K).T.reshape(-1)


def _combine_kernel(x_ref, yg_ref, gate_ref, mod_ref, o_ref):
    gate = gate_ref[...]
    acc = gate[:, 0:1] * yg_ref[0].astype(F32)
    for j in range(1, TOP_K):
        acc = acc + gate[:, j:j + 1] * yg_ref[j].astype(F32)
    o_ref[...] = x_ref[...] + mod_ref[0, 5:6, :] * acc


def _combine(x1, yg, gates, mods, tile_mod):
    n_tok = x1.shape[0]
    return pl.pallas_call(
        _combine_kernel,
        grid=(n_tok // TOK_TILE,),
        in_specs=[
            pl.BlockSpec((TOK_TILE, D_MODEL), lambda i: (i, 0)),
            pl.BlockSpec((TOP_K, TOK_TILE, D_MODEL), lambda i: (0, i, 0)),
            pl.BlockSpec((TOK_TILE, TOP_K), lambda i: (i, 0)),
            pl.BlockSpec((1, 6, D_MODEL), lambda i: (tile_mod(i), 0, 0)),
        ],
        out_specs=pl.BlockSpec((TOK_TILE, D_MODEL), lambda i: (i, 0)),
        out_shape=jax.ShapeDtypeStruct((n_tok, D_MODEL), F32),
        compiler_params=_cparams(("parallel",)),
        name="moe_combine",
    )(x1, yg, gates, mods)


def kernel(x_prompt, x_sample, c, cache_na_k, cache_na_v, cache_swa_k, cache_swa_v, state_rwkv, c_ctx, w_ada, b_ada, norm1_g, norm2_g, w_in, w_out, na_q_norm, na_k_norm, na_rpb, swa_q_norm, swa_k_norm, swa_sink, rk_conv, rk_w0, rk_w2, rk_a0, rk_a2, rk_g2, rk_k_k, rk_k_a, rk_r_k, rk_ln_g, rk_ln_b, moe_router_w, moe_router_b, moe_w1, moe_b1, moe_w2, moe_b2):
    bc, tc, _ = x_prompt.shape
    bl, tl, _ = x_sample.shape
    depth = w_in.shape[0]
    n_ctx = bc * tc
    n_lat = bl * tl
    assert tc == TOK_TILE and tl % TOK_TILE == 0
    n_ctx_tiles = n_ctx // TOK_TILE
    tiles_per_seq = tl // TOK_TILE
    past = cache_na_k.shape[2]

    def tile_mod(i):
        return jnp.where(i < n_ctx_tiles, 0, 1 + (i - n_ctx_tiles) // tiles_per_seq)

    def tile_rope(i):
        return jnp.where(i < n_ctx_tiles, tiles_per_seq, (i - n_ctx_tiles) % tiles_per_seq)

    x = jnp.concatenate([x_prompt.reshape(n_ctx, D_MODEL), x_sample.reshape(n_lat, D_MODEL)], axis=0)

    n_mod = 1 + bl
    mod_rows = -(-n_mod // 8) * 8
    cvecs = jnp.concatenate([c_ctx[None, :], c, jnp.zeros((mod_rows - n_mod, D_MODEL), F32)], axis=0)
    mods_all = _ada_mod(cvecs, w_ada, b_ada).reshape(depth, mod_rows, 6, D_MODEL)
    cos_tab, sin_tab = _rope_tables(tl)
    tile2 = lambda g: jnp.concatenate([g, g])[None, :]
    pad_lanes = lambda z: jnp.pad(z, ((0, 0), (0, LANES - z.shape[1])))
    zeros_lora = jnp.zeros((2, RK_DECAY_LORA, RK_WIDTH), F32)

    na_k_l, na_v_l, sw_k_l, sw_v_l, st_l = [], [], [], [], []
    for l in range(depth):
        mods = mods_all[l]
        qk_gains = jnp.concatenate(
            [tile2(na_q_norm[l]), tile2(na_k_norm[l]), tile2(swa_q_norm[l]), tile2(swa_k_norm[l])], axis=0)
        p = {
            "rk_conv": rk_conv[l], "rk_w0": rk_w0[l], "rk_a0": rk_a0[l], "rk_g2": rk_g2[l],
            "rk_w2_pad": jnp.concatenate([rk_w2[l], zeros_lora], axis=1),
            "rk_a2_pad": jnp.concatenate([zeros_lora, rk_a2[l]], axis=1),
            "rk_k_k": rk_k_k[l][None, :], "rk_k_a": rk_k_a[l][None, :],
            "rk_r_k": rk_r_k[l].reshape(1, RK_WIDTH),
            "rk_ln_g": rk_ln_g[l][None, :], "rk_ln_b": rk_ln_b[l][None, :],
            "w_out_bf16": w_out[l].astype(BF16), "norm2_g": norm2_g[l][None, :],
            "router_w_pad": pad_lanes(moe_router_w[l]), "router_b_pad": pad_lanes(moe_router_b[l][None, :]),
        }

        att, u = _in_proj(x, norm1_g[l][None, :], mods, w_in[l].astype(BF16), qk_gains, cos_tab, sin_tab,
                          tile_mod, tile_rope)
        att_c = att[:n_ctx].reshape(bc, tc, ATT_COLS)
        att_l = att[n_ctx:].reshape(bl, tl, ATT_COLS)
        na_k_l.append(att_c[:, :, NA_WIDTH:2 * NA_WIDTH].reshape(bc, tc, NA_HEADS, HEAD_DIM))
        na_v_l.append(att_c[:, :, 2 * NA_WIDTH:NA_COLS].reshape(bc, tc, NA_HEADS, HEAD_DIM))
        sw_k_l.append(att_c[:, :, NA_COLS + SWA_WIDTH:NA_COLS + SWA_WIDTH + SWA_KV_WIDTH]
                      .reshape(bc, tc, SWA_KV_HEADS, HEAD_DIM))
        sw_v_l.append(att_c[:, :, NA_COLS + SWA_WIDTH + SWA_KV_WIDTH:].reshape(bc, tc, SWA_KV_HEADS, HEAD_DIM))

        sink = swa_sink[l]
        o_na = jnp.concatenate([
            _ctx_attn(att_c, sink, gqa=False).reshape(n_ctx, NA_WIDTH),
            _na_latent(att_l, cache_na_k[:, l].reshape(bl, past, NA_WIDTH),
                       cache_na_v[:, l].reshape(bl, past, NA_WIDTH),
                       _na_bias_tables(na_rpb[l])).reshape(n_lat, NA_WIDTH)], axis=0)
        o_sw = jnp.concatenate([
            _ctx_attn(att_c, sink, gqa=True).reshape(n_ctx, SWA_WIDTH),
            _swa_latent(att_l, cache_swa_k[:, l].reshape(bl, past, SWA_KV_WIDTH),
                        cache_swa_v[:, l].reshape(bl, past, SWA_KV_WIDTH), sink).reshape(n_lat, SWA_WIDTH)],
            axis=0)

        q, g, bonus = _rk_prep(u, p, n_ctx_tiles, tiles_per_seq)
        y_c, s_fin = _rwkv_scan_group(q[:n_ctx], bc, tc, None)
        y_l, _ = _rwkv_scan_group(q[n_ctx:], bl, tl, jnp.moveaxis(state_rwkv[:, l], 1, 0))
        st_l.append(jnp.moveaxis(s_fin, 0, 1))
        y = jnp.concatenate([y_c, y_l], axis=0)

        x1, h2, logits = _out_proj(x, o_na, o_sw, y, bonus, g, p, mods, tile_mod)
        meta, row_tok, gates, dest = _route(logits)
        h2 = jnp.concatenate([h2, jnp.zeros((H2_PAD_ROWS - h2.shape[0], D_MODEL), BF16)], axis=0)
        yb = _moe_blocks(meta, h2[row_tok], moe_w1, moe_b1[:, :, None, :], moe_w2, moe_b2[:, :, None, :], l)
        x = _combine(x1, yb[dest].reshape(TOP_K, n_ctx + n_lat, D_MODEL), gates, mods, tile_mod)

    y_p = x[:n_ctx].reshape(bc, tc, D_MODEL)
    y_s = x[n_ctx:].reshape(bl, tl, D_MODEL)
    return (y_p, y_s, jnp.stack(na_k_l, axis=1), jnp.stack(na_v_l, axis=1), jnp.stack(sw_k_l, axis=1),
            jnp.stack(sw_v_l, axis=1), jnp.stack(st_l, axis=1))
```

```python
import functools

import jax
import jax.numpy as jnp
from jax import lax
from jax.experimental import pallas as pl
from jax.experimental.pallas import tpu as pltpu

F32 = jnp.float32
BF16 = jnp.bfloat16

D_MODEL = 1024
HEAD_DIM = 64
LANES = 128
GRID_W = 64
NA_HEADS = 6
SWA_HEADS = 4
SWA_KV_HEADS = 2
RK_HEADS = 6
NA_WIDTH = NA_HEADS * HEAD_DIM
SWA_WIDTH = SWA_HEADS * HEAD_DIM
SWA_KV_WIDTH = SWA_KV_HEADS * HEAD_DIM
RK_WIDTH = RK_HEADS * HEAD_DIM
RK_DECAY_LORA = 64
RK_A_LORA = 64
RK_GATE_LORA = 128
RK_COLS = 3 * RK_WIDTH + RK_DECAY_LORA + RK_A_LORA + RK_GATE_LORA
NA_COLS = 3 * NA_WIDTH
SWA_COLS = SWA_WIDTH + 2 * SWA_KV_WIDTH
ATT_COLS = NA_COLS + SWA_COLS
IN_COLS = ATT_COLS + RK_COLS
NA_WIN_R = 8
NA_WIN_C = 16
SWA_WIN = 128
ROPE_THETA = 10000.0
ATTN_SCALE = HEAD_DIM ** -0.5
N_EXPERTS = 32
TOP_K = 4
SWIGLU_LIMIT = 7.0
SWIGLU_ALPHA = 1.702
MOE_BLK = 256
RMS_EPS = 1e-6
GN_EPS = 64e-5
NEG_BIG = -1e30

TOK_TILE = 256
VMEM_LIMIT = 48 * 1024 * 1024


def _cparams(sem):
    return pltpu.CompilerParams(dimension_semantics=sem, vmem_limit_bytes=VMEM_LIMIT)


def _dot(a, b):
    return jnp.dot(a, b, preferred_element_type=F32)


def _dot_nt(a, b):
    return lax.dot_general(a, b, (((1,), (1,)), ((), ())), preferred_element_type=F32)


def _split_bf16(x):
    hi = x.astype(BF16)
    lo = (x - hi.astype(F32)).astype(BF16)
    return hi, lo


def _dot3(a, b):
    ah, al = _split_bf16(a)
    bh, bl = _split_bf16(b)
    return _dot(ah, bh) + (_dot(ah, bl) + _dot(al, bh))


def _lane_lo(shape):
    return lax.broadcasted_iota(jnp.int32, shape, len(shape) - 1) < HEAD_DIM


def _pair_sum(x):
    lo = _lane_lo(x.shape)
    s_lo = jnp.sum(jnp.where(lo, x, 0.0), axis=-1, keepdims=True)
    s_hi = jnp.sum(jnp.where(lo, 0.0, x), axis=-1, keepdims=True)
    return jnp.where(lo, s_lo, s_hi)


def _stack_heads(q):
    lo = _lane_lo(q.shape)
    return jnp.concatenate([jnp.where(lo, q, 0.0), jnp.where(lo, 0.0, q)], axis=0)


def _unstack_heads(o2):
    n = o2.shape[0] // 2
    return jnp.where(_lane_lo((n, LANES)), o2[:n], o2[n:])


def _dup_head(x, j):
    keep = _lane_lo(x.shape) == (j == 0)
    return jnp.where(keep, x, pltpu.roll(x, HEAD_DIM, 1))


def _ada_kernel(c_ref, w_ref, b_ref, o_ref):
    cv = c_ref[...]
    s = cv * jax.nn.sigmoid(cv)
    o_ref[0] = _dot3(s, w_ref[0]) + b_ref[0]


def _ada_mod(cvecs, w_ada, b_ada):
    depth, _, n_out = w_ada.shape
    rows = cvecs.shape[0]
    tn = 1024
    return pl.pallas_call(
        _ada_kernel,
        grid=(depth, n_out // tn),
        in_specs=[
            pl.BlockSpec((rows, D_MODEL), lambda l, j: (0, 0)),
            pl.BlockSpec((1, D_MODEL, tn), lambda l, j: (l, 0, j)),
            pl.BlockSpec((1, 1, tn), lambda l, j: (l, 0, j)),
        ],
        out_specs=pl.BlockSpec((1, rows, tn), lambda l, j: (l, 0, j)),
        out_shape=jax.ShapeDtypeStruct((depth, rows, n_out), F32),
        compiler_params=_cparams(("parallel", "parallel")),
        name="ada_mod",
    )(cvecs, w_ada, b_ada.reshape(depth, 1, n_out))


NA_QK_BLOCKS = 2 * NA_WIDTH // LANES
SWA_Q_BLOCK0 = NA_COLS // LANES
SWA_QK_BLOCKS = (SWA_WIDTH + SWA_KV_WIDTH) // LANES


def _in_proj_kernel(x_ref, g_ref, mod_ref, w_ref, qkg_ref, cos_ref, sin_ref, att_ref, u_ref):
    x = x_ref[...]
    y = x * lax.rsqrt(jnp.mean(x * x, axis=-1, keepdims=True) + RMS_EPS)
    h = (y * g_ref[...]) * (1.0 + mod_ref[0, 1:2, :]) + mod_ref[0, 0:1, :]
    proj = _dot(h.astype(BF16), w_ref[...])
    u_ref[...] = proj[:, ATT_COLS:]

    def qk_norm(blk, gain):
        ms = _pair_sum(blk * blk) * (1.0 / HEAD_DIM)
        return blk * lax.rsqrt(ms + RMS_EPS) * gain

    lane = lax.broadcasted_iota(jnp.int32, (x.shape[0], LANES), 1)
    first = (lane % (HEAD_DIM // 2)) < (HEAD_DIM // 4)
    for cb in range(ATT_COLS // LANES):
        blk = proj[:, cb * LANES:(cb + 1) * LANES]
        if cb < NA_QK_BLOCKS:
            gi = 0 if cb < NA_QK_BLOCKS // 2 else 1
            blk = qk_norm(blk, qkg_ref[gi:gi + 1, :])
        elif SWA_Q_BLOCK0 <= cb < SWA_Q_BLOCK0 + SWA_QK_BLOCKS:
            gi = 2 if cb < SWA_Q_BLOCK0 + SWA_WIDTH // LANES else 3
            blk = qk_norm(blk, qkg_ref[gi:gi + 1, :])
            partner = jnp.where(first, pltpu.roll(blk, LANES - HEAD_DIM // 4, 1),
                                pltpu.roll(blk, HEAD_DIM // 4, 1))
            blk = blk * cos_ref[...] + partner * sin_ref[...]
        att_ref[:, cb * LANES:(cb + 1) * LANES] = blk


def _in_proj(x, norm_g, mods, w_in_bf16, qk_gains, cos_tab, sin_tab, tile_mod, tile_rope):
    n_tok = x.shape[0]
    return pl.pallas_call(
        _in_proj_kernel,
        grid=(n_tok // TOK_TILE,),
        in_specs=[
            pl.BlockSpec((TOK_TILE, D_MODEL), lambda i: (i, 0)),
            pl.BlockSpec((1, D_MODEL), lambda i: (0, 0)),
            pl.BlockSpec((1, 6, D_MODEL), lambda i: (tile_mod(i), 0, 0)),
            pl.BlockSpec((D_MODEL, IN_COLS), lambda i: (0, 0)),
            pl.BlockSpec((4, LANES), lambda i: (0, 0)),
            pl.BlockSpec((TOK_TILE, LANES), lambda i: (tile_rope(i), 0)),
            pl.BlockSpec((TOK_TILE, LANES), lambda i: (tile_rope(i), 0)),
        ],
        out_specs=[
            pl.BlockSpec((TOK_TILE, ATT_COLS), lambda i: (i, 0)),
            pl.BlockSpec((TOK_TILE, RK_COLS), lambda i: (i, 0)),
        ],
        out_shape=[
            jax.ShapeDtypeStruct((n_tok, ATT_COLS), F32),
            jax.ShapeDtypeStruct((n_tok, RK_COLS), F32),
        ],
        compiler_params=_cparams(("parallel",)),
        name="in_proj",
    )(x, norm_g, mods, w_in_bf16, qk_gains, cos_tab, sin_tab)


def _rope_tables(n_lat):
    nf = HEAD_DIM // 4
    t = jnp.arange(n_lat)
    lane = jnp.arange(LANES)
    d = lane % HEAD_DIM
    inv = ROPE_THETA ** (-(d % nf).astype(F32) / nf)
    pos = jnp.where((d // (2 * nf))[None, :] == 0, (t // GRID_W)[:, None], (t % GRID_W)[:, None]).astype(F32)
    ang = pos * inv[None, :]
    sign = jnp.where((d % (2 * nf)) < nf, -1.0, 1.0).astype(F32)
    cos = jnp.concatenate([jnp.cos(ang), jnp.ones((TOK_TILE, LANES), F32)], 0)
    sin = jnp.concatenate([jnp.sin(ang) * sign[None, :], jnp.zeros((TOK_TILE, LANES), F32)], 0)
    return cos, sin


def _ctx_attn_kernel(sink_ref, q_ref, k_ref, v_ref, o_ref, *, gqa):
    j = pl.program_id(1)
    k = k_ref[0]
    v = v_ref[0]
    if gqa:
        k = _dup_head(k, j)
        v = _dup_head(v, j)
    n = k.shape[0]
    q2 = _stack_heads(q_ref[0]).astype(BF16)
    s = _dot_nt(q2, k.astype(BF16)) * ATTN_SCALE
    m = jnp.max(s, axis=-1, keepdims=True)
    if gqa:
        row = lax.broadcasted_iota(jnp.int32, (2 * n, 1), 0)
        snk = jnp.where(row < n, sink_ref[2 * j], sink_ref[2 * j + 1])
        m = jnp.maximum(m, snk)
    p = jnp.exp(s - m)
    den = jnp.sum(p, axis=-1, keepdims=True)
    if gqa:
        den = den + jnp.exp(snk - m)
    o2 = _dot(p.astype(BF16), v.astype(BF16)) / den
    o_ref[0] = _unstack_heads(o2)


def _ctx_attn(att, sink, *, gqa):
    b, t, _ = att.shape
    if gqa:
        nq = SWA_WIDTH // LANES
        qb, kb, vb = SWA_Q_BLOCK0, SWA_Q_BLOCK0 + nq, SWA_Q_BLOCK0 + nq + 1
        kmap = lambda bi, j: (bi, 0, kb)
        vmap = lambda bi, j: (bi, 0, vb)
    else:
        nq = NA_WIDTH // LANES
        qb, kb, vb = 0, nq, 2 * nq
        kmap = lambda bi, j: (bi, 0, kb + j)
        vmap = lambda bi, j: (bi, 0, vb + j)
    return pl.pallas_call(
        functools.partial(_ctx_attn_kernel, gqa=gqa),
        grid=(b, nq),
        in_specs=[
            pl.BlockSpec(memory_space=pltpu.SMEM),
            pl.BlockSpec((1, t, LANES), lambda bi, j: (bi, 0, qb + j)),
            pl.BlockSpec((1, t, LANES), kmap),
            pl.BlockSpec((1, t, LANES), vmap),
        ],
        out_specs=pl.BlockSpec((1, t, LANES), lambda bi, j: (bi, 0, j)),
        out_shape=jax.ShapeDtypeStruct((b, t, nq * LANES), F32),
        compiler_params=_cparams(("parallel", "parallel")),
        name="ctx_attn_swa" if gqa else "ctx_attn_na",
    )(sink, att, att, att)


def _na_lat_kernel(q_ref, k_ref, v_ref, kc_ref, vc_ref, tab_ref, o_ref, kb_ref, vb_ref):
    n = q_ref.shape[1]
    rows = n // GRID_W
    win = NA_WIN_R * GRID_W
    kb_ref[...] = k_ref[0].astype(BF16)
    vb_ref[...] = v_ref[0].astype(BF16)
    kc = kc_ref[0].astype(BF16)
    vc = vc_ref[0].astype(BF16)

    def row_block(i, carry):
        start = jnp.clip(i - NA_WIN_R // 2, 0, rows - NA_WIN_R)
        rb = start - i + (NA_WIN_R - 1)
        q0 = pl.multiple_of(i * GRID_W, GRID_W)
        k0 = pl.multiple_of(start * GRID_W, GRID_W)
        q2 = _stack_heads(q_ref[0, pl.ds(q0, GRID_W), :]).astype(BF16)
        kw = kb_ref[pl.ds(k0, win), :]
        vw = vb_ref[pl.ds(k0, win), :]
        s_loc = _dot_nt(q2, kw) * ATTN_SCALE + tab_ref[0, rb]
        s_ctx = _dot_nt(q2, kc) * ATTN_SCALE
        m = jnp.maximum(jnp.max(s_loc, axis=-1, keepdims=True), jnp.max(s_ctx, axis=-1, keepdims=True))
        p_loc = jnp.exp(s_loc - m)
        p_ctx = jnp.exp(s_ctx - m)
        den = jnp.sum(p_loc, axis=-1, keepdims=True) + jnp.sum(p_ctx, axis=-1, keepdims=True)
        o2 = (_dot(p_loc.astype(BF16), vw) + _dot(p_ctx.astype(BF16), vc)) / den
        o_ref[0, pl.ds(q0, GRID_W), :] = _unstack_heads(o2)
        return carry

    lax.fori_loop(0, rows, row_block, 0)


def _na_bias_tables(rpb):
    col = jnp.arange(GRID_W)
    cstart = jnp.clip(col - NA_WIN_C // 2, 0, GRID_W - NA_WIN_C)
    col_mask = (col[None, :] >= cstart[:, None]) & (col[None, :] < cstart[:, None] + NA_WIN_C)
    col_idx = jnp.clip(col[None, :] - col[:, None] + NA_WIN_C - 1, 0, 2 * NA_WIN_C - 2)
    rpb_cols = jnp.where(col_mask[None, None], rpb[:, :, col_idx], NEG_BIG)
    roff = jnp.arange(NA_WIN_R)[:, None] + jnp.arange(NA_WIN_R)[None, :]
    t = rpb_cols[:, roff]
    t = jnp.transpose(t, (0, 1, 3, 2, 4)).reshape(NA_HEADS // 2, 2, NA_WIN_R, GRID_W, NA_WIN_R * GRID_W)
    return jnp.transpose(t, (0, 2, 1, 3, 4)).reshape(NA_HEADS // 2, NA_WIN_R, 2 * GRID_W, NA_WIN_R * GRID_W)


def _na_latent(att, kc, vc, tab):
    b, n, _ = att.shape
    p = kc.shape[1]
    nq = NA_WIDTH // LANES
    return pl.pallas_call(
        _na_lat_kernel,
        grid=(b, nq),
        in_specs=[
            pl.BlockSpec((1, n, LANES), lambda bi, j: (bi, 0, j)),
            pl.BlockSpec((1, n, LANES), lambda bi, j: (bi, 0, nq + j)),
            pl.BlockSpec((1, n, LANES), lambda bi, j: (bi, 0, 2 * nq + j)),
            pl.BlockSpec((1, p, LANES), lambda bi, j: (bi, 0, j)),
            pl.BlockSpec((1, p, LANES), lambda bi, j: (bi, 0, j)),
            pl.BlockSpec((1, NA_WIN_R, 2 * GRID_W, NA_WIN_R * GRID_W), lambda bi, j: (j, 0, 0, 0)),
        ],
        out_specs=pl.BlockSpec((1, n, LANES), lambda bi, j: (bi, 0, j)),
        out_shape=jax.ShapeDtypeStruct((b, n, NA_WIDTH), F32),
        scratch_shapes=[pltpu.VMEM((n, LANES), BF16), pltpu.VMEM((n, LANES), BF16)],
        compiler_params=_cparams(("parallel", "parallel")),
        name="na_latent",
    )(att, att, att, kc, vc, tab)


def _swa_lat_kernel(sink_ref, q_ref, k_ref, v_ref, kc_ref, vc_ref, o_ref, kb_ref, vb_ref):
    j = pl.program_id(1)
    n = q_ref.shape[1]
    blk = SWA_WIN
    span = 3 * blk
    kb_ref[...] = _dup_head(k_ref[0], j).astype(BF16)
    vb_ref[...] = _dup_head(v_ref[0], j).astype(BF16)
    kc = _dup_head(kc_ref[0], j).astype(BF16)
    vc = _dup_head(vc_ref[0], j).astype(BF16)
    row = lax.broadcasted_iota(jnp.int32, (2 * blk, 1), 0)
    snk = jnp.where(row < blk, sink_ref[2 * j], sink_ref[2 * j + 1])
    qoff = lax.broadcasted_iota(jnp.int32, (2 * blk, span), 0) % blk
    koff = lax.broadcasted_iota(jnp.int32, (2 * blk, span), 1)

    def q_block(qi, carry):
        q0 = pl.multiple_of(qi * blk, blk)
        w0 = pl.multiple_of(jnp.clip(q0 - blk, 0, n - span), blk)
        q2 = _stack_heads(q_ref[0, pl.ds(q0, blk), :]).astype(BF16)
        kw = kb_ref[pl.ds(w0, span), :]
        vw = vb_ref[pl.ds(w0, span), :]
        valid = jnp.abs((q0 + qoff) - (w0 + koff)) <= SWA_WIN
        s_loc = jnp.where(valid, _dot_nt(q2, kw) * ATTN_SCALE, NEG_BIG)
        s_ctx = _dot_nt(q2, kc) * ATTN_SCALE
        m = jnp.maximum(jnp.max(s_loc, axis=-1, keepdims=True), jnp.max(s_ctx, axis=-1, keepdims=True))
        m = jnp.maximum(m, snk)
        p_loc = jnp.exp(s_loc - m)
        p_ctx = jnp.exp(s_ctx - m)
        den = (jnp.sum(p_loc, axis=-1, keepdims=True) + jnp.sum(p_ctx, axis=-1, keepdims=True)
               + jnp.exp(snk - m))
        o2 = (_dot(p_loc.astype(BF16), vw) + _dot(p_ctx.astype(BF16), vc)) / den
        o_ref[0, pl.ds(q0, blk), :] = _unstack_heads(o2)
        return carry

    lax.fori_loop(0, n // blk, q_block, 0)


def _swa_latent(att, kc, vc, sink):
    b, n, _ = att.shape
    p = kc.shape[1]
    nq = SWA_WIDTH // LANES
    qb, kb, vb = SWA_Q_BLOCK0, SWA_Q_BLOCK0 + nq, SWA_Q_BLOCK0 + nq + 1
    return pl.pallas_call(
        _swa_lat_kernel,
        grid=(b, nq),
        in_specs=[
            pl.BlockSpec(memory_space=pltpu.SMEM),
            pl.BlockSpec((1, n, LANES), lambda bi, j: (bi, 0, qb + j)),
            pl.BlockSpec((1, n, LANES), lambda bi, j: (bi, 0, kb)),
            pl.BlockSpec((1, n, LANES), lambda bi, j: (bi, 0, vb)),
            pl.BlockSpec((1, p, LANES), lambda bi, j: (bi, 0, 0)),
            pl.BlockSpec((1, p, LANES), lambda bi, j: (bi, 0, 0)),
        ],
        out_specs=pl.BlockSpec((1, n, LANES), lambda bi, j: (bi, 0, j)),
        out_shape=jax.ShapeDtypeStruct((b, n, SWA_WIDTH), F32),
        scratch_shapes=[pltpu.VMEM((n, LANES), BF16), pltpu.VMEM((n, LANES), BF16)],
        compiler_params=_cparams(("parallel", "parallel")),
        name="swa_latent",
    )(sink, att, att, att, kc, vc)


RK_NB = RK_WIDTH // LANES
LORA_BLOCK = 3 * RK_WIDTH // LANES
GATE_BLOCK = LORA_BLOCK + 1
Q_R, Q_V, Q_A, Q_W, Q_K, Q_B = range(6)
Q_STEP = 6
Q_COLS = Q_STEP * 2 * RK_WIDTH


def _softplus(x):
    return jnp.maximum(x, 0.0) + jnp.log(1.0 + jnp.exp(-jnp.abs(x)))


def _rk_prep_kernel(u_ref, up_ref, un_ref, cw_ref, w0_ref, w2_ref, a0_ref, a2_ref, g2_ref, kk_ref, ka_ref,
                    rk_ref, q_ref, g_ref, bonus_ref, *, n_ctx_tiles, tiles_per_seq):
    def put(slot, d, val):
        col = (2 * slot + d) * RK_WIDTH
        q_ref[:, col:col + RK_WIDTH] = val

    def put_both(slot, val):
        put(slot, 0, val)
        put(slot, 1, val)

    i = pl.program_id(0)
    li = i - n_ctx_tiles
    is_lat = i >= n_ctx_tiles
    has_prev = jnp.logical_and(is_lat, li % tiles_per_seq != 0)
    has_next = jnp.logical_and(is_lat, li % tiles_per_seq != tiles_per_seq - 1)
    u = u_ref[...]
    tm = u.shape[0]
    prev_row = jnp.where(has_prev, up_ref[7:8, :], 0.0)
    next_row = jnp.where(has_next, un_ref[0:1, :], 0.0)
    row = lax.broadcasted_iota(jnp.int32, u.shape, 0)
    um = jnp.where(row == 0, prev_row, pltpu.roll(u, 1, 0))
    up = jnp.where(row == tm - 1, next_row, pltpu.roll(u, tm - 1, 0))
    u = um * cw_ref[0:1, :] + u * cw_ref[1:2, :] + up * cw_ref[2:3, :]

    r = u[:, 0:RK_WIDTH]
    k = u[:, RK_WIDTH:2 * RK_WIDTH]
    v = u[:, 2 * RK_WIDTH:3 * RK_WIDTH]
    lora = u[:, LORA_BLOCK * LANES:(LORA_BLOCK + 1) * LANES]
    gl = u[:, GATE_BLOCK * LANES:(GATE_BLOCK + 1) * LANES]
    put_both(Q_R, r)
    put_both(Q_V, v)
    g_ref[...] = _dot3(jax.nn.sigmoid(gl), g2_ref[...])

    kn = k * kk_ref[...]
    kk = jnp.concatenate(
        [kn[:, c * LANES:(c + 1) * LANES]
         * lax.rsqrt(jnp.maximum(_pair_sum(jnp.square(kn[:, c * LANES:(c + 1) * LANES])), 1e-24))
         for c in range(RK_NB)], axis=1)
    put_both(Q_A, -kk)

    lora_t = jnp.tanh(lora)
    kd_sum = None
    for d in range(2):
        w = -_softplus(-(w0_ref[d:d + 1, :] + _dot3(lora_t, w2_ref[d]))) - 0.5
        put(Q_W, d, jnp.exp(-jnp.exp(w)))
        a = jax.nn.sigmoid(a0_ref[d:d + 1, :] + _dot3(lora, a2_ref[d]))
        kd = k * (1.0 + (a - 1.0) * ka_ref[...])
        put(Q_K, d, kd)
        put(Q_B, d, kk * a)
        kd_sum = kd if kd_sum is None else kd_sum + kd

    t = r * kd_sum * rk_ref[...]
    bonus_ref[...] = jnp.concatenate(
        [_pair_sum(t[:, c * LANES:(c + 1) * LANES]) for c in range(RK_NB)], axis=1) * v


def _rk_prep(u, p, n_ctx_tiles, tiles_per_seq):
    n_tok = u.shape[0]
    n_tiles = n_tok // TOK_TILE
    sub = TOK_TILE // 8
    last8 = n_tok // 8 - 1
    tok = lambda i: (i, 0)
    const2 = lambda i: (0, 0)
    const3 = lambda i: (0, 0, 0)
    one = jax.ShapeDtypeStruct((n_tok, RK_WIDTH), F32)
    tok_spec = pl.BlockSpec((TOK_TILE, RK_WIDTH), tok)
    return pl.pallas_call(
        functools.partial(_rk_prep_kernel, n_ctx_tiles=n_ctx_tiles, tiles_per_seq=tiles_per_seq),
        grid=(n_tiles,),
        in_specs=[
            pl.BlockSpec((TOK_TILE, RK_COLS), tok),
            pl.BlockSpec((8, RK_COLS), lambda i: (jnp.maximum(i * sub - 1, 0), 0)),
            pl.BlockSpec((8, RK_COLS), lambda i: (jnp.minimum((i + 1) * sub, last8), 0)),
            pl.BlockSpec((3, RK_COLS), const2),
            pl.BlockSpec((2, RK_WIDTH), const2),
            pl.BlockSpec((2, LANES, RK_WIDTH), const3),
            pl.BlockSpec((2, RK_WIDTH), const2),
            pl.BlockSpec((2, LANES, RK_WIDTH), const3),
            pl.BlockSpec((RK_GATE_LORA, RK_WIDTH), const2),
            pl.BlockSpec((1, RK_WIDTH), const2),
            pl.BlockSpec((1, RK_WIDTH), const2),
            pl.BlockSpec((1, RK_WIDTH), const2),
        ],
        out_specs=[pl.BlockSpec((TOK_TILE, Q_COLS), tok), tok_spec, tok_spec],
        out_shape=[jax.ShapeDtypeStruct((n_tok, Q_COLS), F32), one, one],
        compiler_params=_cparams(("parallel",)),
        name="rk_prep",
    )(u, u, u, p["rk_conv"], p["rk_w0"], p["rk_w2_pad"], p["rk_a0"], p["rk_a2_pad"], p["rk_g2"],
      p["rk_k_k"], p["rk_k_a"], p["rk_r_k"])


SCAN_CHUNK = 16
SCAN_UNROLL = 8


SCAN_GROUP = 8
DIR_LANES = SCAN_GROUP * RK_HEADS
SCAN_LANES = 2 * DIR_LANES


def _rk_scan_kernel(xf_ref, xb_ref, s0_ref, yf_ref, yb_ref, s_ref, m_ref):
    @pl.when(pl.program_id(1) == 0)
    def _():
        s_ref[...] = s0_ref[...]

    tc = xf_ref.shape[0]
    shape = (HEAD_DIM, SCAN_LANES)
    is_fwd = lax.broadcasted_iota(jnp.int32, shape, 1) < DIR_LANES

    def step(t, carry):
        tb = tc - 1 - t
        for q in range(Q_STEP):
            m_ref[q] = jnp.where(is_fwd, xf_ref[t, q], xb_ref[tb, q])

        def row(q, kx):
            return jnp.broadcast_to(m_ref[q, pl.ds(kx, 1), :], shape)

        def sa_body(kq, acc):
            for uu in range(SCAN_UNROLL):
                kx = kq * SCAN_UNROLL + uu
                acc = acc + s_ref[kx] * row(Q_A, kx)
            return acc

        sa = lax.fori_loop(0, HEAD_DIM // SCAN_UNROLL, sa_body, jnp.zeros(shape, F32))
        vt = m_ref[Q_V]

        def upd_body(kq, acc):
            for uu in range(SCAN_UNROLL):
                kx = kq * SCAN_UNROLL + uu
                s_new = s_ref[kx] * row(Q_W, kx) + sa * row(Q_B, kx) + vt * row(Q_K, kx)
                s_ref[kx] = s_new
                acc = acc + s_new * row(Q_R, kx)
            return acc

        y = lax.fori_loop(0, HEAD_DIM // SCAN_UNROLL, upd_body, jnp.zeros(shape, F32))
        yf_ref[t] = y
        yb_ref[tb] = y
        return carry

    lax.fori_loop(0, tc, step, 0)


def _rk_scan(x, s0):
    ng, t = x.shape[:2]
    nc = t // SCAN_CHUNK
    xblk = (None, SCAN_CHUNK, Q_STEP, HEAD_DIM, SCAN_LANES)
    yblk = (None, SCAN_CHUNK, HEAD_DIM, SCAN_LANES)
    st = pl.BlockSpec((None, HEAD_DIM, HEAD_DIM, SCAN_LANES), lambda j, c: (j, 0, 0, 0))
    ysh = jax.ShapeDtypeStruct((ng, t, HEAD_DIM, SCAN_LANES), F32)
    return pl.pallas_call(
        _rk_scan_kernel,
        grid=(ng, nc),
        in_specs=[pl.BlockSpec(xblk, lambda j, c: (j, c, 0, 0, 0)),
                  pl.BlockSpec(xblk, lambda j, c: (j, nc - 1 - c, 0, 0, 0)), st],
        out_specs=[pl.BlockSpec(yblk, lambda j, c: (j, c, 0, 0)),
                   pl.BlockSpec(yblk, lambda j, c: (j, nc - 1 - c, 0, 0)), st],
        out_shape=[ysh, ysh, jax.ShapeDtypeStruct((ng, HEAD_DIM, HEAD_DIM, SCAN_LANES), F32)],
        scratch_shapes=[pltpu.VMEM((Q_STEP, HEAD_DIM, SCAN_LANES), F32)],
        compiler_params=_cparams(("parallel", "arbitrary")),
        name="rk_scan",
    )(x, x, s0)


def _scan_operands(q, bsz, t):
    ng = bsz // SCAN_GROUP
    x = q.reshape(ng, SCAN_GROUP, t, Q_STEP, 2, RK_HEADS, HEAD_DIM)
    return jnp.transpose(x, (0, 2, 3, 6, 4, 1, 5)).reshape(ng, t, Q_STEP, HEAD_DIM, SCAN_LANES)


def _rwkv_scan_group(x, bsz, t, s0):
    ng = bsz // SCAN_GROUP
    if s0 is None:
        s0l = jnp.zeros((ng, HEAD_DIM, HEAD_DIM, SCAN_LANES), F32)
    else:
        s0l = jnp.transpose(s0.reshape(2, ng, SCAN_GROUP, RK_HEADS, HEAD_DIM, HEAD_DIM), (1, 5, 4, 0, 2, 3))
        s0l = s0l.reshape(ng, HEAD_DIM, HEAD_DIM, SCAN_LANES)
    yf, yb, s_fin = _rk_scan(x, s0l)
    y = (yf[..., :DIR_LANES] + yb[..., DIR_LANES:]).reshape(ng, t, HEAD_DIM, SCAN_GROUP, RK_HEADS)
    y = jnp.transpose(y, (0, 3, 1, 4, 2)).reshape(bsz * t, RK_WIDTH)
    s_fin = s_fin.reshape(ng, HEAD_DIM, HEAD_DIM, 2, SCAN_GROUP, RK_HEADS)
    return y, jnp.transpose(s_fin, (3, 0, 4, 5, 2, 1)).reshape(2, bsz, RK_HEADS, HEAD_DIM, HEAD_DIM)


def _out_proj_kernel(x_ref, ona_ref, osw_ref, y_ref, bonus_ref, g_ref, lng_ref, lnb_ref, w_ref, mod_ref,
                     n2_ref, rw_ref, rb_ref, x1_ref, h2_ref, lg_ref):
    y = y_ref[...]
    parts = []
    for c in range(RK_NB):
        yc = y[:, c * LANES:(c + 1) * LANES]
        dc = yc - _pair_sum(yc) * (1.0 / HEAD_DIM)
        var = _pair_sum(dc * dc) * (1.0 / HEAD_DIM)
        parts.append(dc * lax.rsqrt(var + GN_EPS))
    yn = jnp.concatenate(parts, axis=1) * lng_ref[...] + lnb_ref[...]
    o_rk = (yn + bonus_ref[...]) * g_ref[...]
    o = (_dot(ona_ref[...].astype(BF16), w_ref[0:NA_WIDTH, :])
         + _dot(osw_ref[...].astype(BF16), w_ref[NA_WIDTH:NA_WIDTH + SWA_WIDTH, :])
         + _dot(o_rk.astype(BF16), w_ref[NA_WIDTH + SWA_WIDTH:, :]))
    x1 = x_ref[...] + mod_ref[0, 2:3, :] * o
    x1_ref[...] = x1
    yn2 = x1 * lax.rsqrt(jnp.mean(x1 * x1, axis=-1, keepdims=True) + RMS_EPS)
    h2 = (yn2 * n2_ref[...]) * (1.0 + mod_ref[0, 4:5, :]) + mod_ref[0, 3:4, :]
    h2_ref[...] = h2.astype(BF16)
    lg_ref[...] = (_dot3(h2, rw_ref[...]) + rb_ref[...])[:, :N_EXPERTS]


def _out_proj(x, o_na, o_sw, y, bonus, g, p, mods, tile_mod):
    n_tok = x.shape[0]
    tok = lambda i: (i, 0)
    const = lambda i: (0, 0)
    return pl.pallas_call(
        _out_proj_kernel,
        grid=(n_tok // TOK_TILE,),
        in_specs=[
            pl.BlockSpec((TOK_TILE, D_MODEL), tok),
            pl.BlockSpec((TOK_TILE, NA_WIDTH), tok),
            pl.BlockSpec((TOK_TILE, SWA_WIDTH), tok),
            pl.BlockSpec((TOK_TILE, RK_WIDTH), tok),
            pl.BlockSpec((TOK_TILE, RK_WIDTH), tok),
            pl.BlockSpec((TOK_TILE, RK_WIDTH), tok),
            pl.BlockSpec((1, RK_WIDTH), const),
            pl.BlockSpec((1, RK_WIDTH), const),
            pl.BlockSpec((D_MODEL, D_MODEL), const),
            pl.BlockSpec((1, 6, D_MODEL), lambda i: (tile_mod(i), 0, 0)),
            pl.BlockSpec((1, D_MODEL), const),
            pl.BlockSpec((D_MODEL, LANES), const),
            pl.BlockSpec((1, LANES), const),
        ],
        out_specs=[
            pl.BlockSpec((TOK_TILE, D_MODEL), tok),
            pl.BlockSpec((TOK_TILE, D_MODEL), tok),
            pl.BlockSpec((TOK_TILE, N_EXPERTS), tok),
        ],
        out_shape=[
            jax.ShapeDtypeStruct((n_tok, D_MODEL), F32),
            jax.ShapeDtypeStruct((n_tok, D_MODEL), BF16),
            jax.ShapeDtypeStruct((n_tok, N_EXPERTS), F32),
        ],
        compiler_params=_cparams(("parallel",)),
        name="out_proj",
    )(x, o_na, o_sw, y, bonus, g, p["rk_ln_g"], p["rk_ln_b"], p["w_out_bf16"], mods, p["norm2_g"],
      p["router_w_pad"], p["router_b_pad"])


H2_PAD_ROWS = 32768
W2_STAGE_ROWS = 128
MOE_VMEM_LIMIT = 56 * 1024 * 1024


def _moe_kernel(meta_ref, x_ref, w1_ref, b1_ref, w2_ref, b2_ref, o_ref, w1b_ref, w2e_ref, stage_ref):
    i = pl.program_id(0)
    n_blk = meta_ref.shape[0] - 1
    n_used = meta_ref[n_blk]
    d_e = w2_ref.shape[1]
    new_expert = jnp.logical_or(i == 0, meta_ref[i] != meta_ref[jnp.maximum(i - 1, 0)])

    @pl.when(i == 0)
    def _():
        stage_ref[...] = jnp.zeros_like(stage_ref)

    @pl.when(jnp.logical_and(i < n_used, new_expert))
    def _():
        w1b_ref[...] = w1_ref[0].astype(BF16)
        for c in range(d_e // W2_STAGE_ROWS):
            rows = slice(c * W2_STAGE_ROWS, (c + 1) * W2_STAGE_ROWS)
            for cb in range(D_MODEL // LANES):
                cols = slice(cb * LANES, (cb + 1) * LANES)
                stage_ref[cb, pl.ds(0, W2_STAGE_ROWS, stride=2), :] = w2_ref[0, rows, cols]
                w2e_ref[2 * c * W2_STAGE_ROWS:2 * (c + 1) * W2_STAGE_ROWS, cols] = stage_ref[cb].astype(BF16)

    @pl.when(i < n_used)
    def _():
        uu = _dot(x_ref[...], w1b_ref[...]) + b1_ref[0]
        acts = []
        for c in range(uu.shape[1] // LANES):
            blk = uu[:, c * LANES:(c + 1) * LANES]
            glu = jnp.minimum(blk, SWIGLU_LIMIT)
            lin = jnp.clip(pltpu.roll(blk, LANES - 1, 1), -SWIGLU_LIMIT, SWIGLU_LIMIT)
            acts.append((glu * jax.nn.sigmoid(SWIGLU_ALPHA * glu) * (lin + 1.0)).astype(BF16))
        act = jnp.concatenate(acts, axis=1)
        o_ref[...] = (_dot(act, w2e_ref[...]) + b2_ref[0]).astype(BF16)

    @pl.when(i >= n_used)
    def _():
        o_ref[...] = jnp.zeros_like(o_ref)


def _moe_blocks(meta, xb, w1, b1, w2, b2, layer):
    n_rows = xb.shape[0]
    n_blk = n_rows // MOE_BLK
    d_e = w2.shape[2]
    row = lambda i, m: (i, 0)
    exp3 = lambda i, m: (layer, m[i], 0, 0)
    grid_spec = pltpu.PrefetchScalarGridSpec(
        num_scalar_prefetch=1,
        grid=(n_blk,),
        in_specs=[
            pl.BlockSpec((MOE_BLK, D_MODEL), row),
            pl.BlockSpec((None, 1, D_MODEL, 2 * d_e), exp3),
            pl.BlockSpec((None, 1, 1, 2 * d_e), exp3),
            pl.BlockSpec((None, 1, d_e, D_MODEL), exp3),
            pl.BlockSpec((None, 1, 1, D_MODEL), exp3),
        ],
        out_specs=pl.BlockSpec((MOE_BLK, D_MODEL), row),
        scratch_shapes=[
            pltpu.VMEM((D_MODEL, 2 * d_e), BF16),
            pltpu.VMEM((2 * d_e, D_MODEL), BF16),
            pltpu.VMEM((D_MODEL // LANES, 2 * W2_STAGE_ROWS, LANES), F32),
        ],
    )
    return pl.pallas_call(
        _moe_kernel,
        grid_spec=grid_spec,
        out_shape=jax.ShapeDtypeStruct((n_rows, D_MODEL), BF16),
        compiler_params=pltpu.CompilerParams(dimension_semantics=("arbitrary",),
                                             vmem_limit_bytes=MOE_VMEM_LIMIT),
        name="moe_blocks",
    )(meta, xb, w1, b1, w2, b2)


def _route(logits):
    n_tok = logits.shape[0]
    top_v, top_i = lax.top_k(logits, TOP_K)
    gates = jax.nn.softmax(top_v, axis=-1)
    e_flat = top_i.reshape(-1).astype(jnp.int32)
    n_rows = n_tok * TOP_K
    onehot = (e_flat[:, None] == jnp.arange(N_EXPERTS, dtype=jnp.int32)[None, :]).astype(jnp.int32)
    csum = jnp.cumsum(onehot, axis=0)
    rank = jnp.take_along_axis(csum, e_flat[:, None], axis=1)[:, 0] - 1
    counts = csum[-1]
    starts = jnp.cumsum(counts) - counts
    pcounts = (counts + MOE_BLK - 1) // MOE_BLK * MOE_BLK
    pends = jnp.cumsum(pcounts)
    pstarts = pends - pcounts
    dest = pstarts[e_flat] + rank
    n_blk = n_rows // MOE_BLK + N_EXPERTS
    blk_start = jnp.arange(n_blk, dtype=jnp.int32) * MOE_BLK
    blk_exp = jnp.minimum(jnp.sum((blk_start[:, None] >= pends[None, :]).astype(jnp.int32), axis=1), N_EXPERTS - 1)
    order = jnp.argsort(e_flat)
    pos = jnp.arange(n_blk * MOE_BLK, dtype=jnp.int32)
    e_pos = jnp.repeat(blk_exp, MOE_BLK)
    src = jnp.clip(pos - pstarts[e_pos] + starts[e_pos], 0, n_rows - 1)
    row_tok = order[src].astype(jnp.int32) // TOP_K
    meta = jnp.concatenate([blk_exp, (pends[-1:] // MOE_BLK).astype(jnp.int32)])
    return meta, row_tok, gates, dest.reshape(n_tok, TOP_K).T.reshape(-1)


def _combine_kernel(x_ref, yg_ref, gate_ref, mod_ref, o_ref):
    gate = gate_ref[...]
    acc = gate[:, 0:1] * yg_ref[0].astype(F32)
    for j in range(1, TOP_K):
        acc = acc + gate[:, j:j + 1] * yg_ref[j].astype(F32)
    o_ref[...] = x_ref[...] + mod_ref[0, 5:6, :] * acc


def _combine(x1, yg, gates, mods, tile_mod):
    n_tok = x1.shape[0]
    return pl.pallas_call(
        _combine_kernel,
        grid=(n_tok // TOK_TILE,),
        in_specs=[
            pl.BlockSpec((TOK_TILE, D_MODEL), lambda i: (i, 0)),
            pl.BlockSpec((TOP_K, TOK_TILE, D_MODEL), lambda i: (0, i, 0)),
            pl.BlockSpec((TOK_TILE, TOP_K), lambda i: (i, 0)),
            pl.BlockSpec((1, 6, D_MODEL), lambda i: (tile_mod(i), 0, 0)),
        ],
        out_specs=pl.BlockSpec((TOK_TILE, D_MODEL), lambda i: (i, 0)),
        out_shape=jax.ShapeDtypeStruct((n_tok, D_MODEL), F32),
        compiler_params=_cparams(("parallel",)),
        name="moe_combine",
    )(x1, yg, gates, mods)


def kernel(x_prompt, x_sample, c, cache_na_k, cache_na_v, cache_swa_k, cache_swa_v, state_rwkv, c_ctx, w_ada, b_ada, norm1_g, norm2_g, w_in, w_out, na_q_norm, na_k_norm, na_rpb, swa_q_norm, swa_k_norm, swa_sink, rk_conv, rk_w0, rk_w2, rk_a0, rk_a2, rk_g2, rk_k_k, rk_k_a, rk_r_k, rk_ln_g, rk_ln_b, moe_router_w, moe_router_b, moe_w1, moe_b1, moe_w2, moe_b2):
    bc, tc, _ = x_prompt.shape
    bl, tl, _ = x_sample.shape
    depth = w_in.shape[0]
    n_ctx = bc * tc
    n_lat = bl * tl
    assert tc == TOK_TILE and tl % TOK_TILE == 0
    n_ctx_tiles = n_ctx // TOK_TILE
    tiles_per_seq = tl // TOK_TILE
    past = cache_na_k.shape[2]

    def tile_mod(i):
        return jnp.where(i < n_ctx_tiles, 0, 1 + (i - n_ctx_tiles) // tiles_per_seq)

    def tile_rope(i):
        return jnp.where(i < n_ctx_tiles, tiles_per_seq, (i - n_ctx_tiles) % tiles_per_seq)

    x = jnp.concatenate([x_prompt.reshape(n_ctx, D_MODEL), x_sample.reshape(n_lat, D_MODEL)], axis=0)

    n_mod = 1 + bl
    mod_rows = -(-n_mod // 8) * 8
    cvecs = jnp.concatenate([c_ctx[None, :], c, jnp.zeros((mod_rows - n_mod, D_MODEL), F32)], axis=0)
    mods_all = _ada_mod(cvecs, w_ada, b_ada).reshape(depth, mod_rows, 6, D_MODEL)
    cos_tab, sin_tab = _rope_tables(tl)
    tile2 = lambda g: jnp.concatenate([g, g])[None, :]
    pad_lanes = lambda z: jnp.pad(z, ((0, 0), (0, LANES - z.shape[1])))
    zeros_lora = jnp.zeros((2, RK_DECAY_LORA, RK_WIDTH), F32)

    na_k_l, na_v_l, sw_k_l, sw_v_l, st_l = [], [], [], [], []
    for l in range(depth):
        mods = mods_all[l]
        qk_gains = jnp.concatenate(
            [tile2(na_q_norm[l]), tile2(na_k_norm[l]), tile2(swa_q_norm[l]), tile2(swa_k_norm[l])], axis=0)
        p = {
            "rk_conv": rk_conv[l], "rk_w0": rk_w0[l], "rk_a0": rk_a0[l], "rk_g2": rk_g2[l],
            "rk_w2_pad": jnp.concatenate([rk_w2[l], zeros_lora], axis=1),
            "rk_a2_pad": jnp.concatenate([zeros_lora, rk_a2[l]], axis=1),
            "rk_k_k": rk_k_k[l][None, :], "rk_k_a": rk_k_a[l][None, :],
            "rk_r_k": rk_r_k[l].reshape(1, RK_WIDTH),
            "rk_ln_g": rk_ln_g[l][None, :], "rk_ln_b": rk_ln_b[l][None, :],
            "w_out_bf16": w_out[l].astype(BF16), "norm2_g": norm2_g[l][None, :],
            "router_w_pad": pad_lanes(moe_router_w[l]), "router_b_pad": pad_lanes(moe_router_b[l][None, :]),
        }

        att, u = _in_proj(x, norm1_g[l][None, :], mods, w_in[l].astype(BF16), qk_gains, cos_tab, sin_tab,
                          tile_mod, tile_rope)
        q, g, bonus = _rk_prep(u, p, n_ctx_tiles, tiles_per_seq)
        xs_c = _scan_operands(q[:n_ctx], bc, tc)
        xs_l = _scan_operands(q[n_ctx:], bl, tl)
        att_c = att[:n_ctx].reshape(bc, tc, ATT_COLS)
        att_l = att[n_ctx:].reshape(bl, tl, ATT_COLS)
        na_k_l.append(att_c[:, :, NA_WIDTH:2 * NA_WIDTH].reshape(bc, tc, NA_HEADS, HEAD_DIM))
        na_v_l.append(att_c[:, :, 2 * NA_WIDTH:NA_COLS].reshape(bc, tc, NA_HEADS, HEAD_DIM))
        sw_k_l.append(att_c[:, :, NA_COLS + SWA_WIDTH:NA_COLS + SWA_WIDTH + SWA_KV_WIDTH]
                      .reshape(bc, tc, SWA_KV_HEADS, HEAD_DIM))
        sw_v_l.append(att_c[:, :, NA_COLS + SWA_WIDTH + SWA_KV_WIDTH:].reshape(bc, tc, SWA_KV_HEADS, HEAD_DIM))

        sink = swa_sink[l]
        o_na = jnp.concatenate([
            _ctx_attn(att_c, sink, gqa=False).reshape(n_ctx, NA_WIDTH),
            _na_latent(att_l, cache_na_k[:, l].reshape(bl, past, NA_WIDTH),
                       cache_na_v[:, l].reshape(bl, past, NA_WIDTH),
                       _na_bias_tables(na_rpb[l])).reshape(n_lat, NA_WIDTH)], axis=0)
        o_sw = jnp.concatenate([
            _ctx_attn(att_c, sink, gqa=True).reshape(n_ctx, SWA_WIDTH),
            _swa_latent(att_l, cache_swa_k[:, l].reshape(bl, past, SWA_KV_WIDTH),
                        cache_swa_v[:, l].reshape(bl, past, SWA_KV_WIDTH), sink).reshape(n_lat, SWA_WIDTH)],
            axis=0)

        y_c, s_fin = _rwkv_scan_group(xs_c, bc, tc, None)
        y_l, _ = _rwkv_scan_group(xs_l, bl, tl, jnp.moveaxis(state_rwkv[:, l], 1, 0))
        st_l.append(jnp.moveaxis(s_fin, 0, 1))
        y = jnp.concatenate([y_c, y_l], axis=0)

        x1, h2, logits = _out_proj(x, o_na, o_sw, y, bonus, g, p, mods, tile_mod)
        meta, row_tok, gates, dest = _route(logits)
        h2 = jnp.concatenate([h2, jnp.zeros((H2_PAD_ROWS - h2.shape[0], D_MODEL), BF16)], axis=0)
        yb = _moe_blocks(meta, h2[row_tok], moe_w1, moe_b1[:, :, None, :], moe_w2, moe_b2[:, :, None, :], l)
        x = _combine(x1, yb[dest].reshape(TOP_K, n_ctx + n_lat, D_MODEL), gates, mods, tile_mod)

    y_p = x[:n_ctx].reshape(bc, tc, D_MODEL)
    y_s = x[n_ctx:].reshape(bl, tl, D_MODEL)
    return (y_p, y_s, jnp.stack(na_k_l, axis=1), jnp.stack(na_v_l, axis=1), jnp.stack(sw_k_l, axis=1),
            jnp.stack(sw_v_l, axis=1), jnp.stack(st_l, axis=1))
```

```python
import functools

import jax
import jax.numpy as jnp
from jax import lax
from jax.experimental import pallas as pl
from jax.experimental.pallas import tpu as pltpu

F32 = jnp.float32
BF16 = jnp.bfloat16

D_MODEL = 1024
HEAD_DIM = 64
LANES = 128
GRID_W = 64
NA_HEADS = 6
SWA_HEADS = 4
SWA_KV_HEADS = 2
RK_HEADS = 6
NA_WIDTH = NA_HEADS * HEAD_DIM
SWA_WIDTH = SWA_HEADS * HEAD_DIM
SWA_KV_WIDTH = SWA_KV_HEADS * HEAD_DIM
RK_WIDTH = RK_HEADS * HEAD_DIM
RK_DECAY_LORA = 64
RK_A_LORA = 64
RK_GATE_LORA = 128
RK_COLS = 3 * RK_WIDTH + RK_DECAY_LORA + RK_A_LORA + RK_GATE_LORA
NA_COLS = 3 * NA_WIDTH
SWA_COLS = SWA_WIDTH + 2 * SWA_KV_WIDTH
ATT_COLS = NA_COLS + SWA_COLS
IN_COLS = ATT_COLS + RK_COLS
NA_WIN_R = 8
NA_WIN_C = 16
SWA_WIN = 128
ROPE_THETA = 10000.0
ATTN_SCALE = HEAD_DIM ** -0.5
N_EXPERTS = 32
TOP_K = 4
SWIGLU_LIMIT = 7.0
SWIGLU_ALPHA = 1.702
MOE_BLK = 256
RMS_EPS = 1e-6
GN_EPS = 64e-5
NEG_BIG = -1e30

TOK_TILE = 256
VMEM_LIMIT = 48 * 1024 * 1024


def _cparams(sem):
    return pltpu.CompilerParams(dimension_semantics=sem, vmem_limit_bytes=VMEM_LIMIT)


def _dot(a, b):
    return jnp.dot(a, b, preferred_element_type=F32)


def _dot_nt(a, b):
    return lax.dot_general(a, b, (((1,), (1,)), ((), ())), preferred_element_type=F32)


def _split_bf16(x):
    hi = x.astype(BF16)
    lo = (x - hi.astype(F32)).astype(BF16)
    return hi, lo


def _dot3(a, b):
    ah, al = _split_bf16(a)
    bh, bl = _split_bf16(b)
    return _dot(ah, bh) + (_dot(ah, bl) + _dot(al, bh))


def _lane_lo(shape):
    return lax.broadcasted_iota(jnp.int32, shape, len(shape) - 1) < HEAD_DIM


def _pair_sum(x):
    lo = _lane_lo(x.shape)
    s_lo = jnp.sum(jnp.where(lo, x, 0.0), axis=-1, keepdims=True)
    s_hi = jnp.sum(jnp.where(lo, 0.0, x), axis=-1, keepdims=True)
    return jnp.where(lo, s_lo, s_hi)


def _stack_heads(q):
    lo = _lane_lo(q.shape)
    return jnp.concatenate([jnp.where(lo, q, 0.0), jnp.where(lo, 0.0, q)], axis=0)


def _unstack_heads(o2):
    n = o2.shape[0] // 2
    return jnp.where(_lane_lo((n, LANES)), o2[:n], o2[n:])


def _dup_head(x, j):
    keep = _lane_lo(x.shape) == (j == 0)
    return jnp.where(keep, x, pltpu.roll(x, HEAD_DIM, 1))


def _ada_kernel(c_ref, w_ref, b_ref, o_ref):
    cv = c_ref[...]
    s = cv * jax.nn.sigmoid(cv)
    o_ref[0] = _dot3(s, w_ref[0]) + b_ref[0]


def _ada_mod(cvecs, w_ada, b_ada):
    depth, _, n_out = w_ada.shape
    rows = cvecs.shape[0]
    tn = 1024
    return pl.pallas_call(
        _ada_kernel,
        grid=(depth, n_out // tn),
        in_specs=[
            pl.BlockSpec((rows, D_MODEL), lambda l, j: (0, 0)),
            pl.BlockSpec((1, D_MODEL, tn), lambda l, j: (l, 0, j)),
            pl.BlockSpec((1, 1, tn), lambda l, j: (l, 0, j)),
        ],
        out_specs=pl.BlockSpec((1, rows, tn), lambda l, j: (l, 0, j)),
        out_shape=jax.ShapeDtypeStruct((depth, rows, n_out), F32),
        compiler_params=_cparams(("parallel", "parallel")),
        name="ada_mod",
    )(cvecs, w_ada, b_ada.reshape(depth, 1, n_out))


NA_QK_BLOCKS = 2 * NA_WIDTH // LANES
SWA_Q_BLOCK0 = NA_COLS // LANES
SWA_QK_BLOCKS = (SWA_WIDTH + SWA_KV_WIDTH) // LANES


def _in_proj_kernel(x_ref, g_ref, mod_ref, w_ref, qkg_ref, cos_ref, sin_ref, att_ref, u_ref):
    x = x_ref[...]
    y = x * lax.rsqrt(jnp.mean(x * x, axis=-1, keepdims=True) + RMS_EPS)
    h = (y * g_ref[...]) * (1.0 + mod_ref[0, 1:2, :]) + mod_ref[0, 0:1, :]
    proj = _dot(h.astype(BF16), w_ref[...])
    u_ref[...] = proj[:, ATT_COLS:]

    def qk_norm(blk, gain):
        ms = _pair_sum(blk * blk) * (1.0 / HEAD_DIM)
        return blk * lax.rsqrt(ms + RMS_EPS) * gain

    lane = lax.broadcasted_iota(jnp.int32, (x.shape[0], LANES), 1)
    first = (lane % (HEAD_DIM // 2)) < (HEAD_DIM // 4)
    for cb in range(ATT_COLS // LANES):
        blk = proj[:, cb * LANES:(cb + 1) * LANES]
        if cb < NA_QK_BLOCKS:
            gi = 0 if cb < NA_QK_BLOCKS // 2 else 1
            blk = qk_norm(blk, qkg_ref[gi:gi + 1, :])
        elif SWA_Q_BLOCK0 <= cb < SWA_Q_BLOCK0 + SWA_QK_BLOCKS:
            gi = 2 if cb < SWA_Q_BLOCK0 + SWA_WIDTH // LANES else 3
            blk = qk_norm(blk, qkg_ref[gi:gi + 1, :])
            partner = jnp.where(first, pltpu.roll(blk, LANES - HEAD_DIM // 4, 1),
                                pltpu.roll(blk, HEAD_DIM // 4, 1))
            blk = blk * cos_ref[...] + partner * sin_ref[...]
        att_ref[:, cb * LANES:(cb + 1) * LANES] = blk


def _in_proj(x, norm_g, mods, w_in_bf16, qk_gains, cos_tab, sin_tab, tile_mod, tile_rope):
    n_tok = x.shape[0]
    return pl.pallas_call(
        _in_proj_kernel,
        grid=(n_tok // TOK_TILE,),
        in_specs=[
            pl.BlockSpec((TOK_TILE, D_MODEL), lambda i: (i, 0)),
            pl.BlockSpec((1, D_MODEL), lambda i: (0, 0)),
            pl.BlockSpec((1, 6, D_MODEL), lambda i: (tile_mod(i), 0, 0)),
            pl.BlockSpec((D_MODEL, IN_COLS), lambda i: (0, 0)),
            pl.BlockSpec((4, LANES), lambda i: (0, 0)),
            pl.BlockSpec((TOK_TILE, LANES), lambda i: (tile_rope(i), 0)),
            pl.BlockSpec((TOK_TILE, LANES), lambda i: (tile_rope(i), 0)),
        ],
        out_specs=[
            pl.BlockSpec((TOK_TILE, ATT_COLS), lambda i: (i, 0)),
            pl.BlockSpec((TOK_TILE, RK_COLS), lambda i: (i, 0)),
        ],
        out_shape=[
            jax.ShapeDtypeStruct((n_tok, ATT_COLS), F32),
            jax.ShapeDtypeStruct((n_tok, RK_COLS), F32),
        ],
        compiler_params=_cparams(("parallel",)),
        name="in_proj",
    )(x, norm_g, mods, w_in_bf16, qk_gains, cos_tab, sin_tab)


def _rope_tables(n_lat):
    nf = HEAD_DIM // 4
    t = jnp.arange(n_lat)
    lane = jnp.arange(LANES)
    d = lane % HEAD_DIM
    inv = ROPE_THETA ** (-(d % nf).astype(F32) / nf)
    pos = jnp.where((d // (2 * nf))[None, :] == 0, (t // GRID_W)[:, None], (t % GRID_W)[:, None]).astype(F32)
    ang = pos * inv[None, :]
    sign = jnp.where((d % (2 * nf)) < nf, -1.0, 1.0).astype(F32)
    cos = jnp.concatenate([jnp.cos(ang), jnp.ones((TOK_TILE, LANES), F32)], 0)
    sin = jnp.concatenate([jnp.sin(ang) * sign[None, :], jnp.zeros((TOK_TILE, LANES), F32)], 0)
    return cos, sin


def _ctx_attn_kernel(sink_ref, q_ref, k_ref, v_ref, o_ref, *, gqa):
    j = pl.program_id(1)
    k = k_ref[0]
    v = v_ref[0]
    if gqa:
        k = _dup_head(k, j)
        v = _dup_head(v, j)
    n = k.shape[0]
    q2 = _stack_heads(q_ref[0]).astype(BF16)
    s = _dot_nt(q2, k.astype(BF16)) * ATTN_SCALE
    m = jnp.max(s, axis=-1, keepdims=True)
    if gqa:
        row = lax.broadcasted_iota(jnp.int32, (2 * n, 1), 0)
        snk = jnp.where(row < n, sink_ref[2 * j], sink_ref[2 * j + 1])
        m = jnp.maximum(m, snk)
    p = jnp.exp(s - m)
    den = jnp.sum(p, axis=-1, keepdims=True)
    if gqa:
        den = den + jnp.exp(snk - m)
    o2 = _dot(p.astype(BF16), v.astype(BF16)) / den
    o_ref[0] = _unstack_heads(o2)


def _ctx_attn(att, sink, *, gqa):
    b, t, _ = att.shape
    if gqa:
        nq = SWA_WIDTH // LANES
        qb, kb, vb = SWA_Q_BLOCK0, SWA_Q_BLOCK0 + nq, SWA_Q_BLOCK0 + nq + 1
        kmap = lambda bi, j: (bi, 0, kb)
        vmap = lambda bi, j: (bi, 0, vb)
    else:
        nq = NA_WIDTH // LANES
        qb, kb, vb = 0, nq, 2 * nq
        kmap = lambda bi, j: (bi, 0, kb + j)
        vmap = lambda bi, j: (bi, 0, vb + j)
    return pl.pallas_call(
        functools.partial(_ctx_attn_kernel, gqa=gqa),
        grid=(b, nq),
        in_specs=[
            pl.BlockSpec(memory_space=pltpu.SMEM),
            pl.BlockSpec((1, t, LANES), lambda bi, j: (bi, 0, qb + j)),
            pl.BlockSpec((1, t, LANES), kmap),
            pl.BlockSpec((1, t, LANES), vmap),
        ],
        out_specs=pl.BlockSpec((1, t, LANES), lambda bi, j: (bi, 0, j)),
        out_shape=jax.ShapeDtypeStruct((b, t, nq * LANES), F32),
        compiler_params=_cparams(("parallel", "parallel")),
        name="ctx_attn_swa" if gqa else "ctx_attn_na",
    )(sink, att, att, att)


def _na_lat_kernel(q_ref, k_ref, v_ref, kc_ref, vc_ref, tab_ref, o_ref, kb_ref, vb_ref):
    n = q_ref.shape[1]
    rows = n // GRID_W
    win = NA_WIN_R * GRID_W
    kb_ref[...] = k_ref[0].astype(BF16)
    vb_ref[...] = v_ref[0].astype(BF16)
    kc = kc_ref[0].astype(BF16)
    vc = vc_ref[0].astype(BF16)

    def row_block(i, carry):
        start = jnp.clip(i - NA_WIN_R // 2, 0, rows - NA_WIN_R)
        rb = start - i + (NA_WIN_R - 1)
        q0 = pl.multiple_of(i * GRID_W, GRID_W)
        k0 = pl.multiple_of(start * GRID_W, GRID_W)
        q2 = _stack_heads(q_ref[0, pl.ds(q0, GRID_W), :]).astype(BF16)
        kw = kb_ref[pl.ds(k0, win), :]
        vw = vb_ref[pl.ds(k0, win), :]
        s_loc = _dot_nt(q2, kw) * ATTN_SCALE + tab_ref[0, rb]
        s_ctx = _dot_nt(q2, kc) * ATTN_SCALE
        m = jnp.maximum(jnp.max(s_loc, axis=-1, keepdims=True), jnp.max(s_ctx, axis=-1, keepdims=True))
        p_loc = jnp.exp(s_loc - m)
        p_ctx = jnp.exp(s_ctx - m)
        den = jnp.sum(p_loc, axis=-1, keepdims=True) + jnp.sum(p_ctx, axis=-1, keepdims=True)
        o2 = (_dot(p_loc.astype(BF16), vw) + _dot(p_ctx.astype(BF16), vc)) / den
        o_ref[0, pl.ds(q0, GRID_W), :] = _unstack_heads(o2)
        return carry

    lax.fori_loop(0, rows, row_block, 0)


def _na_bias_tables(rpb):
    col = jnp.arange(GRID_W)
    cstart = jnp.clip(col - NA_WIN_C // 2, 0, GRID_W - NA_WIN_C)
    col_mask = (col[None, :] >= cstart[:, None]) & (col[None, :] < cstart[:, None] + NA_WIN_C)
    col_idx = jnp.clip(col[None, :] - col[:, None] + NA_WIN_C - 1, 0, 2 * NA_WIN_C - 2)
    rpb_cols = jnp.where(col_mask[None, None], rpb[:, :, col_idx], NEG_BIG)
    roff = jnp.arange(NA_WIN_R)[:, None] + jnp.arange(NA_WIN_R)[None, :]
    t = rpb_cols[:, roff]
    t = jnp.transpose(t, (0, 1, 3, 2, 4)).reshape(NA_HEADS // 2, 2, NA_WIN_R, GRID_W, NA_WIN_R * GRID_W)
    return jnp.transpose(t, (0, 2, 1, 3, 4)).reshape(NA_HEADS // 2, NA_WIN_R, 2 * GRID_W, NA_WIN_R * GRID_W)


def _na_latent(att, kc, vc, tab):
    b, n, _ = att.shape
    p = kc.shape[1]
    nq = NA_WIDTH // LANES
    return pl.pallas_call(
        _na_lat_kernel,
        grid=(b, nq),
        in_specs=[
            pl.BlockSpec((1, n, LANES), lambda bi, j: (bi, 0, j)),
            pl.BlockSpec((1, n, LANES), lambda bi, j: (bi, 0, nq + j)),
            pl.BlockSpec((1, n, LANES), lambda bi, j: (bi, 0, 2 * nq + j)),
            pl.BlockSpec((1, p, LANES), lambda bi, j: (bi, 0, j)),
            pl.BlockSpec((1, p, LANES), lambda bi, j: (bi, 0, j)),
            pl.BlockSpec((1, NA_WIN_R, 2 * GRID_W, NA_WIN_R * GRID_W), lambda bi, j: (j, 0, 0, 0)),
        ],
        out_specs=pl.BlockSpec((1, n, LANES), lambda bi, j: (bi, 0, j)),
        out_shape=jax.ShapeDtypeStruct((b, n, NA_WIDTH), F32),
        scratch_shapes=[pltpu.VMEM((n, LANES), BF16), pltpu.VMEM((n, LANES), BF16)],
        compiler_params=_cparams(("parallel", "parallel")),
        name="na_latent",
    )(att, att, att, kc, vc, tab)


def _swa_lat_kernel(sink_ref, q_ref, k_ref, v_ref, kc_ref, vc_ref, o_ref, kb_ref, vb_ref):
    j = pl.program_id(1)
    n = q_ref.shape[1]
    blk = SWA_WIN
    span = 3 * blk
    kb_ref[...] = _dup_head(k_ref[0], j).astype(BF16)
    vb_ref[...] = _dup_head(v_ref[0], j).astype(BF16)
    kc = _dup_head(kc_ref[0], j).astype(BF16)
    vc = _dup_head(vc_ref[0], j).astype(BF16)
    row = lax.broadcasted_iota(jnp.int32, (2 * blk, 1), 0)
    snk = jnp.where(row < blk, sink_ref[2 * j], sink_ref[2 * j + 1])
    qoff = lax.broadcasted_iota(jnp.int32, (2 * blk, span), 0) % blk
    koff = lax.broadcasted_iota(jnp.int32, (2 * blk, span), 1)

    def q_block(qi, carry):
        q0 = pl.multiple_of(qi * blk, blk)
        w0 = pl.multiple_of(jnp.clip(q0 - blk, 0, n - span), blk)
        q2 = _stack_heads(q_ref[0, pl.ds(q0, blk), :]).astype(BF16)
        kw = kb_ref[pl.ds(w0, span), :]
        vw = vb_ref[pl.ds(w0, span), :]
        valid = jnp.abs((q0 + qoff) - (w0 + koff)) <= SWA_WIN
        s_loc = jnp.where(valid, _dot_nt(q2, kw) * ATTN_SCALE, NEG_BIG)
        s_ctx = _dot_nt(q2, kc) * ATTN_SCALE
        m = jnp.maximum(jnp.max(s_loc, axis=-1, keepdims=True), jnp.max(s_ctx, axis=-1, keepdims=True))
        m = jnp.maximum(m, snk)
        p_loc = jnp.exp(s_loc - m)
        p_ctx = jnp.exp(s_ctx - m)
        den = (jnp.sum(p_loc, axis=-1, keepdims=True) + jnp.sum(p_ctx, axis=-1, keepdims=True)
               + jnp.exp(snk - m))
        o2 = (_dot(p_loc.astype(BF16), vw) + _dot(p_ctx.astype(BF16), vc)) / den
        o_ref[0, pl.ds(q0, blk), :] = _unstack_heads(o2)
        return carry

    lax.fori_loop(0, n // blk, q_block, 0)


def _swa_latent(att, kc, vc, sink):
    b, n, _ = att.shape
    p = kc.shape[1]
    nq = SWA_WIDTH // LANES
    qb, kb, vb = SWA_Q_BLOCK0, SWA_Q_BLOCK0 + nq, SWA_Q_BLOCK0 + nq + 1
    return pl.pallas_call(
        _swa_lat_kernel,
        grid=(b, nq),
        in_specs=[
            pl.BlockSpec(memory_space=pltpu.SMEM),
            pl.BlockSpec((1, n, LANES), lambda bi, j: (bi, 0, qb + j)),
            pl.BlockSpec((1, n, LANES), lambda bi, j: (bi, 0, kb)),
            pl.BlockSpec((1, n, LANES), lambda bi, j: (bi, 0, vb)),
            pl.BlockSpec((1, p, LANES), lambda bi, j: (bi, 0, 0)),
            pl.BlockSpec((1, p, LANES), lambda bi, j: (bi, 0, 0)),
        ],
        out_specs=pl.BlockSpec((1, n, LANES), lambda bi, j: (bi, 0, j)),
        out_shape=jax.ShapeDtypeStruct((b, n, SWA_WIDTH), F32),
        scratch_shapes=[pltpu.VMEM((n, LANES), BF16), pltpu.VMEM((n, LANES), BF16)],
        compiler_params=_cparams(("parallel", "parallel")),
        name="swa_latent",
    )(sink, att, att, att, kc, vc)


RK_NB = RK_WIDTH // LANES
LORA_BLOCK = 3 * RK_WIDTH // LANES
GATE_BLOCK = LORA_BLOCK + 1
Q_R, Q_V, Q_A, Q_W, Q_K, Q_B = range(6)
Q_DIR = 3
Q_COLS = (6 + Q_DIR) * RK_WIDTH


def _softplus(x):
    return jnp.maximum(x, 0.0) + jnp.log(1.0 + jnp.exp(-jnp.abs(x)))


def _rk_prep_kernel(u_ref, up_ref, un_ref, cw_ref, w0_ref, w2_ref, a0_ref, a2_ref, g2_ref, kk_ref, ka_ref,
                    rk_ref, q_ref, g_ref, bonus_ref, *, n_ctx_tiles, tiles_per_seq):
    def put(slot, val):
        q_ref[:, slot * RK_WIDTH:(slot + 1) * RK_WIDTH] = val

    i = pl.program_id(0)
    li = i - n_ctx_tiles
    is_lat = i >= n_ctx_tiles
    has_prev = jnp.logical_and(is_lat, li % tiles_per_seq != 0)
    has_next = jnp.logical_and(is_lat, li % tiles_per_seq != tiles_per_seq - 1)
    u = u_ref[...]
    tm = u.shape[0]
    prev_row = jnp.where(has_prev, up_ref[7:8, :], 0.0)
    next_row = jnp.where(has_next, un_ref[0:1, :], 0.0)
    row = lax.broadcasted_iota(jnp.int32, u.shape, 0)
    um = jnp.where(row == 0, prev_row, pltpu.roll(u, 1, 0))
    up = jnp.where(row == tm - 1, next_row, pltpu.roll(u, tm - 1, 0))
    u = um * cw_ref[0:1, :] + u * cw_ref[1:2, :] + up * cw_ref[2:3, :]

    r = u[:, 0:RK_WIDTH]
    k = u[:, RK_WIDTH:2 * RK_WIDTH]
    v = u[:, 2 * RK_WIDTH:3 * RK_WIDTH]
    lora = u[:, LORA_BLOCK * LANES:(LORA_BLOCK + 1) * LANES]
    gl = u[:, GATE_BLOCK * LANES:(GATE_BLOCK + 1) * LANES]
    put(Q_R, r)
    put(Q_V, v)
    g_ref[...] = _dot3(jax.nn.sigmoid(gl), g2_ref[...])

    kn = k * kk_ref[...]
    kk = jnp.concatenate(
        [kn[:, c * LANES:(c + 1) * LANES]
         * lax.rsqrt(jnp.maximum(_pair_sum(jnp.square(kn[:, c * LANES:(c + 1) * LANES])), 1e-24))
         for c in range(RK_NB)], axis=1)
    put(Q_A, -kk)

    lora_t = jnp.tanh(lora)
    kd_sum = None
    for d in range(2):
        w = -_softplus(-(w0_ref[d:d + 1, :] + _dot3(lora_t, w2_ref[d]))) - 0.5
        put(Q_W + Q_DIR * d, -jnp.exp(w))
        a = jax.nn.sigmoid(a0_ref[d:d + 1, :] + _dot3(lora, a2_ref[d]))
        kd = k * (1.0 + (a - 1.0) * ka_ref[...])
        put(Q_K + Q_DIR * d, kd)
        put(Q_B + Q_DIR * d, kk * a)
        kd_sum = kd if kd_sum is None else kd_sum + kd

    t = r * kd_sum * rk_ref[...]
    bonus_ref[...] = jnp.concatenate(
        [_pair_sum(t[:, c * LANES:(c + 1) * LANES]) for c in range(RK_NB)], axis=1) * v


def _rk_prep(u, p, n_ctx_tiles, tiles_per_seq):
    n_tok = u.shape[0]
    n_tiles = n_tok // TOK_TILE
    sub = TOK_TILE // 8
    last8 = n_tok // 8 - 1
    tok = lambda i: (i, 0)
    const2 = lambda i: (0, 0)
    const3 = lambda i: (0, 0, 0)
    one = jax.ShapeDtypeStruct((n_tok, RK_WIDTH), F32)
    tok_spec = pl.BlockSpec((TOK_TILE, RK_WIDTH), tok)
    return pl.pallas_call(
        functools.partial(_rk_prep_kernel, n_ctx_tiles=n_ctx_tiles, tiles_per_seq=tiles_per_seq),
        grid=(n_tiles,),
        in_specs=[
            pl.BlockSpec((TOK_TILE, RK_COLS), tok),
            pl.BlockSpec((8, RK_COLS), lambda i: (jnp.maximum(i * sub - 1, 0), 0)),
            pl.BlockSpec((8, RK_COLS), lambda i: (jnp.minimum((i + 1) * sub, last8), 0)),
            pl.BlockSpec((3, RK_COLS), const2),
            pl.BlockSpec((2, RK_WIDTH), const2),
            pl.BlockSpec((2, LANES, RK_WIDTH), const3),
            pl.BlockSpec((2, RK_WIDTH), const2),
            pl.BlockSpec((2, LANES, RK_WIDTH), const3),
            pl.BlockSpec((RK_GATE_LORA, RK_WIDTH), const2),
            pl.BlockSpec((1, RK_WIDTH), const2),
            pl.BlockSpec((1, RK_WIDTH), const2),
            pl.BlockSpec((1, RK_WIDTH), const2),
        ],
        out_specs=[pl.BlockSpec((TOK_TILE, Q_COLS), tok), tok_spec, tok_spec],
        out_shape=[jax.ShapeDtypeStruct((n_tok, Q_COLS), F32), one, one],
        compiler_params=_cparams(("parallel",)),
        name="rk_prep",
    )(u, u, u, p["rk_conv"], p["rk_w0"], p["rk_w2_pad"], p["rk_a0"], p["rk_a2_pad"], p["rk_g2"],
      p["rk_k_k"], p["rk_k_a"], p["rk_r_k"])


RK_CHUNK = 64
PAIR = 2 * HEAD_DIM
STATE_SEQS = 8


def _split3_bf16(x):
    hi = x.astype(BF16)
    r1 = x - hi.astype(F32)
    mid = r1.astype(BF16)
    return hi, mid, (r1 - mid.astype(F32)).astype(BF16)


def _bdot(a, b):
    return _dot(a.astype(BF16), b.astype(BF16))


def _bmm(a, b):
    return lax.dot_general(a.astype(BF16), b.astype(BF16), (((2,), (1,)), ((0,), (0,))),
                           preferred_element_type=F32)


def _bmm_nt(a, b):
    return lax.dot_general(a.astype(BF16), b.astype(BF16), (((2,), (2,)), ((0,), (0,))),
                           preferred_element_type=F32)


def _stack_heads3(x):
    lo = _lane_lo(x.shape)
    return jnp.concatenate([jnp.where(lo, x, 0.0), jnp.where(lo, 0.0, x)], axis=1)


def _rk_chunk_kernel(q_ref, rbar_ref, ybar_ref, phi_ref, psi_ref):
    c = RK_CHUNK
    n = 2 * c
    nu = 2 * RK_NB

    def tiles(slot, per_dir):
        cols = [(slot + (Q_DIR * d if per_dir else 0)) * RK_WIDTH + p * LANES
                for d in range(2) for p in range(RK_NB)]
        return jnp.stack([q_ref[:, lo:lo + LANES] for lo in cols])

    r, v, a = tiles(Q_R, False), tiles(Q_V, False), tiles(Q_A, False)
    lw, k, b = tiles(Q_W, True), tiles(Q_K, True), tiles(Q_B, True)
    sgn = jnp.where(lax.broadcasted_iota(jnp.int32, (nu, 1, 1), 0) >= RK_NB, -1, 1)
    bwd = sgn < 0
    tdiff = lax.broadcasted_iota(jnp.int32, (1, c, c), 2) - lax.broadcasted_iota(jnp.int32, (1, c, c), 1)
    tri = jnp.where(tdiff * sgn <= 0, 1.0, 0.0)
    cum = sum(_bmm(tri, part) for part in _split3_bf16(lw))
    tot = jnp.where(bwd, cum[:, 0:1], cum[:, c - 1:c])
    a_t = a * jnp.exp(cum - lw)
    r_t = r * jnp.exp(cum)
    e_neg = jnp.exp(-cum)
    e_end = jnp.exp(tot - cum)
    g = _bmm_nt(jnp.concatenate([_stack_heads3(a_t), _stack_heads3(r_t)], axis=1),
                jnp.concatenate([_stack_heads3(b * e_neg), _stack_heads3(k * e_neg)], axis=1))
    r2 = lax.broadcasted_iota(jnp.int32, (1, n, n), 1)
    c2 = lax.broadcasted_iota(jnp.int32, (1, n, n), 2)
    order = (jnp.bitwise_and(c2, c - 1) - jnp.bitwise_and(r2, c - 1)) * sgn
    eye = jnp.where(r2 == c2, 1.0, 0.0)
    l_ab = jnp.where(order < 0, g[:, :n, :n], 0.0)
    l_ak = jnp.where(order < 0, g[:, :n, n:], 0.0)
    m_rb = jnp.where(order <= 0, g[:, n:, :n], 0.0)
    m_rk = jnp.where(order <= 0, g[:, n:, n:], 0.0)
    t_inv = eye + l_ab
    pw = l_ab
    for _ in range(5):
        pw = _bmm(pw, pw)
        t_inv = t_inv + _bmm(t_inv, pw)
    sv = _stack_heads3(v)
    au = _bmm(t_inv, jnp.concatenate([_stack_heads3(a_t), _bmm(l_ak, sv)], axis=2))
    ry = _bmm(m_rb, au) + jnp.concatenate([_stack_heads3(r_t), _bmm(m_rk, sv)], axis=2)
    ry = ry[:, :c] + ry[:, c:]
    bt = jnp.swapaxes(_stack_heads3(b * e_end), 1, 2)
    kt = jnp.swapaxes(_stack_heads3(k * e_end), 1, 2)
    pp = _bmm(bt, au)
    phi = eye * jnp.exp(tot) + pp[:, :, :PAIR]
    psi = pp[:, :, PAIR:] + _bmm(kt, sv)
    for d in range(2):
        for p in range(RK_NB):
            u = d * RK_NB + p
            rbar_ref[d, :, p * LANES:(p + 1) * LANES] = ry[u, :, :PAIR]
            ybar_ref[d, :, p * LANES:(p + 1) * LANES] = ry[u, :, PAIR:]
            phi_ref[d, p] = phi[u]
            psi_ref[d, p] = psi[u]


def _rk_chunk(q, tile0, n_seq, t):
    nc = t // RK_CHUNK
    row_sh = jax.ShapeDtypeStruct((2, n_seq, t, RK_WIDTH), F32)
    mat_sh = jax.ShapeDtypeStruct((2, n_seq, nc, RK_NB, PAIR, PAIR), F32)
    row_spec = pl.BlockSpec((2, None, RK_CHUNK, RK_WIDTH), lambda s, c: (0, s, c, 0))
    mat_spec = pl.BlockSpec((2, None, None, RK_NB, PAIR, PAIR), lambda s, c: (0, s, c, 0, 0, 0))
    return pl.pallas_call(
        _rk_chunk_kernel,
        grid=(n_seq, nc),
        in_specs=[pl.BlockSpec((RK_CHUNK, Q_COLS), lambda s, c: (tile0 + s * nc + c, 0))],
        out_specs=[row_spec, row_spec, mat_spec, mat_spec],
        out_shape=[row_sh, row_sh, mat_sh, mat_sh],
        compiler_params=_cparams(("parallel", "parallel")),
        name="rk_chunk",
    )(q)


def _rk_state_kernel(rf_ref, rb_ref, yf_ref, yb_ref, phf_ref, phb_ref, psf_ref, psb_ref, s0_ref,
                     of_ref, ob_ref, s_ref):
    @pl.when(pl.program_id(1) == 0)
    def _():
        s_ref[...] = s0_ref[...]

    dirs = ((rf_ref, yf_ref, phf_ref, psf_ref, of_ref), (rb_ref, yb_ref, phb_ref, psb_ref, ob_ref))

    def seq_body(s, carry):
        for d, (r_ref, y_ref, ph_ref, ps_ref, o_ref) in enumerate(dirs):
            for p in range(RK_NB):
                lanes = slice(p * LANES, (p + 1) * LANES)
                h = s_ref[s, d, p]
                o_ref[s, :, lanes] = _dot3(r_ref[s, :, lanes], h) + y_ref[s, :, lanes]
                s_ref[s, d, p] = _dot3(ph_ref[s, p], h) + ps_ref[s, p]
        return carry

    lax.fori_loop(0, s_ref.shape[0], seq_body, 0)


def _rk_state(rbar, ybar, phi, psi, s0):
    _, n_seq, t, _ = rbar.shape
    nc = t // RK_CHUNK
    sg = STATE_SEQS
    row_blk = (None, sg, RK_CHUNK, RK_WIDTH)
    mat_blk = (None, sg, None, RK_NB, PAIR, PAIR)
    fwd_row = pl.BlockSpec(row_blk, lambda g, c: (0, g, c, 0))
    bwd_row = pl.BlockSpec(row_blk, lambda g, c: (1, g, nc - 1 - c, 0))
    fwd_mat = pl.BlockSpec(mat_blk, lambda g, c: (0, g, c, 0, 0, 0))
    bwd_mat = pl.BlockSpec(mat_blk, lambda g, c: (1, g, nc - 1 - c, 0, 0, 0))
    st = pl.BlockSpec((sg, 2, RK_NB, PAIR, PAIR), lambda g, c: (g, 0, 0, 0, 0))
    out_sh = jax.ShapeDtypeStruct((n_seq, t, RK_WIDTH), F32)
    return pl.pallas_call(
        _rk_state_kernel,
        grid=(n_seq // sg, nc),
        in_specs=[fwd_row, bwd_row, fwd_row, bwd_row, fwd_mat, bwd_mat, fwd_mat, bwd_mat, st],
        out_specs=[pl.BlockSpec((sg, RK_CHUNK, RK_WIDTH), lambda g, c: (g, c, 0)),
                   pl.BlockSpec((sg, RK_CHUNK, RK_WIDTH), lambda g, c: (g, nc - 1 - c, 0)), st],
        out_shape=[out_sh, out_sh, jax.ShapeDtypeStruct((n_seq, 2, RK_NB, PAIR, PAIR), F32)],
        compiler_params=_cparams(("parallel", "arbitrary")),
        name="rk_state",
    )(rbar, rbar, ybar, ybar, phi, phi, psi, psi, s0)


def _pair_states(s):
    bsz = s.shape[0]
    h = jnp.swapaxes(s, -1, -2).reshape(bsz, 2, RK_NB, 2, HEAD_DIM, HEAD_DIM)
    return jnp.einsum("bdphkv,hg->bdphkgv", h, jnp.eye(2, dtype=F32)).reshape(bsz, 2, RK_NB, PAIR, PAIR)


def _head_states(s):
    bsz = s.shape[0]
    h = s.reshape(bsz, 2, RK_NB, 2, HEAD_DIM, 2, HEAD_DIM)
    h = jnp.stack([h[:, :, :, 0, :, 0, :], h[:, :, :, 1, :, 1, :]], axis=3)
    return jnp.swapaxes(h.reshape(bsz, 2, RK_HEADS, HEAD_DIM, HEAD_DIM), -1, -2)


def _rwkv_group(q, tile0, n_seq, t, s0):
    rbar, ybar, phi, psi = _rk_chunk(q, tile0, n_seq, t)
    y_f, y_b, s_fin = _rk_state(rbar, ybar, phi, psi, s0)
    return y_f.reshape(n_seq * t, RK_WIDTH), y_b.reshape(n_seq * t, RK_WIDTH), s_fin


def _out_proj_kernel(x_ref, ona_ref, osw_ref, yf_ref, yb_ref, bonus_ref, g_ref, lng_ref, lnb_ref, w_ref, mod_ref,
                     n2_ref, rw_ref, rb_ref, x1_ref, h2_ref, lg_ref):
    y = yf_ref[...] + yb_ref[...]
    parts = []
    for c in range(RK_NB):
        yc = y[:, c * LANES:(c + 1) * LANES]
        dc = yc - _pair_sum(yc) * (1.0 / HEAD_DIM)
        var = _pair_sum(dc * dc) * (1.0 / HEAD_DIM)
        parts.append(dc * lax.rsqrt(var + GN_EPS))
    yn = jnp.concatenate(parts, axis=1) * lng_ref[...] + lnb_ref[...]
    o_rk = (yn + bonus_ref[...]) * g_ref[...]
    o = (_dot(ona_ref[...].astype(BF16), w_ref[0:NA_WIDTH, :])
         + _dot(osw_ref[...].astype(BF16), w_ref[NA_WIDTH:NA_WIDTH + SWA_WIDTH, :])
         + _dot(o_rk.astype(BF16), w_ref[NA_WIDTH + SWA_WIDTH:, :]))
    x1 = x_ref[...] + mod_ref[0, 2:3, :] * o
    x1_ref[...] = x1
    yn2 = x1 * lax.rsqrt(jnp.mean(x1 * x1, axis=-1, keepdims=True) + RMS_EPS)
    h2 = (yn2 * n2_ref[...]) * (1.0 + mod_ref[0, 4:5, :]) + mod_ref[0, 3:4, :]
    h2_ref[...] = h2.astype(BF16)
    lg_ref[...] = (_dot3(h2, rw_ref[...]) + rb_ref[...])[:, :N_EXPERTS]


def _out_proj(x, o_na, o_sw, y_f, y_b, bonus, g, p, mods, tile_mod):
    n_tok = x.shape[0]
    tok = lambda i: (i, 0)
    const = lambda i: (0, 0)
    return pl.pallas_call(
        _out_proj_kernel,
        grid=(n_tok // TOK_TILE,),
        in_specs=[
            pl.BlockSpec((TOK_TILE, D_MODEL), tok),
            pl.BlockSpec((TOK_TILE, NA_WIDTH), tok),
            pl.BlockSpec((TOK_TILE, SWA_WIDTH), tok),
            pl.BlockSpec((TOK_TILE, RK_WIDTH), tok),
            pl.BlockSpec((TOK_TILE, RK_WIDTH), tok),
            pl.BlockSpec((TOK_TILE, RK_WIDTH), tok),
            pl.BlockSpec((TOK_TILE, RK_WIDTH), tok),
            pl.BlockSpec((1, RK_WIDTH), const),
            pl.BlockSpec((1, RK_WIDTH), const),
            pl.BlockSpec((D_MODEL, D_MODEL), const),
            pl.BlockSpec((1, 6, D_MODEL), lambda i: (tile_mod(i), 0, 0)),
            pl.BlockSpec((1, D_MODEL), const),
            pl.BlockSpec((D_MODEL, LANES), const),
            pl.BlockSpec((1, LANES), const),
        ],
        out_specs=[
            pl.BlockSpec((TOK_TILE, D_MODEL), tok),
            pl.BlockSpec((TOK_TILE, D_MODEL), tok),
            pl.BlockSpec((TOK_TILE, N_EXPERTS), tok),
        ],
        out_shape=[
            jax.ShapeDtypeStruct((n_tok, D_MODEL), F32),
            jax.ShapeDtypeStruct((n_tok, D_MODEL), BF16),
            jax.ShapeDtypeStruct((n_tok, N_EXPERTS), F32),
        ],
        compiler_params=_cparams(("parallel",)),
        name="out_proj",
    )(x, o_na, o_sw, y_f, y_b, bonus, g, p["rk_ln_g"], p["rk_ln_b"], p["w_out_bf16"], mods, p["norm2_g"],
      p["router_w_pad"], p["router_b_pad"])


H2_PAD_ROWS = 32768
W2_STAGE_ROWS = 128
MOE_VMEM_LIMIT = 56 * 1024 * 1024


def _moe_kernel(meta_ref, x_ref, w1_ref, b1_ref, w2_ref, b2_ref, o_ref, w1b_ref, w2e_ref, stage_ref):
    i = pl.program_id(0)
    n_blk = meta_ref.shape[0] - 1
    n_used = meta_ref[n_blk]
    d_e = w2_ref.shape[1]
    new_expert = jnp.logical_or(i == 0, meta_ref[i] != meta_ref[jnp.maximum(i - 1, 0)])

    @pl.when(i == 0)
    def _():
        stage_ref[...] = jnp.zeros_like(stage_ref)

    @pl.when(jnp.logical_and(i < n_used, new_expert))
    def _():
        w1b_ref[...] = w1_ref[0].astype(BF16)
        for c in range(d_e // W2_STAGE_ROWS):
            rows = slice(c * W2_STAGE_ROWS, (c + 1) * W2_STAGE_ROWS)
            for cb in range(D_MODEL // LANES):
                cols = slice(cb * LANES, (cb + 1) * LANES)
                stage_ref[cb, pl.ds(0, W2_STAGE_ROWS, stride=2), :] = w2_ref[0, rows, cols]
                w2e_ref[2 * c * W2_STAGE_ROWS:2 * (c + 1) * W2_STAGE_ROWS, cols] = stage_ref[cb].astype(BF16)

    @pl.when(i < n_used)
    def _():
        uu = _dot(x_ref[...], w1b_ref[...]) + b1_ref[0]
        acts = []
        for c in range(uu.shape[1] // LANES):
            blk = uu[:, c * LANES:(c + 1) * LANES]
            glu = jnp.minimum(blk, SWIGLU_LIMIT)
            lin = jnp.clip(pltpu.roll(blk, LANES - 1, 1), -SWIGLU_LIMIT, SWIGLU_LIMIT)
            acts.append((glu * jax.nn.sigmoid(SWIGLU_ALPHA * glu) * (lin + 1.0)).astype(BF16))
        act = jnp.concatenate(acts, axis=1)
        o_ref[...] = (_dot(act, w2e_ref[...]) + b2_ref[0]).astype(BF16)

    @pl.when(i >= n_used)
    def _():
        o_ref[...] = jnp.zeros_like(o_ref)


def _moe_blocks(meta, xb, w1, b1, w2, b2, layer):
    n_rows = xb.shape[0]
    n_blk = n_rows // MOE_BLK
    d_e = w2.shape[2]
    row = lambda i, m: (i, 0)
    exp3 = lambda i, m: (layer, m[i], 0, 0)
    grid_spec = pltpu.PrefetchScalarGridSpec(
        num_scalar_prefetch=1,
        grid=(n_blk,),
        in_specs=[
            pl.BlockSpec((MOE_BLK, D_MODEL), row),
            pl.BlockSpec((None, 1, D_MODEL, 2 * d_e), exp3),
            pl.BlockSpec((None, 1, 1, 2 * d_e), exp3),
            pl.BlockSpec((None, 1, d_e, D_MODEL), exp3),
            pl.BlockSpec((None, 1, 1, D_MODEL), exp3),
        ],
        out_specs=pl.BlockSpec((MOE_BLK, D_MODEL), row),
        scratch_shapes=[
            pltpu.VMEM((D_MODEL, 2 * d_e), BF16),
            pltpu.VMEM((2 * d_e, D_MODEL), BF16),
            pltpu.VMEM((D_MODEL // LANES, 2 * W2_STAGE_ROWS, LANES), F32),
        ],
    )
    return pl.pallas_call(
        _moe_kernel,
        grid_spec=grid_spec,
        out_shape=jax.ShapeDtypeStruct((n_rows, D_MODEL), BF16),
        compiler_params=pltpu.CompilerParams(dimension_semantics=("arbitrary",),
                                             vmem_limit_bytes=MOE_VMEM_LIMIT),
        name="moe_blocks",
    )(meta, xb, w1, b1, w2, b2)


def _route(logits):
    n_tok = logits.shape[0]
    top_v, top_i = lax.top_k(logits, TOP_K)
    gates = jax.nn.softmax(top_v, axis=-1)
    e_flat = top_i.reshape(-1).astype(jnp.int32)
    n_rows = n_tok * TOP_K
    onehot = (e_flat[:, None] == jnp.arange(N_EXPERTS, dtype=jnp.int32)[None, :]).astype(jnp.int32)
    csum = jnp.cumsum(onehot, axis=0)
    rank = jnp.take_along_axis(csum, e_flat[:, None], axis=1)[:, 0] - 1
    counts = csum[-1]
    starts = jnp.cumsum(counts) - counts
    pcounts = (counts + MOE_BLK - 1) // MOE_BLK * MOE_BLK
    pends = jnp.cumsum(pcounts)
    pstarts = pends - pcounts
    dest = pstarts[e_flat] + rank
    n_blk = n_rows // MOE_BLK + N_EXPERTS
    blk_start = jnp.arange(n_blk, dtype=jnp.int32) * MOE_BLK
    blk_exp = jnp.minimum(jnp.sum((blk_start[:, None] >= pends[None, :]).astype(jnp.int32), axis=1), N_EXPERTS - 1)
    order = jnp.argsort(e_flat)
    pos = jnp.arange(n_blk * MOE_BLK, dtype=jnp.int32)
    e_pos = jnp.repeat(blk_exp, MOE_BLK)
    src = jnp.clip(pos - pstarts[e_pos] + starts[e_pos], 0, n_rows - 1)
    row_tok = order[src].astype(jnp.int32) // TOP_K
    meta = jnp.concatenate([blk_exp, (pends[-1:] // MOE_BLK).astype(jnp.int32)])
    return meta, row_tok, gates, dest.reshape(n_tok, TOP_K).T.reshape(-1)


def _combine_kernel(x_ref, yg_ref, gate_ref, mod_ref, o_ref):
    gate = gate_ref[...]
    acc = gate[:, 0:1] * yg_ref[0].astype(F32)
    for j in range(1, TOP_K):
        acc = acc + gate[:, j:j + 1] * yg_ref[j].astype(F32)
    o_ref[...] = x_ref[...] + mod_ref[0, 5:6, :] * acc


def _combine(x1, yg, gates, mods, tile_mod):
    n_tok = x1.shape[0]
    return pl.pallas_call(
        _combine_kernel,
        grid=(n_tok // TOK_TILE,),
        in_specs=[
            pl.BlockSpec((TOK_TILE, D_MODEL), lambda i: (i, 0)),
            pl.BlockSpec((TOP_K, TOK_TILE, D_MODEL), lambda i: (0, i, 0)),
            pl.BlockSpec((TOK_TILE, TOP_K), lambda i: (i, 0)),
            pl.BlockSpec((1, 6, D_MODEL), lambda i: (tile_mod(i), 0, 0)),
        ],
        out_specs=pl.BlockSpec((TOK_TILE, D_MODEL), lambda i: (i, 0)),
        out_shape=jax.ShapeDtypeStruct((n_tok, D_MODEL), F32),
        compiler_params=_cparams(("parallel",)),
        name="moe_combine",
    )(x1, yg, gates, mods)


def kernel(x_prompt, x_sample, c, cache_na_k, cache_na_v, cache_swa_k, cache_swa_v, state_rwkv, c_ctx, w_ada, b_ada, norm1_g, norm2_g, w_in, w_out, na_q_norm, na_k_norm, na_rpb, swa_q_norm, swa_k_norm, swa_sink, rk_conv, rk_w0, rk_w2, rk_a0, rk_a2, rk_g2, rk_k_k, rk_k_a, rk_r_k, rk_ln_g, rk_ln_b, moe_router_w, moe_router_b, moe_w1, moe_b1, moe_w2, moe_b2):
    bc, tc, _ = x_prompt.shape
    bl, tl, _ = x_sample.shape
    depth = w_in.shape[0]
    n_ctx = bc * tc
    n_lat = bl * tl
    assert tc == TOK_TILE and tl % TOK_TILE == 0
    n_ctx_tiles = n_ctx // TOK_TILE
    tiles_per_seq = tl // TOK_TILE
    past = cache_na_k.shape[2]

    def tile_mod(i):
        return jnp.where(i < n_ctx_tiles, 0, 1 + (i - n_ctx_tiles) // tiles_per_seq)

    def tile_rope(i):
        return jnp.where(i < n_ctx_tiles, tiles_per_seq, (i - n_ctx_tiles) % tiles_per_seq)

    x = jnp.concatenate([x_prompt.reshape(n_ctx, D_MODEL), x_sample.reshape(n_lat, D_MODEL)], axis=0)

    n_mod = 1 + bl
    mod_rows = -(-n_mod // 8) * 8
    cvecs = jnp.concatenate([c_ctx[None, :], c, jnp.zeros((mod_rows - n_mod, D_MODEL), F32)], axis=0)
    mods_all = _ada_mod(cvecs, w_ada, b_ada).reshape(depth, mod_rows, 6, D_MODEL)
    cos_tab, sin_tab = _rope_tables(tl)
    tile2 = lambda g: jnp.concatenate([g, g])[None, :]
    pad_lanes = lambda z: jnp.pad(z, ((0, 0), (0, LANES - z.shape[1])))
    zeros_lora = jnp.zeros((2, RK_DECAY_LORA, RK_WIDTH), F32)

    na_k_l, na_v_l, sw_k_l, sw_v_l, st_l = [], [], [], [], []
    for l in range(depth):
        mods = mods_all[l]
        qk_gains = jnp.concatenate(
            [tile2(na_q_norm[l]), tile2(na_k_norm[l]), tile2(swa_q_norm[l]), tile2(swa_k_norm[l])], axis=0)
        p = {
            "rk_conv": rk_conv[l], "rk_w0": rk_w0[l], "rk_a0": rk_a0[l], "rk_g2": rk_g2[l],
            "rk_w2_pad": jnp.concatenate([rk_w2[l], zeros_lora], axis=1),
            "rk_a2_pad": jnp.concatenate([zeros_lora, rk_a2[l]], axis=1),
            "rk_k_k": rk_k_k[l][None, :], "rk_k_a": rk_k_a[l][None, :],
            "rk_r_k": rk_r_k[l].reshape(1, RK_WIDTH),
            "rk_ln_g": rk_ln_g[l][None, :], "rk_ln_b": rk_ln_b[l][None, :],
            "w_out_bf16": w_out[l].astype(BF16), "norm2_g": norm2_g[l][None, :],
            "router_w_pad": pad_lanes(moe_router_w[l]), "router_b_pad": pad_lanes(moe_router_b[l][None, :]),
        }

        att, u = _in_proj(x, norm1_g[l][None, :], mods, w_in[l].astype(BF16), qk_gains, cos_tab, sin_tab,
                          tile_mod, tile_rope)
        q, g, bonus = _rk_prep(u, p, n_ctx_tiles, tiles_per_seq)
        att_c = att[:n_ctx].reshape(bc, tc, ATT_COLS)
        att_l = att[n_ctx:].reshape(bl, tl, ATT_COLS)
        na_k_l.append(att_c[:, :, NA_WIDTH:2 * NA_WIDTH].reshape(bc, tc, NA_HEADS, HEAD_DIM))
        na_v_l.append(att_c[:, :, 2 * NA_WIDTH:NA_COLS].reshape(bc, tc, NA_HEADS, HEAD_DIM))
        sw_k_l.append(att_c[:, :, NA_COLS + SWA_WIDTH:NA_COLS + SWA_WIDTH + SWA_KV_WIDTH]
                      .reshape(bc, tc, SWA_KV_HEADS, HEAD_DIM))
        sw_v_l.append(att_c[:, :, NA_COLS + SWA_WIDTH + SWA_KV_WIDTH:].reshape(bc, tc, SWA_KV_HEADS, HEAD_DIM))

        sink = swa_sink[l]
        o_na = jnp.concatenate([
            _ctx_attn(att_c, sink, gqa=False).reshape(n_ctx, NA_WIDTH),
            _na_latent(att_l, cache_na_k[:, l].reshape(bl, past, NA_WIDTH),
                       cache_na_v[:, l].reshape(bl, past, NA_WIDTH),
                       _na_bias_tables(na_rpb[l])).reshape(n_lat, NA_WIDTH)], axis=0)
        o_sw = jnp.concatenate([
            _ctx_attn(att_c, sink, gqa=True).reshape(n_ctx, SWA_WIDTH),
            _swa_latent(att_l, cache_swa_k[:, l].reshape(bl, past, SWA_KV_WIDTH),
                        cache_swa_v[:, l].reshape(bl, past, SWA_KV_WIDTH), sink).reshape(n_lat, SWA_WIDTH)],
            axis=0)

        yf_c, yb_c, s_fin = _rwkv_group(q, 0, bc, tc, jnp.zeros((bc, 2, RK_NB, PAIR, PAIR), F32))
        yf_l, yb_l, _ = _rwkv_group(q, n_ctx // RK_CHUNK, bl, tl, _pair_states(state_rwkv[:, l]))
        st_l.append(_head_states(s_fin))
        y_f = jnp.concatenate([yf_c, yf_l], axis=0)
        y_b = jnp.concatenate([yb_c, yb_l], axis=0)

        x1, h2, logits = _out_proj(x, o_na, o_sw, y_f, y_b, bonus, g, p, mods, tile_mod)
        meta, row_tok, gates, dest = _route(logits)
        h2 = jnp.concatenate([h2, jnp.zeros((H2_PAD_ROWS - h2.shape[0], D_MODEL), BF16)], axis=0)
        yb = _moe_blocks(meta, h2[row_tok], moe_w1, moe_b1[:, :, None, :], moe_w2, moe_b2[:, :, None, :], l)
        x = _combine(x1, yb[dest].reshape(TOP_K, n_ctx + n_lat, D_MODEL), gates, mods, tile_mod)

    y_p = x[:n_ctx].reshape(bc, tc, D_MODEL)
    y_s = x[n_ctx:].reshape(bl, tl, D_MODEL)
    return (y_p, y_s, jnp.stack(na_k_l, axis=1), jnp.stack(na_v_l, axis=1), jnp.stack(sw_k_l, axis=1),
            jnp.stack(sw_v_l, axis=1), jnp.stack(st_l, axis=1))
```

```python
import functools

import jax
import jax.numpy as jnp
from jax import lax
from jax.experimental import pallas as pl
from jax.experimental.pallas import tpu as pltpu

F32 = jnp.float32
BF16 = jnp.bfloat16

D_MODEL = 1024
HEAD_DIM = 64
LANES = 128
GRID_W = 64
NA_HEADS = 6
SWA_HEADS = 4
SWA_KV_HEADS = 2
RK_HEADS = 6
NA_WIDTH = NA_HEADS * HEAD_DIM
SWA_WIDTH = SWA_HEADS * HEAD_DIM
SWA_KV_WIDTH = SWA_KV_HEADS * HEAD_DIM
RK_WIDTH = RK_HEADS * HEAD_DIM
RK_DECAY_LORA = 64
RK_A_LORA = 64
RK_GATE_LORA = 128
RK_COLS = 3 * RK_WIDTH + RK_DECAY_LORA + RK_A_LORA + RK_GATE_LORA
NA_COLS = 3 * NA_WIDTH
SWA_COLS = SWA_WIDTH + 2 * SWA_KV_WIDTH
ATT_COLS = NA_COLS + SWA_COLS
IN_COLS = ATT_COLS + RK_COLS
NA_WIN_R = 8
NA_WIN_C = 16
SWA_WIN = 128
ROPE_THETA = 10000.0
ATTN_SCALE = HEAD_DIM ** -0.5
N_EXPERTS = 32
TOP_K = 4
SWIGLU_LIMIT = 7.0
SWIGLU_ALPHA = 1.702
MOE_BLK = 256
RMS_EPS = 1e-6
GN_EPS = 64e-5
NEG_BIG = -1e30

TOK_TILE = 256
VMEM_LIMIT = 48 * 1024 * 1024


def _cparams(sem):
    return pltpu.CompilerParams(dimension_semantics=sem, vmem_limit_bytes=VMEM_LIMIT)


def _dot(a, b):
    return jnp.dot(a, b, preferred_element_type=F32)


def _dot_nt(a, b):
    return lax.dot_general(a, b, (((1,), (1,)), ((), ())), preferred_element_type=F32)


def _split_bf16(x):
    hi = x.astype(BF16)
    lo = (x - hi.astype(F32)).astype(BF16)
    return hi, lo


def _dot3(a, b):
    ah, al = _split_bf16(a)
    bh, bl = _split_bf16(b)
    return _dot(ah, bh) + (_dot(ah, bl) + _dot(al, bh))


def _lane_lo(shape):
    return lax.broadcasted_iota(jnp.int32, shape, len(shape) - 1) < HEAD_DIM


def _pair_sum(x):
    lo = _lane_lo(x.shape)
    s_lo = jnp.sum(jnp.where(lo, x, 0.0), axis=-1, keepdims=True)
    s_hi = jnp.sum(jnp.where(lo, 0.0, x), axis=-1, keepdims=True)
    return jnp.where(lo, s_lo, s_hi)


def _stack_heads(q):
    lo = _lane_lo(q.shape)
    return jnp.concatenate([jnp.where(lo, q, 0.0), jnp.where(lo, 0.0, q)], axis=0)


def _unstack_heads(o2):
    n = o2.shape[0] // 2
    return jnp.where(_lane_lo((n, LANES)), o2[:n], o2[n:])


def _dup_head(x, j):
    keep = _lane_lo(x.shape) == (j == 0)
    return jnp.where(keep, x, pltpu.roll(x, HEAD_DIM, 1))


def _ada_kernel(c_ref, w_ref, b_ref, o_ref):
    cv = c_ref[...]
    s = cv * jax.nn.sigmoid(cv)
    o_ref[0] = _dot3(s, w_ref[0]) + b_ref[0]


def _ada_mod(cvecs, w_ada, b_ada):
    depth, _, n_out = w_ada.shape
    rows = cvecs.shape[0]
    tn = 1024
    return pl.pallas_call(
        _ada_kernel,
        grid=(depth, n_out // tn),
        in_specs=[
            pl.BlockSpec((rows, D_MODEL), lambda l, j: (0, 0)),
            pl.BlockSpec((1, D_MODEL, tn), lambda l, j: (l, 0, j)),
            pl.BlockSpec((1, 1, tn), lambda l, j: (l, 0, j)),
        ],
        out_specs=pl.BlockSpec((1, rows, tn), lambda l, j: (l, 0, j)),
        out_shape=jax.ShapeDtypeStruct((depth, rows, n_out), F32),
        compiler_params=_cparams(("parallel", "parallel")),
        name="ada_mod",
    )(cvecs, w_ada, b_ada.reshape(depth, 1, n_out))


NA_QK_BLOCKS = 2 * NA_WIDTH // LANES
SWA_Q_BLOCK0 = NA_COLS // LANES
SWA_QK_BLOCKS = (SWA_WIDTH + SWA_KV_WIDTH) // LANES


def _in_proj_kernel(x_ref, g_ref, mod_ref, w_ref, qkg_ref, cos_ref, sin_ref, att_ref, u_ref):
    x = x_ref[...]
    y = x * lax.rsqrt(jnp.mean(x * x, axis=-1, keepdims=True) + RMS_EPS)
    h = (y * g_ref[...]) * (1.0 + mod_ref[0, 1:2, :]) + mod_ref[0, 0:1, :]
    proj = _dot(h.astype(BF16), w_ref[...])
    u_ref[...] = proj[:, ATT_COLS:]

    def qk_norm(blk, gain):
        ms = _pair_sum(blk * blk) * (1.0 / HEAD_DIM)
        return blk * lax.rsqrt(ms + RMS_EPS) * gain

    lane = lax.broadcasted_iota(jnp.int32, (x.shape[0], LANES), 1)
    first = (lane % (HEAD_DIM // 2)) < (HEAD_DIM // 4)
    for cb in range(ATT_COLS // LANES):
        blk = proj[:, cb * LANES:(cb + 1) * LANES]
        if cb < NA_QK_BLOCKS:
            gi = 0 if cb < NA_QK_BLOCKS // 2 else 1
            blk = qk_norm(blk, qkg_ref[gi:gi + 1, :])
        elif SWA_Q_BLOCK0 <= cb < SWA_Q_BLOCK0 + SWA_QK_BLOCKS:
            gi = 2 if cb < SWA_Q_BLOCK0 + SWA_WIDTH // LANES else 3
            blk = qk_norm(blk, qkg_ref[gi:gi + 1, :])
            partner = jnp.where(first, pltpu.roll(blk, LANES - HEAD_DIM // 4, 1),
                                pltpu.roll(blk, HEAD_DIM // 4, 1))
            blk = blk * cos_ref[...] + partner * sin_ref[...]
        att_ref[:, cb * LANES:(cb + 1) * LANES] = blk


def _in_proj(x, norm_g, mods, w_in_bf16, qk_gains, cos_tab, sin_tab, tile_mod, tile_rope):
    n_tok = x.shape[0]
    return pl.pallas_call(
        _in_proj_kernel,
        grid=(n_tok // TOK_TILE,),
        in_specs=[
            pl.BlockSpec((TOK_TILE, D_MODEL), lambda i: (i, 0)),
            pl.BlockSpec((1, D_MODEL), lambda i: (0, 0)),
            pl.BlockSpec((1, 6, D_MODEL), lambda i: (tile_mod(i), 0, 0)),
            pl.BlockSpec((D_MODEL, IN_COLS), lambda i: (0, 0)),
            pl.BlockSpec((4, LANES), lambda i: (0, 0)),
            pl.BlockSpec((TOK_TILE, LANES), lambda i: (tile_rope(i), 0)),
            pl.BlockSpec((TOK_TILE, LANES), lambda i: (tile_rope(i), 0)),
        ],
        out_specs=[
            pl.BlockSpec((TOK_TILE, ATT_COLS), lambda i: (i, 0)),
            pl.BlockSpec((TOK_TILE, RK_COLS), lambda i: (i, 0)),
        ],
        out_shape=[
            jax.ShapeDtypeStruct((n_tok, ATT_COLS), F32),
            jax.ShapeDtypeStruct((n_tok, RK_COLS), F32),
        ],
        compiler_params=_cparams(("parallel",)),
        name="in_proj",
    )(x, norm_g, mods, w_in_bf16, qk_gains, cos_tab, sin_tab)


def _rope_tables(n_lat):
    nf = HEAD_DIM // 4
    t = jnp.arange(n_lat)
    lane = jnp.arange(LANES)
    d = lane % HEAD_DIM
    inv = ROPE_THETA ** (-(d % nf).astype(F32) / nf)
    pos = jnp.where((d // (2 * nf))[None, :] == 0, (t // GRID_W)[:, None], (t % GRID_W)[:, None]).astype(F32)
    ang = pos * inv[None, :]
    sign = jnp.where((d % (2 * nf)) < nf, -1.0, 1.0).astype(F32)
    cos = jnp.concatenate([jnp.cos(ang), jnp.ones((TOK_TILE, LANES), F32)], 0)
    sin = jnp.concatenate([jnp.sin(ang) * sign[None, :], jnp.zeros((TOK_TILE, LANES), F32)], 0)
    return cos, sin


def _ctx_attn_kernel(sink_ref, q_ref, k_ref, v_ref, o_ref, *, gqa):
    j = pl.program_id(1)
    k = k_ref[0]
    v = v_ref[0]
    if gqa:
        k = _dup_head(k, j)
        v = _dup_head(v, j)
    n = k.shape[0]
    q2 = _stack_heads(q_ref[0]).astype(BF16)
    s = _dot_nt(q2, k.astype(BF16)) * ATTN_SCALE
    m = jnp.max(s, axis=-1, keepdims=True)
    if gqa:
        row = lax.broadcasted_iota(jnp.int32, (2 * n, 1), 0)
        snk = jnp.where(row < n, sink_ref[2 * j], sink_ref[2 * j + 1])
        m = jnp.maximum(m, snk)
    p = jnp.exp(s - m)
    den = jnp.sum(p, axis=-1, keepdims=True)
    if gqa:
        den = den + jnp.exp(snk - m)
    o2 = _dot(p.astype(BF16), v.astype(BF16)) / den
    o_ref[0] = _unstack_heads(o2)


def _ctx_attn(att, b, sink, *, gqa):
    t = att.shape[1]
    if gqa:
        nq = SWA_WIDTH // LANES
        qb, kb, vb = SWA_Q_BLOCK0, SWA_Q_BLOCK0 + nq, SWA_Q_BLOCK0 + nq + 1
        kmap = lambda bi, j: (bi, 0, kb)
        vmap = lambda bi, j: (bi, 0, vb)
    else:
        nq = NA_WIDTH // LANES
        qb, kb, vb = 0, nq, 2 * nq
        kmap = lambda bi, j: (bi, 0, kb + j)
        vmap = lambda bi, j: (bi, 0, vb + j)
    return pl.pallas_call(
        functools.partial(_ctx_attn_kernel, gqa=gqa),
        grid=(b, nq),
        in_specs=[
            pl.BlockSpec(memory_space=pltpu.SMEM),
            pl.BlockSpec((1, t, LANES), lambda bi, j: (bi, 0, qb + j)),
            pl.BlockSpec((1, t, LANES), kmap),
            pl.BlockSpec((1, t, LANES), vmap),
        ],
        out_specs=pl.BlockSpec((1, t, LANES), lambda bi, j: (bi, 0, j)),
        out_shape=jax.ShapeDtypeStruct((b, t, nq * LANES), F32),
        compiler_params=_cparams(("parallel", "parallel")),
        name="ctx_attn_swa" if gqa else "ctx_attn_na",
    )(sink, att, att, att)


NA_ROWS_PER_ITER = 4


def _na_lat_kernel(q_ref, k_ref, v_ref, kc_ref, vc_ref, tab_ref, o_ref, kb_ref, vb_ref):
    n = q_ref.shape[1]
    rows = n // GRID_W
    win = NA_WIN_R * GRID_W
    kb_ref[...] = k_ref[0].astype(BF16)
    vb_ref[...] = v_ref[0].astype(BF16)
    kc = kc_ref[0].astype(BF16)
    vc = vc_ref[0].astype(BF16)

    def row_group(ig, carry):
        g0 = pl.multiple_of(ig * (NA_ROWS_PER_ITER * GRID_W), NA_ROWS_PER_ITER * GRID_W)
        qg = q_ref[0, pl.ds(g0, NA_ROWS_PER_ITER * GRID_W), :]
        outs = []
        for r in range(NA_ROWS_PER_ITER):
            i = ig * NA_ROWS_PER_ITER + r
            start = jnp.clip(i - NA_WIN_R // 2, 0, rows - NA_WIN_R)
            rb = start - i + (NA_WIN_R - 1)
            k0 = pl.multiple_of(start * GRID_W, GRID_W)
            q2 = _stack_heads(qg[r * GRID_W:(r + 1) * GRID_W]).astype(BF16)
            kw = kb_ref[pl.ds(k0, win), :]
            vw = vb_ref[pl.ds(k0, win), :]
            s_loc = _dot_nt(q2, kw) * ATTN_SCALE + tab_ref[0, rb]
            s_ctx = _dot_nt(q2, kc) * ATTN_SCALE
            m = jnp.maximum(jnp.max(s_loc, axis=-1, keepdims=True), jnp.max(s_ctx, axis=-1, keepdims=True))
            p_loc = jnp.exp(s_loc - m)
            p_ctx = jnp.exp(s_ctx - m)
            den = jnp.sum(p_loc, axis=-1, keepdims=True) + jnp.sum(p_ctx, axis=-1, keepdims=True)
            o2 = (_dot(p_loc.astype(BF16), vw) + _dot(p_ctx.astype(BF16), vc)) / den
            outs.append(_unstack_heads(o2))
        o_ref[0, pl.ds(g0, NA_ROWS_PER_ITER * GRID_W), :] = jnp.concatenate(outs, axis=0)
        return carry

    lax.fori_loop(0, rows // NA_ROWS_PER_ITER, row_group, 0)


def _na_bias_tables(rpb):
    col = jnp.arange(GRID_W)
    cstart = jnp.clip(col - NA_WIN_C // 2, 0, GRID_W - NA_WIN_C)
    col_mask = (col[None, :] >= cstart[:, None]) & (col[None, :] < cstart[:, None] + NA_WIN_C)
    col_idx = jnp.clip(col[None, :] - col[:, None] + NA_WIN_C - 1, 0, 2 * NA_WIN_C - 2)
    rpb_cols = jnp.where(col_mask[None, None], rpb[:, :, col_idx], NEG_BIG)
    roff = jnp.arange(NA_WIN_R)[:, None] + jnp.arange(NA_WIN_R)[None, :]
    t = rpb_cols[:, roff]
    t = jnp.transpose(t, (0, 1, 3, 2, 4)).reshape(NA_HEADS // 2, 2, NA_WIN_R, GRID_W, NA_WIN_R * GRID_W)
    return jnp.transpose(t, (0, 2, 1, 3, 4)).reshape(NA_HEADS // 2, NA_WIN_R, 2 * GRID_W, NA_WIN_R * GRID_W)


def _na_latent(att, s0, kc, vc, tab):
    n = att.shape[1]
    b, p, _ = kc.shape
    nq = NA_WIDTH // LANES
    return pl.pallas_call(
        _na_lat_kernel,
        grid=(b, nq),
        in_specs=[
            pl.BlockSpec((1, n, LANES), lambda bi, j: (s0 + bi, 0, j)),
            pl.BlockSpec((1, n, LANES), lambda bi, j: (s0 + bi, 0, nq + j)),
            pl.BlockSpec((1, n, LANES), lambda bi, j: (s0 + bi, 0, 2 * nq + j)),
            pl.BlockSpec((1, p, LANES), lambda bi, j: (bi, 0, j)),
            pl.BlockSpec((1, p, LANES), lambda bi, j: (bi, 0, j)),
            pl.BlockSpec((1, NA_WIN_R, 2 * GRID_W, NA_WIN_R * GRID_W), lambda bi, j: (j, 0, 0, 0)),
        ],
        out_specs=pl.BlockSpec((1, n, LANES), lambda bi, j: (bi, 0, j)),
        out_shape=jax.ShapeDtypeStruct((b, n, NA_WIDTH), F32),
        scratch_shapes=[pltpu.VMEM((n, LANES), BF16), pltpu.VMEM((n, LANES), BF16)],
        compiler_params=_cparams(("parallel", "parallel")),
        name="na_latent",
    )(att, att, att, kc, vc, tab)


SWA_BLOCKS_PER_ITER = 2


def _swa_lat_kernel(sink_ref, q_ref, k_ref, v_ref, kc_ref, vc_ref, o_ref, kb_ref, vb_ref):
    j = pl.program_id(1)
    n = q_ref.shape[1]
    blk = SWA_WIN
    span = 3 * blk
    kb_ref[...] = _dup_head(k_ref[0], j).astype(BF16)
    vb_ref[...] = _dup_head(v_ref[0], j).astype(BF16)
    kc = _dup_head(kc_ref[0], j).astype(BF16)
    vc = _dup_head(vc_ref[0], j).astype(BF16)
    row = lax.broadcasted_iota(jnp.int32, (2 * blk, 1), 0)
    snk = jnp.where(row < blk, sink_ref[2 * j], sink_ref[2 * j + 1])
    qoff = lax.broadcasted_iota(jnp.int32, (2 * blk, span), 0) % blk
    koff = lax.broadcasted_iota(jnp.int32, (2 * blk, span), 1)

    def q_group(qg, carry):
        g0 = pl.multiple_of(qg * (SWA_BLOCKS_PER_ITER * blk), SWA_BLOCKS_PER_ITER * blk)
        qall = q_ref[0, pl.ds(g0, SWA_BLOCKS_PER_ITER * blk), :]
        outs = []
        for r in range(SWA_BLOCKS_PER_ITER):
            q0 = g0 + r * blk
            w0 = pl.multiple_of(jnp.clip(q0 - blk, 0, n - span), blk)
            q2 = _stack_heads(qall[r * blk:(r + 1) * blk]).astype(BF16)
            kw = kb_ref[pl.ds(w0, span), :]
            vw = vb_ref[pl.ds(w0, span), :]
            valid = jnp.abs((q0 + qoff) - (w0 + koff)) <= SWA_WIN
            s_loc = jnp.where(valid, _dot_nt(q2, kw) * ATTN_SCALE, NEG_BIG)
            s_ctx = _dot_nt(q2, kc) * ATTN_SCALE
            m = jnp.maximum(jnp.max(s_loc, axis=-1, keepdims=True), jnp.max(s_ctx, axis=-1, keepdims=True))
            m = jnp.maximum(m, snk)
            p_loc = jnp.exp(s_loc - m)
            p_ctx = jnp.exp(s_ctx - m)
            den = (jnp.sum(p_loc, axis=-1, keepdims=True) + jnp.sum(p_ctx, axis=-1, keepdims=True)
                   + jnp.exp(snk - m))
            o2 = (_dot(p_loc.astype(BF16), vw) + _dot(p_ctx.astype(BF16), vc)) / den
            outs.append(_unstack_heads(o2))
        o_ref[0, pl.ds(g0, SWA_BLOCKS_PER_ITER * blk), :] = jnp.concatenate(outs, axis=0)
        return carry

    lax.fori_loop(0, n // (SWA_BLOCKS_PER_ITER * blk), q_group, 0)


def _swa_latent(att, s0, kc, vc, sink):
    n = att.shape[1]
    b, p, _ = kc.shape
    nq = SWA_WIDTH // LANES
    qb, kb, vb = SWA_Q_BLOCK0, SWA_Q_BLOCK0 + nq, SWA_Q_BLOCK0 + nq + 1
    return pl.pallas_call(
        _swa_lat_kernel,
        grid=(b, nq),
        in_specs=[
            pl.BlockSpec(memory_space=pltpu.SMEM),
            pl.BlockSpec((1, n, LANES), lambda bi, j: (s0 + bi, 0, qb + j)),
            pl.BlockSpec((1, n, LANES), lambda bi, j: (s0 + bi, 0, kb)),
            pl.BlockSpec((1, n, LANES), lambda bi, j: (s0 + bi, 0, vb)),
            pl.BlockSpec((1, p, LANES), lambda bi, j: (bi, 0, 0)),
            pl.BlockSpec((1, p, LANES), lambda bi, j: (bi, 0, 0)),
        ],
        out_specs=pl.BlockSpec((1, n, LANES), lambda bi, j: (bi, 0, j)),
        out_shape=jax.ShapeDtypeStruct((b, n, SWA_WIDTH), F32),
        scratch_shapes=[pltpu.VMEM((n, LANES), BF16), pltpu.VMEM((n, LANES), BF16)],
        compiler_params=_cparams(("parallel", "parallel")),
        name="swa_latent",
    )(sink, att, att, att, kc, vc)


RK_NB = RK_WIDTH // LANES
LORA_BLOCK = 3 * RK_WIDTH // LANES
GATE_BLOCK = LORA_BLOCK + 1
Q_R, Q_V, Q_A, Q_W, Q_K, Q_B = range(6)
Q_DIR = 3
Q_COLS = (6 + Q_DIR) * RK_WIDTH


def _softplus(x):
    return jnp.maximum(x, 0.0) + jnp.log(1.0 + jnp.exp(-jnp.abs(x)))


def _rk_prep_kernel(u_ref, up_ref, un_ref, cw_ref, w0_ref, w2_ref, a0_ref, a2_ref, g2_ref, kk_ref, ka_ref,
                    rk_ref, q_ref, g_ref, bonus_ref, *, n_ctx_tiles, tiles_per_seq):
    def put(slot, val):
        q_ref[:, slot * RK_WIDTH:(slot + 1) * RK_WIDTH] = val

    i = pl.program_id(0)
    li = i - n_ctx_tiles
    is_lat = i >= n_ctx_tiles
    has_prev = jnp.logical_and(is_lat, li % tiles_per_seq != 0)
    has_next = jnp.logical_and(is_lat, li % tiles_per_seq != tiles_per_seq - 1)
    u = u_ref[...]
    tm = u.shape[0]
    prev_row = jnp.where(has_prev, up_ref[7:8, :], 0.0)
    next_row = jnp.where(has_next, un_ref[0:1, :], 0.0)
    row = lax.broadcasted_iota(jnp.int32, u.shape, 0)
    um = jnp.where(row == 0, prev_row, pltpu.roll(u, 1, 0))
    up = jnp.where(row == tm - 1, next_row, pltpu.roll(u, tm - 1, 0))
    u = um * cw_ref[0:1, :] + u * cw_ref[1:2, :] + up * cw_ref[2:3, :]

    r = u[:, 0:RK_WIDTH]
    k = u[:, RK_WIDTH:2 * RK_WIDTH]
    v = u[:, 2 * RK_WIDTH:3 * RK_WIDTH]
    lora = u[:, LORA_BLOCK * LANES:(LORA_BLOCK + 1) * LANES]
    gl = u[:, GATE_BLOCK * LANES:(GATE_BLOCK + 1) * LANES]
    put(Q_R, r)
    put(Q_V, v)
    g_ref[...] = _dot3(jax.nn.sigmoid(gl), g2_ref[...])

    kn = k * kk_ref[...]
    kk = jnp.concatenate(
        [kn[:, c * LANES:(c + 1) * LANES]
         * lax.rsqrt(jnp.maximum(_pair_sum(jnp.square(kn[:, c * LANES:(c + 1) * LANES])), 1e-24))
         for c in range(RK_NB)], axis=1)
    put(Q_A, -kk)

    lora_t = jnp.tanh(lora)
    kd_sum = None
    for d in range(2):
        w = -_softplus(-(w0_ref[d:d + 1, :] + _dot3(lora_t, w2_ref[d]))) - 0.5
        put(Q_W + Q_DIR * d, -jnp.exp(w))
        a = jax.nn.sigmoid(a0_ref[d:d + 1, :] + _dot3(lora, a2_ref[d]))
        kd = k * (1.0 + (a - 1.0) * ka_ref[...])
        put(Q_K + Q_DIR * d, kd)
        put(Q_B + Q_DIR * d, kk * a)
        kd_sum = kd if kd_sum is None else kd_sum + kd

    t = r * kd_sum * rk_ref[...]
    bonus_ref[...] = jnp.concatenate(
        [_pair_sum(t[:, c * LANES:(c + 1) * LANES]) for c in range(RK_NB)], axis=1) * v


def _rk_prep(u, p, n_ctx_tiles, tiles_per_seq):
    n_tok = u.shape[0]
    n_tiles = n_tok // TOK_TILE
    sub = TOK_TILE // 8
    last8 = n_tok // 8 - 1
    tok = lambda i: (i, 0)
    const2 = lambda i: (0, 0)
    const3 = lambda i: (0, 0, 0)
    one = jax.ShapeDtypeStruct((n_tok, RK_WIDTH), F32)
    tok_spec = pl.BlockSpec((TOK_TILE, RK_WIDTH), tok)
    return pl.pallas_call(
        functools.partial(_rk_prep_kernel, n_ctx_tiles=n_ctx_tiles, tiles_per_seq=tiles_per_seq),
        grid=(n_tiles,),
        in_specs=[
            pl.BlockSpec((TOK_TILE, RK_COLS), tok),
            pl.BlockSpec((8, RK_COLS), lambda i: (jnp.maximum(i * sub - 1, 0), 0)),
            pl.BlockSpec((8, RK_COLS), lambda i: (jnp.minimum((i + 1) * sub, last8), 0)),
            pl.BlockSpec((3, RK_COLS), const2),
            pl.BlockSpec((2, RK_WIDTH), const2),
            pl.BlockSpec((2, LANES, RK_WIDTH), const3),
            pl.BlockSpec((2, RK_WIDTH), const2),
            pl.BlockSpec((2, LANES, RK_WIDTH), const3),
            pl.BlockSpec((RK_GATE_LORA, RK_WIDTH), const2),
            pl.BlockSpec((1, RK_WIDTH), const2),
            pl.BlockSpec((1, RK_WIDTH), const2),
            pl.BlockSpec((1, RK_WIDTH), const2),
        ],
        out_specs=[pl.BlockSpec((TOK_TILE, Q_COLS), tok), tok_spec, tok_spec],
        out_shape=[jax.ShapeDtypeStruct((n_tok, Q_COLS), F32), one, one],
        compiler_params=_cparams(("parallel",)),
        name="rk_prep",
    )(u, u, u, p["rk_conv"], p["rk_w0"], p["rk_w2_pad"], p["rk_a0"], p["rk_a2_pad"], p["rk_g2"],
      p["rk_k_k"], p["rk_k_a"], p["rk_r_k"])


RK_CHUNK = 64
PAIR = 2 * HEAD_DIM
STATE_SEQS = 8


def _split3_bf16(x):
    hi = x.astype(BF16)
    r1 = x - hi.astype(F32)
    mid = r1.astype(BF16)
    return hi, mid, (r1 - mid.astype(F32)).astype(BF16)


def _bdot(a, b):
    return _dot(a.astype(BF16), b.astype(BF16))


def _bmm(a, b):
    return lax.dot_general(a.astype(BF16), b.astype(BF16), (((2,), (1,)), ((0,), (0,))),
                           preferred_element_type=F32)


def _bmm_nt(a, b):
    return lax.dot_general(a.astype(BF16), b.astype(BF16), (((2,), (2,)), ((0,), (0,))),
                           preferred_element_type=F32)


def _stack_heads3(x):
    lo = _lane_lo(x.shape)
    return jnp.concatenate([jnp.where(lo, x, 0.0), jnp.where(lo, 0.0, x)], axis=1)


def _rk_chunk_kernel(q_ref, rbar_ref, ybar_ref, phi_ref, psi_ref):
    c = RK_CHUNK
    n = 2 * c
    nu = 2 * RK_NB

    def tiles(slot, per_dir):
        cols = [(slot + (Q_DIR * d if per_dir else 0)) * RK_WIDTH + p * LANES
                for d in range(2) for p in range(RK_NB)]
        return jnp.stack([q_ref[:, lo:lo + LANES] for lo in cols])

    r, v, a = tiles(Q_R, False), tiles(Q_V, False), tiles(Q_A, False)
    lw, k, b = tiles(Q_W, True), tiles(Q_K, True), tiles(Q_B, True)
    sgn = jnp.where(lax.broadcasted_iota(jnp.int32, (nu, 1, 1), 0) >= RK_NB, -1, 1)
    bwd = sgn < 0
    tdiff = lax.broadcasted_iota(jnp.int32, (1, c, c), 2) - lax.broadcasted_iota(jnp.int32, (1, c, c), 1)
    tri = jnp.where(tdiff * sgn <= 0, 1.0, 0.0)
    cum = sum(_bmm(tri, part) for part in _split3_bf16(lw))
    tot = jnp.where(bwd, cum[:, 0:1], cum[:, c - 1:c])
    a_t = a * jnp.exp(cum - lw)
    r_t = r * jnp.exp(cum)
    e_neg = jnp.exp(-cum)
    e_end = jnp.exp(tot - cum)
    g = _bmm_nt(jnp.concatenate([_stack_heads3(a_t), _stack_heads3(r_t)], axis=1),
                jnp.concatenate([_stack_heads3(b * e_neg), _stack_heads3(k * e_neg)], axis=1))
    r2 = lax.broadcasted_iota(jnp.int32, (1, n, n), 1)
    c2 = lax.broadcasted_iota(jnp.int32, (1, n, n), 2)
    order = (jnp.bitwise_and(c2, c - 1) - jnp.bitwise_and(r2, c - 1)) * sgn
    eye = jnp.where(r2 == c2, 1.0, 0.0)
    l_ab = jnp.where(order < 0, g[:, :n, :n], 0.0)
    l_ak = jnp.where(order < 0, g[:, :n, n:], 0.0)
    m_rb = jnp.where(order <= 0, g[:, n:, :n], 0.0)
    m_rk = jnp.where(order <= 0, g[:, n:, n:], 0.0)
    t_inv = eye + l_ab
    pw = l_ab
    for _ in range(5):
        pw = _bmm(pw, pw)
        t_inv = t_inv + _bmm(t_inv, pw)
    sv = _stack_heads3(v)
    au = _bmm(t_inv, jnp.concatenate([_stack_heads3(a_t), _bmm(l_ak, sv)], axis=2))
    ry = _bmm(m_rb, au) + jnp.concatenate([_stack_heads3(r_t), _bmm(m_rk, sv)], axis=2)
    ry = ry[:, :c] + ry[:, c:]
    bt = jnp.swapaxes(_stack_heads3(b * e_end), 1, 2)
    kt = jnp.swapaxes(_stack_heads3(k * e_end), 1, 2)
    pp = _bmm(bt, au)
    phi = eye * jnp.exp(tot) + pp[:, :, :PAIR]
    psi = pp[:, :, PAIR:] + _bmm(kt, sv)
    for d in range(2):
        for p in range(RK_NB):
            u = d * RK_NB + p
            rbar_ref[d, :, p * LANES:(p + 1) * LANES] = ry[u, :, :PAIR]
            ybar_ref[d, :, p * LANES:(p + 1) * LANES] = ry[u, :, PAIR:]
            phi_ref[d, p] = phi[u]
            psi_ref[d, p] = psi[u]


def _rk_chunk(q, tile0, n_seq, t):
    nc = t // RK_CHUNK
    row_sh = jax.ShapeDtypeStruct((2, n_seq, t, RK_WIDTH), F32)
    mat_sh = jax.ShapeDtypeStruct((2, n_seq, nc, RK_NB, PAIR, PAIR), F32)
    row_spec = pl.BlockSpec((2, None, RK_CHUNK, RK_WIDTH), lambda s, c: (0, s, c, 0))
    mat_spec = pl.BlockSpec((2, None, None, RK_NB, PAIR, PAIR), lambda s, c: (0, s, c, 0, 0, 0))
    return pl.pallas_call(
        _rk_chunk_kernel,
        grid=(n_seq, nc),
        in_specs=[pl.BlockSpec((RK_CHUNK, Q_COLS), lambda s, c: (tile0 + s * nc + c, 0))],
        out_specs=[row_spec, row_spec, mat_spec, mat_spec],
        out_shape=[row_sh, row_sh, mat_sh, mat_sh],
        compiler_params=_cparams(("parallel", "parallel")),
        name="rk_chunk",
    )(q)


def _rk_state_kernel(rf_ref, rb_ref, yf_ref, yb_ref, phf_ref, phb_ref, psf_ref, psb_ref, s0_ref,
                     of_ref, ob_ref, s_ref):
    @pl.when(pl.program_id(1) == 0)
    def _():
        s_ref[...] = s0_ref[...]

    dirs = ((rf_ref, yf_ref, phf_ref, psf_ref, of_ref), (rb_ref, yb_ref, phb_ref, psb_ref, ob_ref))

    def seq_body(s, carry):
        ys, hs = [], []
        for d, (r_ref, y_ref, ph_ref, ps_ref, _) in enumerate(dirs):
            for p in range(RK_NB):
                lanes = slice(p * LANES, (p + 1) * LANES)
                h = s_ref[s, d, p]
                ys.append(_dot3(r_ref[s, :, lanes], h) + y_ref[s, :, lanes])
                hs.append(_dot3(ph_ref[s, p], h) + ps_ref[s, p])
        for d in range(2):
            dirs[d][4][s] = jnp.concatenate(ys[d * RK_NB:(d + 1) * RK_NB], axis=1)
            for p in range(RK_NB):
                s_ref[s, d, p] = hs[d * RK_NB + p]
        return carry

    lax.fori_loop(0, s_ref.shape[0], seq_body, 0)


def _rk_state(rbar, ybar, phi, psi, s0):
    _, n_seq, t, _ = rbar.shape
    nc = t // RK_CHUNK
    sg = STATE_SEQS
    row_blk = (None, sg, RK_CHUNK, RK_WIDTH)
    mat_blk = (None, sg, None, RK_NB, PAIR, PAIR)
    fwd_row = pl.BlockSpec(row_blk, lambda g, c: (0, g, c, 0))
    bwd_row = pl.BlockSpec(row_blk, lambda g, c: (1, g, nc - 1 - c, 0))
    fwd_mat = pl.BlockSpec(mat_blk, lambda g, c: (0, g, c, 0, 0, 0))
    bwd_mat = pl.BlockSpec(mat_blk, lambda g, c: (1, g, nc - 1 - c, 0, 0, 0))
    st = pl.BlockSpec((sg, 2, RK_NB, PAIR, PAIR), lambda g, c: (g, 0, 0, 0, 0))
    out_sh = jax.ShapeDtypeStruct((n_seq, t, RK_WIDTH), F32)
    return pl.pallas_call(
        _rk_state_kernel,
        grid=(n_seq // sg, nc),
        in_specs=[fwd_row, bwd_row, fwd_row, bwd_row, fwd_mat, bwd_mat, fwd_mat, bwd_mat, st],
        out_specs=[pl.BlockSpec((sg, RK_CHUNK, RK_WIDTH), lambda g, c: (g, c, 0)),
                   pl.BlockSpec((sg, RK_CHUNK, RK_WIDTH), lambda g, c: (g, nc - 1 - c, 0)), st],
        out_shape=[out_sh, out_sh, jax.ShapeDtypeStruct((n_seq, 2, RK_NB, PAIR, PAIR), F32)],
        compiler_params=_cparams(("parallel", "arbitrary")),
        name="rk_state",
    )(rbar, rbar, ybar, ybar, phi, phi, psi, psi, s0)


def _pair_states(s):
    bsz = s.shape[0]
    h = jnp.swapaxes(s, -1, -2).reshape(bsz, 2, RK_NB, 2, HEAD_DIM, HEAD_DIM)
    return jnp.einsum("bdphkv,hg->bdphkgv", h, jnp.eye(2, dtype=F32)).reshape(bsz, 2, RK_NB, PAIR, PAIR)


def _head_states(s):
    bsz = s.shape[0]
    h = s.reshape(bsz, 2, RK_NB, 2, HEAD_DIM, 2, HEAD_DIM)
    h = jnp.stack([h[:, :, :, 0, :, 0, :], h[:, :, :, 1, :, 1, :]], axis=3)
    return jnp.swapaxes(h.reshape(bsz, 2, RK_HEADS, HEAD_DIM, HEAD_DIM), -1, -2)


def _rwkv_group(q, tile0, n_seq, t, s0):
    rbar, ybar, phi, psi = _rk_chunk(q, tile0, n_seq, t)
    y_f, y_b, s_fin = _rk_state(rbar, ybar, phi, psi, s0)
    return y_f.reshape(n_seq * t, RK_WIDTH), y_b.reshape(n_seq * t, RK_WIDTH), s_fin


def _out_proj_kernel(x_ref, ona_ref, osw_ref, yf_ref, yb_ref, bonus_ref, g_ref, lng_ref, lnb_ref, w_ref, mod_ref,
                     n2_ref, rw_ref, rb_ref, x1_ref, h2_ref, lg_ref):
    y = yf_ref[...] + yb_ref[...]
    parts = []
    for c in range(RK_NB):
        yc = y[:, c * LANES:(c + 1) * LANES]
        dc = yc - _pair_sum(yc) * (1.0 / HEAD_DIM)
        var = _pair_sum(dc * dc) * (1.0 / HEAD_DIM)
        parts.append(dc * lax.rsqrt(var + GN_EPS))
    yn = jnp.concatenate(parts, axis=1) * lng_ref[...] + lnb_ref[...]
    o_rk = (yn + bonus_ref[...]) * g_ref[...]
    o = (_dot(ona_ref[...].astype(BF16), w_ref[0:NA_WIDTH, :])
         + _dot(osw_ref[...].astype(BF16), w_ref[NA_WIDTH:NA_WIDTH + SWA_WIDTH, :])
         + _dot(o_rk.astype(BF16), w_ref[NA_WIDTH + SWA_WIDTH:, :]))
    x1 = x_ref[...] + mod_ref[0, 2:3, :] * o
    x1_ref[...] = x1
    yn2 = x1 * lax.rsqrt(jnp.mean(x1 * x1, axis=-1, keepdims=True) + RMS_EPS)
    h2 = (yn2 * n2_ref[...]) * (1.0 + mod_ref[0, 4:5, :]) + mod_ref[0, 3:4, :]
    h2_ref[...] = h2.astype(BF16)
    lg_ref[...] = (_dot3(h2, rw_ref[...]) + rb_ref[...])[:, :N_EXPERTS]


def _out_proj(x, o_na, o_sw, y_f, y_b, bonus, g, p, mods, tile_mod):
    n_tok = x.shape[0]
    tok = lambda i: (i, 0)
    const = lambda i: (0, 0)
    return pl.pallas_call(
        _out_proj_kernel,
        grid=(n_tok // TOK_TILE,),
        in_specs=[
            pl.BlockSpec((TOK_TILE, D_MODEL), tok),
            pl.BlockSpec((TOK_TILE, NA_WIDTH), tok),
            pl.BlockSpec((TOK_TILE, SWA_WIDTH), tok),
            pl.BlockSpec((TOK_TILE, RK_WIDTH), tok),
            pl.BlockSpec((TOK_TILE, RK_WIDTH), tok),
            pl.BlockSpec((TOK_TILE, RK_WIDTH), tok),
            pl.BlockSpec((TOK_TILE, RK_WIDTH), tok),
            pl.BlockSpec((1, RK_WIDTH), const),
            pl.BlockSpec((1, RK_WIDTH), const),
            pl.BlockSpec((D_MODEL, D_MODEL), const),
            pl.BlockSpec((1, 6, D_MODEL), lambda i: (tile_mod(i), 0, 0)),
            pl.BlockSpec((1, D_MODEL), const),
            pl.BlockSpec((D_MODEL, LANES), const),
            pl.BlockSpec((1, LANES), const),
        ],
        out_specs=[
            pl.BlockSpec((TOK_TILE, D_MODEL), tok),
            pl.BlockSpec((TOK_TILE, D_MODEL), tok),
            pl.BlockSpec((TOK_TILE, N_EXPERTS), tok),
        ],
        out_shape=[
            jax.ShapeDtypeStruct((n_tok, D_MODEL), F32),
            jax.ShapeDtypeStruct((n_tok, D_MODEL), BF16),
            jax.ShapeDtypeStruct((n_tok, N_EXPERTS), F32),
        ],
        compiler_params=_cparams(("parallel",)),
        name="out_proj",
    )(x, o_na, o_sw, y_f, y_b, bonus, g, p["rk_ln_g"], p["rk_ln_b"], p["w_out_bf16"], mods, p["norm2_g"],
      p["router_w_pad"], p["router_b_pad"])


H2_PAD_ROWS = 32768
W2_STAGE_ROWS = 128
MOE_VMEM_LIMIT = 56 * 1024 * 1024


def _moe_kernel(meta_ref, x_ref, w1_ref, b1_ref, w2_ref, b2_ref, o_ref, w1b_ref, w2e_ref, stage_ref):
    i = pl.program_id(0)
    n_blk = meta_ref.shape[0] - 1
    n_used = meta_ref[n_blk]
    d_e = w2_ref.shape[1]
    new_expert = jnp.logical_or(i == 0, meta_ref[i] != meta_ref[jnp.maximum(i - 1, 0)])

    @pl.when(i == 0)
    def _():
        stage_ref[...] = jnp.zeros_like(stage_ref)

    @pl.when(jnp.logical_and(i < n_used, new_expert))
    def _():
        w1b_ref[...] = w1_ref[0].astype(BF16)
        for c in range(d_e // W2_STAGE_ROWS):
            rows = slice(c * W2_STAGE_ROWS, (c + 1) * W2_STAGE_ROWS)
            for cb in range(D_MODEL // LANES):
                cols = slice(cb * LANES, (cb + 1) * LANES)
                stage_ref[cb, pl.ds(0, W2_STAGE_ROWS, stride=2), :] = w2_ref[0, rows, cols]
                w2e_ref[2 * c * W2_STAGE_ROWS:2 * (c + 1) * W2_STAGE_ROWS, cols] = stage_ref[cb].astype(BF16)

    @pl.when(i < n_used)
    def _():
        uu = _dot(x_ref[...], w1b_ref[...]) + b1_ref[0]
        acts = []
        for c in range(uu.shape[1] // LANES):
            blk = uu[:, c * LANES:(c + 1) * LANES]
            glu = jnp.minimum(blk, SWIGLU_LIMIT)
            lin = jnp.clip(pltpu.roll(blk, LANES - 1, 1), -SWIGLU_LIMIT, SWIGLU_LIMIT)
            acts.append((glu * jax.nn.sigmoid(SWIGLU_ALPHA * glu) * (lin + 1.0)).astype(BF16))
        act = jnp.concatenate(acts, axis=1)
        o_ref[...] = (_dot(act, w2e_ref[...]) + b2_ref[0]).astype(BF16)

    @pl.when(i >= n_used)
    def _():
        o_ref[...] = jnp.zeros_like(o_ref)


def _moe_blocks(meta, xb, w1, b1, w2, b2, layer):
    n_rows = xb.shape[0]
    n_blk = n_rows // MOE_BLK
    d_e = w2.shape[2]
    row = lambda i, m: (i, 0)
    exp3 = lambda i, m: (layer, m[i], 0, 0)
    grid_spec = pltpu.PrefetchScalarGridSpec(
        num_scalar_prefetch=1,
        grid=(n_blk,),
        in_specs=[
            pl.BlockSpec((MOE_BLK, D_MODEL), row),
            pl.BlockSpec((None, 1, D_MODEL, 2 * d_e), exp3),
            pl.BlockSpec((None, 1, 1, 2 * d_e), exp3),
            pl.BlockSpec((None, 1, d_e, D_MODEL), exp3),
            pl.BlockSpec((None, 1, 1, D_MODEL), exp3),
        ],
        out_specs=pl.BlockSpec((MOE_BLK, D_MODEL), row),
        scratch_shapes=[
            pltpu.VMEM((D_MODEL, 2 * d_e), BF16),
            pltpu.VMEM((2 * d_e, D_MODEL), BF16),
            pltpu.VMEM((D_MODEL // LANES, 2 * W2_STAGE_ROWS, LANES), F32),
        ],
    )
    return pl.pallas_call(
        _moe_kernel,
        grid_spec=grid_spec,
        out_shape=jax.ShapeDtypeStruct((n_rows, D_MODEL), BF16),
        compiler_params=pltpu.CompilerParams(dimension_semantics=("arbitrary",),
                                             vmem_limit_bytes=MOE_VMEM_LIMIT),
        name="moe_blocks",
    )(meta, xb, w1, b1, w2, b2)


def _route(logits):
    n_tok = logits.shape[0]
    top_v, top_i = lax.top_k(logits, TOP_K)
    gates = jax.nn.softmax(top_v, axis=-1)
    e_flat = top_i.reshape(-1).astype(jnp.int32)
    n_rows = n_tok * TOP_K
    onehot = (e_flat[:, None] == jnp.arange(N_EXPERTS, dtype=jnp.int32)[None, :]).astype(jnp.int32)
    csum = jnp.cumsum(onehot, axis=0)
    rank = jnp.take_along_axis(csum, e_flat[:, None], axis=1)[:, 0] - 1
    counts = csum[-1]
    starts = jnp.cumsum(counts) - counts
    pcounts = (counts + MOE_BLK - 1) // MOE_BLK * MOE_BLK
    pends = jnp.cumsum(pcounts)
    pstarts = pends - pcounts
    dest = pstarts[e_flat] + rank
    n_blk = n_rows // MOE_BLK + N_EXPERTS
    blk_start = jnp.arange(n_blk, dtype=jnp.int32) * MOE_BLK
    blk_exp = jnp.minimum(jnp.sum((blk_start[:, None] >= pends[None, :]).astype(jnp.int32), axis=1), N_EXPERTS - 1)
    order = jnp.argsort(e_flat)
    pos = jnp.arange(n_blk * MOE_BLK, dtype=jnp.int32)
    e_pos = jnp.repeat(blk_exp, MOE_BLK)
    src = jnp.clip(pos - pstarts[e_pos] + starts[e_pos], 0, n_rows - 1)
    row_tok = order[src].astype(jnp.int32) // TOP_K
    meta = jnp.concatenate([blk_exp, (pends[-1:] // MOE_BLK).astype(jnp.int32)])
    return meta, row_tok, gates, dest.reshape(n_tok, TOP_K).T.reshape(-1)


def _combine_kernel(x_ref, yg_ref, gate_ref, mod_ref, o_ref):
    gate = gate_ref[...]
    acc = gate[:, 0:1] * yg_ref[0].astype(F32)
    for j in range(1, TOP_K):
        acc = acc + gate[:, j:j + 1] * yg_ref[j].astype(F32)
    o_ref[...] = x_ref[...] + mod_ref[0, 5:6, :] * acc


def _combine(x1, yg, gates, mods, tile_mod):
    n_tok = x1.shape[0]
    return pl.pallas_call(
        _combine_kernel,
        grid=(n_tok // TOK_TILE,),
        in_specs=[
            pl.BlockSpec((TOK_TILE, D_MODEL), lambda i: (i, 0)),
            pl.BlockSpec((TOP_K, TOK_TILE, D_MODEL), lambda i: (0, i, 0)),
            pl.BlockSpec((TOK_TILE, TOP_K), lambda i: (i, 0)),
            pl.BlockSpec((1, 6, D_MODEL), lambda i: (tile_mod(i), 0, 0)),
        ],
        out_specs=pl.BlockSpec((TOK_TILE, D_MODEL), lambda i: (i, 0)),
        out_shape=jax.ShapeDtypeStruct((n_tok, D_MODEL), F32),
        compiler_params=_cparams(("parallel",)),
        name="moe_combine",
    )(x1, yg, gates, mods)


def kernel(x_prompt, x_sample, c, cache_na_k, cache_na_v, cache_swa_k, cache_swa_v, state_rwkv, c_ctx, w_ada, b_ada, norm1_g, norm2_g, w_in, w_out, na_q_norm, na_k_norm, na_rpb, swa_q_norm, swa_k_norm, swa_sink, rk_conv, rk_w0, rk_w2, rk_a0, rk_a2, rk_g2, rk_k_k, rk_k_a, rk_r_k, rk_ln_g, rk_ln_b, moe_router_w, moe_router_b, moe_w1, moe_b1, moe_w2, moe_b2):
    bc, tc, _ = x_prompt.shape
    bl, tl, _ = x_sample.shape
    depth = w_in.shape[0]
    n_ctx = bc * tc
    n_lat = bl * tl
    assert tc == TOK_TILE and tl % TOK_TILE == 0 and n_ctx % tl == 0
    n_ctx_tiles = n_ctx // TOK_TILE
    tiles_per_seq = tl // TOK_TILE
    past = cache_na_k.shape[2]

    def tile_mod(i):
        return jnp.where(i < n_ctx_tiles, 0, 1 + (i - n_ctx_tiles) // tiles_per_seq)

    def tile_rope(i):
        return jnp.where(i < n_ctx_tiles, tiles_per_seq, (i - n_ctx_tiles) % tiles_per_seq)

    x = jnp.concatenate([x_prompt.reshape(n_ctx, D_MODEL), x_sample.reshape(n_lat, D_MODEL)], axis=0)

    n_mod = 1 + bl
    mod_rows = -(-n_mod // 8) * 8
    cvecs = jnp.concatenate([c_ctx[None, :], c, jnp.zeros((mod_rows - n_mod, D_MODEL), F32)], axis=0)
    mods_all = _ada_mod(cvecs, w_ada, b_ada).reshape(depth, mod_rows, 6, D_MODEL)
    cos_tab, sin_tab = _rope_tables(tl)
    tile2 = lambda g: jnp.concatenate([g, g])[None, :]
    pad_lanes = lambda z: jnp.pad(z, ((0, 0), (0, LANES - z.shape[1])))
    zeros_lora = jnp.zeros((2, RK_DECAY_LORA, RK_WIDTH), F32)

    na_k_l, na_v_l, sw_k_l, sw_v_l, st_l = [], [], [], [], []
    for l in range(depth):
        mods = mods_all[l]
        qk_gains = jnp.concatenate(
            [tile2(na_q_norm[l]), tile2(na_k_norm[l]), tile2(swa_q_norm[l]), tile2(swa_k_norm[l])], axis=0)
        p = {
            "rk_conv": rk_conv[l], "rk_w0": rk_w0[l], "rk_a0": rk_a0[l], "rk_g2": rk_g2[l],
            "rk_w2_pad": jnp.concatenate([rk_w2[l], zeros_lora], axis=1),
            "rk_a2_pad": jnp.concatenate([zeros_lora, rk_a2[l]], axis=1),
            "rk_k_k": rk_k_k[l][None, :], "rk_k_a": rk_k_a[l][None, :],
            "rk_r_k": rk_r_k[l].reshape(1, RK_WIDTH),
            "rk_ln_g": rk_ln_g[l][None, :], "rk_ln_b": rk_ln_b[l][None, :],
            "w_out_bf16": w_out[l].astype(BF16), "norm2_g": norm2_g[l][None, :],
            "router_w_pad": pad_lanes(moe_router_w[l]), "router_b_pad": pad_lanes(moe_router_b[l][None, :]),
        }

        att, u = _in_proj(x, norm1_g[l][None, :], mods, w_in[l].astype(BF16), qk_gains, cos_tab, sin_tab,
                          tile_mod, tile_rope)
        q, g, bonus = _rk_prep(u, p, n_ctx_tiles, tiles_per_seq)
        att_c = att[:n_ctx].reshape(bc, tc, ATT_COLS)
        att_by_ctx_len = att.reshape((n_ctx + n_lat) // tc, tc, ATT_COLS)
        att_by_lat_len = att.reshape((n_ctx + n_lat) // tl, tl, ATT_COLS)
        na_k_l.append(att_c[:, :, NA_WIDTH:2 * NA_WIDTH].reshape(bc, tc, NA_HEADS, HEAD_DIM))
        na_v_l.append(att_c[:, :, 2 * NA_WIDTH:NA_COLS].reshape(bc, tc, NA_HEADS, HEAD_DIM))
        sw_k_l.append(att_c[:, :, NA_COLS + SWA_WIDTH:NA_COLS + SWA_WIDTH + SWA_KV_WIDTH]
                      .reshape(bc, tc, SWA_KV_HEADS, HEAD_DIM))
        sw_v_l.append(att_c[:, :, NA_COLS + SWA_WIDTH + SWA_KV_WIDTH:].reshape(bc, tc, SWA_KV_HEADS, HEAD_DIM))

        sink = swa_sink[l]
        o_na = jnp.concatenate([
            _ctx_attn(att_by_ctx_len, bc, sink, gqa=False).reshape(n_ctx, NA_WIDTH),
            _na_latent(att_by_lat_len, n_ctx // tl, cache_na_k[:, l].reshape(bl, past, NA_WIDTH),
                       cache_na_v[:, l].reshape(bl, past, NA_WIDTH),
                       _na_bias_tables(na_rpb[l])).reshape(n_lat, NA_WIDTH)], axis=0)
        o_sw = jnp.concatenate([
            _ctx_attn(att_by_ctx_len, bc, sink, gqa=True).reshape(n_ctx, SWA_WIDTH),
            _swa_latent(att_by_lat_len, n_ctx // tl, cache_swa_k[:, l].reshape(bl, past, SWA_KV_WIDTH),
                        cache_swa_v[:, l].reshape(bl, past, SWA_KV_WIDTH), sink).reshape(n_lat, SWA_WIDTH)],
            axis=0)

        yf_c, yb_c, s_fin = _rwkv_group(q, 0, bc, tc, jnp.zeros((bc, 2, RK_NB, PAIR, PAIR), F32))
        yf_l, yb_l, _ = _rwkv_group(q, n_ctx // RK_CHUNK, bl, tl, _pair_states(state_rwkv[:, l]))
        st_l.append(_head_states(s_fin))
        y_f = jnp.concatenate([yf_c, yf_l], axis=0)
        y_b = jnp.concatenate([yb_c, yb_l], axis=0)

        x1, h2, logits = _out_proj(x, o_na, o_sw, y_f, y_b, bonus, g, p, mods, tile_mod)
        meta, row_tok, gates, dest = _route(logits)
        h2 = jnp.concatenate([h2, jnp.zeros((H2_PAD_ROWS - h2.shape[0], D_MODEL), BF16)], axis=0)
        yb = _moe_blocks(meta, h2[row_tok], moe_w1, moe_b1[:, :, None, :], moe_w2, moe_b2[:, :, None, :], l)
        x = _combine(x1, yb[dest].reshape(TOP_K, n_ctx + n_lat, D_MODEL), gates, mods, tile_mod)

    y_p = x[:n_ctx].reshape(bc, tc, D_MODEL)
    y_s = x[n_ctx:].reshape(bl, tl, D_MODEL)
    return (y_p, y_s, jnp.stack(na_k_l, axis=1), jnp.stack(na_v_l, axis=1), jnp.stack(sw_k_l, axis=1),
            jnp.stack(sw_v_l, axis=1), jnp.stack(st_l, axis=1))
```

```python
import functools

import jax
import jax.numpy as jnp
from jax import lax
from jax.experimental import pallas as pl
from jax.experimental.pallas import tpu as pltpu

F32 = jnp.float32
BF16 = jnp.bfloat16

D_MODEL = 1024
HEAD_DIM = 64
LANES = 128
GRID_W = 64
NA_HEADS = 6
SWA_HEADS = 4
SWA_KV_HEADS = 2
RK_HEADS = 6
NA_WIDTH = NA_HEADS * HEAD_DIM
SWA_WIDTH = SWA_HEADS * HEAD_DIM
SWA_KV_WIDTH = SWA_KV_HEADS * HEAD_DIM
RK_WIDTH = RK_HEADS * HEAD_DIM
RK_DECAY_LORA = 64
RK_A_LORA = 64
RK_GATE_LORA = 128
RK_COLS = 3 * RK_WIDTH + RK_DECAY_LORA + RK_A_LORA + RK_GATE_LORA
NA_COLS = 3 * NA_WIDTH
SWA_COLS = SWA_WIDTH + 2 * SWA_KV_WIDTH
ATT_COLS = NA_COLS + SWA_COLS
IN_COLS = ATT_COLS + RK_COLS
NA_WIN_R = 8
NA_WIN_C = 16
SWA_WIN = 128
ROPE_THETA = 10000.0
ATTN_SCALE = HEAD_DIM ** -0.5
N_EXPERTS = 32
TOP_K = 4
SWIGLU_LIMIT = 7.0
SWIGLU_ALPHA = 1.702
MOE_BLK = 256
RMS_EPS = 1e-6
GN_EPS = 64e-5
NEG_BIG = -1e30

TOK_TILE = 256
VMEM_LIMIT = 48 * 1024 * 1024


def _cparams(sem):
    return pltpu.CompilerParams(dimension_semantics=sem, vmem_limit_bytes=VMEM_LIMIT)


def _dot(a, b):
    return jnp.dot(a, b, preferred_element_type=F32)


def _dot_nt(a, b):
    return lax.dot_general(a, b, (((1,), (1,)), ((), ())), preferred_element_type=F32)


def _split_bf16(x):
    hi = x.astype(BF16)
    lo = (x - hi.astype(F32)).astype(BF16)
    return hi, lo


def _dot3(a, b):
    ah, al = _split_bf16(a)
    bh, bl = _split_bf16(b)
    return _dot(ah, bh) + (_dot(ah, bl) + _dot(al, bh))


def _bmm_raw(a, b):
    return lax.dot_general(a, b, (((2,), (1,)), ((0,), (0,))), preferred_element_type=F32)


def _bmm(a, b):
    return _bmm_raw(a.astype(BF16), b.astype(BF16))


def _bmm_nt(a, b):
    return lax.dot_general(a.astype(BF16), b.astype(BF16), (((2,), (2,)), ((0,), (0,))),
                           preferred_element_type=F32)


def _bmm3(a, b):
    ah, al = _split_bf16(a)
    bh, bl = _split_bf16(b)
    return _bmm_raw(ah, bh) + (_bmm_raw(ah, bl) + _bmm_raw(al, bh))


def _lane_lo(shape):
    return lax.broadcasted_iota(jnp.int32, shape, len(shape) - 1) < HEAD_DIM


def _pair_sum(x):
    lo = _lane_lo(x.shape)
    s_lo = jnp.sum(jnp.where(lo, x, 0.0), axis=-1, keepdims=True)
    s_hi = jnp.sum(jnp.where(lo, 0.0, x), axis=-1, keepdims=True)
    return jnp.where(lo, s_lo, s_hi)


def _stack_heads(q):
    lo = _lane_lo(q.shape)
    return jnp.concatenate([jnp.where(lo, q, 0.0), jnp.where(lo, 0.0, q)], axis=0)


def _stack_heads3(x):
    lo = _lane_lo(x.shape)
    return jnp.concatenate([jnp.where(lo, x, 0.0), jnp.where(lo, 0.0, x)], axis=1)


def _unstack_heads(o2):
    n = o2.shape[0] // 2
    return jnp.where(_lane_lo((n, LANES)), o2[:n], o2[n:])


def _dup_head(x, j):
    keep = _lane_lo(x.shape) == (j == 0)
    return jnp.where(keep, x, pltpu.roll(x, HEAD_DIM, 1))


def _ada_kernel(c_ref, w_ref, b_ref, o_ref):
    cv = c_ref[...]
    s = cv * jax.nn.sigmoid(cv)
    o_ref[0] = _dot3(s, w_ref[0]) + b_ref[0]


def _ada_mod(cvecs, w_ada, b_ada):
    depth, _, n_out = w_ada.shape
    rows = cvecs.shape[0]
    tn = 1024
    return pl.pallas_call(
        _ada_kernel,
        grid=(depth, n_out // tn),
        in_specs=[
            pl.BlockSpec((rows, D_MODEL), lambda l, j: (0, 0)),
            pl.BlockSpec((1, D_MODEL, tn), lambda l, j: (l, 0, j)),
            pl.BlockSpec((1, 1, tn), lambda l, j: (l, 0, j)),
        ],
        out_specs=pl.BlockSpec((1, rows, tn), lambda l, j: (l, 0, j)),
        out_shape=jax.ShapeDtypeStruct((depth, rows, n_out), F32),
        compiler_params=_cparams(("parallel", "parallel")),
        name="ada_mod",
    )(cvecs, w_ada, b_ada.reshape(depth, 1, n_out))


NA_QK_BLOCKS = 2 * NA_WIDTH // LANES
SWA_Q_BLOCK0 = NA_COLS // LANES
SWA_QK_BLOCKS = (SWA_WIDTH + SWA_KV_WIDTH) // LANES


def _in_proj_kernel(x_ref, g_ref, mod_ref, w_ref, qkg_ref, cos_ref, sin_ref, att_ref, u_ref):
    x = x_ref[...]
    y = x * lax.rsqrt(jnp.mean(x * x, axis=-1, keepdims=True) + RMS_EPS)
    h = (y * g_ref[...]) * (1.0 + mod_ref[0, 1:2, :]) + mod_ref[0, 0:1, :]
    proj = _dot(h.astype(BF16), w_ref[...])
    u_ref[...] = proj[:, ATT_COLS:]

    def qk_norm(blk, gain):
        ms = _pair_sum(blk * blk) * (1.0 / HEAD_DIM)
        return blk * lax.rsqrt(ms + RMS_EPS) * gain

    lane = lax.broadcasted_iota(jnp.int32, (x.shape[0], LANES), 1)
    first = (lane % (HEAD_DIM // 2)) < (HEAD_DIM // 4)
    for cb in range(ATT_COLS // LANES):
        blk = proj[:, cb * LANES:(cb + 1) * LANES]
        if cb < NA_QK_BLOCKS:
            gi = 0 if cb < NA_QK_BLOCKS // 2 else 1
            blk = qk_norm(blk, qkg_ref[gi:gi + 1, :])
        elif SWA_Q_BLOCK0 <= cb < SWA_Q_BLOCK0 + SWA_QK_BLOCKS:
            gi = 2 if cb < SWA_Q_BLOCK0 + SWA_WIDTH // LANES else 3
            blk = qk_norm(blk, qkg_ref[gi:gi + 1, :])
            partner = jnp.where(first, pltpu.roll(blk, LANES - HEAD_DIM // 4, 1),
                                pltpu.roll(blk, HEAD_DIM // 4, 1))
            blk = blk * cos_ref[...] + partner * sin_ref[...]
        att_ref[:, cb * LANES:(cb + 1) * LANES] = blk


def _in_proj(x, norm_g, mods, w_in_bf16, qk_gains, cos_tab, sin_tab, tile_mod, tile_rope):
    n_tok = x.shape[0]
    return pl.pallas_call(
        _in_proj_kernel,
        grid=(n_tok // TOK_TILE,),
        in_specs=[
            pl.BlockSpec((TOK_TILE, D_MODEL), lambda i: (i, 0)),
            pl.BlockSpec((1, D_MODEL), lambda i: (0, 0)),
            pl.BlockSpec((1, 6, D_MODEL), lambda i: (tile_mod(i), 0, 0)),
            pl.BlockSpec((D_MODEL, IN_COLS), lambda i: (0, 0)),
            pl.BlockSpec((4, LANES), lambda i: (0, 0)),
            pl.BlockSpec((TOK_TILE, LANES), lambda i: (tile_rope(i), 0)),
            pl.BlockSpec((TOK_TILE, LANES), lambda i: (tile_rope(i), 0)),
        ],
        out_specs=[
            pl.BlockSpec((TOK_TILE, ATT_COLS), lambda i: (i, 0)),
            pl.BlockSpec((TOK_TILE, RK_COLS), lambda i: (i, 0)),
        ],
        out_shape=[
            jax.ShapeDtypeStruct((n_tok, ATT_COLS), F32),
            jax.ShapeDtypeStruct((n_tok, RK_COLS), F32),
        ],
        compiler_params=_cparams(("parallel",)),
        name="in_proj",
    )(x, norm_g, mods, w_in_bf16, qk_gains, cos_tab, sin_tab)


def _rope_tables(n_lat):
    nf = HEAD_DIM // 4
    t = jnp.arange(n_lat)
    lane = jnp.arange(LANES)
    d = lane % HEAD_DIM
    inv = ROPE_THETA ** (-(d % nf).astype(F32) / nf)
    pos = jnp.where((d // (2 * nf))[None, :] == 0, (t // GRID_W)[:, None], (t % GRID_W)[:, None]).astype(F32)
    ang = pos * inv[None, :]
    sign = jnp.where((d % (2 * nf)) < nf, -1.0, 1.0).astype(F32)
    cos = jnp.concatenate([jnp.cos(ang), jnp.ones((TOK_TILE, LANES), F32)], 0)
    sin = jnp.concatenate([jnp.sin(ang) * sign[None, :], jnp.zeros((TOK_TILE, LANES), F32)], 0)
    return cos, sin


def _ctx_attn_kernel(sink_ref, q_ref, k_ref, v_ref, o_ref, *, gqa):
    j = pl.program_id(1)
    k = k_ref[0]
    v = v_ref[0]
    if gqa:
        k = _dup_head(k, j)
        v = _dup_head(v, j)
    n = k.shape[0]
    q2 = _stack_heads(q_ref[0]).astype(BF16)
    s = _dot_nt(q2, k.astype(BF16)) * ATTN_SCALE
    m = jnp.max(s, axis=-1, keepdims=True)
    if gqa:
        row = lax.broadcasted_iota(jnp.int32, (2 * n, 1), 0)
        snk = jnp.where(row < n, sink_ref[2 * j], sink_ref[2 * j + 1])
        m = jnp.maximum(m, snk)
    p = jnp.exp(s - m)
    den = jnp.sum(p, axis=-1, keepdims=True)
    if gqa:
        den = den + jnp.exp(snk - m)
    o2 = _dot(p.astype(BF16), v.astype(BF16)) / den
    o_ref[0] = _unstack_heads(o2)


def _ctx_attn(att, b, sink, *, gqa):
    t = att.shape[1]
    if gqa:
        nq = SWA_WIDTH // LANES
        qb, kb, vb = SWA_Q_BLOCK0, SWA_Q_BLOCK0 + nq, SWA_Q_BLOCK0 + nq + 1
        kmap = lambda bi, j: (bi, 0, kb)
        vmap = lambda bi, j: (bi, 0, vb)
    else:
        nq = NA_WIDTH // LANES
        qb, kb, vb = 0, nq, 2 * nq
        kmap = lambda bi, j: (bi, 0, kb + j)
        vmap = lambda bi, j: (bi, 0, vb + j)
    return pl.pallas_call(
        functools.partial(_ctx_attn_kernel, gqa=gqa),
        grid=(b, nq),
        in_specs=[
            pl.BlockSpec(memory_space=pltpu.SMEM),
            pl.BlockSpec((1, t, LANES), lambda bi, j: (bi, 0, qb + j)),
            pl.BlockSpec((1, t, LANES), kmap),
            pl.BlockSpec((1, t, LANES), vmap),
        ],
        out_specs=pl.BlockSpec((1, t, LANES), lambda bi, j: (bi, 0, j)),
        out_shape=jax.ShapeDtypeStruct((b, t, nq * LANES), F32),
        compiler_params=_cparams(("parallel", "parallel")),
        name="ctx_attn_swa" if gqa else "ctx_attn_na",
    )(sink, att, att, att)


NA_ROWS_PER_ITER = 4


def _na_lat_kernel(q_ref, k_ref, v_ref, kc_ref, vc_ref, tab_ref, o_ref, kb_ref, vb_ref):
    n = q_ref.shape[1]
    rows = n // GRID_W
    win = NA_WIN_R * GRID_W
    kb_ref[...] = k_ref[0].astype(BF16)
    vb_ref[...] = v_ref[0].astype(BF16)
    kc = kc_ref[0].astype(BF16)
    vc = vc_ref[0].astype(BF16)

    def row_group(ig, carry):
        nr = NA_ROWS_PER_ITER
        g0 = pl.multiple_of(ig * (nr * GRID_W), nr * GRID_W)
        q2 = _stack_heads3(q_ref[0, pl.ds(g0, nr * GRID_W), :].reshape(nr, GRID_W, LANES)).astype(BF16)
        kws, vws, biases = [], [], []
        for r in range(nr):
            i = ig * nr + r
            start = jnp.clip(i - NA_WIN_R // 2, 0, rows - NA_WIN_R)
            k0 = pl.multiple_of(start * GRID_W, GRID_W)
            kws.append(kb_ref[pl.ds(k0, win), :])
            vws.append(vb_ref[pl.ds(k0, win), :])
            biases.append(tab_ref[0, start - i + (NA_WIN_R - 1)])
        s_loc = _bmm_nt(q2, jnp.stack(kws)) * ATTN_SCALE + jnp.stack(biases)
        s_ctx = _dot_nt(q2.reshape(nr * 2 * GRID_W, LANES), kc).reshape(nr, 2 * GRID_W, -1) * ATTN_SCALE
        m = jnp.maximum(jnp.max(s_loc, axis=-1, keepdims=True), jnp.max(s_ctx, axis=-1, keepdims=True))
        p_loc = jnp.exp(s_loc - m)
        p_ctx = jnp.exp(s_ctx - m)
        den = jnp.sum(p_loc, axis=-1, keepdims=True) + jnp.sum(p_ctx, axis=-1, keepdims=True)
        o_ctx = _dot(p_ctx.reshape(nr * 2 * GRID_W, -1).astype(BF16), vc).reshape(nr, 2 * GRID_W, LANES)
        o2 = (_bmm(p_loc, jnp.stack(vws)) + o_ctx) / den
        out = jnp.where(_lane_lo((nr, GRID_W, LANES)), o2[:, :GRID_W], o2[:, GRID_W:])
        o_ref[0, pl.ds(g0, nr * GRID_W), :] = out.reshape(nr * GRID_W, LANES)
        return carry

    lax.fori_loop(0, rows // NA_ROWS_PER_ITER, row_group, 0)


def _na_bias_tables(rpb):
    col = jnp.arange(GRID_W)
    cstart = jnp.clip(col - NA_WIN_C // 2, 0, GRID_W - NA_WIN_C)
    col_mask = (col[None, :] >= cstart[:, None]) & (col[None, :] < cstart[:, None] + NA_WIN_C)
    col_idx = jnp.clip(col[None, :] - col[:, None] + NA_WIN_C - 1, 0, 2 * NA_WIN_C - 2)
    rpb_cols = jnp.where(col_mask[None, None], rpb[:, :, col_idx], NEG_BIG)
    roff = jnp.arange(NA_WIN_R)[:, None] + jnp.arange(NA_WIN_R)[None, :]
    t = rpb_cols[:, roff]
    t = jnp.transpose(t, (0, 1, 3, 2, 4)).reshape(NA_HEADS // 2, 2, NA_WIN_R, GRID_W, NA_WIN_R * GRID_W)
    return jnp.transpose(t, (0, 2, 1, 3, 4)).reshape(NA_HEADS // 2, NA_WIN_R, 2 * GRID_W, NA_WIN_R * GRID_W)


def _na_latent(att, s0, kc, vc, tab):
    n = att.shape[1]
    b, p, _ = kc.shape
    nq = NA_WIDTH // LANES
    return pl.pallas_call(
        _na_lat_kernel,
        grid=(b, nq),
        in_specs=[
            pl.BlockSpec((1, n, LANES), lambda bi, j: (s0 + bi, 0, j)),
            pl.BlockSpec((1, n, LANES), lambda bi, j: (s0 + bi, 0, nq + j)),
            pl.BlockSpec((1, n, LANES), lambda bi, j: (s0 + bi, 0, 2 * nq + j)),
            pl.BlockSpec((1, p, LANES), lambda bi, j: (bi, 0, j)),
            pl.BlockSpec((1, p, LANES), lambda bi, j: (bi, 0, j)),
            pl.BlockSpec((1, NA_WIN_R, 2 * GRID_W, NA_WIN_R * GRID_W), lambda bi, j: (j, 0, 0, 0)),
        ],
        out_specs=pl.BlockSpec((1, n, LANES), lambda bi, j: (bi, 0, j)),
        out_shape=jax.ShapeDtypeStruct((b, n, NA_WIDTH), F32),
        scratch_shapes=[pltpu.VMEM((n, LANES), BF16), pltpu.VMEM((n, LANES), BF16)],
        compiler_params=_cparams(("parallel", "parallel")),
        name="na_latent",
    )(att, att, att, kc, vc, tab)


SWA_BLOCKS_PER_ITER = 2


def _swa_lat_kernel(sink_ref, q_ref, k_ref, v_ref, kc_ref, vc_ref, o_ref, kb_ref, vb_ref):
    j = pl.program_id(1)
    n = q_ref.shape[1]
    blk = SWA_WIN
    span = 3 * blk
    kb_ref[...] = _dup_head(k_ref[0], j).astype(BF16)
    vb_ref[...] = _dup_head(v_ref[0], j).astype(BF16)
    kc = _dup_head(kc_ref[0], j).astype(BF16)
    vc = _dup_head(vc_ref[0], j).astype(BF16)
    row = lax.broadcasted_iota(jnp.int32, (2 * blk, 1), 0)
    snk = jnp.where(row < blk, sink_ref[2 * j], sink_ref[2 * j + 1])
    qoff = lax.broadcasted_iota(jnp.int32, (2 * blk, span), 0) % blk
    koff = lax.broadcasted_iota(jnp.int32, (2 * blk, span), 1)

    def q_group(qg, carry):
        nr = SWA_BLOCKS_PER_ITER
        g0 = pl.multiple_of(qg * (nr * blk), nr * blk)
        q2 = _stack_heads3(q_ref[0, pl.ds(g0, nr * blk), :].reshape(nr, blk, LANES)).astype(BF16)
        kws, vws, valids = [], [], []
        for r in range(nr):
            q0 = g0 + r * blk
            w0 = pl.multiple_of(jnp.clip(q0 - blk, 0, n - span), blk)
            kws.append(kb_ref[pl.ds(w0, span), :])
            vws.append(vb_ref[pl.ds(w0, span), :])
            valids.append(jnp.abs((q0 + qoff) - (w0 + koff)) <= SWA_WIN)
        s_loc = jnp.where(jnp.stack(valids), _bmm_nt(q2, jnp.stack(kws)) * ATTN_SCALE, NEG_BIG)
        s_ctx = _dot_nt(q2.reshape(nr * 2 * blk, LANES), kc).reshape(nr, 2 * blk, -1) * ATTN_SCALE
        m = jnp.maximum(jnp.max(s_loc, axis=-1, keepdims=True), jnp.max(s_ctx, axis=-1, keepdims=True))
        m = jnp.maximum(m, snk)
        p_loc = jnp.exp(s_loc - m)
        p_ctx = jnp.exp(s_ctx - m)
        den = (jnp.sum(p_loc, axis=-1, keepdims=True) + jnp.sum(p_ctx, axis=-1, keepdims=True)
               + jnp.exp(snk - m))
        o_ctx = _dot(p_ctx.reshape(nr * 2 * blk, -1).astype(BF16), vc).reshape(nr, 2 * blk, LANES)
        o2 = (_bmm(p_loc, jnp.stack(vws)) + o_ctx) / den
        out = jnp.where(_lane_lo((nr, blk, LANES)), o2[:, :blk], o2[:, blk:])
        o_ref[0, pl.ds(g0, nr * blk), :] = out.reshape(nr * blk, LANES)
        return carry

    lax.fori_loop(0, n // (SWA_BLOCKS_PER_ITER * blk), q_group, 0)


def _swa_latent(att, s0, kc, vc, sink):
    n = att.shape[1]
    b, p, _ = kc.shape
    nq = SWA_WIDTH // LANES
    qb, kb, vb = SWA_Q_BLOCK0, SWA_Q_BLOCK0 + nq, SWA_Q_BLOCK0 + nq + 1
    return pl.pallas_call(
        _swa_lat_kernel,
        grid=(b, nq),
        in_specs=[
            pl.BlockSpec(memory_space=pltpu.SMEM),
            pl.BlockSpec((1, n, LANES), lambda bi, j: (s0 + bi, 0, qb + j)),
            pl.BlockSpec((1, n, LANES), lambda bi, j: (s0 + bi, 0, kb)),
            pl.BlockSpec((1, n, LANES), lambda bi, j: (s0 + bi, 0, vb)),
            pl.BlockSpec((1, p, LANES), lambda bi, j: (bi, 0, 0)),
            pl.BlockSpec((1, p, LANES), lambda bi, j: (bi, 0, 0)),
        ],
        out_specs=pl.BlockSpec((1, n, LANES), lambda bi, j: (bi, 0, j)),
        out_shape=jax.ShapeDtypeStruct((b, n, SWA_WIDTH), F32),
        scratch_shapes=[pltpu.VMEM((n, LANES), BF16), pltpu.VMEM((n, LANES), BF16)],
        compiler_params=_cparams(("parallel", "parallel")),
        name="swa_latent",
    )(sink, att, att, att, kc, vc)


RK_NB = RK_WIDTH // LANES
LORA_BLOCK = 3 * RK_WIDTH // LANES
GATE_BLOCK = LORA_BLOCK + 1
Q_R, Q_V, Q_A, Q_W, Q_K, Q_B = range(6)
Q_DIR = 3
Q_COLS = (6 + Q_DIR) * RK_WIDTH


def _softplus(x):
    return jnp.maximum(x, 0.0) + jnp.log(1.0 + jnp.exp(-jnp.abs(x)))


def _rk_prep_kernel(u_ref, up_ref, un_ref, cw_ref, w0_ref, w2_ref, a0_ref, a2_ref, g2_ref, kk_ref, ka_ref,
                    rk_ref, q_ref, g_ref, bonus_ref, *, n_ctx_tiles, tiles_per_seq):
    def put(slot, val):
        q_ref[:, slot * RK_WIDTH:(slot + 1) * RK_WIDTH] = val

    i = pl.program_id(0)
    li = i - n_ctx_tiles
    is_lat = i >= n_ctx_tiles
    has_prev = jnp.logical_and(is_lat, li % tiles_per_seq != 0)
    has_next = jnp.logical_and(is_lat, li % tiles_per_seq != tiles_per_seq - 1)
    u = u_ref[...]
    tm = u.shape[0]
    prev_row = jnp.where(has_prev, up_ref[7:8, :], 0.0)
    next_row = jnp.where(has_next, un_ref[0:1, :], 0.0)
    row = lax.broadcasted_iota(jnp.int32, u.shape, 0)
    um = jnp.where(row == 0, prev_row, pltpu.roll(u, 1, 0))
    up = jnp.where(row == tm - 1, next_row, pltpu.roll(u, tm - 1, 0))
    u = um * cw_ref[0:1, :] + u * cw_ref[1:2, :] + up * cw_ref[2:3, :]

    r = u[:, 0:RK_WIDTH]
    k = u[:, RK_WIDTH:2 * RK_WIDTH]
    v = u[:, 2 * RK_WIDTH:3 * RK_WIDTH]
    lora = u[:, LORA_BLOCK * LANES:(LORA_BLOCK + 1) * LANES]
    gl = u[:, GATE_BLOCK * LANES:(GATE_BLOCK + 1) * LANES]
    put(Q_R, r)
    put(Q_V, v)
    g_ref[...] = _dot3(jax.nn.sigmoid(gl), g2_ref[...])

    kn = k * kk_ref[...]
    kk = jnp.concatenate(
        [kn[:, c * LANES:(c + 1) * LANES]
         * lax.rsqrt(jnp.maximum(_pair_sum(jnp.square(kn[:, c * LANES:(c + 1) * LANES])), 1e-24))
         for c in range(RK_NB)], axis=1)
    put(Q_A, -kk)

    lora_t = jnp.tanh(lora)
    kd_sum = None
    for d in range(2):
        w = -_softplus(-(w0_ref[d:d + 1, :] + _dot3(lora_t, w2_ref[d]))) - 0.5
        put(Q_W + Q_DIR * d, -jnp.exp(w))
        a = jax.nn.sigmoid(a0_ref[d:d + 1, :] + _dot3(lora, a2_ref[d]))
        kd = k * (1.0 + (a - 1.0) * ka_ref[...])
        put(Q_K + Q_DIR * d, kd)
        put(Q_B + Q_DIR * d, kk * a)
        kd_sum = kd if kd_sum is None else kd_sum + kd

    t = r * kd_sum * rk_ref[...]
    bonus_ref[...] = jnp.concatenate(
        [_pair_sum(t[:, c * LANES:(c + 1) * LANES]) for c in range(RK_NB)], axis=1) * v


def _rk_prep(u, p, n_ctx_tiles, tiles_per_seq):
    n_tok = u.shape[0]
    n_tiles = n_tok // TOK_TILE
    sub = TOK_TILE // 8
    last8 = n_tok // 8 - 1
    tok = lambda i: (i, 0)
    const2 = lambda i: (0, 0)
    const3 = lambda i: (0, 0, 0)
    one = jax.ShapeDtypeStruct((n_tok, RK_WIDTH), F32)
    tok_spec = pl.BlockSpec((TOK_TILE, RK_WIDTH), tok)
    return pl.pallas_call(
        functools.partial(_rk_prep_kernel, n_ctx_tiles=n_ctx_tiles, tiles_per_seq=tiles_per_seq),
        grid=(n_tiles,),
        in_specs=[
            pl.BlockSpec((TOK_TILE, RK_COLS), tok),
            pl.BlockSpec((8, RK_COLS), lambda i: (jnp.maximum(i * sub - 1, 0), 0)),
            pl.BlockSpec((8, RK_COLS), lambda i: (jnp.minimum((i + 1) * sub, last8), 0)),
            pl.BlockSpec((3, RK_COLS), const2),
            pl.BlockSpec((2, RK_WIDTH), const2),
            pl.BlockSpec((2, LANES, RK_WIDTH), const3),
            pl.BlockSpec((2, RK_WIDTH), const2),
            pl.BlockSpec((2, LANES, RK_WIDTH), const3),
            pl.BlockSpec((RK_GATE_LORA, RK_WIDTH), const2),
            pl.BlockSpec((1, RK_WIDTH), const2),
            pl.BlockSpec((1, RK_WIDTH), const2),
            pl.BlockSpec((1, RK_WIDTH), const2),
        ],
        out_specs=[pl.BlockSpec((TOK_TILE, Q_COLS), tok), tok_spec, tok_spec],
        out_shape=[jax.ShapeDtypeStruct((n_tok, Q_COLS), F32), one, one],
        compiler_params=_cparams(("parallel",)),
        name="rk_prep",
    )(u, u, u, p["rk_conv"], p["rk_w0"], p["rk_w2_pad"], p["rk_a0"], p["rk_a2_pad"], p["rk_g2"],
      p["rk_k_k"], p["rk_k_a"], p["rk_r_k"])


RK_CHUNK = 64
PAIR = 2 * HEAD_DIM
STATE_SEQS = 8


def _split3_bf16(x):
    hi = x.astype(BF16)
    r1 = x - hi.astype(F32)
    mid = r1.astype(BF16)
    return hi, mid, (r1 - mid.astype(F32)).astype(BF16)


def _rk_chunk_kernel(q_ref, rbar_ref, ybar_ref, phi_ref, psi_ref):
    c = RK_CHUNK
    n = 2 * c
    nu = 2 * RK_NB

    def tiles(slot, per_dir):
        cols = [(slot + (Q_DIR * d if per_dir else 0)) * RK_WIDTH + p * LANES
                for d in range(2) for p in range(RK_NB)]
        return jnp.stack([q_ref[:, lo:lo + LANES] for lo in cols])

    r, v, a = tiles(Q_R, False), tiles(Q_V, False), tiles(Q_A, False)
    lw, k, b = tiles(Q_W, True), tiles(Q_K, True), tiles(Q_B, True)
    sgn = jnp.where(lax.broadcasted_iota(jnp.int32, (nu, 1, 1), 0) >= RK_NB, -1, 1)
    bwd = sgn < 0
    tdiff = lax.broadcasted_iota(jnp.int32, (1, c, c), 2) - lax.broadcasted_iota(jnp.int32, (1, c, c), 1)
    tri = jnp.where(tdiff * sgn <= 0, 1.0, 0.0)
    cum = sum(_bmm(tri, part) for part in _split3_bf16(lw))
    tot = jnp.where(bwd, cum[:, 0:1], cum[:, c - 1:c])
    a_t = a * jnp.exp(cum - lw)
    r_t = r * jnp.exp(cum)
    e_neg = jnp.exp(-cum)
    e_end = jnp.exp(tot - cum)
    g = _bmm_nt(jnp.concatenate([_stack_heads3(a_t), _stack_heads3(r_t)], axis=1),
                jnp.concatenate([_stack_heads3(b * e_neg), _stack_heads3(k * e_neg)], axis=1))
    r2 = lax.broadcasted_iota(jnp.int32, (1, n, n), 1)
    c2 = lax.broadcasted_iota(jnp.int32, (1, n, n), 2)
    order = (jnp.bitwise_and(c2, c - 1) - jnp.bitwise_and(r2, c - 1)) * sgn
    eye = jnp.where(r2 == c2, 1.0, 0.0)
    l_ab = jnp.where(order < 0, g[:, :n, :n], 0.0)
    l_ak = jnp.where(order < 0, g[:, :n, n:], 0.0)
    m_rb = jnp.where(order <= 0, g[:, n:, :n], 0.0)
    m_rk = jnp.where(order <= 0, g[:, n:, n:], 0.0)
    t_inv = eye + l_ab
    pw = l_ab
    for _ in range(5):
        pw = _bmm(pw, pw)
        t_inv = t_inv + _bmm(t_inv, pw)
    sv = _stack_heads3(v)
    au = _bmm(t_inv, jnp.concatenate([_stack_heads3(a_t), _bmm(l_ak, sv)], axis=2))
    ry = _bmm(m_rb, au) + jnp.concatenate([_stack_heads3(r_t), _bmm(m_rk, sv)], axis=2)
    ry = ry[:, :c] + ry[:, c:]
    bt = jnp.swapaxes(_stack_heads3(b * e_end), 1, 2)
    kt = jnp.swapaxes(_stack_heads3(k * e_end), 1, 2)
    pp = _bmm(bt, au)
    phi = eye * jnp.exp(tot) + pp[:, :, :PAIR]
    psi = pp[:, :, PAIR:] + _bmm(kt, sv)
    for d in range(2):
        for p in range(RK_NB):
            u = d * RK_NB + p
            rbar_ref[d, :, p * LANES:(p + 1) * LANES] = ry[u, :, :PAIR]
            ybar_ref[d, :, p * LANES:(p + 1) * LANES] = ry[u, :, PAIR:]
            phi_ref[d, p] = phi[u]
            psi_ref[d, p] = psi[u]


def _rk_chunk(q, tile0, n_seq, t):
    nc = t // RK_CHUNK
    row_sh = jax.ShapeDtypeStruct((2, n_seq, t, RK_WIDTH), F32)
    mat_sh = jax.ShapeDtypeStruct((2, n_seq, nc, RK_NB, PAIR, PAIR), F32)
    row_spec = pl.BlockSpec((2, None, RK_CHUNK, RK_WIDTH), lambda s, c: (0, s, c, 0))
    mat_spec = pl.BlockSpec((2, None, None, RK_NB, PAIR, PAIR), lambda s, c: (0, s, c, 0, 0, 0))
    return pl.pallas_call(
        _rk_chunk_kernel,
        grid=(n_seq, nc),
        in_specs=[pl.BlockSpec((RK_CHUNK, Q_COLS), lambda s, c: (tile0 + s * nc + c, 0))],
        out_specs=[row_spec, row_spec, mat_spec, mat_spec],
        out_shape=[row_sh, row_sh, mat_sh, mat_sh],
        compiler_params=_cparams(("parallel", "parallel")),
        name="rk_chunk",
    )(q)


def _rk_state_kernel(rf_ref, rb_ref, yf_ref, yb_ref, phf_ref, phb_ref, psf_ref, psb_ref, s0_ref,
                     of_ref, ob_ref, s_ref):
    @pl.when(pl.program_id(1) == 0)
    def _():
        s_ref[...] = s0_ref[...]

    def pair_tiles(ref_f, ref_b, s):
        return jnp.stack([ref[s, :, p * LANES:(p + 1) * LANES] for ref in (ref_f, ref_b) for p in range(RK_NB)])

    def seq_body(s, carry):
        nu = 2 * RK_NB
        h = s_ref[s].reshape(nu, PAIR, PAIR)
        y = _bmm3(pair_tiles(rf_ref, rb_ref, s), h) + pair_tiles(yf_ref, yb_ref, s)
        phi = jnp.concatenate([phf_ref[s], phb_ref[s]], axis=0)
        psi = jnp.concatenate([psf_ref[s], psb_ref[s]], axis=0)
        s_ref[s] = (_bmm3(phi, h) + psi).reshape(2, RK_NB, PAIR, PAIR)
        of_ref[s] = jnp.concatenate([y[p] for p in range(RK_NB)], axis=1)
        ob_ref[s] = jnp.concatenate([y[RK_NB + p] for p in range(RK_NB)], axis=1)
        return carry

    lax.fori_loop(0, s_ref.shape[0], seq_body, 0)


def _rk_state(rbar, ybar, phi, psi, s0):
    _, n_seq, t, _ = rbar.shape
    nc = t // RK_CHUNK
    sg = STATE_SEQS
    row_blk = (None, sg, RK_CHUNK, RK_WIDTH)
    mat_blk = (None, sg, None, RK_NB, PAIR, PAIR)
    fwd_row = pl.BlockSpec(row_blk, lambda g, c: (0, g, c, 0))
    bwd_row = pl.BlockSpec(row_blk, lambda g, c: (1, g, nc - 1 - c, 0))
    fwd_mat = pl.BlockSpec(mat_blk, lambda g, c: (0, g, c, 0, 0, 0))
    bwd_mat = pl.BlockSpec(mat_blk, lambda g, c: (1, g, nc - 1 - c, 0, 0, 0))
    st = pl.BlockSpec((sg, 2, RK_NB, PAIR, PAIR), lambda g, c: (g, 0, 0, 0, 0))
    out_sh = jax.ShapeDtypeStruct((n_seq, t, RK_WIDTH), F32)
    return pl.pallas_call(
        _rk_state_kernel,
        grid=(n_seq // sg, nc),
        in_specs=[fwd_row, bwd_row, fwd_row, bwd_row, fwd_mat, bwd_mat, fwd_mat, bwd_mat, st],
        out_specs=[pl.BlockSpec((sg, RK_CHUNK, RK_WIDTH), lambda g, c: (g, c, 0)),
                   pl.BlockSpec((sg, RK_CHUNK, RK_WIDTH), lambda g, c: (g, nc - 1 - c, 0)), st],
        out_shape=[out_sh, out_sh, jax.ShapeDtypeStruct((n_seq, 2, RK_NB, PAIR, PAIR), F32)],
        compiler_params=_cparams(("parallel", "arbitrary")),
        name="rk_state",
    )(rbar, rbar, ybar, ybar, phi, phi, psi, psi, s0)


def _pair_states(s):
    bsz = s.shape[0]
    h = jnp.swapaxes(s, -1, -2).reshape(bsz, 2, RK_NB, 2, HEAD_DIM, HEAD_DIM)
    return jnp.einsum("bdphkv,hg->bdphkgv", h, jnp.eye(2, dtype=F32)).reshape(bsz, 2, RK_NB, PAIR, PAIR)


def _head_states(s):
    bsz = s.shape[0]
    h = s.reshape(bsz, 2, RK_NB, 2, HEAD_DIM, 2, HEAD_DIM)
    h = jnp.stack([h[:, :, :, 0, :, 0, :], h[:, :, :, 1, :, 1, :]], axis=3)
    return jnp.swapaxes(h.reshape(bsz, 2, RK_HEADS, HEAD_DIM, HEAD_DIM), -1, -2)


def _rwkv_group(q, tile0, n_seq, t, s0):
    rbar, ybar, phi, psi = _rk_chunk(q, tile0, n_seq, t)
    y_f, y_b, s_fin = _rk_state(rbar, ybar, phi, psi, s0)
    return y_f.reshape(n_seq * t, RK_WIDTH), y_b.reshape(n_seq * t, RK_WIDTH), s_fin


def _out_proj_kernel(x_ref, ona_ref, osw_ref, yf_ref, yb_ref, bonus_ref, g_ref, lng_ref, lnb_ref, w_ref, mod_ref,
                     n2_ref, rw_ref, rb_ref, x1_ref, h2_ref, lg_ref):
    y = yf_ref[...] + yb_ref[...]
    parts = []
    for c in range(RK_NB):
        yc = y[:, c * LANES:(c + 1) * LANES]
        dc = yc - _pair_sum(yc) * (1.0 / HEAD_DIM)
        var = _pair_sum(dc * dc) * (1.0 / HEAD_DIM)
        parts.append(dc * lax.rsqrt(var + GN_EPS))
    yn = jnp.concatenate(parts, axis=1) * lng_ref[...] + lnb_ref[...]
    o_rk = (yn + bonus_ref[...]) * g_ref[...]
    o = (_dot(ona_ref[...].astype(BF16), w_ref[0:NA_WIDTH, :])
         + _dot(osw_ref[...].astype(BF16), w_ref[NA_WIDTH:NA_WIDTH + SWA_WIDTH, :])
         + _dot(o_rk.astype(BF16), w_ref[NA_WIDTH + SWA_WIDTH:, :]))
    x1 = x_ref[...] + mod_ref[0, 2:3, :] * o
    x1_ref[...] = x1
    yn2 = x1 * lax.rsqrt(jnp.mean(x1 * x1, axis=-1, keepdims=True) + RMS_EPS)
    h2 = (yn2 * n2_ref[...]) * (1.0 + mod_ref[0, 4:5, :]) + mod_ref[0, 3:4, :]
    h2_ref[...] = h2.astype(BF16)
    lg_ref[...] = (_dot3(h2, rw_ref[...]) + rb_ref[...])[:, :N_EXPERTS]


def _out_proj(x, o_na, o_sw, y_f, y_b, bonus, g, p, mods, tile_mod):
    n_tok = x.shape[0]
    tok = lambda i: (i, 0)
    const = lambda i: (0, 0)
    return pl.pallas_call(
        _out_proj_kernel,
        grid=(n_tok // TOK_TILE,),
        in_specs=[
            pl.BlockSpec((TOK_TILE, D_MODEL), tok),
            pl.BlockSpec((TOK_TILE, NA_WIDTH), tok),
            pl.BlockSpec((TOK_TILE, SWA_WIDTH), tok),
            pl.BlockSpec((TOK_TILE, RK_WIDTH), tok),
            pl.BlockSpec((TOK_TILE, RK_WIDTH), tok),
            pl.BlockSpec((TOK_TILE, RK_WIDTH), tok),
            pl.BlockSpec((TOK_TILE, RK_WIDTH), tok),
            pl.BlockSpec((1, RK_WIDTH), const),
            pl.BlockSpec((1, RK_WIDTH), const),
            pl.BlockSpec((D_MODEL, D_MODEL), const),
            pl.BlockSpec((1, 6, D_MODEL), lambda i: (tile_mod(i), 0, 0)),
            pl.BlockSpec((1, D_MODEL), const),
            pl.BlockSpec((D_MODEL, LANES), const),
            pl.BlockSpec((1, LANES), const),
        ],
        out_specs=[
            pl.BlockSpec((TOK_TILE, D_MODEL), tok),
            pl.BlockSpec((TOK_TILE, D_MODEL), tok),
            pl.BlockSpec((TOK_TILE, N_EXPERTS), tok),
        ],
        out_shape=[
            jax.ShapeDtypeStruct((n_tok, D_MODEL), F32),
            jax.ShapeDtypeStruct((n_tok, D_MODEL), BF16),
            jax.ShapeDtypeStruct((n_tok, N_EXPERTS), F32),
        ],
        compiler_params=_cparams(("parallel",)),
        name="out_proj",
    )(x, o_na, o_sw, y_f, y_b, bonus, g, p["rk_ln_g"], p["rk_ln_b"], p["w_out_bf16"], mods, p["norm2_g"],
      p["router_w_pad"], p["router_b_pad"])


H2_PAD_ROWS = 32768
W2_STAGE_ROWS = 128
MOE_VMEM_LIMIT = 56 * 1024 * 1024


def _moe_kernel(meta_ref, x_ref, w1_ref, b1_ref, w2_ref, b2_ref, o_ref, w1b_ref, w2e_ref, stage_ref):
    i = pl.program_id(0)
    n_blk = meta_ref.shape[0] - 1
    n_used = meta_ref[n_blk]
    d_e = w2_ref.shape[1]
    new_expert = jnp.logical_or(i == 0, meta_ref[i] != meta_ref[jnp.maximum(i - 1, 0)])

    @pl.when(i == 0)
    def _():
        stage_ref[...] = jnp.zeros_like(stage_ref)

    @pl.when(jnp.logical_and(i < n_used, new_expert))
    def _():
        w1b_ref[...] = w1_ref[0].astype(BF16)
        for c in range(d_e // W2_STAGE_ROWS):
            rows = slice(c * W2_STAGE_ROWS, (c + 1) * W2_STAGE_ROWS)
            for cb in range(D_MODEL // LANES):
                cols = slice(cb * LANES, (cb + 1) * LANES)
                stage_ref[cb, pl.ds(0, W2_STAGE_ROWS, stride=2), :] = w2_ref[0, rows, cols]
                w2e_ref[2 * c * W2_STAGE_ROWS:2 * (c + 1) * W2_STAGE_ROWS, cols] = stage_ref[cb].astype(BF16)

    @pl.when(i < n_used)
    def _():
        uu = _dot(x_ref[...], w1b_ref[...]) + b1_ref[0]
        acts = []
        for c in range(uu.shape[1] // LANES):
            blk = uu[:, c * LANES:(c + 1) * LANES]
            glu = jnp.minimum(blk, SWIGLU_LIMIT)
            lin = jnp.clip(pltpu.roll(blk, LANES - 1, 1), -SWIGLU_LIMIT, SWIGLU_LIMIT)
            acts.append((glu * jax.nn.sigmoid(SWIGLU_ALPHA * glu) * (lin + 1.0)).astype(BF16))
        act = jnp.concatenate(acts, axis=1)
        o_ref[...] = (_dot(act, w2e_ref[...]) + b2_ref[0]).astype(BF16)

    @pl.when(i >= n_used)
    def _():
        o_ref[...] = jnp.zeros_like(o_ref)


def _moe_blocks(meta, xb, w1, b1, w2, b2, layer):
    n_rows = xb.shape[0]
    n_blk = n_rows // MOE_BLK
    d_e = w2.shape[2]
    row = lambda i, m: (i, 0)
    exp3 = lambda i, m: (layer, m[i], 0, 0)
    grid_spec = pltpu.PrefetchScalarGridSpec(
        num_scalar_prefetch=1,
        grid=(n_blk,),
        in_specs=[
            pl.BlockSpec((MOE_BLK, D_MODEL), row),
            pl.BlockSpec((None, 1, D_MODEL, 2 * d_e), exp3),
            pl.BlockSpec((None, 1, 1, 2 * d_e), exp3),
            pl.BlockSpec((None, 1, d_e, D_MODEL), exp3),
            pl.BlockSpec((None, 1, 1, D_MODEL), exp3),
        ],
        out_specs=pl.BlockSpec((MOE_BLK, D_MODEL), row),
        scratch_shapes=[
            pltpu.VMEM((D_MODEL, 2 * d_e), BF16),
            pltpu.VMEM((2 * d_e, D_MODEL), BF16),
            pltpu.VMEM((D_MODEL // LANES, 2 * W2_STAGE_ROWS, LANES), F32),
        ],
    )
    return pl.pallas_call(
        _moe_kernel,
        grid_spec=grid_spec,
        out_shape=jax.ShapeDtypeStruct((n_rows, D_MODEL), BF16),
        compiler_params=pltpu.CompilerParams(dimension_semantics=("arbitrary",),
                                             vmem_limit_bytes=MOE_VMEM_LIMIT),
        name="moe_blocks",
    )(meta, xb, w1, b1, w2, b2)


def _route(logits):
    n_tok = logits.shape[0]
    top_v, top_i = lax.top_k(logits, TOP_K)
    gates = jax.nn.softmax(top_v, axis=-1)
    e_flat = top_i.reshape(-1).astype(jnp.int32)
    n_rows = n_tok * TOP_K
    onehot = (e_flat[:, None] == jnp.arange(N_EXPERTS, dtype=jnp.int32)[None, :]).astype(jnp.int32)
    csum = jnp.cumsum(onehot, axis=0)
    rank = jnp.take_along_axis(csum, e_flat[:, None], axis=1)[:, 0] - 1
    counts = csum[-1]
    starts = jnp.cumsum(counts) - counts
    pcounts = (counts + MOE_BLK - 1) // MOE_BLK * MOE_BLK
    pends = jnp.cumsum(pcounts)
    pstarts = pends - pcounts
    dest = pstarts[e_flat] + rank
    n_blk = n_rows // MOE_BLK + N_EXPERTS
    blk_start = jnp.arange(n_blk, dtype=jnp.int32) * MOE_BLK
    blk_exp = jnp.minimum(jnp.sum((blk_start[:, None] >= pends[None, :]).astype(jnp.int32), axis=1), N_EXPERTS - 1)
    order = jnp.argsort(e_flat)
    pos = jnp.arange(n_blk * MOE_BLK, dtype=jnp.int32)
    e_pos = jnp.repeat(blk_exp, MOE_BLK)
    src = jnp.clip(pos - pstarts[e_pos] + starts[e_pos], 0, n_rows - 1)
    row_tok = order[src].astype(jnp.int32) // TOP_K
    meta = jnp.concatenate([blk_exp, (pends[-1:] // MOE_BLK).astype(jnp.int32)])
    return meta, row_tok, gates, dest.reshape(n_tok, TOP_K).T.reshape(-1)


def _combine_kernel(x_ref, yg_ref, gate_ref, mod_ref, o_ref):
    gate = gate_ref[...]
    acc = gate[:, 0:1] * yg_ref[0].astype(F32)
    for j in range(1, TOP_K):
        acc = acc + gate[:, j:j + 1] * yg_ref[j].astype(F32)
    o_ref[...] = x_ref[...] + mod_ref[0, 5:6, :] * acc


def _combine(x1, yg, gates, mods, tile_mod):
    n_tok = x1.shape[0]
    return pl.pallas_call(
        _combine_kernel,
        grid=(n_tok // TOK_TILE,),
        in_specs=[
            pl.BlockSpec((TOK_TILE, D_MODEL), lambda i: (i, 0)),
            pl.BlockSpec((TOP_K, TOK_TILE, D_MODEL), lambda i: (0, i, 0)),
            pl.BlockSpec((TOK_TILE, TOP_K), lambda i: (i, 0)),
            pl.BlockSpec((1, 6, D_MODEL), lambda i: (tile_mod(i), 0, 0)),
        ],
        out_specs=pl.BlockSpec((TOK_TILE, D_MODEL), lambda i: (i, 0)),
        out_shape=jax.ShapeDtypeStruct((n_tok, D_MODEL), F32),
        compiler_params=_cparams(("parallel",)),
        name="moe_combine",
    )(x1, yg, gates, mods)


def kernel(x_prompt, x_sample, c, cache_na_k, cache_na_v, cache_swa_k, cache_swa_v, state_rwkv, c_ctx, w_ada, b_ada, norm1_g, norm2_g, w_in, w_out, na_q_norm, na_k_norm, na_rpb, swa_q_norm, swa_k_norm, swa_sink, rk_conv, rk_w0, rk_w2, rk_a0, rk_a2, rk_g2, rk_k_k, rk_k_a, rk_r_k, rk_ln_g, rk_ln_b, moe_router_w, moe_router_b, moe_w1, moe_b1, moe_w2, moe_b2):
    bc, tc, _ = x_prompt.shape
    bl, tl, _ = x_sample.shape
    depth = w_in.shape[0]
    n_ctx = bc * tc
    n_lat = bl * tl
    assert tc == TOK_TILE and tl % TOK_TILE == 0 and n_ctx % tl == 0
    n_ctx_tiles = n_ctx // TOK_TILE
    tiles_per_seq = tl // TOK_TILE
    past = cache_na_k.shape[2]

    def tile_mod(i):
        return jnp.where(i < n_ctx_tiles, 0, 1 + (i - n_ctx_tiles) // tiles_per_seq)

    def tile_rope(i):
        return jnp.where(i < n_ctx_tiles, tiles_per_seq, (i - n_ctx_tiles) % tiles_per_seq)

    x = jnp.concatenate([x_prompt.reshape(n_ctx, D_MODEL), x_sample.reshape(n_lat, D_MODEL)], axis=0)

    n_mod = 1 + bl
    mod_rows = -(-n_mod // 8) * 8
    cvecs = jnp.concatenate([c_ctx[None, :], c, jnp.zeros((mod_rows - n_mod, D_MODEL), F32)], axis=0)
    mods_all = _ada_mod(cvecs, w_ada, b_ada).reshape(depth, mod_rows, 6, D_MODEL)
    cos_tab, sin_tab = _rope_tables(tl)
    tile2 = lambda g: jnp.concatenate([g, g])[None, :]
    pad_lanes = lambda z: jnp.pad(z, ((0, 0), (0, LANES - z.shape[1])))
    zeros_lora = jnp.zeros((2, RK_DECAY_LORA, RK_WIDTH), F32)

    na_k_l, na_v_l, sw_k_l, sw_v_l, st_l = [], [], [], [], []
    for l in range(depth):
        mods = mods_all[l]
        qk_gains = jnp.concatenate(
            [tile2(na_q_norm[l]), tile2(na_k_norm[l]), tile2(swa_q_norm[l]), tile2(swa_k_norm[l])], axis=0)
        p = {
            "rk_conv": rk_conv[l], "rk_w0": rk_w0[l], "rk_a0": rk_a0[l], "rk_g2": rk_g2[l],
            "rk_w2_pad": jnp.concatenate([rk_w2[l], zeros_lora], axis=1),
            "rk_a2_pad": jnp.concatenate([zeros_lora, rk_a2[l]], axis=1),
            "rk_k_k": rk_k_k[l][None, :], "rk_k_a": rk_k_a[l][None, :],
            "rk_r_k": rk_r_k[l].reshape(1, RK_WIDTH),
            "rk_ln_g": rk_ln_g[l][None, :], "rk_ln_b": rk_ln_b[l][None, :],
            "w_out_bf16": w_out[l].astype(BF16), "norm2_g": norm2_g[l][None, :],
            "router_w_pad": pad_lanes(moe_router_w[l]), "router_b_pad": pad_lanes(moe_router_b[l][None, :]),
        }

        att, u = _in_proj(x, norm1_g[l][None, :], mods, w_in[l].astype(BF16), qk_gains, cos_tab, sin_tab,
                          tile_mod, tile_rope)
        q, g, bonus = _rk_prep(u, p, n_ctx_tiles, tiles_per_seq)
        att_c = att[:n_ctx].reshape(bc, tc, ATT_COLS)
        att_by_ctx_len = att.reshape((n_ctx + n_lat) // tc, tc, ATT_COLS)
        att_by_lat_len = att.reshape((n_ctx + n_lat) // tl, tl, ATT_COLS)
        na_k_l.append(att_c[:, :, NA_WIDTH:2 * NA_WIDTH].reshape(bc, tc, NA_HEADS, HEAD_DIM))
        na_v_l.append(att_c[:, :, 2 * NA_WIDTH:NA_COLS].reshape(bc, tc, NA_HEADS, HEAD_DIM))
        sw_k_l.append(att_c[:, :, NA_COLS + SWA_WIDTH:NA_COLS + SWA_WIDTH + SWA_KV_WIDTH]
                      .reshape(bc, tc, SWA_KV_HEADS, HEAD_DIM))
        sw_v_l.append(att_c[:, :, NA_COLS + SWA_WIDTH + SWA_KV_WIDTH:].reshape(bc, tc, SWA_KV_HEADS, HEAD_DIM))

        sink = swa_sink[l]
        o_na = jnp.concatenate([
            _ctx_attn(att_by_ctx_len, bc, sink, gqa=False).reshape(n_ctx, NA_WIDTH),
            _na_latent(att_by_lat_len, n_ctx // tl, cache_na_k[:, l].reshape(bl, past, NA_WIDTH),
                       cache_na_v[:, l].reshape(bl, past, NA_WIDTH),
                       _na_bias_tables(na_rpb[l])).reshape(n_lat, NA_WIDTH)], axis=0)
        o_sw = jnp.concatenate([
            _ctx_attn(att_by_ctx_len, bc, sink, gqa=True).reshape(n_ctx, SWA_WIDTH),
            _swa_latent(att_by_lat_len, n_ctx // tl, cache_swa_k[:, l].reshape(bl, past, SWA_KV_WIDTH),
                        cache_swa_v[:, l].reshape(bl, past, SWA_KV_WIDTH), sink).reshape(n_lat, SWA_WIDTH)],
            axis=0)

        yf_c, yb_c, s_fin = _rwkv_group(q, 0, bc, tc, jnp.zeros((bc, 2, RK_NB, PAIR, PAIR), F32))
        yf_l, yb_l, _ = _rwkv_group(q, n_ctx // RK_CHUNK, bl, tl, _pair_states(state_rwkv[:, l]))
        st_l.append(_head_states(s_fin))
        y_f = jnp.concatenate([yf_c, yf_l], axis=0)
        y_b = jnp.concatenate([yb_c, yb_l], axis=0)

        x1, h2, logits = _out_proj(x, o_na, o_sw, y_f, y_b, bonus, g, p, mods, tile_mod)
        meta, row_tok, gates, dest = _route(logits)
        h2 = jnp.concatenate([h2, jnp.zeros((H2_PAD_ROWS - h2.shape[0], D_MODEL), BF16)], axis=0)
        yb = _moe_blocks(meta, h2[row_tok], moe_w1, moe_b1[:, :, None, :], moe_w2, moe_b2[:, :, None, :], l)
        x = _combine(x1, yb[dest].reshape(TOP_K, n_ctx + n_lat, D_MODEL), gates, mods, tile_mod)

    y_p = x[:n_ctx].reshape(bc, tc, D_MODEL)
    y_s = x[n_ctx:].reshape(bl, tl, D_MODEL)
    return (y_p, y_s, jnp.stack(na_k_l, axis=1), jnp.stack(na_v_l, axis=1), jnp.stack(sw_k_l, axis=1),
            jnp.stack(sw_v_l, axis=1), jnp.stack(st_l, axis=1))
```

```python
import functools

import jax
import jax.numpy as jnp
from jax import lax
from jax.experimental import pallas as pl
from jax.experimental.pallas import tpu as pltpu

F32 = jnp.float32
BF16 = jnp.bfloat16

D_MODEL = 1024
HEAD_DIM = 64
LANES = 128
GRID_W = 64
NA_HEADS = 6
SWA_HEADS = 4
SWA_KV_HEADS = 2
RK_HEADS = 6
NA_WIDTH = NA_HEADS * HEAD_DIM
SWA_WIDTH = SWA_HEADS * HEAD_DIM
SWA_KV_WIDTH = SWA_KV_HEADS * HEAD_DIM
RK_WIDTH = RK_HEADS * HEAD_DIM
RK_DECAY_LORA = 64
RK_A_LORA = 64
RK_GATE_LORA = 128
RK_COLS = 3 * RK_WIDTH + RK_DECAY_LORA + RK_A_LORA + RK_GATE_LORA
NA_COLS = 3 * NA_WIDTH
SWA_COLS = SWA_WIDTH + 2 * SWA_KV_WIDTH
ATT_COLS = NA_COLS + SWA_COLS
IN_COLS = ATT_COLS + RK_COLS
NA_WIN_R = 8
NA_WIN_C = 16
SWA_WIN = 128
ROPE_THETA = 10000.0
ATTN_SCALE = HEAD_DIM ** -0.5
N_EXPERTS = 32
TOP_K = 4
SWIGLU_LIMIT = 7.0
SWIGLU_ALPHA = 1.702
MOE_BLK = 256
RMS_EPS = 1e-6
GN_EPS = 64e-5
NEG_BIG = -1e30

TOK_TILE = 256
VMEM_LIMIT = 48 * 1024 * 1024


def _cparams(sem):
    return pltpu.CompilerParams(dimension_semantics=sem, vmem_limit_bytes=VMEM_LIMIT)


def _dot(a, b):
    return jnp.dot(a, b, preferred_element_type=F32)


def _dot_nt(a, b):
    return lax.dot_general(a, b, (((1,), (1,)), ((), ())), preferred_element_type=F32)


def _split_bf16(x):
    hi = x.astype(BF16)
    lo = (x - hi.astype(F32)).astype(BF16)
    return hi, lo


def _dot3(a, b):
    ah, al = _split_bf16(a)
    bh, bl = _split_bf16(b)
    return _dot(ah, bh) + (_dot(ah, bl) + _dot(al, bh))


def _bmm_raw(a, b):
    return lax.dot_general(a, b, (((2,), (1,)), ((0,), (0,))), preferred_element_type=F32)


def _bmm(a, b):
    return _bmm_raw(a.astype(BF16), b.astype(BF16))


def _bmm_nt(a, b):
    return lax.dot_general(a.astype(BF16), b.astype(BF16), (((2,), (2,)), ((0,), (0,))),
                           preferred_element_type=F32)


def _bmm3(a, b):
    ah, al = _split_bf16(a)
    bh, bl = _split_bf16(b)
    return _bmm_raw(ah, bh) + (_bmm_raw(ah, bl) + _bmm_raw(al, bh))


def _lane_lo(shape):
    return lax.broadcasted_iota(jnp.int32, shape, len(shape) - 1) < HEAD_DIM


def _pair_sum(x):
    lo = _lane_lo(x.shape)
    s_lo = jnp.sum(jnp.where(lo, x, 0.0), axis=-1, keepdims=True)
    s_hi = jnp.sum(jnp.where(lo, 0.0, x), axis=-1, keepdims=True)
    return jnp.where(lo, s_lo, s_hi)


def _stack_heads(q):
    lo = _lane_lo(q.shape)
    return jnp.concatenate([jnp.where(lo, q, 0.0), jnp.where(lo, 0.0, q)], axis=0)


def _stack_heads3(x):
    lo = _lane_lo(x.shape)
    return jnp.concatenate([jnp.where(lo, x, 0.0), jnp.where(lo, 0.0, x)], axis=1)


def _unstack_heads(o2):
    n = o2.shape[0] // 2
    return jnp.where(_lane_lo((n, LANES)), o2[:n], o2[n:])


def _dup_head(x, j):
    keep = _lane_lo(x.shape) == (j == 0)
    return jnp.where(keep, x, pltpu.roll(x, HEAD_DIM, 1))


def _ada_kernel(c_ref, w_ref, b_ref, o_ref):
    cv = c_ref[...]
    s = cv * jax.nn.sigmoid(cv)
    o_ref[0] = _dot3(s, w_ref[0]) + b_ref[0]


def _ada_mod(cvecs, w_ada, b_ada):
    depth, _, n_out = w_ada.shape
    rows = cvecs.shape[0]
    tn = 1024
    return pl.pallas_call(
        _ada_kernel,
        grid=(depth, n_out // tn),
        in_specs=[
            pl.BlockSpec((rows, D_MODEL), lambda l, j: (0, 0)),
            pl.BlockSpec((1, D_MODEL, tn), lambda l, j: (l, 0, j)),
            pl.BlockSpec((1, 1, tn), lambda l, j: (l, 0, j)),
        ],
        out_specs=pl.BlockSpec((1, rows, tn), lambda l, j: (l, 0, j)),
        out_shape=jax.ShapeDtypeStruct((depth, rows, n_out), F32),
        compiler_params=_cparams(("parallel", "parallel")),
        name="ada_mod",
    )(cvecs, w_ada, b_ada.reshape(depth, 1, n_out))


NA_QK_BLOCKS = 2 * NA_WIDTH // LANES
SWA_Q_BLOCK0 = NA_COLS // LANES
SWA_QK_BLOCKS = (SWA_WIDTH + SWA_KV_WIDTH) // LANES


def _in_proj_kernel(x_ref, g_ref, mod_ref, w_ref, qkg_ref, cos_ref, sin_ref, att_ref, u_ref):
    x = x_ref[...]
    y = x * lax.rsqrt(jnp.mean(x * x, axis=-1, keepdims=True) + RMS_EPS)
    h = (y * g_ref[...]) * (1.0 + mod_ref[0, 1:2, :]) + mod_ref[0, 0:1, :]
    proj = _dot(h.astype(BF16), w_ref[...])
    u_ref[...] = proj[:, ATT_COLS:]

    def qk_norm(blk, gain):
        ms = _pair_sum(blk * blk) * (1.0 / HEAD_DIM)
        return blk * lax.rsqrt(ms + RMS_EPS) * gain

    lane = lax.broadcasted_iota(jnp.int32, (x.shape[0], LANES), 1)
    first = (lane % (HEAD_DIM // 2)) < (HEAD_DIM // 4)
    for cb in range(ATT_COLS // LANES):
        blk = proj[:, cb * LANES:(cb + 1) * LANES]
        if cb < NA_QK_BLOCKS:
            gi = 0 if cb < NA_QK_BLOCKS // 2 else 1
            blk = qk_norm(blk, qkg_ref[gi:gi + 1, :])
        elif SWA_Q_BLOCK0 <= cb < SWA_Q_BLOCK0 + SWA_QK_BLOCKS:
            gi = 2 if cb < SWA_Q_BLOCK0 + SWA_WIDTH // LANES else 3
            blk = qk_norm(blk, qkg_ref[gi:gi + 1, :])
            partner = jnp.where(first, pltpu.roll(blk, LANES - HEAD_DIM // 4, 1),
                                pltpu.roll(blk, HEAD_DIM // 4, 1))
            blk = blk * cos_ref[...] + partner * sin_ref[...]
        att_ref[:, cb * LANES:(cb + 1) * LANES] = blk


def _in_proj(x, norm_g, mods, w_in_bf16, qk_gains, cos_tab, sin_tab, tile_mod, tile_rope):
    n_tok = x.shape[0]
    return pl.pallas_call(
        _in_proj_kernel,
        grid=(n_tok // TOK_TILE,),
        in_specs=[
            pl.BlockSpec((TOK_TILE, D_MODEL), lambda i: (i, 0)),
            pl.BlockSpec((1, D_MODEL), lambda i: (0, 0)),
            pl.BlockSpec((1, 6, D_MODEL), lambda i: (tile_mod(i), 0, 0)),
            pl.BlockSpec((D_MODEL, IN_COLS), lambda i: (0, 0)),
            pl.BlockSpec((4, LANES), lambda i: (0, 0)),
            pl.BlockSpec((TOK_TILE, LANES), lambda i: (tile_rope(i), 0)),
            pl.BlockSpec((TOK_TILE, LANES), lambda i: (tile_rope(i), 0)),
        ],
        out_specs=[
            pl.BlockSpec((TOK_TILE, ATT_COLS), lambda i: (i, 0)),
            pl.BlockSpec((TOK_TILE, RK_COLS), lambda i: (i, 0)),
        ],
        out_shape=[
            jax.ShapeDtypeStruct((n_tok, ATT_COLS), F32),
            jax.ShapeDtypeStruct((n_tok, RK_COLS), F32),
        ],
        compiler_params=_cparams(("parallel",)),
        name="in_proj",
    )(x, norm_g, mods, w_in_bf16, qk_gains, cos_tab, sin_tab)


def _rope_tables(n_lat):
    nf = HEAD_DIM // 4
    t = jnp.arange(n_lat)
    lane = jnp.arange(LANES)
    d = lane % HEAD_DIM
    inv = ROPE_THETA ** (-(d % nf).astype(F32) / nf)
    pos = jnp.where((d // (2 * nf))[None, :] == 0, (t // GRID_W)[:, None], (t % GRID_W)[:, None]).astype(F32)
    ang = pos * inv[None, :]
    sign = jnp.where((d % (2 * nf)) < nf, -1.0, 1.0).astype(F32)
    cos = jnp.concatenate([jnp.cos(ang), jnp.ones((TOK_TILE, LANES), F32)], 0)
    sin = jnp.concatenate([jnp.sin(ang) * sign[None, :], jnp.zeros((TOK_TILE, LANES), F32)], 0)
    return cos, sin


def _ctx_attn_kernel(sink_ref, q_ref, k_ref, v_ref, o_ref, *, gqa):
    j = pl.program_id(1)
    k = k_ref[0]
    v = v_ref[0]
    if gqa:
        k = _dup_head(k, j)
        v = _dup_head(v, j)
    n = k.shape[0]
    q2 = _stack_heads(q_ref[0]).astype(BF16)
    s = _dot_nt(q2, k.astype(BF16)) * ATTN_SCALE
    m = jnp.max(s, axis=-1, keepdims=True)
    if gqa:
        row = lax.broadcasted_iota(jnp.int32, (2 * n, 1), 0)
        snk = jnp.where(row < n, sink_ref[2 * j], sink_ref[2 * j + 1])
        m = jnp.maximum(m, snk)
    p = jnp.exp(s - m)
    den = jnp.sum(p, axis=-1, keepdims=True)
    if gqa:
        den = den + jnp.exp(snk - m)
    o2 = _dot(p.astype(BF16), v.astype(BF16)) / den
    o_ref[0] = _unstack_heads(o2)


def _ctx_attn(att, b, sink, *, gqa):
    t = att.shape[1]
    if gqa:
        nq = SWA_WIDTH // LANES
        qb, kb, vb = SWA_Q_BLOCK0, SWA_Q_BLOCK0 + nq, SWA_Q_BLOCK0 + nq + 1
        kmap = lambda bi, j: (bi, 0, kb)
        vmap = lambda bi, j: (bi, 0, vb)
    else:
        nq = NA_WIDTH // LANES
        qb, kb, vb = 0, nq, 2 * nq
        kmap = lambda bi, j: (bi, 0, kb + j)
        vmap = lambda bi, j: (bi, 0, vb + j)
    return pl.pallas_call(
        functools.partial(_ctx_attn_kernel, gqa=gqa),
        grid=(b, nq),
        in_specs=[
            pl.BlockSpec(memory_space=pltpu.SMEM),
            pl.BlockSpec((1, t, LANES), lambda bi, j: (bi, 0, qb + j)),
            pl.BlockSpec((1, t, LANES), kmap),
            pl.BlockSpec((1, t, LANES), vmap),
        ],
        out_specs=pl.BlockSpec((1, t, LANES), lambda bi, j: (bi, 0, j)),
        out_shape=jax.ShapeDtypeStruct((b, t, nq * LANES), F32),
        compiler_params=_cparams(("parallel", "parallel")),
        name="ctx_attn_swa" if gqa else "ctx_attn_na",
    )(sink, att, att, att)


NA_ROWS_PER_ITER = 4


def _na_lat_kernel(q_ref, k_ref, v_ref, kc_ref, vc_ref, tab_ref, o_ref, kb_ref, vb_ref):
    n = q_ref.shape[1]
    rows = n // GRID_W
    win = NA_WIN_R * GRID_W
    kb_ref[...] = k_ref[0].astype(BF16)
    vb_ref[...] = v_ref[0].astype(BF16)
    kc = kc_ref[0].astype(BF16)
    vc = vc_ref[0].astype(BF16)

    def row_group(ig, carry):
        nr = NA_ROWS_PER_ITER
        g0 = pl.multiple_of(ig * (nr * GRID_W), nr * GRID_W)
        q2 = _stack_heads3(q_ref[0, pl.ds(g0, nr * GRID_W), :].reshape(nr, GRID_W, LANES)).astype(BF16)
        kws, vws, biases = [], [], []
        for r in range(nr):
            i = ig * nr + r
            start = jnp.clip(i - NA_WIN_R // 2, 0, rows - NA_WIN_R)
            k0 = pl.multiple_of(start * GRID_W, GRID_W)
            kws.append(kb_ref[pl.ds(k0, win), :])
            vws.append(vb_ref[pl.ds(k0, win), :])
            biases.append(tab_ref[0, start - i + (NA_WIN_R - 1)])
        s_loc = _bmm_nt(q2, jnp.stack(kws)) * ATTN_SCALE + jnp.stack(biases)
        s_ctx = _dot_nt(q2.reshape(nr * 2 * GRID_W, LANES), kc).reshape(nr, 2 * GRID_W, -1) * ATTN_SCALE
        m = jnp.maximum(jnp.max(s_loc, axis=-1, keepdims=True), jnp.max(s_ctx, axis=-1, keepdims=True))
        p_loc = jnp.exp(s_loc - m)
        p_ctx = jnp.exp(s_ctx - m)
        den = jnp.sum(p_loc, axis=-1, keepdims=True) + jnp.sum(p_ctx, axis=-1, keepdims=True)
        o_ctx = _dot(p_ctx.reshape(nr * 2 * GRID_W, -1).astype(BF16), vc).reshape(nr, 2 * GRID_W, LANES)
        o2 = (_bmm(p_loc, jnp.stack(vws)) + o_ctx) / den
        out = jnp.where(_lane_lo((nr, GRID_W, LANES)), o2[:, :GRID_W], o2[:, GRID_W:])
        o_ref[0, pl.ds(g0, nr * GRID_W), :] = out.reshape(nr * GRID_W, LANES)
        return carry

    lax.fori_loop(0, rows // NA_ROWS_PER_ITER, row_group, 0)


def _na_bias_tables(rpb):
    col = jnp.arange(GRID_W)
    cstart = jnp.clip(col - NA_WIN_C // 2, 0, GRID_W - NA_WIN_C)
    col_mask = (col[None, :] >= cstart[:, None]) & (col[None, :] < cstart[:, None] + NA_WIN_C)
    col_idx = jnp.clip(col[None, :] - col[:, None] + NA_WIN_C - 1, 0, 2 * NA_WIN_C - 2)
    rpb_cols = jnp.where(col_mask[None, None], rpb[:, :, col_idx], NEG_BIG)
    roff = jnp.arange(NA_WIN_R)[:, None] + jnp.arange(NA_WIN_R)[None, :]
    t = rpb_cols[:, roff]
    t = jnp.transpose(t, (0, 1, 3, 2, 4)).reshape(NA_HEADS // 2, 2, NA_WIN_R, GRID_W, NA_WIN_R * GRID_W)
    return jnp.transpose(t, (0, 2, 1, 3, 4)).reshape(NA_HEADS // 2, NA_WIN_R, 2 * GRID_W, NA_WIN_R * GRID_W)


def _na_latent(att, s0, kc, vc, tab):
    n = att.shape[1]
    b, p, _ = kc.shape
    nq = NA_WIDTH // LANES
    return pl.pallas_call(
        _na_lat_kernel,
        grid=(b, nq),
        in_specs=[
            pl.BlockSpec((1, n, LANES), lambda bi, j: (s0 + bi, 0, j)),
            pl.BlockSpec((1, n, LANES), lambda bi, j: (s0 + bi, 0, nq + j)),
            pl.BlockSpec((1, n, LANES), lambda bi, j: (s0 + bi, 0, 2 * nq + j)),
            pl.BlockSpec((1, p, LANES), lambda bi, j: (bi, 0, j)),
            pl.BlockSpec((1, p, LANES), lambda bi, j: (bi, 0, j)),
            pl.BlockSpec((1, NA_WIN_R, 2 * GRID_W, NA_WIN_R * GRID_W), lambda bi, j: (j, 0, 0, 0)),
        ],
        out_specs=pl.BlockSpec((1, n, LANES), lambda bi, j: (bi, 0, j)),
        out_shape=jax.ShapeDtypeStruct((b, n, NA_WIDTH), F32),
        scratch_shapes=[pltpu.VMEM((n, LANES), BF16), pltpu.VMEM((n, LANES), BF16)],
        compiler_params=_cparams(("parallel", "parallel")),
        name="na_latent",
    )(att, att, att, kc, vc, tab)


SWA_BLOCKS_PER_ITER = 2


def _swa_lat_kernel(sink_ref, q_ref, k_ref, v_ref, kc_ref, vc_ref, o_ref, kb_ref, vb_ref):
    j = pl.program_id(1)
    n = q_ref.shape[1]
    blk = SWA_WIN
    span = 3 * blk
    kb_ref[...] = _dup_head(k_ref[0], j).astype(BF16)
    vb_ref[...] = _dup_head(v_ref[0], j).astype(BF16)
    kc = _dup_head(kc_ref[0], j).astype(BF16)
    vc = _dup_head(vc_ref[0], j).astype(BF16)
    row = lax.broadcasted_iota(jnp.int32, (2 * blk, 1), 0)
    snk = jnp.where(row < blk, sink_ref[2 * j], sink_ref[2 * j + 1])
    qoff = lax.broadcasted_iota(jnp.int32, (2 * blk, span), 0) % blk
    koff = lax.broadcasted_iota(jnp.int32, (2 * blk, span), 1)

    def q_group(qg, carry):
        nr = SWA_BLOCKS_PER_ITER
        g0 = pl.multiple_of(qg * (nr * blk), nr * blk)
        q2 = _stack_heads3(q_ref[0, pl.ds(g0, nr * blk), :].reshape(nr, blk, LANES)).astype(BF16)
        kws, vws, valids = [], [], []
        for r in range(nr):
            q0 = g0 + r * blk
            w0 = pl.multiple_of(jnp.clip(q0 - blk, 0, n - span), blk)
            kws.append(kb_ref[pl.ds(w0, span), :])
            vws.append(vb_ref[pl.ds(w0, span), :])
            valids.append(jnp.abs((q0 + qoff) - (w0 + koff)) <= SWA_WIN)
        s_loc = jnp.where(jnp.stack(valids), _bmm_nt(q2, jnp.stack(kws)) * ATTN_SCALE, NEG_BIG)
        s_ctx = _dot_nt(q2.reshape(nr * 2 * blk, LANES), kc).reshape(nr, 2 * blk, -1) * ATTN_SCALE
        m = jnp.maximum(jnp.max(s_loc, axis=-1, keepdims=True), jnp.max(s_ctx, axis=-1, keepdims=True))
        m = jnp.maximum(m, snk)
        p_loc = jnp.exp(s_loc - m)
        p_ctx = jnp.exp(s_ctx - m)
        den = (jnp.sum(p_loc, axis=-1, keepdims=True) + jnp.sum(p_ctx, axis=-1, keepdims=True)
               + jnp.exp(snk - m))
        o_ctx = _dot(p_ctx.reshape(nr * 2 * blk, -1).astype(BF16), vc).reshape(nr, 2 * blk, LANES)
        o2 = (_bmm(p_loc, jnp.stack(vws)) + o_ctx) / den
        out = jnp.where(_lane_lo((nr, blk, LANES)), o2[:, :blk], o2[:, blk:])
        o_ref[0, pl.ds(g0, nr * blk), :] = out.reshape(nr * blk, LANES)
        return carry

    lax.fori_loop(0, n // (SWA_BLOCKS_PER_ITER * blk), q_group, 0)


def _swa_latent(att, s0, kc, vc, sink):
    n = att.shape[1]
    b, p, _ = kc.shape
    nq = SWA_WIDTH // LANES
    qb, kb, vb = SWA_Q_BLOCK0, SWA_Q_BLOCK0 + nq, SWA_Q_BLOCK0 + nq + 1
    return pl.pallas_call(
        _swa_lat_kernel,
        grid=(b, nq),
        in_specs=[
            pl.BlockSpec(memory_space=pltpu.SMEM),
            pl.BlockSpec((1, n, LANES), lambda bi, j: (s0 + bi, 0, qb + j)),
            pl.BlockSpec((1, n, LANES), lambda bi, j: (s0 + bi, 0, kb)),
            pl.BlockSpec((1, n, LANES), lambda bi, j: (s0 + bi, 0, vb)),
            pl.BlockSpec((1, p, LANES), lambda bi, j: (bi, 0, 0)),
            pl.BlockSpec((1, p, LANES), lambda bi, j: (bi, 0, 0)),
        ],
        out_specs=pl.BlockSpec((1, n, LANES), lambda bi, j: (bi, 0, j)),
        out_shape=jax.ShapeDtypeStruct((b, n, SWA_WIDTH), F32),
        scratch_shapes=[pltpu.VMEM((n, LANES), BF16), pltpu.VMEM((n, LANES), BF16)],
        compiler_params=_cparams(("parallel", "parallel")),
        name="swa_latent",
    )(sink, att, att, att, kc, vc)


RK_NB = RK_WIDTH // LANES
LORA_BLOCK = 3 * RK_WIDTH // LANES
GATE_BLOCK = LORA_BLOCK + 1
Q_R, Q_V, Q_A, Q_W, Q_K, Q_B = range(6)
Q_DIR = 3
Q_COLS = (6 + Q_DIR) * RK_WIDTH


def _softplus(x):
    return jnp.maximum(x, 0.0) + jnp.log(1.0 + jnp.exp(-jnp.abs(x)))


def _rk_prep_kernel(u_ref, up_ref, un_ref, cw_ref, w0_ref, w2_ref, a0_ref, a2_ref, g2_ref, kk_ref, ka_ref,
                    rk_ref, q_ref, g_ref, bonus_ref, *, n_ctx_tiles, tiles_per_seq):
    def put(slot, val):
        q_ref[:, slot * RK_WIDTH:(slot + 1) * RK_WIDTH] = val

    i = pl.program_id(0)
    li = i - n_ctx_tiles
    is_lat = i >= n_ctx_tiles
    has_prev = jnp.logical_and(is_lat, li % tiles_per_seq != 0)
    has_next = jnp.logical_and(is_lat, li % tiles_per_seq != tiles_per_seq - 1)
    u = u_ref[...]
    tm = u.shape[0]
    prev_row = jnp.where(has_prev, up_ref[7:8, :], 0.0)
    next_row = jnp.where(has_next, un_ref[0:1, :], 0.0)
    row = lax.broadcasted_iota(jnp.int32, u.shape, 0)
    um = jnp.where(row == 0, prev_row, pltpu.roll(u, 1, 0))
    up = jnp.where(row == tm - 1, next_row, pltpu.roll(u, tm - 1, 0))
    u = um * cw_ref[0:1, :] + u * cw_ref[1:2, :] + up * cw_ref[2:3, :]

    r = u[:, 0:RK_WIDTH]
    k = u[:, RK_WIDTH:2 * RK_WIDTH]
    v = u[:, 2 * RK_WIDTH:3 * RK_WIDTH]
    lora = u[:, LORA_BLOCK * LANES:(LORA_BLOCK + 1) * LANES]
    gl = u[:, GATE_BLOCK * LANES:(GATE_BLOCK + 1) * LANES]
    put(Q_R, r)
    put(Q_V, v)
    g_ref[...] = _dot3(jax.nn.sigmoid(gl), g2_ref[...])

    kn = k * kk_ref[...]
    kk = jnp.concatenate(
        [kn[:, c * LANES:(c + 1) * LANES]
         * lax.rsqrt(jnp.maximum(_pair_sum(jnp.square(kn[:, c * LANES:(c + 1) * LANES])), 1e-24))
         for c in range(RK_NB)], axis=1)
    put(Q_A, -kk)

    lora_t = jnp.tanh(lora)
    kd_sum = None
    for d in range(2):
        w = -_softplus(-(w0_ref[d:d + 1, :] + _dot3(lora_t, w2_ref[d]))) - 0.5
        put(Q_W + Q_DIR * d, -jnp.exp(w))
        a = jax.nn.sigmoid(a0_ref[d:d + 1, :] + _dot3(lora, a2_ref[d]))
        kd = k * (1.0 + (a - 1.0) * ka_ref[...])
        put(Q_K + Q_DIR * d, kd)
        put(Q_B + Q_DIR * d, kk * a)
        kd_sum = kd if kd_sum is None else kd_sum + kd

    t = r * kd_sum * rk_ref[...]
    bonus_ref[...] = jnp.concatenate(
        [_pair_sum(t[:, c * LANES:(c + 1) * LANES]) for c in range(RK_NB)], axis=1) * v


def _rk_prep(u, p, n_ctx_tiles, tiles_per_seq):
    n_tok = u.shape[0]
    n_tiles = n_tok // TOK_TILE
    sub = TOK_TILE // 8
    last8 = n_tok // 8 - 1
    tok = lambda i: (i, 0)
    const2 = lambda i: (0, 0)
    const3 = lambda i: (0, 0, 0)
    one = jax.ShapeDtypeStruct((n_tok, RK_WIDTH), F32)
    tok_spec = pl.BlockSpec((TOK_TILE, RK_WIDTH), tok)
    return pl.pallas_call(
        functools.partial(_rk_prep_kernel, n_ctx_tiles=n_ctx_tiles, tiles_per_seq=tiles_per_seq),
        grid=(n_tiles,),
        in_specs=[
            pl.BlockSpec((TOK_TILE, RK_COLS), tok),
            pl.BlockSpec((8, RK_COLS), lambda i: (jnp.maximum(i * sub - 1, 0), 0)),
            pl.BlockSpec((8, RK_COLS), lambda i: (jnp.minimum((i + 1) * sub, last8), 0)),
            pl.BlockSpec((3, RK_COLS), const2),
            pl.BlockSpec((2, RK_WIDTH), const2),
            pl.BlockSpec((2, LANES, RK_WIDTH), const3),
            pl.BlockSpec((2, RK_WIDTH), const2),
            pl.BlockSpec((2, LANES, RK_WIDTH), const3),
            pl.BlockSpec((RK_GATE_LORA, RK_WIDTH), const2),
            pl.BlockSpec((1, RK_WIDTH), const2),
            pl.BlockSpec((1, RK_WIDTH), const2),
            pl.BlockSpec((1, RK_WIDTH), const2),
        ],
        out_specs=[pl.BlockSpec((TOK_TILE, Q_COLS), tok), tok_spec, tok_spec],
        out_shape=[jax.ShapeDtypeStruct((n_tok, Q_COLS), F32), one, one],
        compiler_params=_cparams(("parallel",)),
        name="rk_prep",
    )(u, u, u, p["rk_conv"], p["rk_w0"], p["rk_w2_pad"], p["rk_a0"], p["rk_a2_pad"], p["rk_g2"],
      p["rk_k_k"], p["rk_k_a"], p["rk_r_k"])


RK_CHUNK = 64
PAIR = 2 * HEAD_DIM
STATE_SEQS = 8


def _split3_bf16(x):
    hi = x.astype(BF16)
    r1 = x - hi.astype(F32)
    mid = r1.astype(BF16)
    return hi, mid, (r1 - mid.astype(F32)).astype(BF16)


def _rk_chunk_kernel(q_ref, rbar_ref, ybar_ref, phi_ref, psi_ref):
    c = RK_CHUNK
    n = 2 * c
    nu = 2 * RK_NB

    def tiles(slot, per_dir):
        cols = [(slot + (Q_DIR * d if per_dir else 0)) * RK_WIDTH + p * LANES
                for d in range(2) for p in range(RK_NB)]
        return jnp.stack([q_ref[:, lo:lo + LANES] for lo in cols])

    r, v, a = tiles(Q_R, False), tiles(Q_V, False), tiles(Q_A, False)
    lw, k, b = tiles(Q_W, True), tiles(Q_K, True), tiles(Q_B, True)
    sgn = jnp.where(lax.broadcasted_iota(jnp.int32, (nu, 1, 1), 0) >= RK_NB, -1, 1)
    bwd = sgn < 0
    tdiff = lax.broadcasted_iota(jnp.int32, (1, c, c), 2) - lax.broadcasted_iota(jnp.int32, (1, c, c), 1)
    tri = jnp.where(tdiff * sgn <= 0, 1.0, 0.0)
    cum = sum(_bmm(tri, part) for part in _split3_bf16(lw))
    tot = jnp.where(bwd, cum[:, 0:1], cum[:, c - 1:c])
    a_t = a * jnp.exp(cum - lw)
    r_t = r * jnp.exp(cum)
    e_neg = jnp.exp(-cum)
    e_end = jnp.exp(tot - cum)
    g = _bmm_nt(jnp.concatenate([_stack_heads3(a_t), _stack_heads3(r_t)], axis=1),
                jnp.concatenate([_stack_heads3(b * e_neg), _stack_heads3(k * e_neg)], axis=1))
    r2 = lax.broadcasted_iota(jnp.int32, (1, n, n), 1)
    c2 = lax.broadcasted_iota(jnp.int32, (1, n, n), 2)
    order = (jnp.bitwise_and(c2, c - 1) - jnp.bitwise_and(r2, c - 1)) * sgn
    eye = jnp.where(r2 == c2, 1.0, 0.0)
    l_ab = jnp.where(order < 0, g[:, :n, :n], 0.0)
    l_ak = jnp.where(order < 0, g[:, :n, n:], 0.0)
    m_rb = jnp.where(order <= 0, g[:, n:, :n], 0.0)
    m_rk = jnp.where(order <= 0, g[:, n:, n:], 0.0)
    t_inv = eye + l_ab
    pw = l_ab
    for _ in range(5):
        pw = _bmm(pw, pw)
        t_inv = t_inv + _bmm(t_inv, pw)
    sv = _stack_heads3(v)
    au = _bmm(t_inv, jnp.concatenate([_stack_heads3(a_t), _bmm(l_ak, sv)], axis=2))
    ry = _bmm(m_rb, au) + jnp.concatenate([_stack_heads3(r_t), _bmm(m_rk, sv)], axis=2)
    ry = ry[:, :c] + ry[:, c:]
    bt = jnp.swapaxes(_stack_heads3(b * e_end), 1, 2)
    kt = jnp.swapaxes(_stack_heads3(k * e_end), 1, 2)
    pp = _bmm(bt, au)
    phi = eye * jnp.exp(tot) + pp[:, :, :PAIR]
    psi = pp[:, :, PAIR:] + _bmm(kt, sv)
    for d in range(2):
        for p in range(RK_NB):
            u = d * RK_NB + p
            rbar_ref[d, :, p * LANES:(p + 1) * LANES] = ry[u, :, :PAIR]
            ybar_ref[d, :, p * LANES:(p + 1) * LANES] = ry[u, :, PAIR:]
            phi_ref[d, p] = phi[u]
            psi_ref[d, p] = psi[u]


def _rk_chunk(q, tile0, n_seq, t):
    nc = t // RK_CHUNK
    row_sh = jax.ShapeDtypeStruct((2, n_seq, t, RK_WIDTH), F32)
    mat_sh = jax.ShapeDtypeStruct((2, n_seq, nc, RK_NB, PAIR, PAIR), F32)
    row_spec = pl.BlockSpec((2, None, RK_CHUNK, RK_WIDTH), lambda s, c: (0, s, c, 0))
    mat_spec = pl.BlockSpec((2, None, None, RK_NB, PAIR, PAIR), lambda s, c: (0, s, c, 0, 0, 0))
    return pl.pallas_call(
        _rk_chunk_kernel,
        grid=(n_seq, nc),
        in_specs=[pl.BlockSpec((RK_CHUNK, Q_COLS), lambda s, c: (tile0 + s * nc + c, 0))],
        out_specs=[row_spec, row_spec, mat_spec, mat_spec],
        out_shape=[row_sh, row_sh, mat_sh, mat_sh],
        compiler_params=_cparams(("parallel", "parallel")),
        name="rk_chunk",
    )(q)


def _rk_state_kernel(rf_ref, rb_ref, yf_ref, yb_ref, phf_ref, phb_ref, psf_ref, psb_ref, s0_ref,
                     of_ref, ob_ref, s_ref):
    @pl.when(pl.program_id(1) == 0)
    def _():
        s_ref[...] = s0_ref[...]

    def pair_tiles(ref_f, ref_b, s):
        return jnp.stack([ref[s, :, p * LANES:(p + 1) * LANES] for ref in (ref_f, ref_b) for p in range(RK_NB)])

    def seq_body(s, carry):
        nu = 2 * RK_NB
        h = s_ref[s].reshape(nu, PAIR, PAIR)
        y = _bmm3(pair_tiles(rf_ref, rb_ref, s), h) + pair_tiles(yf_ref, yb_ref, s)
        phi = jnp.concatenate([phf_ref[s], phb_ref[s]], axis=0)
        psi = jnp.concatenate([psf_ref[s], psb_ref[s]], axis=0)
        s_ref[s] = (_bmm3(phi, h) + psi).reshape(2, RK_NB, PAIR, PAIR)
        of_ref[s] = jnp.concatenate([y[p] for p in range(RK_NB)], axis=1)
        ob_ref[s] = jnp.concatenate([y[RK_NB + p] for p in range(RK_NB)], axis=1)
        return carry

    lax.fori_loop(0, s_ref.shape[0], seq_body, 0)


def _rk_state(rbar, ybar, phi, psi, s0):
    _, n_seq, t, _ = rbar.shape
    nc = t // RK_CHUNK
    sg = STATE_SEQS
    row_blk = (None, sg, RK_CHUNK, RK_WIDTH)
    mat_blk = (None, sg, None, RK_NB, PAIR, PAIR)
    fwd_row = pl.BlockSpec(row_blk, lambda g, c: (0, g, c, 0))
    bwd_row = pl.BlockSpec(row_blk, lambda g, c: (1, g, nc - 1 - c, 0))
    fwd_mat = pl.BlockSpec(mat_blk, lambda g, c: (0, g, c, 0, 0, 0))
    bwd_mat = pl.BlockSpec(mat_blk, lambda g, c: (1, g, nc - 1 - c, 0, 0, 0))
    st = pl.BlockSpec((sg, 2, RK_NB, PAIR, PAIR), lambda g, c: (g, 0, 0, 0, 0))
    out_sh = jax.ShapeDtypeStruct((n_seq, t, RK_WIDTH), F32)
    return pl.pallas_call(
        _rk_state_kernel,
        grid=(n_seq // sg, nc),
        in_specs=[fwd_row, bwd_row, fwd_row, bwd_row, fwd_mat, bwd_mat, fwd_mat, bwd_mat, st],
        out_specs=[pl.BlockSpec((sg, RK_CHUNK, RK_WIDTH), lambda g, c: (g, c, 0)),
                   pl.BlockSpec((sg, RK_CHUNK, RK_WIDTH), lambda g, c: (g, nc - 1 - c, 0)), st],
        out_shape=[out_sh, out_sh, jax.ShapeDtypeStruct((n_seq, 2, RK_NB, PAIR, PAIR), F32)],
        compiler_params=_cparams(("parallel", "arbitrary")),
        name="rk_state",
    )(rbar, rbar, ybar, ybar, phi, phi, psi, psi, s0)


def _pair_states(s):
    bsz = s.shape[0]
    h = jnp.swapaxes(s, -1, -2).reshape(bsz, 2, RK_NB, 2, HEAD_DIM, HEAD_DIM)
    return jnp.einsum("bdphkv,hg->bdphkgv", h, jnp.eye(2, dtype=F32)).reshape(bsz, 2, RK_NB, PAIR, PAIR)


def _head_states(s):
    bsz = s.shape[0]
    h = s.reshape(bsz, 2, RK_NB, 2, HEAD_DIM, 2, HEAD_DIM)
    h = jnp.stack([h[:, :, :, 0, :, 0, :], h[:, :, :, 1, :, 1, :]], axis=3)
    return jnp.swapaxes(h.reshape(bsz, 2, RK_HEADS, HEAD_DIM, HEAD_DIM), -1, -2)


def _rwkv_group(q, tile0, n_seq, t, s0):
    rbar, ybar, phi, psi = _rk_chunk(q, tile0, n_seq, t)
    y_f, y_b, s_fin = _rk_state(rbar, ybar, phi, psi, s0)
    return y_f.reshape(n_seq * t, RK_WIDTH), y_b.reshape(n_seq * t, RK_WIDTH), s_fin


def _out_proj_kernel(x_ref, ona_c, ona_l, osw_c, osw_l, yf_c, yf_l, yb_c, yb_l, bonus_ref, g_ref, lng_ref, lnb_ref,
                     w_ref, mod_ref, n2_ref, rw_ref, rb_ref, x1_ref, h2_ref, lg_ref, *, n_ctx_tiles):
    is_ctx = pl.program_id(0) < n_ctx_tiles
    pick = lambda c_ref, l_ref: jnp.where(is_ctx, c_ref[...], l_ref[...])
    ona = pick(ona_c, ona_l)
    osw = pick(osw_c, osw_l)
    y = pick(yf_c, yf_l) + pick(yb_c, yb_l)
    parts = []
    for c in range(RK_NB):
        yc = y[:, c * LANES:(c + 1) * LANES]
        dc = yc - _pair_sum(yc) * (1.0 / HEAD_DIM)
        var = _pair_sum(dc * dc) * (1.0 / HEAD_DIM)
        parts.append(dc * lax.rsqrt(var + GN_EPS))
    yn = jnp.concatenate(parts, axis=1) * lng_ref[...] + lnb_ref[...]
    o_rk = (yn + bonus_ref[...]) * g_ref[...]
    o = (_dot(ona.astype(BF16), w_ref[0:NA_WIDTH, :])
         + _dot(osw.astype(BF16), w_ref[NA_WIDTH:NA_WIDTH + SWA_WIDTH, :])
         + _dot(o_rk.astype(BF16), w_ref[NA_WIDTH + SWA_WIDTH:, :]))
    x1 = x_ref[...] + mod_ref[0, 2:3, :] * o
    x1_ref[...] = x1
    yn2 = x1 * lax.rsqrt(jnp.mean(x1 * x1, axis=-1, keepdims=True) + RMS_EPS)
    h2 = (yn2 * n2_ref[...]) * (1.0 + mod_ref[0, 4:5, :]) + mod_ref[0, 3:4, :]
    h2_ref[...] = h2.astype(BF16)
    lg_ref[...] = (_dot3(h2, rw_ref[...]) + rb_ref[...])[:, :N_EXPERTS]


def _out_proj(x, o_na, o_sw, y_f, y_b, bonus, g, p, mods, tile_mod, n_ctx_tiles):
    n_tok = x.shape[0]
    tok = lambda i: (i, 0)
    const = lambda i: (0, 0)
    ctx_tile = lambda i: (jnp.minimum(i, n_ctx_tiles - 1), 0)
    lat_tile = lambda i: (jnp.maximum(i - n_ctx_tiles, 0), 0)
    pair = lambda w: [pl.BlockSpec((TOK_TILE, w), ctx_tile), pl.BlockSpec((TOK_TILE, w), lat_tile)]
    return pl.pallas_call(
        functools.partial(_out_proj_kernel, n_ctx_tiles=n_ctx_tiles),
        grid=(n_tok // TOK_TILE,),
        in_specs=[
            pl.BlockSpec((TOK_TILE, D_MODEL), tok),
            *pair(NA_WIDTH), *pair(SWA_WIDTH), *pair(RK_WIDTH), *pair(RK_WIDTH),
            pl.BlockSpec((TOK_TILE, RK_WIDTH), tok),
            pl.BlockSpec((TOK_TILE, RK_WIDTH), tok),
            pl.BlockSpec((1, RK_WIDTH), const),
            pl.BlockSpec((1, RK_WIDTH), const),
            pl.BlockSpec((D_MODEL, D_MODEL), const),
            pl.BlockSpec((1, 6, D_MODEL), lambda i: (tile_mod(i), 0, 0)),
            pl.BlockSpec((1, D_MODEL), const),
            pl.BlockSpec((D_MODEL, LANES), const),
            pl.BlockSpec((1, LANES), const),
        ],
        out_specs=[
            pl.BlockSpec((TOK_TILE, D_MODEL), tok),
            pl.BlockSpec((TOK_TILE, D_MODEL), tok),
            pl.BlockSpec((TOK_TILE, N_EXPERTS), tok),
        ],
        out_shape=[
            jax.ShapeDtypeStruct((n_tok, D_MODEL), F32),
            jax.ShapeDtypeStruct((n_tok, D_MODEL), BF16),
            jax.ShapeDtypeStruct((n_tok, N_EXPERTS), F32),
        ],
        compiler_params=_cparams(("parallel",)),
        name="out_proj",
    )(x, *o_na, *o_sw, *y_f, *y_b, bonus, g, p["rk_ln_g"], p["rk_ln_b"], p["w_out_bf16"], mods, p["norm2_g"],
      p["router_w_pad"], p["router_b_pad"])


H2_PAD_ROWS = 32768
W1_SEL_COLS = 256
MOE_VMEM_LIMIT = 56 * 1024 * 1024


def _moe_kernel(meta_ref, x_ref, w1_ref, b1g_ref, b1l_ref, w2_ref, b2_ref, o_ref, w1g_ref, w1l_ref, w2b_ref):
    i = pl.program_id(0)
    n_blk = meta_ref.shape[0] - 1
    n_used = meta_ref[n_blk]
    d_e = w2_ref.shape[1]
    new_expert = jnp.logical_or(i == 0, meta_ref[i] != meta_ref[jnp.maximum(i - 1, 0)])

    @pl.when(jnp.logical_and(i < n_used, new_expert))
    def _():
        src = lax.broadcasted_iota(jnp.int32, (2 * W1_SEL_COLS, 2 * W1_SEL_COLS), 0)
        dst = lax.broadcasted_iota(jnp.int32, (2 * W1_SEL_COLS, 2 * W1_SEL_COLS), 1)
        pick = jnp.where(dst < W1_SEL_COLS, 2 * dst, 2 * (dst - W1_SEL_COLS) + 1)
        sel = jnp.where(src == pick, 1.0, 0.0).astype(BF16)
        for t in range(d_e // W1_SEL_COLS):
            cols = _dot(w1_ref[0, :, 2 * t * W1_SEL_COLS:2 * (t + 1) * W1_SEL_COLS].astype(BF16), sel)
            w1g_ref[:, t * W1_SEL_COLS:(t + 1) * W1_SEL_COLS] = cols[:, :W1_SEL_COLS].astype(BF16)
            w1l_ref[:, t * W1_SEL_COLS:(t + 1) * W1_SEL_COLS] = cols[:, W1_SEL_COLS:].astype(BF16)
        w2b_ref[...] = w2_ref[0].astype(BF16)

    @pl.when(i < n_used)
    def _():
        x = x_ref[...]
        glu = jnp.minimum(_dot(x, w1g_ref[...]) + b1g_ref[0], SWIGLU_LIMIT)
        lin = jnp.clip(_dot(x, w1l_ref[...]) + b1l_ref[0], -SWIGLU_LIMIT, SWIGLU_LIMIT)
        act = glu * jax.nn.sigmoid(SWIGLU_ALPHA * glu) * (lin + 1.0)
        o_ref[...] = (_dot(act.astype(BF16), w2b_ref[...]) + b2_ref[0]).astype(BF16)

    @pl.when(i >= n_used)
    def _():
        o_ref[...] = jnp.zeros_like(o_ref)


def _moe_blocks(meta, xb, w1, b1g, b1l, w2, b2, layer):
    n_rows = xb.shape[0]
    n_blk = n_rows // MOE_BLK
    d_e = w2.shape[2]
    row = lambda i, m: (i, 0)
    exp3 = lambda i, m: (layer, m[i], 0, 0)
    grid_spec = pltpu.PrefetchScalarGridSpec(
        num_scalar_prefetch=1,
        grid=(n_blk,),
        in_specs=[
            pl.BlockSpec((MOE_BLK, D_MODEL), row),
            pl.BlockSpec((None, 1, D_MODEL, 2 * d_e), exp3),
            pl.BlockSpec((None, 1, 1, d_e), exp3),
            pl.BlockSpec((None, 1, 1, d_e), exp3),
            pl.BlockSpec((None, 1, d_e, D_MODEL), exp3),
            pl.BlockSpec((None, 1, 1, D_MODEL), exp3),
        ],
        out_specs=pl.BlockSpec((MOE_BLK, D_MODEL), row),
        scratch_shapes=[
            pltpu.VMEM((D_MODEL, d_e), BF16),
            pltpu.VMEM((D_MODEL, d_e), BF16),
            pltpu.VMEM((d_e, D_MODEL), BF16),
        ],
    )
    return pl.pallas_call(
        _moe_kernel,
        grid_spec=grid_spec,
        out_shape=jax.ShapeDtypeStruct((n_rows, D_MODEL), BF16),
        compiler_params=pltpu.CompilerParams(dimension_semantics=("arbitrary",),
                                             vmem_limit_bytes=MOE_VMEM_LIMIT),
        name="moe_blocks",
    )(meta, xb, w1, b1g, b1l, w2, b2)


def _route(logits):
    n_tok = logits.shape[0]
    top_v, top_i = lax.top_k(logits, TOP_K)
    gates = jax.nn.softmax(top_v, axis=-1)
    e_flat = top_i.reshape(-1).astype(jnp.int32)
    n_rows = n_tok * TOP_K
    onehot = (e_flat[:, None] == jnp.arange(N_EXPERTS, dtype=jnp.int32)[None, :]).astype(jnp.int32)
    csum = jnp.cumsum(onehot, axis=0)
    rank = jnp.take_along_axis(csum, e_flat[:, None], axis=1)[:, 0] - 1
    counts = csum[-1]
    starts = jnp.cumsum(counts) - counts
    pcounts = (counts + MOE_BLK - 1) // MOE_BLK * MOE_BLK
    pends = jnp.cumsum(pcounts)
    pstarts = pends - pcounts
    dest = pstarts[e_flat] + rank
    n_blk = n_rows // MOE_BLK + N_EXPERTS
    blk_start = jnp.arange(n_blk, dtype=jnp.int32) * MOE_BLK
    blk_exp = jnp.minimum(jnp.sum((blk_start[:, None] >= pends[None, :]).astype(jnp.int32), axis=1), N_EXPERTS - 1)
    order = jnp.argsort(e_flat)
    pos = jnp.arange(n_blk * MOE_BLK, dtype=jnp.int32)
    e_pos = jnp.repeat(blk_exp, MOE_BLK)
    src = jnp.clip(pos - pstarts[e_pos] + starts[e_pos], 0, n_rows - 1)
    row_tok = order[src].astype(jnp.int32) // TOP_K
    meta = jnp.concatenate([blk_exp, (pends[-1:] // MOE_BLK).astype(jnp.int32)])
    return meta, row_tok, gates, dest.reshape(n_tok, TOP_K).T.reshape(-1)


def _combine_kernel(x_ref, yg_ref, gate_ref, mod_ref, o_ref):
    gate = gate_ref[...]
    acc = gate[:, 0:1] * yg_ref[0].astype(F32)
    for j in range(1, TOP_K):
        acc = acc + gate[:, j:j + 1] * yg_ref[j].astype(F32)
    o_ref[...] = x_ref[...] + mod_ref[0, 5:6, :] * acc


def _combine(x1, yg, gates, mods, tile_mod):
    n_tok = x1.shape[0]
    return pl.pallas_call(
        _combine_kernel,
        grid=(n_tok // TOK_TILE,),
        in_specs=[
            pl.BlockSpec((TOK_TILE, D_MODEL), lambda i: (i, 0)),
            pl.BlockSpec((TOP_K, TOK_TILE, D_MODEL), lambda i: (0, i, 0)),
            pl.BlockSpec((TOK_TILE, TOP_K), lambda i: (i, 0)),
            pl.BlockSpec((1, 6, D_MODEL), lambda i: (tile_mod(i), 0, 0)),
        ],
        out_specs=pl.BlockSpec((TOK_TILE, D_MODEL), lambda i: (i, 0)),
        out_shape=jax.ShapeDtypeStruct((n_tok, D_MODEL), F32),
        compiler_params=_cparams(("parallel",)),
        name="moe_combine",
    )(x1, yg, gates, mods)


def kernel(x_prompt, x_sample, c, cache_na_k, cache_na_v, cache_swa_k, cache_swa_v, state_rwkv, c_ctx, w_ada, b_ada, norm1_g, norm2_g, w_in, w_out, na_q_norm, na_k_norm, na_rpb, swa_q_norm, swa_k_norm, swa_sink, rk_conv, rk_w0, rk_w2, rk_a0, rk_a2, rk_g2, rk_k_k, rk_k_a, rk_r_k, rk_ln_g, rk_ln_b, moe_router_w, moe_router_b, moe_w1, moe_b1, moe_w2, moe_b2):
    bc, tc, _ = x_prompt.shape
    bl, tl, _ = x_sample.shape
    depth = w_in.shape[0]
    n_ctx = bc * tc
    n_lat = bl * tl
    assert tc == TOK_TILE and tl % TOK_TILE == 0 and n_ctx % tl == 0
    n_ctx_tiles = n_ctx // TOK_TILE
    tiles_per_seq = tl // TOK_TILE
    past = cache_na_k.shape[2]

    def tile_mod(i):
        return jnp.where(i < n_ctx_tiles, 0, 1 + (i - n_ctx_tiles) // tiles_per_seq)

    def tile_rope(i):
        return jnp.where(i < n_ctx_tiles, tiles_per_seq, (i - n_ctx_tiles) % tiles_per_seq)

    x = jnp.concatenate([x_prompt.reshape(n_ctx, D_MODEL), x_sample.reshape(n_lat, D_MODEL)], axis=0)

    n_mod = 1 + bl
    mod_rows = -(-n_mod // 8) * 8
    cvecs = jnp.concatenate([c_ctx[None, :], c, jnp.zeros((mod_rows - n_mod, D_MODEL), F32)], axis=0)
    mods_all = _ada_mod(cvecs, w_ada, b_ada).reshape(depth, mod_rows, 6, D_MODEL)
    cos_tab, sin_tab = _rope_tables(tl)
    tile2 = lambda g: jnp.concatenate([g, g])[None, :]
    pad_lanes = lambda z: jnp.pad(z, ((0, 0), (0, LANES - z.shape[1])))
    zeros_lora = jnp.zeros((2, RK_DECAY_LORA, RK_WIDTH), F32)

    na_k_l, na_v_l, sw_k_l, sw_v_l, st_l = [], [], [], [], []
    for l in range(depth):
        mods = mods_all[l]
        qk_gains = jnp.concatenate(
            [tile2(na_q_norm[l]), tile2(na_k_norm[l]), tile2(swa_q_norm[l]), tile2(swa_k_norm[l])], axis=0)
        p = {
            "rk_conv": rk_conv[l], "rk_w0": rk_w0[l], "rk_a0": rk_a0[l], "rk_g2": rk_g2[l],
            "rk_w2_pad": jnp.concatenate([rk_w2[l], zeros_lora], axis=1),
            "rk_a2_pad": jnp.concatenate([zeros_lora, rk_a2[l]], axis=1),
            "rk_k_k": rk_k_k[l][None, :], "rk_k_a": rk_k_a[l][None, :],
            "rk_r_k": rk_r_k[l].reshape(1, RK_WIDTH),
            "rk_ln_g": rk_ln_g[l][None, :], "rk_ln_b": rk_ln_b[l][None, :],
            "w_out_bf16": w_out[l].astype(BF16), "norm2_g": norm2_g[l][None, :],
            "router_w_pad": pad_lanes(moe_router_w[l]), "router_b_pad": pad_lanes(moe_router_b[l][None, :]),
        }

        att, u = _in_proj(x, norm1_g[l][None, :], mods, w_in[l].astype(BF16), qk_gains, cos_tab, sin_tab,
                          tile_mod, tile_rope)
        q, g, bonus = _rk_prep(u, p, n_ctx_tiles, tiles_per_seq)
        att_c = att[:n_ctx].reshape(bc, tc, ATT_COLS)
        att_by_ctx_len = att.reshape((n_ctx + n_lat) // tc, tc, ATT_COLS)
        att_by_lat_len = att.reshape((n_ctx + n_lat) // tl, tl, ATT_COLS)
        na_k_l.append(att_c[:, :, NA_WIDTH:2 * NA_WIDTH].reshape(bc, tc, NA_HEADS, HEAD_DIM))
        na_v_l.append(att_c[:, :, 2 * NA_WIDTH:NA_COLS].reshape(bc, tc, NA_HEADS, HEAD_DIM))
        sw_k_l.append(att_c[:, :, NA_COLS + SWA_WIDTH:NA_COLS + SWA_WIDTH + SWA_KV_WIDTH]
                      .reshape(bc, tc, SWA_KV_HEADS, HEAD_DIM))
        sw_v_l.append(att_c[:, :, NA_COLS + SWA_WIDTH + SWA_KV_WIDTH:].reshape(bc, tc, SWA_KV_HEADS, HEAD_DIM))

        sink = swa_sink[l]
        o_na = (_ctx_attn(att_by_ctx_len, bc, sink, gqa=False).reshape(n_ctx, NA_WIDTH),
                _na_latent(att_by_lat_len, n_ctx // tl, cache_na_k[:, l].reshape(bl, past, NA_WIDTH),
                           cache_na_v[:, l].reshape(bl, past, NA_WIDTH),
                           _na_bias_tables(na_rpb[l])).reshape(n_lat, NA_WIDTH))
        o_sw = (_ctx_attn(att_by_ctx_len, bc, sink, gqa=True).reshape(n_ctx, SWA_WIDTH),
                _swa_latent(att_by_lat_len, n_ctx // tl, cache_swa_k[:, l].reshape(bl, past, SWA_KV_WIDTH),
                            cache_swa_v[:, l].reshape(bl, past, SWA_KV_WIDTH), sink).reshape(n_lat, SWA_WIDTH))

        yf_c, yb_c, s_fin = _rwkv_group(q, 0, bc, tc, jnp.zeros((bc, 2, RK_NB, PAIR, PAIR), F32))
        yf_l, yb_l, _ = _rwkv_group(q, n_ctx // RK_CHUNK, bl, tl, _pair_states(state_rwkv[:, l]))
        st_l.append(_head_states(s_fin))

        x1, h2, logits = _out_proj(x, o_na, o_sw, (yf_c, yf_l), (yb_c, yb_l), bonus, g, p, mods, tile_mod,
                                   n_ctx_tiles)
        meta, row_tok, gates, dest = _route(logits)
        h2 = jnp.concatenate([h2, jnp.zeros((H2_PAD_ROWS - h2.shape[0], D_MODEL), BF16)], axis=0)
        yb = _moe_blocks(meta, h2[row_tok], moe_w1, moe_b1[:, :, None, 0::2], moe_b1[:, :, None, 1::2], moe_w2,
                         moe_b2[:, :, None, :], l)
        x = _combine(x1, yb[dest].reshape(TOP_K, n_ctx + n_lat, D_MODEL), gates, mods, tile_mod)

    y_p = x[:n_ctx].reshape(bc, tc, D_MODEL)
    y_s = x[n_ctx:].reshape(bl, tl, D_MODEL)
    return (y_p, y_s, jnp.stack(na_k_l, axis=1), jnp.stack(na_v_l, axis=1), jnp.stack(sw_k_l, axis=1),
            jnp.stack(sw_v_l, axis=1), jnp.stack(st_l, axis=1))
```

```python
import functools

import jax
import jax.numpy as jnp
from jax import lax
from jax.experimental import pallas as pl
from jax.experimental.pallas import tpu as pltpu

F32 = jnp.float32
BF16 = jnp.bfloat16

D_MODEL = 1024
HEAD_DIM = 64
LANES = 128
GRID_W = 64
NA_HEADS = 6
SWA_HEADS = 4
SWA_KV_HEADS = 2
RK_HEADS = 6
NA_WIDTH = NA_HEADS * HEAD_DIM
SWA_WIDTH = SWA_HEADS * HEAD_DIM
SWA_KV_WIDTH = SWA_KV_HEADS * HEAD_DIM
RK_WIDTH = RK_HEADS * HEAD_DIM
RK_DECAY_LORA = 64
RK_A_LORA = 64
RK_GATE_LORA = 128
RK_COLS = 3 * RK_WIDTH + RK_DECAY_LORA + RK_A_LORA + RK_GATE_LORA
NA_COLS = 3 * NA_WIDTH
SWA_COLS = SWA_WIDTH + 2 * SWA_KV_WIDTH
ATT_COLS = NA_COLS + SWA_COLS
IN_COLS = ATT_COLS + RK_COLS
NA_WIN_R = 8
NA_WIN_C = 16
SWA_WIN = 128
ROPE_THETA = 10000.0
ATTN_SCALE = HEAD_DIM ** -0.5
N_EXPERTS = 32
TOP_K = 4
SWIGLU_LIMIT = 7.0
SWIGLU_ALPHA = 1.702
MOE_BLK = 256
RMS_EPS = 1e-6
GN_EPS = 64e-5
NEG_BIG = -1e30

TOK_TILE = 256
VMEM_LIMIT = 48 * 1024 * 1024


def _cparams(sem):
    return pltpu.CompilerParams(dimension_semantics=sem, vmem_limit_bytes=VMEM_LIMIT)


def _dot(a, b):
    return jnp.dot(a, b, preferred_element_type=F32)


def _dot_nt(a, b):
    return lax.dot_general(a, b, (((1,), (1,)), ((), ())), preferred_element_type=F32)


def _split_bf16(x):
    hi = x.astype(BF16)
    lo = (x - hi.astype(F32)).astype(BF16)
    return hi, lo


def _dot3(a, b):
    ah, al = _split_bf16(a)
    bh, bl = _split_bf16(b)
    return _dot(ah, bh) + (_dot(ah, bl) + _dot(al, bh))


def _bmm_raw(a, b):
    return lax.dot_general(a, b, (((2,), (1,)), ((0,), (0,))), preferred_element_type=F32)


def _bmm(a, b):
    return _bmm_raw(a.astype(BF16), b.astype(BF16))


def _bmm_nt(a, b):
    return lax.dot_general(a.astype(BF16), b.astype(BF16), (((2,), (2,)), ((0,), (0,))),
                           preferred_element_type=F32)


def _bmm3(a, b):
    ah, al = _split_bf16(a)
    bh, bl = _split_bf16(b)
    return _bmm_raw(ah, bh) + (_bmm_raw(ah, bl) + _bmm_raw(al, bh))


def _lane_lo(shape):
    return lax.broadcasted_iota(jnp.int32, shape, len(shape) - 1) < HEAD_DIM


def _pair_sum(x):
    lo = _lane_lo(x.shape)
    s_lo = jnp.sum(jnp.where(lo, x, 0.0), axis=-1, keepdims=True)
    s_hi = jnp.sum(jnp.where(lo, 0.0, x), axis=-1, keepdims=True)
    return jnp.where(lo, s_lo, s_hi)


def _stack_heads(q):
    lo = _lane_lo(q.shape)
    return jnp.concatenate([jnp.where(lo, q, 0.0), jnp.where(lo, 0.0, q)], axis=0)


def _stack_heads3(x):
    lo = _lane_lo(x.shape)
    return jnp.concatenate([jnp.where(lo, x, 0.0), jnp.where(lo, 0.0, x)], axis=1)


def _unstack_heads(o2):
    n = o2.shape[0] // 2
    return jnp.where(_lane_lo((n, LANES)), o2[:n], o2[n:])


def _dup_head(x, j):
    keep = _lane_lo(x.shape) == (j == 0)
    return jnp.where(keep, x, pltpu.roll(x, HEAD_DIM, 1))


def _ada_kernel(c_ref, w_ref, b_ref, o_ref):
    cv = c_ref[...]
    s = cv * jax.nn.sigmoid(cv)
    o_ref[0] = _dot3(s, w_ref[0]) + b_ref[0]


def _ada_mod(cvecs, w_ada, b_ada):
    depth, _, n_out = w_ada.shape
    rows = cvecs.shape[0]
    tn = 1024
    return pl.pallas_call(
        _ada_kernel,
        grid=(depth, n_out // tn),
        in_specs=[
            pl.BlockSpec((rows, D_MODEL), lambda l, j: (0, 0)),
            pl.BlockSpec((1, D_MODEL, tn), lambda l, j: (l, 0, j)),
            pl.BlockSpec((1, 1, tn), lambda l, j: (l, 0, j)),
        ],
        out_specs=pl.BlockSpec((1, rows, tn), lambda l, j: (l, 0, j)),
        out_shape=jax.ShapeDtypeStruct((depth, rows, n_out), F32),
        compiler_params=_cparams(("parallel", "parallel")),
        name="ada_mod",
    )(cvecs, w_ada, b_ada.reshape(depth, 1, n_out))


NA_QK_BLOCKS = 2 * NA_WIDTH // LANES
SWA_Q_BLOCK0 = NA_COLS // LANES
SWA_QK_BLOCKS = (SWA_WIDTH + SWA_KV_WIDTH) // LANES


def _in_proj_kernel(x_ref, g_ref, mod_ref, w_ref, qkg_ref, cos_ref, sin_ref, att_ref, u_ref):
    x = x_ref[...]
    y = x * lax.rsqrt(jnp.mean(x * x, axis=-1, keepdims=True) + RMS_EPS)
    h = (y * g_ref[...]) * (1.0 + mod_ref[0, 1:2, :]) + mod_ref[0, 0:1, :]
    proj = _dot(h.astype(BF16), w_ref[...])
    u_ref[...] = proj[:, ATT_COLS:]

    def qk_norm(blk, gain):
        ms = _pair_sum(blk * blk) * (1.0 / HEAD_DIM)
        return blk * lax.rsqrt(ms + RMS_EPS) * gain

    lane = lax.broadcasted_iota(jnp.int32, (x.shape[0], LANES), 1)
    first = (lane % (HEAD_DIM // 2)) < (HEAD_DIM // 4)
    for cb in range(ATT_COLS // LANES):
        blk = proj[:, cb * LANES:(cb + 1) * LANES]
        if cb < NA_QK_BLOCKS:
            gi = 0 if cb < NA_QK_BLOCKS // 2 else 1
            blk = qk_norm(blk, qkg_ref[gi:gi + 1, :])
        elif SWA_Q_BLOCK0 <= cb < SWA_Q_BLOCK0 + SWA_QK_BLOCKS:
            gi = 2 if cb < SWA_Q_BLOCK0 + SWA_WIDTH // LANES else 3
            blk = qk_norm(blk, qkg_ref[gi:gi + 1, :])
            partner = jnp.where(first, pltpu.roll(blk, LANES - HEAD_DIM // 4, 1),
                                pltpu.roll(blk, HEAD_DIM // 4, 1))
            blk = blk * cos_ref[...] + partner * sin_ref[...]
        att_ref[:, cb * LANES:(cb + 1) * LANES] = blk


def _in_proj(x, norm_g, mods, w_in_bf16, qk_gains, cos_tab, sin_tab, tile_mod, tile_rope):
    n_tok = x.shape[0]
    return pl.pallas_call(
        _in_proj_kernel,
        grid=(n_tok // TOK_TILE,),
        in_specs=[
            pl.BlockSpec((TOK_TILE, D_MODEL), lambda i: (i, 0)),
            pl.BlockSpec((1, D_MODEL), lambda i: (0, 0)),
            pl.BlockSpec((1, 6, D_MODEL), lambda i: (tile_mod(i), 0, 0)),
            pl.BlockSpec((D_MODEL, IN_COLS), lambda i: (0, 0)),
            pl.BlockSpec((4, LANES), lambda i: (0, 0)),
            pl.BlockSpec((TOK_TILE, LANES), lambda i: (tile_rope(i), 0)),
            pl.BlockSpec((TOK_TILE, LANES), lambda i: (tile_rope(i), 0)),
        ],
        out_specs=[
            pl.BlockSpec((TOK_TILE, ATT_COLS), lambda i: (i, 0)),
            pl.BlockSpec((TOK_TILE, RK_COLS), lambda i: (i, 0)),
        ],
        out_shape=[
            jax.ShapeDtypeStruct((n_tok, ATT_COLS), F32),
            jax.ShapeDtypeStruct((n_tok, RK_COLS), F32),
        ],
        compiler_params=_cparams(("parallel",)),
        name="in_proj",
    )(x, norm_g, mods, w_in_bf16, qk_gains, cos_tab, sin_tab)


def _rope_tables(n_lat):
    nf = HEAD_DIM // 4
    t = jnp.arange(n_lat)
    lane = jnp.arange(LANES)
    d = lane % HEAD_DIM
    inv = ROPE_THETA ** (-(d % nf).astype(F32) / nf)
    pos = jnp.where((d // (2 * nf))[None, :] == 0, (t // GRID_W)[:, None], (t % GRID_W)[:, None]).astype(F32)
    ang = pos * inv[None, :]
    sign = jnp.where((d % (2 * nf)) < nf, -1.0, 1.0).astype(F32)
    cos = jnp.concatenate([jnp.cos(ang), jnp.ones((TOK_TILE, LANES), F32)], 0)
    sin = jnp.concatenate([jnp.sin(ang) * sign[None, :], jnp.zeros((TOK_TILE, LANES), F32)], 0)
    return cos, sin


def _ctx_attn_kernel(sink_ref, q_ref, k_ref, v_ref, o_ref, *, gqa):
    j = pl.program_id(1)
    k = k_ref[0]
    v = v_ref[0]
    if gqa:
        k = _dup_head(k, j)
        v = _dup_head(v, j)
    n = k.shape[0]
    q2 = _stack_heads(q_ref[0]).astype(BF16)
    s = _dot_nt(q2, k.astype(BF16)) * ATTN_SCALE
    m = jnp.max(s, axis=-1, keepdims=True)
    if gqa:
        row = lax.broadcasted_iota(jnp.int32, (2 * n, 1), 0)
        snk = jnp.where(row < n, sink_ref[2 * j], sink_ref[2 * j + 1])
        m = jnp.maximum(m, snk)
    p = jnp.exp(s - m)
    den = jnp.sum(p, axis=-1, keepdims=True)
    if gqa:
        den = den + jnp.exp(snk - m)
    o2 = _dot(p.astype(BF16), v.astype(BF16)) / den
    o_ref[0] = _unstack_heads(o2)


def _ctx_attn(att, b, sink, *, gqa):
    t = att.shape[1]
    if gqa:
        nq = SWA_WIDTH // LANES
        qb, kb, vb = SWA_Q_BLOCK0, SWA_Q_BLOCK0 + nq, SWA_Q_BLOCK0 + nq + 1
        kmap = lambda bi, j: (bi, 0, kb)
        vmap = lambda bi, j: (bi, 0, vb)
    else:
        nq = NA_WIDTH // LANES
        qb, kb, vb = 0, nq, 2 * nq
        kmap = lambda bi, j: (bi, 0, kb + j)
        vmap = lambda bi, j: (bi, 0, vb + j)
    return pl.pallas_call(
        functools.partial(_ctx_attn_kernel, gqa=gqa),
        grid=(b, nq),
        in_specs=[
            pl.BlockSpec(memory_space=pltpu.SMEM),
            pl.BlockSpec((1, t, LANES), lambda bi, j: (bi, 0, qb + j)),
            pl.BlockSpec((1, t, LANES), kmap),
            pl.BlockSpec((1, t, LANES), vmap),
        ],
        out_specs=pl.BlockSpec((1, t, LANES), lambda bi, j: (bi, 0, j)),
        out_shape=jax.ShapeDtypeStruct((b, t, nq * LANES), F32),
        compiler_params=_cparams(("parallel", "parallel")),
        name="ctx_attn_swa" if gqa else "ctx_attn_na",
    )(sink, att, att, att)


NA_ROWS_PER_ITER = 4


def _na_lat_kernel(q_ref, k_ref, v_ref, kc_ref, vc_ref, tab_ref, o_ref, kb_ref, vb_ref):
    n = q_ref.shape[1]
    rows = n // GRID_W
    win = NA_WIN_R * GRID_W
    kb_ref[...] = k_ref[0].astype(BF16)
    vb_ref[...] = v_ref[0].astype(BF16)
    kc = kc_ref[0].astype(BF16)
    vc = vc_ref[0].astype(BF16)

    def row_group(ig, carry):
        nr = NA_ROWS_PER_ITER
        g0 = pl.multiple_of(ig * (nr * GRID_W), nr * GRID_W)
        q2 = _stack_heads3(q_ref[0, pl.ds(g0, nr * GRID_W), :].reshape(nr, GRID_W, LANES)).astype(BF16)
        kws, vws, biases = [], [], []
        for r in range(nr):
            i = ig * nr + r
            start = jnp.clip(i - NA_WIN_R // 2, 0, rows - NA_WIN_R)
            k0 = pl.multiple_of(start * GRID_W, GRID_W)
            kws.append(kb_ref[pl.ds(k0, win), :])
            vws.append(vb_ref[pl.ds(k0, win), :])
            biases.append(tab_ref[0, start - i + (NA_WIN_R - 1)])
        s_loc = _bmm_nt(q2, jnp.stack(kws)) * ATTN_SCALE + jnp.stack(biases)
        s_ctx = _dot_nt(q2.reshape(nr * 2 * GRID_W, LANES), kc).reshape(nr, 2 * GRID_W, -1) * ATTN_SCALE
        m = jnp.maximum(jnp.max(s_loc, axis=-1, keepdims=True), jnp.max(s_ctx, axis=-1, keepdims=True))
        p_loc = jnp.exp(s_loc - m)
        p_ctx = jnp.exp(s_ctx - m)
        den = jnp.sum(p_loc, axis=-1, keepdims=True) + jnp.sum(p_ctx, axis=-1, keepdims=True)
        o_ctx = _dot(p_ctx.reshape(nr * 2 * GRID_W, -1).astype(BF16), vc).reshape(nr, 2 * GRID_W, LANES)
        o2 = (_bmm(p_loc, jnp.stack(vws)) + o_ctx) / den
        out = jnp.where(_lane_lo((nr, GRID_W, LANES)), o2[:, :GRID_W], o2[:, GRID_W:])
        o_ref[0, pl.ds(g0, nr * GRID_W), :] = out.reshape(nr * GRID_W, LANES)
        return carry

    lax.fori_loop(0, rows // NA_ROWS_PER_ITER, row_group, 0)


def _na_bias_tables(rpb):
    col = jnp.arange(GRID_W)
    cstart = jnp.clip(col - NA_WIN_C // 2, 0, GRID_W - NA_WIN_C)
    col_mask = (col[None, :] >= cstart[:, None]) & (col[None, :] < cstart[:, None] + NA_WIN_C)
    col_idx = jnp.clip(col[None, :] - col[:, None] + NA_WIN_C - 1, 0, 2 * NA_WIN_C - 2)
    rpb_cols = jnp.where(col_mask[None, None], rpb[:, :, col_idx], NEG_BIG)
    roff = jnp.arange(NA_WIN_R)[:, None] + jnp.arange(NA_WIN_R)[None, :]
    t = rpb_cols[:, roff]
    t = jnp.transpose(t, (0, 1, 3, 2, 4)).reshape(NA_HEADS // 2, 2, NA_WIN_R, GRID_W, NA_WIN_R * GRID_W)
    return jnp.transpose(t, (0, 2, 1, 3, 4)).reshape(NA_HEADS // 2, NA_WIN_R, 2 * GRID_W, NA_WIN_R * GRID_W)


def _na_latent(att, s0, kc, vc, tab):
    n = att.shape[1]
    b, p, _ = kc.shape
    nq = NA_WIDTH // LANES
    return pl.pallas_call(
        _na_lat_kernel,
        grid=(b, nq),
        in_specs=[
            pl.BlockSpec((1, n, LANES), lambda bi, j: (s0 + bi, 0, j)),
            pl.BlockSpec((1, n, LANES), lambda bi, j: (s0 + bi, 0, nq + j)),
            pl.BlockSpec((1, n, LANES), lambda bi, j: (s0 + bi, 0, 2 * nq + j)),
            pl.BlockSpec((1, p, LANES), lambda bi, j: (bi, 0, j)),
            pl.BlockSpec((1, p, LANES), lambda bi, j: (bi, 0, j)),
            pl.BlockSpec((1, NA_WIN_R, 2 * GRID_W, NA_WIN_R * GRID_W), lambda bi, j: (j, 0, 0, 0)),
        ],
        out_specs=pl.BlockSpec((1, n, LANES), lambda bi, j: (bi, 0, j)),
        out_shape=jax.ShapeDtypeStruct((b, n, NA_WIDTH), F32),
        scratch_shapes=[pltpu.VMEM((n, LANES), BF16), pltpu.VMEM((n, LANES), BF16)],
        compiler_params=_cparams(("parallel", "parallel")),
        name="na_latent",
    )(att, att, att, kc, vc, tab)


SWA_BLOCKS_PER_ITER = 2


def _swa_lat_kernel(sink_ref, q_ref, k_ref, v_ref, kc_ref, vc_ref, o_ref, kb_ref, vb_ref):
    j = pl.program_id(1)
    n = q_ref.shape[1]
    blk = SWA_WIN
    span = 3 * blk
    kb_ref[...] = _dup_head(k_ref[0], j).astype(BF16)
    vb_ref[...] = _dup_head(v_ref[0], j).astype(BF16)
    kc = _dup_head(kc_ref[0], j).astype(BF16)
    vc = _dup_head(vc_ref[0], j).astype(BF16)
    row = lax.broadcasted_iota(jnp.int32, (2 * blk, 1), 0)
    snk = jnp.where(row < blk, sink_ref[2 * j], sink_ref[2 * j + 1])
    qoff = lax.broadcasted_iota(jnp.int32, (2 * blk, span), 0) % blk
    koff = lax.broadcasted_iota(jnp.int32, (2 * blk, span), 1)

    def q_group(qg, carry):
        nr = SWA_BLOCKS_PER_ITER
        g0 = pl.multiple_of(qg * (nr * blk), nr * blk)
        q2 = _stack_heads3(q_ref[0, pl.ds(g0, nr * blk), :].reshape(nr, blk, LANES)).astype(BF16)
        kws, vws, valids = [], [], []
        for r in range(nr):
            q0 = g0 + r * blk
            w0 = pl.multiple_of(jnp.clip(q0 - blk, 0, n - span), blk)
            kws.append(kb_ref[pl.ds(w0, span), :])
            vws.append(vb_ref[pl.ds(w0, span), :])
            valids.append(jnp.abs((q0 + qoff) - (w0 + koff)) <= SWA_WIN)
        s_loc = jnp.where(jnp.stack(valids), _bmm_nt(q2, jnp.stack(kws)) * ATTN_SCALE, NEG_BIG)
        s_ctx = _dot_nt(q2.reshape(nr * 2 * blk, LANES), kc).reshape(nr, 2 * blk, -1) * ATTN_SCALE
        m = jnp.maximum(jnp.max(s_loc, axis=-1, keepdims=True), jnp.max(s_ctx, axis=-1, keepdims=True))
        m = jnp.maximum(m, snk)
        p_loc = jnp.exp(s_loc - m)
        p_ctx = jnp.exp(s_ctx - m)
        den = (jnp.sum(p_loc, axis=-1, keepdims=True) + jnp.sum(p_ctx, axis=-1, keepdims=True)
               + jnp.exp(snk - m))
        o_ctx = _dot(p_ctx.reshape(nr * 2 * blk, -1).astype(BF16), vc).reshape(nr, 2 * blk, LANES)
        o2 = (_bmm(p_loc, jnp.stack(vws)) + o_ctx) / den
        out = jnp.where(_lane_lo((nr, blk, LANES)), o2[:, :blk], o2[:, blk:])
        o_ref[0, pl.ds(g0, nr * blk), :] = out.reshape(nr * blk, LANES)
        return carry

    lax.fori_loop(0, n // (SWA_BLOCKS_PER_ITER * blk), q_group, 0)


def _swa_latent(att, s0, kc, vc, sink):
    n = att.shape[1]
    b, p, _ = kc.shape
    nq = SWA_WIDTH // LANES
    qb, kb, vb = SWA_Q_BLOCK0, SWA_Q_BLOCK0 + nq, SWA_Q_BLOCK0 + nq + 1
    return pl.pallas_call(
        _swa_lat_kernel,
        grid=(b, nq),
        in_specs=[
            pl.BlockSpec(memory_space=pltpu.SMEM),
            pl.BlockSpec((1, n, LANES), lambda bi, j: (s0 + bi, 0, qb + j)),
            pl.BlockSpec((1, n, LANES), lambda bi, j: (s0 + bi, 0, kb)),
            pl.BlockSpec((1, n, LANES), lambda bi, j: (s0 + bi, 0, vb)),
            pl.BlockSpec((1, p, LANES), lambda bi, j: (bi, 0, 0)),
            pl.BlockSpec((1, p, LANES), lambda bi, j: (bi, 0, 0)),
        ],
        out_specs=pl.BlockSpec((1, n, LANES), lambda bi, j: (bi, 0, j)),
        out_shape=jax.ShapeDtypeStruct((b, n, SWA_WIDTH), F32),
        scratch_shapes=[pltpu.VMEM((n, LANES), BF16), pltpu.VMEM((n, LANES), BF16)],
        compiler_params=_cparams(("parallel", "parallel")),
        name="swa_latent",
    )(sink, att, att, att, kc, vc)


RK_NB = RK_WIDTH // LANES
LORA_BLOCK = 3 * RK_WIDTH // LANES
GATE_BLOCK = LORA_BLOCK + 1
Q_R, Q_V, Q_A, Q_W, Q_K, Q_B = range(6)
Q_DIR = 3
Q_COLS = (6 + Q_DIR) * RK_WIDTH


def _softplus(x):
    return jnp.maximum(x, 0.0) + jnp.log(1.0 + jnp.exp(-jnp.abs(x)))


def _rk_prep_kernel(u_ref, up_ref, un_ref, cw_ref, w0_ref, w2_ref, a0_ref, a2_ref, g2_ref, kk_ref, ka_ref,
                    rk_ref, q_ref, g_ref, bonus_ref, *, n_ctx_tiles, tiles_per_seq):
    def put(slot, val):
        q_ref[:, slot * RK_WIDTH:(slot + 1) * RK_WIDTH] = val

    i = pl.program_id(0)
    li = i - n_ctx_tiles
    is_lat = i >= n_ctx_tiles
    has_prev = jnp.logical_and(is_lat, li % tiles_per_seq != 0)
    has_next = jnp.logical_and(is_lat, li % tiles_per_seq != tiles_per_seq - 1)
    u = u_ref[...]
    tm = u.shape[0]
    prev_row = jnp.where(has_prev, up_ref[7:8, :], 0.0)
    next_row = jnp.where(has_next, un_ref[0:1, :], 0.0)
    row = lax.broadcasted_iota(jnp.int32, u.shape, 0)
    um = jnp.where(row == 0, prev_row, pltpu.roll(u, 1, 0))
    up = jnp.where(row == tm - 1, next_row, pltpu.roll(u, tm - 1, 0))
    u = um * cw_ref[0:1, :] + u * cw_ref[1:2, :] + up * cw_ref[2:3, :]

    r = u[:, 0:RK_WIDTH]
    k = u[:, RK_WIDTH:2 * RK_WIDTH]
    v = u[:, 2 * RK_WIDTH:3 * RK_WIDTH]
    lora = u[:, LORA_BLOCK * LANES:(LORA_BLOCK + 1) * LANES]
    gl = u[:, GATE_BLOCK * LANES:(GATE_BLOCK + 1) * LANES]
    put(Q_R, r)
    put(Q_V, v)
    g_ref[...] = _dot3(jax.nn.sigmoid(gl), g2_ref[...])

    kn = k * kk_ref[...]
    kk = jnp.concatenate(
        [kn[:, c * LANES:(c + 1) * LANES]
         * lax.rsqrt(jnp.maximum(_pair_sum(jnp.square(kn[:, c * LANES:(c + 1) * LANES])), 1e-24))
         for c in range(RK_NB)], axis=1)
    put(Q_A, -kk)

    lora_t = jnp.tanh(lora)
    kd_sum = None
    for d in range(2):
        w = -_softplus(-(w0_ref[d:d + 1, :] + _dot3(lora_t, w2_ref[d]))) - 0.5
        put(Q_W + Q_DIR * d, -jnp.exp(w))
        a = jax.nn.sigmoid(a0_ref[d:d + 1, :] + _dot3(lora, a2_ref[d]))
        kd = k * (1.0 + (a - 1.0) * ka_ref[...])
        put(Q_K + Q_DIR * d, kd)
        put(Q_B + Q_DIR * d, kk * a)
        kd_sum = kd if kd_sum is None else kd_sum + kd

    t = r * kd_sum * rk_ref[...]
    bonus_ref[...] = jnp.concatenate(
        [_pair_sum(t[:, c * LANES:(c + 1) * LANES]) for c in range(RK_NB)], axis=1) * v


def _rk_prep(u, p, n_ctx_tiles, tiles_per_seq):
    n_tok = u.shape[0]
    n_tiles = n_tok // TOK_TILE
    sub = TOK_TILE // 8
    last8 = n_tok // 8 - 1
    tok = lambda i: (i, 0)
    const2 = lambda i: (0, 0)
    const3 = lambda i: (0, 0, 0)
    one = jax.ShapeDtypeStruct((n_tok, RK_WIDTH), F32)
    tok_spec = pl.BlockSpec((TOK_TILE, RK_WIDTH), tok)
    return pl.pallas_call(
        functools.partial(_rk_prep_kernel, n_ctx_tiles=n_ctx_tiles, tiles_per_seq=tiles_per_seq),
        grid=(n_tiles,),
        in_specs=[
            pl.BlockSpec((TOK_TILE, RK_COLS), tok),
            pl.BlockSpec((8, RK_COLS), lambda i: (jnp.maximum(i * sub - 1, 0), 0)),
            pl.BlockSpec((8, RK_COLS), lambda i: (jnp.minimum((i + 1) * sub, last8), 0)),
            pl.BlockSpec((3, RK_COLS), const2),
            pl.BlockSpec((2, RK_WIDTH), const2),
            pl.BlockSpec((2, LANES, RK_WIDTH), const3),
            pl.BlockSpec((2, RK_WIDTH), const2),
            pl.BlockSpec((2, LANES, RK_WIDTH), const3),
            pl.BlockSpec((RK_GATE_LORA, RK_WIDTH), const2),
            pl.BlockSpec((1, RK_WIDTH), const2),
            pl.BlockSpec((1, RK_WIDTH), const2),
            pl.BlockSpec((1, RK_WIDTH), const2),
        ],
        out_specs=[pl.BlockSpec((TOK_TILE, Q_COLS), tok), tok_spec, tok_spec],
        out_shape=[jax.ShapeDtypeStruct((n_tok, Q_COLS), F32), one, one],
        compiler_params=_cparams(("parallel",)),
        name="rk_prep",
    )(u, u, u, p["rk_conv"], p["rk_w0"], p["rk_w2_pad"], p["rk_a0"], p["rk_a2_pad"], p["rk_g2"],
      p["rk_k_k"], p["rk_k_a"], p["rk_r_k"])


RK_CHUNK = 64
PAIR = 2 * HEAD_DIM
STATE_SEQS = 8
RK_STEP_CHUNKS = 4


def _split3_bf16(x):
    hi = x.astype(BF16)
    r1 = x - hi.astype(F32)
    mid = r1.astype(BF16)
    return hi, mid, (r1 - mid.astype(F32)).astype(BF16)


def _rk_chunk_kernel(q_ref, rbar_ref, ybar_ref, phi_ref, psi_ref):
    c = RK_CHUNK
    n = 2 * c
    nd = 2 * RK_NB
    nu = RK_STEP_CHUNKS * nd

    def tiles(slot, per_dir):
        cols = [(slot + (Q_DIR * d if per_dir else 0)) * RK_WIDTH + p * LANES
                for d in range(2) for p in range(RK_NB)]
        return jnp.stack([q_ref[ck * c:(ck + 1) * c, lo:lo + LANES] for ck in range(RK_STEP_CHUNKS) for lo in cols])

    r, v, a = tiles(Q_R, False), tiles(Q_V, False), tiles(Q_A, False)
    lw, k, b = tiles(Q_W, True), tiles(Q_K, True), tiles(Q_B, True)
    unit = lax.broadcasted_iota(jnp.int32, (nu, 1, 1), 0)
    sgn = jnp.ones((nu, 1, 1), jnp.int32)
    for ck in range(RK_STEP_CHUNKS):
        sgn = jnp.where(jnp.logical_and(unit >= ck * nd + RK_NB, unit < (ck + 1) * nd), -1, sgn)
    bwd = sgn < 0
    tdiff = lax.broadcasted_iota(jnp.int32, (1, c, c), 2) - lax.broadcasted_iota(jnp.int32, (1, c, c), 1)
    tri = jnp.where(tdiff * sgn <= 0, 1.0, 0.0)
    cum = sum(_bmm(tri, part) for part in _split3_bf16(lw))
    tot = jnp.where(bwd, cum[:, 0:1], cum[:, c - 1:c])
    a_t = a * jnp.exp(cum - lw)
    r_t = r * jnp.exp(cum)
    e_neg = jnp.exp(-cum)
    e_end = jnp.exp(tot - cum)
    g = _bmm_nt(jnp.concatenate([_stack_heads3(a_t), _stack_heads3(r_t)], axis=1),
                jnp.concatenate([_stack_heads3(b * e_neg), _stack_heads3(k * e_neg)], axis=1))
    r2 = lax.broadcasted_iota(jnp.int32, (1, n, n), 1)
    c2 = lax.broadcasted_iota(jnp.int32, (1, n, n), 2)
    order = (jnp.bitwise_and(c2, c - 1) - jnp.bitwise_and(r2, c - 1)) * sgn
    eye = jnp.where(r2 == c2, 1.0, 0.0)
    l_ab = jnp.where(order < 0, g[:, :n, :n], 0.0)
    l_ak = jnp.where(order < 0, g[:, :n, n:], 0.0)
    m_rb = jnp.where(order <= 0, g[:, n:, :n], 0.0)
    m_rk = jnp.where(order <= 0, g[:, n:, n:], 0.0)
    t_inv = eye + l_ab
    pw = l_ab
    for _ in range(5):
        pw = _bmm(pw, pw)
        t_inv = t_inv + _bmm(t_inv, pw)
    sv = _stack_heads3(v)
    au = _bmm(t_inv, jnp.concatenate([_stack_heads3(a_t), _bmm(l_ak, sv)], axis=2))
    ry = _bmm(m_rb, au) + jnp.concatenate([_stack_heads3(r_t), _bmm(m_rk, sv)], axis=2)
    ry = ry[:, :c] + ry[:, c:]
    bt = jnp.swapaxes(_stack_heads3(b * e_end), 1, 2)
    kt = jnp.swapaxes(_stack_heads3(k * e_end), 1, 2)
    pp = _bmm(bt, au)
    phi = eye * jnp.exp(tot) + pp[:, :, :PAIR]
    psi = pp[:, :, PAIR:] + _bmm(kt, sv)
    for ck in range(RK_STEP_CHUNKS):
        for d in range(2):
            for p in range(RK_NB):
                u = ck * nd + d * RK_NB + p
                rbar_ref[d, ck * c:(ck + 1) * c, p * LANES:(p + 1) * LANES] = ry[u, :, :PAIR]
                ybar_ref[d, ck * c:(ck + 1) * c, p * LANES:(p + 1) * LANES] = ry[u, :, PAIR:]
                phi_ref[d, ck, p] = phi[u]
                psi_ref[d, ck, p] = psi[u]


def _rk_chunk(q, tile0, n_seq, t):
    nc = t // RK_CHUNK
    sc = RK_STEP_CHUNKS
    assert nc % sc == 0 and tile0 % sc == 0
    row_sh = jax.ShapeDtypeStruct((2, n_seq, t, RK_WIDTH), F32)
    mat_sh = jax.ShapeDtypeStruct((2, n_seq, nc, RK_NB, PAIR, PAIR), F32)
    row_spec = pl.BlockSpec((2, None, sc * RK_CHUNK, RK_WIDTH), lambda s, c: (0, s, c, 0))
    mat_spec = pl.BlockSpec((2, None, sc, RK_NB, PAIR, PAIR), lambda s, c: (0, s, c, 0, 0, 0))
    return pl.pallas_call(
        _rk_chunk_kernel,
        grid=(n_seq, nc // sc),
        in_specs=[pl.BlockSpec((sc * RK_CHUNK, Q_COLS), lambda s, c: ((tile0 + s * nc) // sc + c, 0))],
        out_specs=[row_spec, row_spec, mat_spec, mat_spec],
        out_shape=[row_sh, row_sh, mat_sh, mat_sh],
        compiler_params=_cparams(("parallel", "parallel")),
        name="rk_chunk",
    )(q)


def _rk_state_kernel(rf_ref, rb_ref, yf_ref, yb_ref, phf_ref, phb_ref, psf_ref, psb_ref, s0_ref,
                     of_ref, ob_ref, s_ref):
    @pl.when(pl.program_id(1) == 0)
    def _():
        s_ref[...] = s0_ref[...]

    def pair_tiles(ref_f, ref_b, s):
        return jnp.stack([ref[s, :, p * LANES:(p + 1) * LANES] for ref in (ref_f, ref_b) for p in range(RK_NB)])

    def seq_body(s, carry):
        nu = 2 * RK_NB
        h = s_ref[s].reshape(nu, PAIR, PAIR)
        y = _bmm3(pair_tiles(rf_ref, rb_ref, s), h) + pair_tiles(yf_ref, yb_ref, s)
        phi = jnp.concatenate([phf_ref[s], phb_ref[s]], axis=0)
        psi = jnp.concatenate([psf_ref[s], psb_ref[s]], axis=0)
        s_ref[s] = (_bmm3(phi, h) + psi).reshape(2, RK_NB, PAIR, PAIR)
        of_ref[s] = jnp.concatenate([y[p] for p in range(RK_NB)], axis=1)
        ob_ref[s] = jnp.concatenate([y[RK_NB + p] for p in range(RK_NB)], axis=1)
        return carry

    lax.fori_loop(0, s_ref.shape[0], seq_body, 0)


def _rk_state(rbar, ybar, phi, psi, s0):
    _, n_seq, t, _ = rbar.shape
    nc = t // RK_CHUNK
    sg = STATE_SEQS
    row_blk = (None, sg, RK_CHUNK, RK_WIDTH)
    mat_blk = (None, sg, None, RK_NB, PAIR, PAIR)
    fwd_row = pl.BlockSpec(row_blk, lambda g, c: (0, g, c, 0))
    bwd_row = pl.BlockSpec(row_blk, lambda g, c: (1, g, nc - 1 - c, 0))
    fwd_mat = pl.BlockSpec(mat_blk, lambda g, c: (0, g, c, 0, 0, 0))
    bwd_mat = pl.BlockSpec(mat_blk, lambda g, c: (1, g, nc - 1 - c, 0, 0, 0))
    st = pl.BlockSpec((sg, 2, RK_NB, PAIR, PAIR), lambda g, c: (g, 0, 0, 0, 0))
    out_sh = jax.ShapeDtypeStruct((n_seq, t, RK_WIDTH), F32)
    return pl.pallas_call(
        _rk_state_kernel,
        grid=(n_seq // sg, nc),
        in_specs=[fwd_row, bwd_row, fwd_row, bwd_row, fwd_mat, bwd_mat, fwd_mat, bwd_mat, st],
        out_specs=[pl.BlockSpec((sg, RK_CHUNK, RK_WIDTH), lambda g, c: (g, c, 0)),
                   pl.BlockSpec((sg, RK_CHUNK, RK_WIDTH), lambda g, c: (g, nc - 1 - c, 0)), st],
        out_shape=[out_sh, out_sh, jax.ShapeDtypeStruct((n_seq, 2, RK_NB, PAIR, PAIR), F32)],
        compiler_params=_cparams(("parallel", "arbitrary")),
        name="rk_state",
    )(rbar, rbar, ybar, ybar, phi, phi, psi, psi, s0)


def _pair_states(s):
    bsz = s.shape[0]
    h = jnp.swapaxes(s, -1, -2).reshape(bsz, 2, RK_NB, 2, HEAD_DIM, HEAD_DIM)
    return jnp.einsum("bdphkv,hg->bdphkgv", h, jnp.eye(2, dtype=F32)).reshape(bsz, 2, RK_NB, PAIR, PAIR)


def _head_states(s):
    bsz = s.shape[0]
    h = jnp.stack([s[..., :HEAD_DIM, :HEAD_DIM], s[..., HEAD_DIM:, HEAD_DIM:]], axis=3)
    return jnp.swapaxes(h.reshape(bsz, 2, RK_HEADS, HEAD_DIM, HEAD_DIM), -1, -2)


def _rwkv_group(q, tile0, n_seq, t, s0):
    rbar, ybar, phi, psi = _rk_chunk(q, tile0, n_seq, t)
    y_f, y_b, s_fin = _rk_state(rbar, ybar, phi, psi, s0)
    return y_f.reshape(n_seq * t, RK_WIDTH), y_b.reshape(n_seq * t, RK_WIDTH), s_fin


def _out_proj_kernel(x_ref, ona_c, ona_l, osw_c, osw_l, yf_c, yf_l, yb_c, yb_l, bonus_ref, g_ref, lng_ref, lnb_ref,
                     w_ref, mod_ref, n2_ref, rw_ref, rb_ref, x1_ref, h2_ref, lg_ref, *, n_ctx_tiles):
    is_ctx = pl.program_id(0) < n_ctx_tiles
    pick = lambda c_ref, l_ref: jnp.where(is_ctx, c_ref[...], l_ref[...])
    ona = pick(ona_c, ona_l)
    osw = pick(osw_c, osw_l)
    y = pick(yf_c, yf_l) + pick(yb_c, yb_l)
    parts = []
    for c in range(RK_NB):
        yc = y[:, c * LANES:(c + 1) * LANES]
        dc = yc - _pair_sum(yc) * (1.0 / HEAD_DIM)
        var = _pair_sum(dc * dc) * (1.0 / HEAD_DIM)
        parts.append(dc * lax.rsqrt(var + GN_EPS))
    yn = jnp.concatenate(parts, axis=1) * lng_ref[...] + lnb_ref[...]
    o_rk = (yn + bonus_ref[...]) * g_ref[...]
    o = (_dot(ona.astype(BF16), w_ref[0:NA_WIDTH, :])
         + _dot(osw.astype(BF16), w_ref[NA_WIDTH:NA_WIDTH + SWA_WIDTH, :])
         + _dot(o_rk.astype(BF16), w_ref[NA_WIDTH + SWA_WIDTH:, :]))
    x1 = x_ref[...] + mod_ref[0, 2:3, :] * o
    x1_ref[...] = x1
    yn2 = x1 * lax.rsqrt(jnp.mean(x1 * x1, axis=-1, keepdims=True) + RMS_EPS)
    h2 = (yn2 * n2_ref[...]) * (1.0 + mod_ref[0, 4:5, :]) + mod_ref[0, 3:4, :]
    h2_ref[...] = h2.astype(BF16)
    lg_ref[...] = (_dot3(h2, rw_ref[...]) + rb_ref[...])[:, :N_EXPERTS]


def _out_proj(x, o_na, o_sw, y_f, y_b, bonus, g, p, mods, tile_mod, n_ctx_tiles):
    n_tok = x.shape[0]
    tok = lambda i: (i, 0)
    const = lambda i: (0, 0)
    ctx_tile = lambda i: (jnp.minimum(i, n_ctx_tiles - 1), 0)
    lat_tile = lambda i: (jnp.maximum(i - n_ctx_tiles, 0), 0)
    pair = lambda w: [pl.BlockSpec((TOK_TILE, w), ctx_tile), pl.BlockSpec((TOK_TILE, w), lat_tile)]
    return pl.pallas_call(
        functools.partial(_out_proj_kernel, n_ctx_tiles=n_ctx_tiles),
        grid=(n_tok // TOK_TILE,),
        in_specs=[
            pl.BlockSpec((TOK_TILE, D_MODEL), tok),
            *pair(NA_WIDTH), *pair(SWA_WIDTH), *pair(RK_WIDTH), *pair(RK_WIDTH),
            pl.BlockSpec((TOK_TILE, RK_WIDTH), tok),
            pl.BlockSpec((TOK_TILE, RK_WIDTH), tok),
            pl.BlockSpec((1, RK_WIDTH), const),
            pl.BlockSpec((1, RK_WIDTH), const),
            pl.BlockSpec((D_MODEL, D_MODEL), const),
            pl.BlockSpec((1, 6, D_MODEL), lambda i: (tile_mod(i), 0, 0)),
            pl.BlockSpec((1, D_MODEL), const),
            pl.BlockSpec((D_MODEL, LANES), const),
            pl.BlockSpec((1, LANES), const),
        ],
        out_specs=[
            pl.BlockSpec((TOK_TILE, D_MODEL), tok),
            pl.BlockSpec((TOK_TILE, D_MODEL), tok),
            pl.BlockSpec((TOK_TILE, N_EXPERTS), tok),
        ],
        out_shape=[
            jax.ShapeDtypeStruct((n_tok, D_MODEL), F32),
            jax.ShapeDtypeStruct((n_tok, D_MODEL), BF16),
            jax.ShapeDtypeStruct((n_tok, N_EXPERTS), F32),
        ],
        compiler_params=_cparams(("parallel",)),
        name="out_proj",
    )(x, *o_na, *o_sw, *y_f, *y_b, bonus, g, p["rk_ln_g"], p["rk_ln_b"], p["w_out_bf16"], mods, p["norm2_g"],
      p["router_w_pad"], p["router_b_pad"])


H2_PAD_ROWS = 32768
W1_SEL_COLS = 256
MOE_VMEM_LIMIT = 56 * 1024 * 1024


def _moe_kernel(meta_ref, x_ref, w1_ref, b1g_ref, b1l_ref, w2_ref, b2_ref, o_ref, w1g_ref, w1l_ref, w2b_ref):
    i = pl.program_id(0)
    n_blk = meta_ref.shape[0] - 1
    n_used = meta_ref[n_blk]
    d_e = w2_ref.shape[1]
    new_expert = jnp.logical_or(i == 0, meta_ref[i] != meta_ref[jnp.maximum(i - 1, 0)])

    @pl.when(jnp.logical_and(i < n_used, new_expert))
    def _():
        src = lax.broadcasted_iota(jnp.int32, (2 * W1_SEL_COLS, 2 * W1_SEL_COLS), 0)
        dst = lax.broadcasted_iota(jnp.int32, (2 * W1_SEL_COLS, 2 * W1_SEL_COLS), 1)
        pick = jnp.where(dst < W1_SEL_COLS, 2 * dst, 2 * (dst - W1_SEL_COLS) + 1)
        sel = jnp.where(src == pick, 1.0, 0.0).astype(BF16)
        for t in range(d_e // W1_SEL_COLS):
            cols = _dot(w1_ref[0, :, 2 * t * W1_SEL_COLS:2 * (t + 1) * W1_SEL_COLS].astype(BF16), sel)
            w1g_ref[:, t * W1_SEL_COLS:(t + 1) * W1_SEL_COLS] = cols[:, :W1_SEL_COLS].astype(BF16)
            w1l_ref[:, t * W1_SEL_COLS:(t + 1) * W1_SEL_COLS] = cols[:, W1_SEL_COLS:].astype(BF16)
        w2b_ref[...] = w2_ref[0].astype(BF16)

    @pl.when(i < n_used)
    def _():
        x = x_ref[...]
        glu = jnp.minimum(_dot(x, w1g_ref[...]) + b1g_ref[0], SWIGLU_LIMIT)
        lin = jnp.clip(_dot(x, w1l_ref[...]) + b1l_ref[0], -SWIGLU_LIMIT, SWIGLU_LIMIT)
        act = glu * jax.nn.sigmoid(SWIGLU_ALPHA * glu) * (lin + 1.0)
        o_ref[...] = (_dot(act.astype(BF16), w2b_ref[...]) + b2_ref[0]).astype(BF16)

    @pl.when(i >= n_used)
    def _():
        o_ref[...] = jnp.zeros_like(o_ref)


def _moe_blocks(meta, xb, w1, b1g, b1l, w2, b2, layer):
    n_rows = xb.shape[0]
    n_blk = n_rows // MOE_BLK
    d_e = w2.shape[2]
    row = lambda i, m: (i, 0)
    exp3 = lambda i, m: (layer, m[i], 0, 0)
    grid_spec = pltpu.PrefetchScalarGridSpec(
        num_scalar_prefetch=1,
        grid=(n_blk,),
        in_specs=[
            pl.BlockSpec((MOE_BLK, D_MODEL), row),
            pl.BlockSpec((None, 1, D_MODEL, 2 * d_e), exp3),
            pl.BlockSpec((None, 1, 1, d_e), exp3),
            pl.BlockSpec((None, 1, 1, d_e), exp3),
            pl.BlockSpec((None, 1, d_e, D_MODEL), exp3),
            pl.BlockSpec((None, 1, 1, D_MODEL), exp3),
        ],
        out_specs=pl.BlockSpec((MOE_BLK, D_MODEL), row),
        scratch_shapes=[
            pltpu.VMEM((D_MODEL, d_e), BF16),
            pltpu.VMEM((D_MODEL, d_e), BF16),
            pltpu.VMEM((d_e, D_MODEL), BF16),
        ],
    )
    return pl.pallas_call(
        _moe_kernel,
        grid_spec=grid_spec,
        out_shape=jax.ShapeDtypeStruct((n_rows, D_MODEL), BF16),
        compiler_params=pltpu.CompilerParams(dimension_semantics=("arbitrary",),
                                             vmem_limit_bytes=MOE_VMEM_LIMIT),
        name="moe_blocks",
    )(meta, xb, w1, b1g, b1l, w2, b2)


def _route(logits):
    n_tok = logits.shape[0]
    top_v, top_i = lax.top_k(logits, TOP_K)
    gates = jax.nn.softmax(top_v, axis=-1)
    e_flat = top_i.reshape(-1).astype(jnp.int32)
    n_rows = n_tok * TOP_K
    onehot = (e_flat[:, None] == jnp.arange(N_EXPERTS, dtype=jnp.int32)[None, :]).astype(jnp.int32)
    csum = jnp.cumsum(onehot, axis=0)
    counts = csum[-1]
    starts = jnp.cumsum(counts) - counts
    pcounts = (counts + MOE_BLK - 1) // MOE_BLK * MOE_BLK
    pends = jnp.cumsum(pcounts)
    pstarts = pends - pcounts
    dest = jnp.take_along_axis(csum + (pstarts - 1)[None, :], e_flat[:, None], axis=1)[:, 0]
    n_blk = n_rows // MOE_BLK + N_EXPERTS
    blk_start = jnp.arange(n_blk, dtype=jnp.int32) * MOE_BLK
    blk_exp = jnp.minimum(jnp.sum((blk_start[:, None] >= pends[None, :]).astype(jnp.int32), axis=1), N_EXPERTS - 1)
    order = jnp.argsort(e_flat)
    pos = jnp.arange(n_blk * MOE_BLK, dtype=jnp.int32)
    src = jnp.clip(pos + jnp.repeat((starts - pstarts)[blk_exp], MOE_BLK), 0, n_rows - 1)
    row_tok = order[src].astype(jnp.int32) // TOP_K
    meta = jnp.concatenate([blk_exp, (pends[-1:] // MOE_BLK).astype(jnp.int32)])
    return meta, row_tok, gates, dest.reshape(n_tok, TOP_K).T.reshape(-1)


def _combine_kernel(x_ref, yg_ref, gate_ref, mod_ref, o_ref):
    gate = gate_ref[...]
    acc = gate[:, 0:1] * yg_ref[0].astype(F32)
    for j in range(1, TOP_K):
        acc = acc + gate[:, j:j + 1] * yg_ref[j].astype(F32)
    o_ref[...] = x_ref[...] + mod_ref[0, 5:6, :] * acc


def _combine(x1, yg, gates, mods, tile_mod):
    n_tok = x1.shape[0]
    return pl.pallas_call(
        _combine_kernel,
        grid=(n_tok // TOK_TILE,),
        in_specs=[
            pl.BlockSpec((TOK_TILE, D_MODEL), lambda i: (i, 0)),
            pl.BlockSpec((TOP_K, TOK_TILE, D_MODEL), lambda i: (0, i, 0)),
            pl.BlockSpec((TOK_TILE, TOP_K), lambda i: (i, 0)),
            pl.BlockSpec((1, 6, D_MODEL), lambda i: (tile_mod(i), 0, 0)),
        ],
        out_specs=pl.BlockSpec((TOK_TILE, D_MODEL), lambda i: (i, 0)),
        out_shape=jax.ShapeDtypeStruct((n_tok, D_MODEL), F32),
        compiler_params=_cparams(("parallel",)),
        name="moe_combine",
    )(x1, yg, gates, mods)


def kernel(x_prompt, x_sample, c, cache_na_k, cache_na_v, cache_swa_k, cache_swa_v, state_rwkv, c_ctx, w_ada, b_ada, norm1_g, norm2_g, w_in, w_out, na_q_norm, na_k_norm, na_rpb, swa_q_norm, swa_k_norm, swa_sink, rk_conv, rk_w0, rk_w2, rk_a0, rk_a2, rk_g2, rk_k_k, rk_k_a, rk_r_k, rk_ln_g, rk_ln_b, moe_router_w, moe_router_b, moe_w1, moe_b1, moe_w2, moe_b2):
    bc, tc, _ = x_prompt.shape
    bl, tl, _ = x_sample.shape
    depth = w_in.shape[0]
    n_ctx = bc * tc
    n_lat = bl * tl
    assert tc == TOK_TILE and tl % TOK_TILE == 0 and n_ctx % tl == 0
    n_ctx_tiles = n_ctx // TOK_TILE
    tiles_per_seq = tl // TOK_TILE
    past = cache_na_k.shape[2]

    def tile_mod(i):
        return jnp.where(i < n_ctx_tiles, 0, 1 + (i - n_ctx_tiles) // tiles_per_seq)

    def tile_rope(i):
        return jnp.where(i < n_ctx_tiles, tiles_per_seq, (i - n_ctx_tiles) % tiles_per_seq)

    x = jnp.concatenate([x_prompt.reshape(n_ctx, D_MODEL), x_sample.reshape(n_lat, D_MODEL)], axis=0)

    n_mod = 1 + bl
    mod_rows = -(-n_mod // 8) * 8
    cvecs = jnp.concatenate([c_ctx[None, :], c, jnp.zeros((mod_rows - n_mod, D_MODEL), F32)], axis=0)
    mods_all = _ada_mod(cvecs, w_ada, b_ada).reshape(depth, mod_rows, 6, D_MODEL)
    cos_tab, sin_tab = _rope_tables(tl)
    tile2 = lambda g: jnp.concatenate([g, g])[None, :]
    pad_lanes = lambda z: jnp.pad(z, ((0, 0), (0, LANES - z.shape[1])))
    zeros_lora = jnp.zeros((2, RK_DECAY_LORA, RK_WIDTH), F32)

    na_k_l, na_v_l, sw_k_l, sw_v_l, st_l = [], [], [], [], []
    for l in range(depth):
        mods = mods_all[l]
        qk_gains = jnp.concatenate(
            [tile2(na_q_norm[l]), tile2(na_k_norm[l]), tile2(swa_q_norm[l]), tile2(swa_k_norm[l])], axis=0)
        p = {
            "rk_conv": rk_conv[l], "rk_w0": rk_w0[l], "rk_a0": rk_a0[l], "rk_g2": rk_g2[l],
            "rk_w2_pad": jnp.concatenate([rk_w2[l], zeros_lora], axis=1),
            "rk_a2_pad": jnp.concatenate([zeros_lora, rk_a2[l]], axis=1),
            "rk_k_k": rk_k_k[l][None, :], "rk_k_a": rk_k_a[l][None, :],
            "rk_r_k": rk_r_k[l].reshape(1, RK_WIDTH),
            "rk_ln_g": rk_ln_g[l][None, :], "rk_ln_b": rk_ln_b[l][None, :],
            "w_out_bf16": w_out[l].astype(BF16), "norm2_g": norm2_g[l][None, :],
            "router_w_pad": pad_lanes(moe_router_w[l]), "router_b_pad": pad_lanes(moe_router_b[l][None, :]),
        }

        att, u = _in_proj(x, norm1_g[l][None, :], mods, w_in[l].astype(BF16), qk_gains, cos_tab, sin_tab,
                          tile_mod, tile_rope)
        q, g, bonus = _rk_prep(u, p, n_ctx_tiles, tiles_per_seq)
        att_c = att[:n_ctx].reshape(bc, tc, ATT_COLS)
        att_by_ctx_len = att.reshape((n_ctx + n_lat) // tc, tc, ATT_COLS)
        att_by_lat_len = att.reshape((n_ctx + n_lat) // tl, tl, ATT_COLS)
        na_k_l.append(att_c[:, :, NA_WIDTH:2 * NA_WIDTH].reshape(bc, tc, NA_HEADS, HEAD_DIM))
        na_v_l.append(att_c[:, :, 2 * NA_WIDTH:NA_COLS].reshape(bc, tc, NA_HEADS, HEAD_DIM))
        sw_k_l.append(att_c[:, :, NA_COLS + SWA_WIDTH:NA_COLS + SWA_WIDTH + SWA_KV_WIDTH]
                      .reshape(bc, tc, SWA_KV_HEADS, HEAD_DIM))
        sw_v_l.append(att_c[:, :, NA_COLS + SWA_WIDTH + SWA_KV_WIDTH:].reshape(bc, tc, SWA_KV_HEADS, HEAD_DIM))

        sink = swa_sink[l]
        o_na = (_ctx_attn(att_by_ctx_len, bc, sink, gqa=False).reshape(n_ctx, NA_WIDTH),
                _na_latent(att_by_lat_len, n_ctx // tl, cache_na_k[:, l].reshape(bl, past, NA_WIDTH),
                           cache_na_v[:, l].reshape(bl, past, NA_WIDTH),
                           _na_bias_tables(na_rpb[l])).reshape(n_lat, NA_WIDTH))
        o_sw = (_ctx_attn(att_by_ctx_len, bc, sink, gqa=True).reshape(n_ctx, SWA_WIDTH),
                _swa_latent(att_by_lat_len, n_ctx // tl, cache_swa_k[:, l].reshape(bl, past, SWA_KV_WIDTH),
                            cache_swa_v[:, l].reshape(bl, past, SWA_KV_WIDTH), sink).reshape(n_lat, SWA_WIDTH))

        yf_c, yb_c, s_fin = _rwkv_group(q, 0, bc, tc, jnp.zeros((bc, 2, RK_NB, PAIR, PAIR), F32))
        yf_l, yb_l, _ = _rwkv_group(q, n_ctx // RK_CHUNK, bl, tl, _pair_states(state_rwkv[:, l]))
        st_l.append(_head_states(s_fin))

        x1, h2, logits = _out_proj(x, o_na, o_sw, (yf_c, yf_l), (yb_c, yb_l), bonus, g, p, mods, tile_mod,
                                   n_ctx_tiles)
        meta, row_tok, gates, dest = _route(logits)
        h2 = jnp.concatenate([h2, jnp.zeros((H2_PAD_ROWS - h2.shape[0], D_MODEL), BF16)], axis=0)
        yb = _moe_blocks(meta, h2[row_tok], moe_w1, moe_b1[:, :, None, 0::2], moe_b1[:, :, None, 1::2], moe_w2,
                         moe_b2[:, :, None, :], l)
        x = _combine(x1, yb[dest].reshape(TOP_K, n_ctx + n_lat, D_MODEL), gates, mods, tile_mod)

    y_p = x[:n_ctx].reshape(bc, tc, D_MODEL)
    y_s = x[n_ctx:].reshape(bl, tl, D_MODEL)
    return (y_p, y_s, jnp.stack(na_k_l, axis=1), jnp.stack(na_v_l, axis=1), jnp.stack(sw_k_l, axis=1),
            jnp.stack(sw_v_l, axis=1), jnp.stack(st_l, axis=1))
```

```python
import functools

import jax
import jax.numpy as jnp
from jax import lax
from jax.experimental import pallas as pl
from jax.experimental.pallas import tpu as pltpu

F32 = jnp.float32
BF16 = jnp.bfloat16

D_MODEL = 1024
HEAD_DIM = 64
LANES = 128
GRID_W = 64
NA_HEADS = 6
SWA_HEADS = 4
SWA_KV_HEADS = 2
RK_HEADS = 6
NA_WIDTH = NA_HEADS * HEAD_DIM
SWA_WIDTH = SWA_HEADS * HEAD_DIM
SWA_KV_WIDTH = SWA_KV_HEADS * HEAD_DIM
RK_WIDTH = RK_HEADS * HEAD_DIM
RK_DECAY_LORA = 64
RK_A_LORA = 64
RK_GATE_LORA = 128
RK_COLS = 3 * RK_WIDTH + RK_DECAY_LORA + RK_A_LORA + RK_GATE_LORA
NA_COLS = 3 * NA_WIDTH
SWA_COLS = SWA_WIDTH + 2 * SWA_KV_WIDTH
ATT_COLS = NA_COLS + SWA_COLS
IN_COLS = ATT_COLS + RK_COLS
NA_WIN_R = 8
NA_WIN_C = 16
SWA_WIN = 128
ROPE_THETA = 10000.0
ATTN_SCALE = HEAD_DIM ** -0.5
N_EXPERTS = 32
TOP_K = 4
SWIGLU_LIMIT = 7.0
SWIGLU_ALPHA = 1.702
MOE_BLK = 256
RMS_EPS = 1e-6
GN_EPS = 64e-5
NEG_BIG = -1e30

TOK_TILE = 256
VMEM_LIMIT = 48 * 1024 * 1024


def _cparams(sem):
    return pltpu.CompilerParams(dimension_semantics=sem, vmem_limit_bytes=VMEM_LIMIT)


def _dot(a, b):
    return jnp.dot(a, b, preferred_element_type=F32)


def _dot_nt(a, b):
    return lax.dot_general(a, b, (((1,), (1,)), ((), ())), preferred_element_type=F32)


def _split_bf16(x):
    hi = x.astype(BF16)
    lo = (x - hi.astype(F32)).astype(BF16)
    return hi, lo


def _dot3(a, b):
    ah, al = _split_bf16(a)
    bh, bl = _split_bf16(b)
    return _dot(ah, bh) + (_dot(ah, bl) + _dot(al, bh))


def _bmm_raw(a, b):
    return lax.dot_general(a, b, (((2,), (1,)), ((0,), (0,))), preferred_element_type=F32)


def _bmm(a, b):
    return _bmm_raw(a.astype(BF16), b.astype(BF16))


def _bmm_nt(a, b):
    return lax.dot_general(a.astype(BF16), b.astype(BF16), (((2,), (2,)), ((0,), (0,))),
                           preferred_element_type=F32)


def _bmm3(a, b):
    ah, al = _split_bf16(a)
    bh, bl = _split_bf16(b)
    return _bmm_raw(ah, bh) + (_bmm_raw(ah, bl) + _bmm_raw(al, bh))


def _lane_lo(shape):
    return lax.broadcasted_iota(jnp.int32, shape, len(shape) - 1) < HEAD_DIM


def _pair_sum(x):
    lo = _lane_lo(x.shape)
    s_lo = jnp.sum(jnp.where(lo, x, 0.0), axis=-1, keepdims=True)
    s_hi = jnp.sum(jnp.where(lo, 0.0, x), axis=-1, keepdims=True)
    return jnp.where(lo, s_lo, s_hi)


def _stack_heads(q):
    lo = _lane_lo(q.shape)
    return jnp.concatenate([jnp.where(lo, q, 0.0), jnp.where(lo, 0.0, q)], axis=0)


def _stack_heads3(x):
    lo = _lane_lo(x.shape)
    return jnp.concatenate([jnp.where(lo, x, 0.0), jnp.where(lo, 0.0, x)], axis=1)


def _unstack_heads(o2):
    n = o2.shape[0] // 2
    return jnp.where(_lane_lo((n, LANES)), o2[:n], o2[n:])


def _dup_head(x, j):
    keep = _lane_lo(x.shape) == (j == 0)
    return jnp.where(keep, x, pltpu.roll(x, HEAD_DIM, 1))


def _ada_kernel(c_ref, w_ref, b_ref, o_ref):
    cv = c_ref[...]
    s = cv * jax.nn.sigmoid(cv)
    o_ref[0] = _dot3(s, w_ref[0]) + b_ref[0]


def _ada_mod(cvecs, w_ada, b_ada):
    depth, _, n_out = w_ada.shape
    rows = cvecs.shape[0]
    tn = 1024
    return pl.pallas_call(
        _ada_kernel,
        grid=(depth, n_out // tn),
        in_specs=[
            pl.BlockSpec((rows, D_MODEL), lambda l, j: (0, 0)),
            pl.BlockSpec((1, D_MODEL, tn), lambda l, j: (l, 0, j)),
            pl.BlockSpec((1, 1, tn), lambda l, j: (l, 0, j)),
        ],
        out_specs=pl.BlockSpec((1, rows, tn), lambda l, j: (l, 0, j)),
        out_shape=jax.ShapeDtypeStruct((depth, rows, n_out), F32),
        compiler_params=_cparams(("parallel", "parallel")),
        name="ada_mod",
    )(cvecs, w_ada, b_ada.reshape(depth, 1, n_out))


NA_QK_BLOCKS = 2 * NA_WIDTH // LANES
SWA_Q_BLOCK0 = NA_COLS // LANES
SWA_QK_BLOCKS = (SWA_WIDTH + SWA_KV_WIDTH) // LANES


def _in_proj_kernel(x_ref, g_ref, mod_ref, w_ref, qkg_ref, cos_ref, sin_ref, att_ref, u_ref):
    x = x_ref[...]
    y = x * lax.rsqrt(jnp.mean(x * x, axis=-1, keepdims=True) + RMS_EPS)
    h = (y * g_ref[...]) * (1.0 + mod_ref[0, 1:2, :]) + mod_ref[0, 0:1, :]
    proj = _dot(h.astype(BF16), w_ref[...])
    u_ref[...] = proj[:, ATT_COLS:]

    def qk_norm(blk, gain):
        ms = _pair_sum(blk * blk) * (1.0 / HEAD_DIM)
        return blk * lax.rsqrt(ms + RMS_EPS) * gain

    lane = lax.broadcasted_iota(jnp.int32, (x.shape[0], LANES), 1)
    first = (lane % (HEAD_DIM // 2)) < (HEAD_DIM // 4)
    for cb in range(ATT_COLS // LANES):
        blk = proj[:, cb * LANES:(cb + 1) * LANES]
        if cb < NA_QK_BLOCKS:
            gi = 0 if cb < NA_QK_BLOCKS // 2 else 1
            blk = qk_norm(blk, qkg_ref[gi:gi + 1, :])
        elif SWA_Q_BLOCK0 <= cb < SWA_Q_BLOCK0 + SWA_QK_BLOCKS:
            gi = 2 if cb < SWA_Q_BLOCK0 + SWA_WIDTH // LANES else 3
            blk = qk_norm(blk, qkg_ref[gi:gi + 1, :])
            partner = jnp.where(first, pltpu.roll(blk, LANES - HEAD_DIM // 4, 1),
                                pltpu.roll(blk, HEAD_DIM // 4, 1))
            blk = blk * cos_ref[...] + partner * sin_ref[...]
        att_ref[:, cb * LANES:(cb + 1) * LANES] = blk


def _in_proj(x, norm_g, mods, w_in_bf16, qk_gains, cos_tab, sin_tab, tile_mod, tile_rope):
    n_tok = x.shape[0]
    return pl.pallas_call(
        _in_proj_kernel,
        grid=(n_tok // TOK_TILE,),
        in_specs=[
            pl.BlockSpec((TOK_TILE, D_MODEL), lambda i: (i, 0)),
            pl.BlockSpec((1, D_MODEL), lambda i: (0, 0)),
            pl.BlockSpec((1, 6, D_MODEL), lambda i: (tile_mod(i), 0, 0)),
            pl.BlockSpec((D_MODEL, IN_COLS), lambda i: (0, 0)),
            pl.BlockSpec((4, LANES), lambda i: (0, 0)),
            pl.BlockSpec((TOK_TILE, LANES), lambda i: (tile_rope(i), 0)),
            pl.BlockSpec((TOK_TILE, LANES), lambda i: (tile_rope(i), 0)),
        ],
        out_specs=[
            pl.BlockSpec((TOK_TILE, ATT_COLS), lambda i: (i, 0)),
            pl.BlockSpec((TOK_TILE, RK_COLS), lambda i: (i, 0)),
        ],
        out_shape=[
            jax.ShapeDtypeStruct((n_tok, ATT_COLS), F32),
            jax.ShapeDtypeStruct((n_tok, RK_COLS), F32),
        ],
        compiler_params=_cparams(("parallel",)),
        name="in_proj",
    )(x, norm_g, mods, w_in_bf16, qk_gains, cos_tab, sin_tab)


def _rope_tables(n_lat):
    nf = HEAD_DIM // 4
    t = jnp.arange(n_lat)
    lane = jnp.arange(LANES)
    d = lane % HEAD_DIM
    inv = ROPE_THETA ** (-(d % nf).astype(F32) / nf)
    pos = jnp.where((d // (2 * nf))[None, :] == 0, (t // GRID_W)[:, None], (t % GRID_W)[:, None]).astype(F32)
    ang = pos * inv[None, :]
    sign = jnp.where((d % (2 * nf)) < nf, -1.0, 1.0).astype(F32)
    cos = jnp.concatenate([jnp.cos(ang), jnp.ones((TOK_TILE, LANES), F32)], 0)
    sin = jnp.concatenate([jnp.sin(ang) * sign[None, :], jnp.zeros((TOK_TILE, LANES), F32)], 0)
    return cos, sin


def _ctx_attn_kernel(sink_ref, q_ref, k_ref, v_ref, o_ref, *, gqa):
    j = pl.program_id(1)
    k = k_ref[0]
    v = v_ref[0]
    if gqa:
        k = _dup_head(k, j)
        v = _dup_head(v, j)
    n = k.shape[0]
    q2 = _stack_heads(q_ref[0]).astype(BF16)
    s = _dot_nt(q2, k.astype(BF16)) * ATTN_SCALE
    m = jnp.max(s, axis=-1, keepdims=True)
    if gqa:
        row = lax.broadcasted_iota(jnp.int32, (2 * n, 1), 0)
        snk = jnp.where(row < n, sink_ref[2 * j], sink_ref[2 * j + 1])
        m = jnp.maximum(m, snk)
    p = jnp.exp(s - m)
    den = jnp.sum(p, axis=-1, keepdims=True)
    if gqa:
        den = den + jnp.exp(snk - m)
    o2 = _dot(p.astype(BF16), v.astype(BF16)) / den
    o_ref[0] = _unstack_heads(o2)


def _ctx_attn(att, b, sink, *, gqa):
    t = att.shape[1]
    if gqa:
        nq = SWA_WIDTH // LANES
        qb, kb, vb = SWA_Q_BLOCK0, SWA_Q_BLOCK0 + nq, SWA_Q_BLOCK0 + nq + 1
        kmap = lambda bi, j: (bi, 0, kb)
        vmap = lambda bi, j: (bi, 0, vb)
    else:
        nq = NA_WIDTH // LANES
        qb, kb, vb = 0, nq, 2 * nq
        kmap = lambda bi, j: (bi, 0, kb + j)
        vmap = lambda bi, j: (bi, 0, vb + j)
    return pl.pallas_call(
        functools.partial(_ctx_attn_kernel, gqa=gqa),
        grid=(b, nq),
        in_specs=[
            pl.BlockSpec(memory_space=pltpu.SMEM),
            pl.BlockSpec((1, t, LANES), lambda bi, j: (bi, 0, qb + j)),
            pl.BlockSpec((1, t, LANES), kmap),
            pl.BlockSpec((1, t, LANES), vmap),
        ],
        out_specs=pl.BlockSpec((1, t, LANES), lambda bi, j: (bi, 0, j)),
        out_shape=jax.ShapeDtypeStruct((b, t, nq * LANES), F32),
        compiler_params=_cparams(("parallel", "parallel")),
        name="ctx_attn_swa" if gqa else "ctx_attn_na",
    )(sink, att, att, att)


NA_ROWS_PER_ITER = 4


def _na_lat_kernel(q_ref, k_ref, v_ref, kc_ref, vc_ref, tab_ref, o_ref, kb_ref, vb_ref):
    n = q_ref.shape[1]
    rows = n // GRID_W
    win = NA_WIN_R * GRID_W
    kb_ref[...] = k_ref[0].astype(BF16)
    vb_ref[...] = v_ref[0].astype(BF16)
    kc = kc_ref[0].astype(BF16)
    vc = vc_ref[0].astype(BF16)

    def row_group(ig, carry):
        nr = NA_ROWS_PER_ITER
        g0 = pl.multiple_of(ig * (nr * GRID_W), nr * GRID_W)
        q2 = _stack_heads3(q_ref[0, pl.ds(g0, nr * GRID_W), :].reshape(nr, GRID_W, LANES)).astype(BF16)
        kws, vws, biases = [], [], []
        for r in range(nr):
            i = ig * nr + r
            start = jnp.clip(i - NA_WIN_R // 2, 0, rows - NA_WIN_R)
            k0 = pl.multiple_of(start * GRID_W, GRID_W)
            kws.append(kb_ref[pl.ds(k0, win), :])
            vws.append(vb_ref[pl.ds(k0, win), :])
            biases.append(tab_ref[0, start - i + (NA_WIN_R - 1)])
        s_loc = _bmm_nt(q2, jnp.stack(kws)) * ATTN_SCALE + jnp.stack(biases)
        s_ctx = _dot_nt(q2.reshape(nr * 2 * GRID_W, LANES), kc).reshape(nr, 2 * GRID_W, -1) * ATTN_SCALE
        m = jnp.maximum(jnp.max(s_loc, axis=-1, keepdims=True), jnp.max(s_ctx, axis=-1, keepdims=True))
        p_loc = jnp.exp(s_loc - m)
        p_ctx = jnp.exp(s_ctx - m)
        den = jnp.sum(p_loc, axis=-1, keepdims=True) + jnp.sum(p_ctx, axis=-1, keepdims=True)
        o_ctx = _dot(p_ctx.reshape(nr * 2 * GRID_W, -1).astype(BF16), vc).reshape(nr, 2 * GRID_W, LANES)
        o2 = (_bmm(p_loc, jnp.stack(vws)) + o_ctx) / den
        out = jnp.where(_lane_lo((nr, GRID_W, LANES)), o2[:, :GRID_W], o2[:, GRID_W:])
        o_ref[0, pl.ds(g0, nr * GRID_W), :] = out.reshape(nr * GRID_W, LANES)
        return carry

    lax.fori_loop(0, rows // NA_ROWS_PER_ITER, row_group, 0)


def _na_bias_tables(rpb):
    col = jnp.arange(GRID_W)
    cstart = jnp.clip(col - NA_WIN_C // 2, 0, GRID_W - NA_WIN_C)
    col_mask = (col[None, :] >= cstart[:, None]) & (col[None, :] < cstart[:, None] + NA_WIN_C)
    col_idx = jnp.clip(col[None, :] - col[:, None] + NA_WIN_C - 1, 0, 2 * NA_WIN_C - 2)
    rpb_cols = jnp.where(col_mask[None, None], rpb[:, :, col_idx], NEG_BIG)
    roff = jnp.arange(NA_WIN_R)[:, None] + jnp.arange(NA_WIN_R)[None, :]
    t = rpb_cols[:, roff]
    t = jnp.transpose(t, (0, 1, 3, 2, 4)).reshape(NA_HEADS // 2, 2, NA_WIN_R, GRID_W, NA_WIN_R * GRID_W)
    return jnp.transpose(t, (0, 2, 1, 3, 4)).reshape(NA_HEADS // 2, NA_WIN_R, 2 * GRID_W, NA_WIN_R * GRID_W)


def _na_latent(att, s0, kc, vc, tab):
    n = att.shape[1]
    b, p, _ = kc.shape
    nq = NA_WIDTH // LANES
    return pl.pallas_call(
        _na_lat_kernel,
        grid=(b, nq),
        in_specs=[
            pl.BlockSpec((1, n, LANES), lambda bi, j: (s0 + bi, 0, j)),
            pl.BlockSpec((1, n, LANES), lambda bi, j: (s0 + bi, 0, nq + j)),
            pl.BlockSpec((1, n, LANES), lambda bi, j: (s0 + bi, 0, 2 * nq + j)),
            pl.BlockSpec((1, p, LANES), lambda bi, j: (bi, 0, j)),
            pl.BlockSpec((1, p, LANES), lambda bi, j: (bi, 0, j)),
            pl.BlockSpec((1, NA_WIN_R, 2 * GRID_W, NA_WIN_R * GRID_W), lambda bi, j: (j, 0, 0, 0)),
        ],
        out_specs=pl.BlockSpec((1, n, LANES), lambda bi, j: (bi, 0, j)),
        out_shape=jax.ShapeDtypeStruct((b, n, NA_WIDTH), F32),
        scratch_shapes=[pltpu.VMEM((n, LANES), BF16), pltpu.VMEM((n, LANES), BF16)],
        compiler_params=_cparams(("parallel", "parallel")),
        name="na_latent",
    )(att, att, att, kc, vc, tab)


SWA_BLOCKS_PER_ITER = 2


def _swa_lat_kernel(sink_ref, q_ref, k_ref, v_ref, kc_ref, vc_ref, o_ref, kb_ref, vb_ref):
    j = pl.program_id(1)
    n = q_ref.shape[1]
    blk = SWA_WIN
    span = 3 * blk
    kb_ref[...] = _dup_head(k_ref[0], j).astype(BF16)
    vb_ref[...] = _dup_head(v_ref[0], j).astype(BF16)
    kc = _dup_head(kc_ref[0], j).astype(BF16)
    vc = _dup_head(vc_ref[0], j).astype(BF16)
    row = lax.broadcasted_iota(jnp.int32, (2 * blk, 1), 0)
    snk = jnp.where(row < blk, sink_ref[2 * j], sink_ref[2 * j + 1])
    qoff = lax.broadcasted_iota(jnp.int32, (2 * blk, span), 0) % blk
    koff = lax.broadcasted_iota(jnp.int32, (2 * blk, span), 1)

    def q_group(qg, carry):
        nr = SWA_BLOCKS_PER_ITER
        g0 = pl.multiple_of(qg * (nr * blk), nr * blk)
        q2 = _stack_heads3(q_ref[0, pl.ds(g0, nr * blk), :].reshape(nr, blk, LANES)).astype(BF16)
        kws, vws, valids = [], [], []
        for r in range(nr):
            q0 = g0 + r * blk
            w0 = pl.multiple_of(jnp.clip(q0 - blk, 0, n - span), blk)
            kws.append(kb_ref[pl.ds(w0, span), :])
            vws.append(vb_ref[pl.ds(w0, span), :])
            valids.append(jnp.abs((q0 + qoff) - (w0 + koff)) <= SWA_WIN)
        s_loc = jnp.where(jnp.stack(valids), _bmm_nt(q2, jnp.stack(kws)) * ATTN_SCALE, NEG_BIG)
        s_ctx = _dot_nt(q2.reshape(nr * 2 * blk, LANES), kc).reshape(nr, 2 * blk, -1) * ATTN_SCALE
        m = jnp.maximum(jnp.max(s_loc, axis=-1, keepdims=True), jnp.max(s_ctx, axis=-1, keepdims=True))
        m = jnp.maximum(m, snk)
        p_loc = jnp.exp(s_loc - m)
        p_ctx = jnp.exp(s_ctx - m)
        den = (jnp.sum(p_loc, axis=-1, keepdims=True) + jnp.sum(p_ctx, axis=-1, keepdims=True)
               + jnp.exp(snk - m))
        o_ctx = _dot(p_ctx.reshape(nr * 2 * blk, -1).astype(BF16), vc).reshape(nr, 2 * blk, LANES)
        o2 = (_bmm(p_loc, jnp.stack(vws)) + o_ctx) / den
        out = jnp.where(_lane_lo((nr, blk, LANES)), o2[:, :blk], o2[:, blk:])
        o_ref[0, pl.ds(g0, nr * blk), :] = out.reshape(nr * blk, LANES)
        return carry

    lax.fori_loop(0, n // (SWA_BLOCKS_PER_ITER * blk), q_group, 0)


def _swa_latent(att, s0, kc, vc, sink):
    n = att.shape[1]
    b, p, _ = kc.shape
    nq = SWA_WIDTH // LANES
    qb, kb, vb = SWA_Q_BLOCK0, SWA_Q_BLOCK0 + nq, SWA_Q_BLOCK0 + nq + 1
    return pl.pallas_call(
        _swa_lat_kernel,
        grid=(b, nq),
        in_specs=[
            pl.BlockSpec(memory_space=pltpu.SMEM),
            pl.BlockSpec((1, n, LANES), lambda bi, j: (s0 + bi, 0, qb + j)),
            pl.BlockSpec((1, n, LANES), lambda bi, j: (s0 + bi, 0, kb)),
            pl.BlockSpec((1, n, LANES), lambda bi, j: (s0 + bi, 0, vb)),
            pl.BlockSpec((1, p, LANES), lambda bi, j: (bi, 0, 0)),
            pl.BlockSpec((1, p, LANES), lambda bi, j: (bi, 0, 0)),
        ],
        out_specs=pl.BlockSpec((1, n, LANES), lambda bi, j: (bi, 0, j)),
        out_shape=jax.ShapeDtypeStruct((b, n, SWA_WIDTH), F32),
        scratch_shapes=[pltpu.VMEM((n, LANES), BF16), pltpu.VMEM((n, LANES), BF16)],
        compiler_params=_cparams(("parallel", "parallel")),
        name="swa_latent",
    )(sink, att, att, att, kc, vc)


RK_NB = RK_WIDTH // LANES
LORA_BLOCK = 3 * RK_WIDTH // LANES
GATE_BLOCK = LORA_BLOCK + 1
Q_R, Q_V, Q_A, Q_W, Q_K, Q_B = range(6)
Q_DIR = 3
Q_COLS = (6 + Q_DIR) * RK_WIDTH


def _softplus(x):
    return jnp.maximum(x, 0.0) + jnp.log(1.0 + jnp.exp(-jnp.abs(x)))


def _rk_prep_kernel(u_ref, up_ref, un_ref, cw_ref, w0_ref, w2_ref, a0_ref, a2_ref, g2_ref, kk_ref, ka_ref,
                    rk_ref, q_ref, g_ref, bonus_ref, *, n_ctx_tiles, tiles_per_seq):
    def put(slot, val):
        q_ref[:, slot * RK_WIDTH:(slot + 1) * RK_WIDTH] = val

    i = pl.program_id(0)
    li = i - n_ctx_tiles
    is_lat = i >= n_ctx_tiles
    has_prev = jnp.logical_and(is_lat, li % tiles_per_seq != 0)
    has_next = jnp.logical_and(is_lat, li % tiles_per_seq != tiles_per_seq - 1)
    u = u_ref[...]
    tm = u.shape[0]
    prev_row = jnp.where(has_prev, up_ref[7:8, :], 0.0)
    next_row = jnp.where(has_next, un_ref[0:1, :], 0.0)
    row = lax.broadcasted_iota(jnp.int32, u.shape, 0)
    um = jnp.where(row == 0, prev_row, pltpu.roll(u, 1, 0))
    up = jnp.where(row == tm - 1, next_row, pltpu.roll(u, tm - 1, 0))
    u = um * cw_ref[0:1, :] + u * cw_ref[1:2, :] + up * cw_ref[2:3, :]

    r = u[:, 0:RK_WIDTH]
    k = u[:, RK_WIDTH:2 * RK_WIDTH]
    v = u[:, 2 * RK_WIDTH:3 * RK_WIDTH]
    lora = u[:, LORA_BLOCK * LANES:(LORA_BLOCK + 1) * LANES]
    gl = u[:, GATE_BLOCK * LANES:(GATE_BLOCK + 1) * LANES]
    put(Q_R, r)
    put(Q_V, v)
    g_ref[...] = _dot3(jax.nn.sigmoid(gl), g2_ref[...])

    kn = k * kk_ref[...]
    kk = jnp.concatenate(
        [kn[:, c * LANES:(c + 1) * LANES]
         * lax.rsqrt(jnp.maximum(_pair_sum(jnp.square(kn[:, c * LANES:(c + 1) * LANES])), 1e-24))
         for c in range(RK_NB)], axis=1)
    put(Q_A, -kk)

    lora_t = jnp.tanh(lora)
    kd_sum = None
    for d in range(2):
        w = -_softplus(-(w0_ref[d:d + 1, :] + _dot3(lora_t, w2_ref[d]))) - 0.5
        put(Q_W + Q_DIR * d, -jnp.exp(w))
        a = jax.nn.sigmoid(a0_ref[d:d + 1, :] + _dot3(lora, a2_ref[d]))
        kd = k * (1.0 + (a - 1.0) * ka_ref[...])
        put(Q_K + Q_DIR * d, kd)
        put(Q_B + Q_DIR * d, kk * a)
        kd_sum = kd if kd_sum is None else kd_sum + kd

    t = r * kd_sum * rk_ref[...]
    bonus_ref[...] = jnp.concatenate(
        [_pair_sum(t[:, c * LANES:(c + 1) * LANES]) for c in range(RK_NB)], axis=1) * v


def _rk_prep(u, p, n_ctx_tiles, tiles_per_seq):
    n_tok = u.shape[0]
    n_tiles = n_tok // TOK_TILE
    sub = TOK_TILE // 8
    last8 = n_tok // 8 - 1
    tok = lambda i: (i, 0)
    const2 = lambda i: (0, 0)
    const3 = lambda i: (0, 0, 0)
    one = jax.ShapeDtypeStruct((n_tok, RK_WIDTH), F32)
    tok_spec = pl.BlockSpec((TOK_TILE, RK_WIDTH), tok)
    return pl.pallas_call(
        functools.partial(_rk_prep_kernel, n_ctx_tiles=n_ctx_tiles, tiles_per_seq=tiles_per_seq),
        grid=(n_tiles,),
        in_specs=[
            pl.BlockSpec((TOK_TILE, RK_COLS), tok),
            pl.BlockSpec((8, RK_COLS), lambda i: (jnp.maximum(i * sub - 1, 0), 0)),
            pl.BlockSpec((8, RK_COLS), lambda i: (jnp.minimum((i + 1) * sub, last8), 0)),
            pl.BlockSpec((3, RK_COLS), const2),
            pl.BlockSpec((2, RK_WIDTH), const2),
            pl.BlockSpec((2, LANES, RK_WIDTH), const3),
            pl.BlockSpec((2, RK_WIDTH), const2),
            pl.BlockSpec((2, LANES, RK_WIDTH), const3),
            pl.BlockSpec((RK_GATE_LORA, RK_WIDTH), const2),
            pl.BlockSpec((1, RK_WIDTH), const2),
            pl.BlockSpec((1, RK_WIDTH), const2),
            pl.BlockSpec((1, RK_WIDTH), const2),
        ],
        out_specs=[pl.BlockSpec((TOK_TILE, Q_COLS), tok), tok_spec, tok_spec],
        out_shape=[jax.ShapeDtypeStruct((n_tok, Q_COLS), F32), one, one],
        compiler_params=_cparams(("parallel",)),
        name="rk_prep",
    )(u, u, u, p["rk_conv"], p["rk_w0"], p["rk_w2_pad"], p["rk_a0"], p["rk_a2_pad"], p["rk_g2"],
      p["rk_k_k"], p["rk_k_a"], p["rk_r_k"])


RK_CHUNK = 64
PAIR = 2 * HEAD_DIM
STATE_SEQS = 8
RK_STEP_CHUNKS = 4


def _split3_bf16(x):
    hi = x.astype(BF16)
    r1 = x - hi.astype(F32)
    mid = r1.astype(BF16)
    return hi, mid, (r1 - mid.astype(F32)).astype(BF16)


def _rk_chunk_kernel(q_ref, rbar_ref, ybar_ref, phi_ref, psi_ref):
    c = RK_CHUNK
    n = 2 * c
    nd = 2 * RK_NB
    nu = RK_STEP_CHUNKS * nd

    def tiles(slot, per_dir):
        cols = [(slot + (Q_DIR * d if per_dir else 0)) * RK_WIDTH + p * LANES
                for d in range(2) for p in range(RK_NB)]
        return jnp.stack([q_ref[ck * c:(ck + 1) * c, lo:lo + LANES] for ck in range(RK_STEP_CHUNKS) for lo in cols])

    r, v, a = tiles(Q_R, False), tiles(Q_V, False), tiles(Q_A, False)
    lw, k, b = tiles(Q_W, True), tiles(Q_K, True), tiles(Q_B, True)
    unit = lax.broadcasted_iota(jnp.int32, (nu, 1, 1), 0)
    sgn = jnp.ones((nu, 1, 1), jnp.int32)
    for ck in range(RK_STEP_CHUNKS):
        sgn = jnp.where(jnp.logical_and(unit >= ck * nd + RK_NB, unit < (ck + 1) * nd), -1, sgn)
    bwd = sgn < 0
    tdiff = lax.broadcasted_iota(jnp.int32, (1, c, c), 2) - lax.broadcasted_iota(jnp.int32, (1, c, c), 1)
    tri = jnp.where(tdiff * sgn <= 0, 1.0, 0.0)
    cum = sum(_bmm(tri, part) for part in _split3_bf16(lw))
    tot = jnp.where(bwd, cum[:, 0:1], cum[:, c - 1:c])
    a_t = a * jnp.exp(cum - lw)
    r_t = r * jnp.exp(cum)
    e_neg = jnp.exp(-cum)
    e_end = jnp.exp(tot - cum)
    g = _bmm_nt(jnp.concatenate([_stack_heads3(a_t), _stack_heads3(r_t)], axis=1),
                jnp.concatenate([_stack_heads3(b * e_neg), _stack_heads3(k * e_neg)], axis=1))
    r2 = lax.broadcasted_iota(jnp.int32, (1, n, n), 1)
    c2 = lax.broadcasted_iota(jnp.int32, (1, n, n), 2)
    order = (jnp.bitwise_and(c2, c - 1) - jnp.bitwise_and(r2, c - 1)) * sgn
    eye = jnp.where(r2 == c2, 1.0, 0.0)
    l_ab = jnp.where(order < 0, g[:, :n, :n], 0.0)
    l_ak = jnp.where(order < 0, g[:, :n, n:], 0.0)
    m_rb = jnp.where(order <= 0, g[:, n:, :n], 0.0)
    m_rk = jnp.where(order <= 0, g[:, n:, n:], 0.0)
    t_inv = eye + l_ab
    pw = l_ab
    for _ in range(5):
        pw = _bmm(pw, pw)
        t_inv = t_inv + _bmm(t_inv, pw)
    sv = _stack_heads3(v)
    au = _bmm(t_inv, jnp.concatenate([_stack_heads3(a_t), _bmm(l_ak, sv)], axis=2))
    ry = _bmm(m_rb, au) + jnp.concatenate([_stack_heads3(r_t), _bmm(m_rk, sv)], axis=2)
    ry = ry[:, :c] + ry[:, c:]
    bt = jnp.swapaxes(_stack_heads3(b * e_end), 1, 2)
    kt = jnp.swapaxes(_stack_heads3(k * e_end), 1, 2)
    pp = _bmm(bt, au)
    phi = eye * jnp.exp(tot) + pp[:, :, :PAIR]
    psi = pp[:, :, PAIR:] + _bmm(kt, sv)
    for ck in range(RK_STEP_CHUNKS):
        for d in range(2):
            for p in range(RK_NB):
                u = ck * nd + d * RK_NB + p
                rbar_ref[d, ck * c:(ck + 1) * c, p * LANES:(p + 1) * LANES] = ry[u, :, :PAIR]
                ybar_ref[d, ck * c:(ck + 1) * c, p * LANES:(p + 1) * LANES] = ry[u, :, PAIR:]
                phi_ref[d, ck, p] = phi[u]
                psi_ref[d, ck, p] = psi[u]


def _rk_chunk(q, tile0, n_seq, t):
    nc = t // RK_CHUNK
    sc = RK_STEP_CHUNKS
    assert nc % sc == 0 and tile0 % sc == 0
    row_sh = jax.ShapeDtypeStruct((2, n_seq, t, RK_WIDTH), F32)
    mat_sh = jax.ShapeDtypeStruct((2, n_seq, nc, RK_NB, PAIR, PAIR), F32)
    row_spec = pl.BlockSpec((2, None, sc * RK_CHUNK, RK_WIDTH), lambda s, c: (0, s, c, 0))
    mat_spec = pl.BlockSpec((2, None, sc, RK_NB, PAIR, PAIR), lambda s, c: (0, s, c, 0, 0, 0))
    return pl.pallas_call(
        _rk_chunk_kernel,
        grid=(n_seq, nc // sc),
        in_specs=[pl.BlockSpec((sc * RK_CHUNK, Q_COLS), lambda s, c: ((tile0 + s * nc) // sc + c, 0))],
        out_specs=[row_spec, row_spec, mat_spec, mat_spec],
        out_shape=[row_sh, row_sh, mat_sh, mat_sh],
        compiler_params=_cparams(("parallel", "parallel")),
        name="rk_chunk",
    )(q)


def _rk_state_kernel(rf_ref, rb_ref, yf_ref, yb_ref, phf_ref, phb_ref, psf_ref, psb_ref, s0_ref,
                     of_ref, ob_ref, s_ref):
    @pl.when(pl.program_id(1) == 0)
    def _():
        s_ref[...] = s0_ref[...]

    ns = s_ref.shape[0]
    nd = 2 * RK_NB

    def pair_tiles(ref_f, ref_b):
        tiles = [ref[:, :, p * LANES:(p + 1) * LANES] for ref in (ref_f, ref_b) for p in range(RK_NB)]
        return jnp.stack(tiles, axis=1).reshape(ns * nd, RK_CHUNK, LANES)

    def mats(ref_f, ref_b):
        return jnp.concatenate([ref_f[...], ref_b[...]], axis=1).reshape(ns * nd, PAIR, PAIR)

    h = s_ref[...].reshape(ns * nd, PAIR, PAIR)
    y = (_bmm3(pair_tiles(rf_ref, rb_ref), h) + pair_tiles(yf_ref, yb_ref)).reshape(ns, nd, RK_CHUNK, LANES)
    s_ref[...] = (_bmm3(mats(phf_ref, phb_ref), h) + mats(psf_ref, psb_ref)).reshape(ns, 2, RK_NB, PAIR, PAIR)
    of_ref[...] = jnp.concatenate([y[:, p] for p in range(RK_NB)], axis=2)
    ob_ref[...] = jnp.concatenate([y[:, RK_NB + p] for p in range(RK_NB)], axis=2)


def _rk_state(rbar, ybar, phi, psi, s0):
    _, n_seq, t, _ = rbar.shape
    nc = t // RK_CHUNK
    sg = STATE_SEQS
    row_blk = (None, sg, RK_CHUNK, RK_WIDTH)
    mat_blk = (None, sg, None, RK_NB, PAIR, PAIR)
    fwd_row = pl.BlockSpec(row_blk, lambda g, c: (0, g, c, 0))
    bwd_row = pl.BlockSpec(row_blk, lambda g, c: (1, g, nc - 1 - c, 0))
    fwd_mat = pl.BlockSpec(mat_blk, lambda g, c: (0, g, c, 0, 0, 0))
    bwd_mat = pl.BlockSpec(mat_blk, lambda g, c: (1, g, nc - 1 - c, 0, 0, 0))
    st = pl.BlockSpec((sg, 2, RK_NB, PAIR, PAIR), lambda g, c: (g, 0, 0, 0, 0))
    out_sh = jax.ShapeDtypeStruct((n_seq, t, RK_WIDTH), F32)
    return pl.pallas_call(
        _rk_state_kernel,
        grid=(n_seq // sg, nc),
        in_specs=[fwd_row, bwd_row, fwd_row, bwd_row, fwd_mat, bwd_mat, fwd_mat, bwd_mat, st],
        out_specs=[pl.BlockSpec((sg, RK_CHUNK, RK_WIDTH), lambda g, c: (g, c, 0)),
                   pl.BlockSpec((sg, RK_CHUNK, RK_WIDTH), lambda g, c: (g, nc - 1 - c, 0)), st],
        out_shape=[out_sh, out_sh, jax.ShapeDtypeStruct((n_seq, 2, RK_NB, PAIR, PAIR), F32)],
        compiler_params=_cparams(("parallel", "arbitrary")),
        name="rk_state",
    )(rbar, rbar, ybar, ybar, phi, phi, psi, psi, s0)


def _pair_states(s):
    bsz = s.shape[0]
    h = jnp.swapaxes(s, -1, -2).reshape(bsz, 2, RK_NB, 2, HEAD_DIM, HEAD_DIM)
    return jnp.einsum("bdphkv,hg->bdphkgv", h, jnp.eye(2, dtype=F32)).reshape(bsz, 2, RK_NB, PAIR, PAIR)


def _head_states(s):
    bsz = s.shape[0]
    h = jnp.stack([s[..., :HEAD_DIM, :HEAD_DIM], s[..., HEAD_DIM:, HEAD_DIM:]], axis=3)
    return jnp.swapaxes(h.reshape(bsz, 2, RK_HEADS, HEAD_DIM, HEAD_DIM), -1, -2)


def _rwkv_group(q, tile0, n_seq, t, s0):
    rbar, ybar, phi, psi = _rk_chunk(q, tile0, n_seq, t)
    y_f, y_b, s_fin = _rk_state(rbar, ybar, phi, psi, s0)
    return y_f.reshape(n_seq * t, RK_WIDTH), y_b.reshape(n_seq * t, RK_WIDTH), s_fin


def _out_proj_kernel(x_ref, ona_c, ona_l, osw_c, osw_l, yf_c, yf_l, yb_c, yb_l, bonus_ref, g_ref, lng_ref, lnb_ref,
                     w_ref, mod_ref, n2_ref, rw_ref, rb_ref, x1_ref, h2_ref, gate_ref, top_ref, *, n_ctx_tiles):
    is_ctx = pl.program_id(0) < n_ctx_tiles
    pick = lambda c_ref, l_ref: jnp.where(is_ctx, c_ref[...], l_ref[...])
    ona = pick(ona_c, ona_l)
    osw = pick(osw_c, osw_l)
    y = pick(yf_c, yf_l) + pick(yb_c, yb_l)
    parts = []
    for c in range(RK_NB):
        yc = y[:, c * LANES:(c + 1) * LANES]
        dc = yc - _pair_sum(yc) * (1.0 / HEAD_DIM)
        var = _pair_sum(dc * dc) * (1.0 / HEAD_DIM)
        parts.append(dc * lax.rsqrt(var + GN_EPS))
    yn = jnp.concatenate(parts, axis=1) * lng_ref[...] + lnb_ref[...]
    o_rk = (yn + bonus_ref[...]) * g_ref[...]
    o = (_dot(ona.astype(BF16), w_ref[0:NA_WIDTH, :])
         + _dot(osw.astype(BF16), w_ref[NA_WIDTH:NA_WIDTH + SWA_WIDTH, :])
         + _dot(o_rk.astype(BF16), w_ref[NA_WIDTH + SWA_WIDTH:, :]))
    x1 = x_ref[...] + mod_ref[0, 2:3, :] * o
    x1_ref[...] = x1
    yn2 = x1 * lax.rsqrt(jnp.mean(x1 * x1, axis=-1, keepdims=True) + RMS_EPS)
    h2 = (yn2 * n2_ref[...]) * (1.0 + mod_ref[0, 4:5, :]) + mod_ref[0, 3:4, :]
    h2_ref[...] = h2.astype(BF16)
    lane = lax.broadcasted_iota(jnp.int32, (h2.shape[0], LANES), 1).astype(F32)
    logit = jnp.where(lane < N_EXPERTS, _dot3(h2, rw_ref[...]) + rb_ref[...], -jnp.inf)
    vals, idxs = [], []
    for _ in range(TOP_K):
        best = jnp.max(logit, axis=-1, keepdims=True)
        idx = jnp.min(jnp.where(logit == best, lane, float(LANES)), axis=-1, keepdims=True)
        vals.append(best)
        idxs.append(idx)
        logit = jnp.where(lane == idx, -jnp.inf, logit)
    e = jnp.exp(jnp.concatenate(vals, axis=1) - vals[0])
    gate_ref[...] = e / jnp.sum(e, axis=-1, keepdims=True)
    top_ref[...] = jnp.concatenate(idxs, axis=1).astype(jnp.int32)


def _out_proj(x, o_na, o_sw, y_f, y_b, bonus, g, p, mods, tile_mod, n_ctx_tiles):
    n_tok = x.shape[0]
    tok = lambda i: (i, 0)
    const = lambda i: (0, 0)
    ctx_tile = lambda i: (jnp.minimum(i, n_ctx_tiles - 1), 0)
    lat_tile = lambda i: (jnp.maximum(i - n_ctx_tiles, 0), 0)
    pair = lambda w: [pl.BlockSpec((TOK_TILE, w), ctx_tile), pl.BlockSpec((TOK_TILE, w), lat_tile)]
    return pl.pallas_call(
        functools.partial(_out_proj_kernel, n_ctx_tiles=n_ctx_tiles),
        grid=(n_tok // TOK_TILE,),
        in_specs=[
            pl.BlockSpec((TOK_TILE, D_MODEL), tok),
            *pair(NA_WIDTH), *pair(SWA_WIDTH), *pair(RK_WIDTH), *pair(RK_WIDTH),
            pl.BlockSpec((TOK_TILE, RK_WIDTH), tok),
            pl.BlockSpec((TOK_TILE, RK_WIDTH), tok),
            pl.BlockSpec((1, RK_WIDTH), const),
            pl.BlockSpec((1, RK_WIDTH), const),
            pl.BlockSpec((D_MODEL, D_MODEL), const),
            pl.BlockSpec((1, 6, D_MODEL), lambda i: (tile_mod(i), 0, 0)),
            pl.BlockSpec((1, D_MODEL), const),
            pl.BlockSpec((D_MODEL, LANES), const),
            pl.BlockSpec((1, LANES), const),
        ],
        out_specs=[
            pl.BlockSpec((TOK_TILE, D_MODEL), tok),
            pl.BlockSpec((TOK_TILE, D_MODEL), tok),
            pl.BlockSpec((TOK_TILE, TOP_K), tok),
            pl.BlockSpec((TOK_TILE, TOP_K), tok),
        ],
        out_shape=[
            jax.ShapeDtypeStruct((n_tok, D_MODEL), F32),
            jax.ShapeDtypeStruct((n_tok, D_MODEL), BF16),
            jax.ShapeDtypeStruct((n_tok, TOP_K), F32),
            jax.ShapeDtypeStruct((n_tok, TOP_K), jnp.int32),
        ],
        compiler_params=_cparams(("parallel",)),
        name="out_proj",
    )(x, *o_na, *o_sw, *y_f, *y_b, bonus, g, p["rk_ln_g"], p["rk_ln_b"], p["w_out_bf16"], mods, p["norm2_g"],
      p["router_w_pad"], p["router_b_pad"])


H2_PAD_ROWS = 32768
W1_SEL_COLS = 256
MOE_VMEM_LIMIT = 56 * 1024 * 1024


def _moe_kernel(meta_ref, x_ref, w1_ref, b1g_ref, b1l_ref, w2_ref, b2_ref, o_ref, w1g_ref, w1l_ref, w2b_ref):
    i = pl.program_id(0)
    n_blk = meta_ref.shape[0] - 1
    n_used = meta_ref[n_blk]
    d_e = w2_ref.shape[1]
    new_expert = jnp.logical_or(i == 0, meta_ref[i] != meta_ref[jnp.maximum(i - 1, 0)])

    @pl.when(jnp.logical_and(i < n_used, new_expert))
    def _():
        src = lax.broadcasted_iota(jnp.int32, (2 * W1_SEL_COLS, 2 * W1_SEL_COLS), 0)
        dst = lax.broadcasted_iota(jnp.int32, (2 * W1_SEL_COLS, 2 * W1_SEL_COLS), 1)
        pick = jnp.where(dst < W1_SEL_COLS, 2 * dst, 2 * (dst - W1_SEL_COLS) + 1)
        sel = jnp.where(src == pick, 1.0, 0.0).astype(BF16)
        for t in range(d_e // W1_SEL_COLS):
            cols = _dot(w1_ref[0, :, 2 * t * W1_SEL_COLS:2 * (t + 1) * W1_SEL_COLS].astype(BF16), sel)
            w1g_ref[:, t * W1_SEL_COLS:(t + 1) * W1_SEL_COLS] = cols[:, :W1_SEL_COLS].astype(BF16)
            w1l_ref[:, t * W1_SEL_COLS:(t + 1) * W1_SEL_COLS] = cols[:, W1_SEL_COLS:].astype(BF16)
        w2b_ref[...] = w2_ref[0].astype(BF16)

    @pl.when(i < n_used)
    def _():
        x = x_ref[...]
        glu = jnp.minimum(_dot(x, w1g_ref[...]) + b1g_ref[0], SWIGLU_LIMIT)
        lin = jnp.clip(_dot(x, w1l_ref[...]) + b1l_ref[0], -SWIGLU_LIMIT, SWIGLU_LIMIT)
        act = glu * jax.nn.sigmoid(SWIGLU_ALPHA * glu) * (lin + 1.0)
        o_ref[...] = (_dot(act.astype(BF16), w2b_ref[...]) + b2_ref[0]).astype(BF16)

    @pl.when(i >= n_used)
    def _():
        o_ref[...] = jnp.zeros_like(o_ref)


def _moe_blocks(meta, xb, w1, b1g, b1l, w2, b2, layer):
    n_rows = xb.shape[0]
    n_blk = n_rows // MOE_BLK
    d_e = w2.shape[2]
    row = lambda i, m: (i, 0)
    exp3 = lambda i, m: (layer, m[i], 0, 0)
    grid_spec = pltpu.PrefetchScalarGridSpec(
        num_scalar_prefetch=1,
        grid=(n_blk,),
        in_specs=[
            pl.BlockSpec((MOE_BLK, D_MODEL), row),
            pl.BlockSpec((None, 1, D_MODEL, 2 * d_e), exp3),
            pl.BlockSpec((None, 1, 1, d_e), exp3),
            pl.BlockSpec((None, 1, 1, d_e), exp3),
            pl.BlockSpec((None, 1, d_e, D_MODEL), exp3),
            pl.BlockSpec((None, 1, 1, D_MODEL), exp3),
        ],
        out_specs=pl.BlockSpec((MOE_BLK, D_MODEL), row),
        scratch_shapes=[
            pltpu.VMEM((D_MODEL, d_e), BF16),
            pltpu.VMEM((D_MODEL, d_e), BF16),
            pltpu.VMEM((d_e, D_MODEL), BF16),
        ],
    )
    return pl.pallas_call(
        _moe_kernel,
        grid_spec=grid_spec,
        out_shape=jax.ShapeDtypeStruct((n_rows, D_MODEL), BF16),
        compiler_params=pltpu.CompilerParams(dimension_semantics=("arbitrary",),
                                             vmem_limit_bytes=MOE_VMEM_LIMIT),
        name="moe_blocks",
    )(meta, xb, w1, b1g, b1l, w2, b2)


def _route(top_i):
    n_tok = top_i.shape[0]
    e_flat = top_i.reshape(-1)
    n_rows = n_tok * TOP_K
    onehot = (e_flat[:, None] == jnp.arange(N_EXPERTS, dtype=jnp.int32)[None, :]).astype(jnp.int32)
    csum = jnp.cumsum(onehot, axis=0)
    counts = csum[-1]
    starts = jnp.cumsum(counts) - counts
    pcounts = (counts + MOE_BLK - 1) // MOE_BLK * MOE_BLK
    pends = jnp.cumsum(pcounts)
    pstarts = pends - pcounts
    dest = jnp.take_along_axis(csum + (pstarts - 1)[None, :], e_flat[:, None], axis=1)[:, 0]
    n_blk = n_rows // MOE_BLK + N_EXPERTS
    blk_start = jnp.arange(n_blk, dtype=jnp.int32) * MOE_BLK
    blk_exp = jnp.minimum(jnp.sum((blk_start[:, None] >= pends[None, :]).astype(jnp.int32), axis=1), N_EXPERTS - 1)
    order = jnp.argsort(e_flat)
    pos = jnp.arange(n_blk * MOE_BLK, dtype=jnp.int32)
    src = (pos + jnp.repeat((starts - pstarts)[blk_exp], MOE_BLK)) % n_rows
    row_tok = order[src].astype(jnp.int32) // TOP_K
    meta = jnp.concatenate([blk_exp, (pends[-1:] // MOE_BLK).astype(jnp.int32)])
    return meta, row_tok, dest.reshape(n_tok, TOP_K).T.reshape(-1)


def _combine_kernel(x_ref, yg_ref, gate_ref, mod_ref, o_ref):
    gate = gate_ref[...]
    acc = gate[:, 0:1] * yg_ref[0].astype(F32)
    for j in range(1, TOP_K):
        acc = acc + gate[:, j:j + 1] * yg_ref[j].astype(F32)
    o_ref[...] = x_ref[...] + mod_ref[0, 5:6, :] * acc


def _combine(x1, yg, gates, mods, tile_mod):
    n_tok = x1.shape[0]
    return pl.pallas_call(
        _combine_kernel,
        grid=(n_tok // TOK_TILE,),
        in_specs=[
            pl.BlockSpec((TOK_TILE, D_MODEL), lambda i: (i, 0)),
            pl.BlockSpec((TOP_K, TOK_TILE, D_MODEL), lambda i: (0, i, 0)),
            pl.BlockSpec((TOK_TILE, TOP_K), lambda i: (i, 0)),
            pl.BlockSpec((1, 6, D_MODEL), lambda i: (tile_mod(i), 0, 0)),
        ],
        out_specs=pl.BlockSpec((TOK_TILE, D_MODEL), lambda i: (i, 0)),
        out_shape=jax.ShapeDtypeStruct((n_tok, D_MODEL), F32),
        compiler_params=_cparams(("parallel",)),
        name="moe_combine",
    )(x1, yg, gates, mods)


def kernel(x_prompt, x_sample, c, cache_na_k, cache_na_v, cache_swa_k, cache_swa_v, state_rwkv, c_ctx, w_ada, b_ada, norm1_g, norm2_g, w_in, w_out, na_q_norm, na_k_norm, na_rpb, swa_q_norm, swa_k_norm, swa_sink, rk_conv, rk_w0, rk_w2, rk_a0, rk_a2, rk_g2, rk_k_k, rk_k_a, rk_r_k, rk_ln_g, rk_ln_b, moe_router_w, moe_router_b, moe_w1, moe_b1, moe_w2, moe_b2):
    bc, tc, _ = x_prompt.shape
    bl, tl, _ = x_sample.shape
    depth = w_in.shape[0]
    n_ctx = bc * tc
    n_lat = bl * tl
    assert tc == TOK_TILE and tl % TOK_TILE == 0 and n_ctx % tl == 0
    n_ctx_tiles = n_ctx // TOK_TILE
    tiles_per_seq = tl // TOK_TILE
    past = cache_na_k.shape[2]

    def tile_mod(i):
        return jnp.where(i < n_ctx_tiles, 0, 1 + (i - n_ctx_tiles) // tiles_per_seq)

    def tile_rope(i):
        return jnp.where(i < n_ctx_tiles, tiles_per_seq, (i - n_ctx_tiles) % tiles_per_seq)

    x = jnp.concatenate([x_prompt.reshape(n_ctx, D_MODEL), x_sample.reshape(n_lat, D_MODEL)], axis=0)

    n_mod = 1 + bl
    mod_rows = -(-n_mod // 8) * 8
    cvecs = jnp.concatenate([c_ctx[None, :], c, jnp.zeros((mod_rows - n_mod, D_MODEL), F32)], axis=0)
    mods_all = _ada_mod(cvecs, w_ada, b_ada).reshape(depth, mod_rows, 6, D_MODEL)
    cos_tab, sin_tab = _rope_tables(tl)
    tile2 = lambda g: jnp.concatenate([g, g])[None, :]
    pad_lanes = lambda z: jnp.pad(z, ((0, 0), (0, LANES - z.shape[1])))
    zeros_lora = jnp.zeros((2, RK_DECAY_LORA, RK_WIDTH), F32)

    na_k_l, na_v_l, sw_k_l, sw_v_l, st_l = [], [], [], [], []
    for l in range(depth):
        mods = mods_all[l]
        qk_gains = jnp.concatenate(
            [tile2(na_q_norm[l]), tile2(na_k_norm[l]), tile2(swa_q_norm[l]), tile2(swa_k_norm[l])], axis=0)
        p = {
            "rk_conv": rk_conv[l], "rk_w0": rk_w0[l], "rk_a0": rk_a0[l], "rk_g2": rk_g2[l],
            "rk_w2_pad": jnp.concatenate([rk_w2[l], zeros_lora], axis=1),
            "rk_a2_pad": jnp.concatenate([zeros_lora, rk_a2[l]], axis=1),
            "rk_k_k": rk_k_k[l][None, :], "rk_k_a": rk_k_a[l][None, :],
            "rk_r_k": rk_r_k[l].reshape(1, RK_WIDTH),
            "rk_ln_g": rk_ln_g[l][None, :], "rk_ln_b": rk_ln_b[l][None, :],
            "w_out_bf16": w_out[l].astype(BF16), "norm2_g": norm2_g[l][None, :],
            "router_w_pad": pad_lanes(moe_router_w[l]), "router_b_pad": pad_lanes(moe_router_b[l][None, :]),
        }

        att, u = _in_proj(x, norm1_g[l][None, :], mods, w_in[l].astype(BF16), qk_gains, cos_tab, sin_tab,
                          tile_mod, tile_rope)
        q, g, bonus = _rk_prep(u, p, n_ctx_tiles, tiles_per_seq)
        att_c = att[:n_ctx].reshape(bc, tc, ATT_COLS)
        att_by_ctx_len = att.reshape((n_ctx + n_lat) // tc, tc, ATT_COLS)
        att_by_lat_len = att.reshape((n_ctx + n_lat) // tl, tl, ATT_COLS)
        na_k_l.append(att_c[:, :, NA_WIDTH:2 * NA_WIDTH].reshape(bc, tc, NA_HEADS, HEAD_DIM))
        na_v_l.append(att_c[:, :, 2 * NA_WIDTH:NA_COLS].reshape(bc, tc, NA_HEADS, HEAD_DIM))
        sw_k_l.append(att_c[:, :, NA_COLS + SWA_WIDTH:NA_COLS + SWA_WIDTH + SWA_KV_WIDTH]
                      .reshape(bc, tc, SWA_KV_HEADS, HEAD_DIM))
        sw_v_l.append(att_c[:, :, NA_COLS + SWA_WIDTH + SWA_KV_WIDTH:].reshape(bc, tc, SWA_KV_HEADS, HEAD_DIM))

        sink = swa_sink[l]
        o_na = (_ctx_attn(att_by_ctx_len, bc, sink, gqa=False).reshape(n_ctx, NA_WIDTH),
                _na_latent(att_by_lat_len, n_ctx // tl, cache_na_k[:, l].reshape(bl, past, NA_WIDTH),
                           cache_na_v[:, l].reshape(bl, past, NA_WIDTH),
                           _na_bias_tables(na_rpb[l])).reshape(n_lat, NA_WIDTH))
        o_sw = (_ctx_attn(att_by_ctx_len, bc, sink, gqa=True).reshape(n_ctx, SWA_WIDTH),
                _swa_latent(att_by_lat_len, n_ctx // tl, cache_swa_k[:, l].reshape(bl, past, SWA_KV_WIDTH),
                            cache_swa_v[:, l].reshape(bl, past, SWA_KV_WIDTH), sink).reshape(n_lat, SWA_WIDTH))

        yf_c, yb_c, s_fin = _rwkv_group(q, 0, bc, tc, jnp.zeros((bc, 2, RK_NB, PAIR, PAIR), F32))
        yf_l, yb_l, _ = _rwkv_group(q, n_ctx // RK_CHUNK, bl, tl, _pair_states(state_rwkv[:, l]))
        st_l.append(_head_states(s_fin))

        x1, h2, gates, top_i = _out_proj(x, o_na, o_sw, (yf_c, yf_l), (yb_c, yb_l), bonus, g, p, mods, tile_mod,
                                         n_ctx_tiles)
        meta, row_tok, dest = _route(top_i)
        h2 = jnp.concatenate([h2, jnp.zeros((H2_PAD_ROWS - h2.shape[0], D_MODEL), BF16)], axis=0)
        yb = _moe_blocks(meta, h2[row_tok], moe_w1, moe_b1[:, :, None, 0::2], moe_b1[:, :, None, 1::2], moe_w2,
                         moe_b2[:, :, None, :], l)
        x = _combine(x1, yb[dest].reshape(TOP_K, n_ctx + n_lat, D_MODEL), gates, mods, tile_mod)

    y_p = x[:n_ctx].reshape(bc, tc, D_MODEL)
    y_s = x[n_ctx:].reshape(bl, tl, D_MODEL)
    return (y_p, y_s, jnp.stack(na_k_l, axis=1), jnp.stack(na_v_l, axis=1), jnp.stack(sw_k_l, axis=1),
            jnp.stack(sw_v_l, axis=1), jnp.stack(st_l, axis=1))
```

```python
import functools

import jax
import jax.numpy as jnp
from jax import lax
from jax.experimental import pallas as pl
from jax.experimental.pallas import tpu as pltpu

F32 = jnp.float32
BF16 = jnp.bfloat16

D_MODEL = 1024
HEAD_DIM = 64
LANES = 128
GRID_W = 64
NA_HEADS = 6
SWA_HEADS = 4
SWA_KV_HEADS = 2
RK_HEADS = 6
NA_WIDTH = NA_HEADS * HEAD_DIM
SWA_WIDTH = SWA_HEADS * HEAD_DIM
SWA_KV_WIDTH = SWA_KV_HEADS * HEAD_DIM
RK_WIDTH = RK_HEADS * HEAD_DIM
RK_DECAY_LORA = 64
RK_A_LORA = 64
RK_GATE_LORA = 128
RK_COLS = 3 * RK_WIDTH + RK_DECAY_LORA + RK_A_LORA + RK_GATE_LORA
NA_COLS = 3 * NA_WIDTH
SWA_COLS = SWA_WIDTH + 2 * SWA_KV_WIDTH
ATT_COLS = NA_COLS + SWA_COLS
IN_COLS = ATT_COLS + RK_COLS
NA_WIN_R = 8
NA_WIN_C = 16
SWA_WIN = 128
ROPE_THETA = 10000.0
ATTN_SCALE = HEAD_DIM ** -0.5
N_EXPERTS = 32
TOP_K = 4
SWIGLU_LIMIT = 7.0
SWIGLU_ALPHA = 1.702
MOE_BLK = 256
RMS_EPS = 1e-6
GN_EPS = 64e-5
NEG_BIG = -1e30

TOK_TILE = 256
VMEM_LIMIT = 48 * 1024 * 1024


def _cparams(sem):
    return pltpu.CompilerParams(dimension_semantics=sem, vmem_limit_bytes=VMEM_LIMIT)


def _dot(a, b):
    return jnp.dot(a, b, preferred_element_type=F32)


def _dot_nt(a, b):
    return lax.dot_general(a, b, (((1,), (1,)), ((), ())), preferred_element_type=F32)


def _split_bf16(x):
    hi = x.astype(BF16)
    lo = (x - hi.astype(F32)).astype(BF16)
    return hi, lo


def _dot3(a, b):
    ah, al = _split_bf16(a)
    bh, bl = _split_bf16(b)
    return _dot(ah, bh) + (_dot(ah, bl) + _dot(al, bh))


def _bmm_raw(a, b):
    return lax.dot_general(a, b, (((2,), (1,)), ((0,), (0,))), preferred_element_type=F32)


def _bmm(a, b):
    return _bmm_raw(a.astype(BF16), b.astype(BF16))


def _bmm_nt(a, b):
    return lax.dot_general(a.astype(BF16), b.astype(BF16), (((2,), (2,)), ((0,), (0,))),
                           preferred_element_type=F32)


def _bmm3(a, b):
    ah, al = _split_bf16(a)
    bh, bl = _split_bf16(b)
    return _bmm_raw(ah, bh) + (_bmm_raw(ah, bl) + _bmm_raw(al, bh))


def _lane_lo(shape):
    return lax.broadcasted_iota(jnp.int32, shape, len(shape) - 1) < HEAD_DIM


def _pair_sum(x):
    lo = _lane_lo(x.shape)
    s_lo = jnp.sum(jnp.where(lo, x, 0.0), axis=-1, keepdims=True)
    s_hi = jnp.sum(jnp.where(lo, 0.0, x), axis=-1, keepdims=True)
    return jnp.where(lo, s_lo, s_hi)


def _stack_heads(q):
    lo = _lane_lo(q.shape)
    return jnp.concatenate([jnp.where(lo, q, 0.0), jnp.where(lo, 0.0, q)], axis=0)


def _stack_heads3(x):
    lo = _lane_lo(x.shape)
    return jnp.concatenate([jnp.where(lo, x, 0.0), jnp.where(lo, 0.0, x)], axis=1)


def _unstack_heads(o2):
    n = o2.shape[0] // 2
    return jnp.where(_lane_lo((n, LANES)), o2[:n], o2[n:])


def _dup_head(x, j):
    keep = _lane_lo(x.shape) == (j == 0)
    return jnp.where(keep, x, pltpu.roll(x, HEAD_DIM, 1))


def _ada_kernel(c_ref, w_ref, b_ref, o_ref):
    cv = c_ref[...]
    s = cv * jax.nn.sigmoid(cv)
    o_ref[0] = _dot3(s, w_ref[0]) + b_ref[0]


def _ada_mod(cvecs, w_ada, b_ada):
    depth, _, n_out = w_ada.shape
    rows = cvecs.shape[0]
    tn = 1024
    return pl.pallas_call(
        _ada_kernel,
        grid=(depth, n_out // tn),
        in_specs=[
            pl.BlockSpec((rows, D_MODEL), lambda l, j: (0, 0)),
            pl.BlockSpec((1, D_MODEL, tn), lambda l, j: (l, 0, j)),
            pl.BlockSpec((1, 1, tn), lambda l, j: (l, 0, j)),
        ],
        out_specs=pl.BlockSpec((1, rows, tn), lambda l, j: (l, 0, j)),
        out_shape=jax.ShapeDtypeStruct((depth, rows, n_out), F32),
        compiler_params=_cparams(("parallel", "parallel")),
        name="ada_mod",
    )(cvecs, w_ada, b_ada.reshape(depth, 1, n_out))


NA_QK_BLOCKS = 2 * NA_WIDTH // LANES
SWA_Q_BLOCK0 = NA_COLS // LANES
SWA_QK_BLOCKS = (SWA_WIDTH + SWA_KV_WIDTH) // LANES


def _in_proj_kernel(xc_ref, xl_ref, g_ref, mod_ref, w_ref, qkg_ref, cos_ref, sin_ref, att_ref, u_ref, *, n_ctx_tiles):
    x = jnp.where(pl.program_id(0) < n_ctx_tiles, xc_ref[...], xl_ref[...])
    y = x * lax.rsqrt(jnp.mean(x * x, axis=-1, keepdims=True) + RMS_EPS)
    h = (y * g_ref[...]) * (1.0 + mod_ref[0, 1:2, :]) + mod_ref[0, 0:1, :]
    proj = _dot(h.astype(BF16), w_ref[...])
    u_ref[...] = proj[:, ATT_COLS:]

    def qk_norm(blk, gain):
        ms = _pair_sum(blk * blk) * (1.0 / HEAD_DIM)
        return blk * lax.rsqrt(ms + RMS_EPS) * gain

    lane = lax.broadcasted_iota(jnp.int32, (x.shape[0], LANES), 1)
    first = (lane % (HEAD_DIM // 2)) < (HEAD_DIM // 4)
    for cb in range(ATT_COLS // LANES):
        blk = proj[:, cb * LANES:(cb + 1) * LANES]
        if cb < NA_QK_BLOCKS:
            gi = 0 if cb < NA_QK_BLOCKS // 2 else 1
            blk = qk_norm(blk, qkg_ref[gi:gi + 1, :])
        elif SWA_Q_BLOCK0 <= cb < SWA_Q_BLOCK0 + SWA_QK_BLOCKS:
            gi = 2 if cb < SWA_Q_BLOCK0 + SWA_WIDTH // LANES else 3
            blk = qk_norm(blk, qkg_ref[gi:gi + 1, :])
            partner = jnp.where(first, pltpu.roll(blk, LANES - HEAD_DIM // 4, 1),
                                pltpu.roll(blk, HEAD_DIM // 4, 1))
            blk = blk * cos_ref[...] + partner * sin_ref[...]
        att_ref[:, cb * LANES:(cb + 1) * LANES] = blk


def _group_tile_specs(width, n_ctx_tiles):
    return [pl.BlockSpec((TOK_TILE, width), lambda i: (jnp.minimum(i, n_ctx_tiles - 1), 0)),
            pl.BlockSpec((TOK_TILE, width), lambda i: (jnp.maximum(i - n_ctx_tiles, 0), 0))]


def _in_proj(x, norm_g, mods, w_in_bf16, qk_gains, cos_tab, sin_tab, tile_mod, tile_rope, n_ctx_tiles):
    n_tok = x[0].shape[0] + x[1].shape[0]
    return pl.pallas_call(
        functools.partial(_in_proj_kernel, n_ctx_tiles=n_ctx_tiles),
        grid=(n_tok // TOK_TILE,),
        in_specs=[
            *_group_tile_specs(D_MODEL, n_ctx_tiles),
            pl.BlockSpec((1, D_MODEL), lambda i: (0, 0)),
            pl.BlockSpec((1, 6, D_MODEL), lambda i: (tile_mod(i), 0, 0)),
            pl.BlockSpec((D_MODEL, IN_COLS), lambda i: (0, 0)),
            pl.BlockSpec((4, LANES), lambda i: (0, 0)),
            pl.BlockSpec((TOK_TILE, LANES), lambda i: (tile_rope(i), 0)),
            pl.BlockSpec((TOK_TILE, LANES), lambda i: (tile_rope(i), 0)),
        ],
        out_specs=[
            pl.BlockSpec((TOK_TILE, ATT_COLS), lambda i: (i, 0)),
            pl.BlockSpec((TOK_TILE, RK_COLS), lambda i: (i, 0)),
        ],
        out_shape=[
            jax.ShapeDtypeStruct((n_tok, ATT_COLS), F32),
            jax.ShapeDtypeStruct((n_tok, RK_COLS), F32),
        ],
        compiler_params=_cparams(("parallel",)),
        name="in_proj",
    )(*x, norm_g, mods, w_in_bf16, qk_gains, cos_tab, sin_tab)


def _rope_tables(n_lat):
    nf = HEAD_DIM // 4
    t = jnp.arange(n_lat)
    lane = jnp.arange(LANES)
    d = lane % HEAD_DIM
    inv = ROPE_THETA ** (-(d % nf).astype(F32) / nf)
    pos = jnp.where((d // (2 * nf))[None, :] == 0, (t // GRID_W)[:, None], (t % GRID_W)[:, None]).astype(F32)
    ang = pos * inv[None, :]
    sign = jnp.where((d % (2 * nf)) < nf, -1.0, 1.0).astype(F32)
    cos = jnp.concatenate([jnp.cos(ang), jnp.ones((TOK_TILE, LANES), F32)], 0)
    sin = jnp.concatenate([jnp.sin(ang) * sign[None, :], jnp.zeros((TOK_TILE, LANES), F32)], 0)
    return cos, sin


def _ctx_attn_kernel(sink_ref, q_ref, k_ref, v_ref, o_ref, *, gqa):
    j = pl.program_id(1)
    k = k_ref[0]
    v = v_ref[0]
    if gqa:
        k = _dup_head(k, j)
        v = _dup_head(v, j)
    n = k.shape[0]
    q2 = _stack_heads(q_ref[0]).astype(BF16)
    s = _dot_nt(q2, k.astype(BF16)) * ATTN_SCALE
    m = jnp.max(s, axis=-1, keepdims=True)
    if gqa:
        row = lax.broadcasted_iota(jnp.int32, (2 * n, 1), 0)
        snk = jnp.where(row < n, sink_ref[2 * j], sink_ref[2 * j + 1])
        m = jnp.maximum(m, snk)
    p = jnp.exp(s - m)
    den = jnp.sum(p, axis=-1, keepdims=True)
    if gqa:
        den = den + jnp.exp(snk - m)
    o2 = _dot(p.astype(BF16), v.astype(BF16)) / den
    o_ref[0] = _unstack_heads(o2)


def _ctx_attn(att, b, sink, *, gqa):
    t = att.shape[1]
    if gqa:
        nq = SWA_WIDTH // LANES
        qb, kb, vb = SWA_Q_BLOCK0, SWA_Q_BLOCK0 + nq, SWA_Q_BLOCK0 + nq + 1
        kmap = lambda bi, j: (bi, 0, kb)
        vmap = lambda bi, j: (bi, 0, vb)
    else:
        nq = NA_WIDTH // LANES
        qb, kb, vb = 0, nq, 2 * nq
        kmap = lambda bi, j: (bi, 0, kb + j)
        vmap = lambda bi, j: (bi, 0, vb + j)
    return pl.pallas_call(
        functools.partial(_ctx_attn_kernel, gqa=gqa),
        grid=(b, nq),
        in_specs=[
            pl.BlockSpec(memory_space=pltpu.SMEM),
            pl.BlockSpec((1, t, LANES), lambda bi, j: (bi, 0, qb + j)),
            pl.BlockSpec((1, t, LANES), kmap),
            pl.BlockSpec((1, t, LANES), vmap),
        ],
        out_specs=pl.BlockSpec((1, t, LANES), lambda bi, j: (bi, 0, j)),
        out_shape=jax.ShapeDtypeStruct((b, t, nq * LANES), F32),
        compiler_params=_cparams(("parallel", "parallel")),
        name="ctx_attn_swa" if gqa else "ctx_attn_na",
    )(sink, att, att, att)


NA_ROWS_PER_ITER = 4


def _na_lat_kernel(q_ref, k_ref, v_ref, kc_ref, vc_ref, tab_ref, o_ref, kb_ref, vb_ref):
    n = q_ref.shape[1]
    rows = n // GRID_W
    win = NA_WIN_R * GRID_W
    kb_ref[...] = k_ref[0].astype(BF16)
    vb_ref[...] = v_ref[0].astype(BF16)
    kc = kc_ref[0].astype(BF16)
    vc = vc_ref[0].astype(BF16)

    def row_group(ig, carry):
        nr = NA_ROWS_PER_ITER
        g0 = pl.multiple_of(ig * (nr * GRID_W), nr * GRID_W)
        q2 = _stack_heads3(q_ref[0, pl.ds(g0, nr * GRID_W), :].reshape(nr, GRID_W, LANES)).astype(BF16)
        kws, vws, biases = [], [], []
        for r in range(nr):
            i = ig * nr + r
            start = jnp.clip(i - NA_WIN_R // 2, 0, rows - NA_WIN_R)
            k0 = pl.multiple_of(start * GRID_W, GRID_W)
            kws.append(kb_ref[pl.ds(k0, win), :])
            vws.append(vb_ref[pl.ds(k0, win), :])
            biases.append(tab_ref[0, start - i + (NA_WIN_R - 1)])
        s_loc = _bmm_nt(q2, jnp.stack(kws)) * ATTN_SCALE + jnp.stack(biases)
        s_ctx = _dot_nt(q2.reshape(nr * 2 * GRID_W, LANES), kc).reshape(nr, 2 * GRID_W, -1) * ATTN_SCALE
        m = jnp.maximum(jnp.max(s_loc, axis=-1, keepdims=True), jnp.max(s_ctx, axis=-1, keepdims=True))
        p_loc = jnp.exp(s_loc - m)
        p_ctx = jnp.exp(s_ctx - m)
        den = jnp.sum(p_loc, axis=-1, keepdims=True) + jnp.sum(p_ctx, axis=-1, keepdims=True)
        o_ctx = _dot(p_ctx.reshape(nr * 2 * GRID_W, -1).astype(BF16), vc).reshape(nr, 2 * GRID_W, LANES)
        o2 = (_bmm(p_loc, jnp.stack(vws)) + o_ctx) / den
        out = jnp.where(_lane_lo((nr, GRID_W, LANES)), o2[:, :GRID_W], o2[:, GRID_W:])
        o_ref[0, pl.ds(g0, nr * GRID_W), :] = out.reshape(nr * GRID_W, LANES)
        return carry

    lax.fori_loop(0, rows // NA_ROWS_PER_ITER, row_group, 0)


def _na_bias_tables(rpb):
    col = jnp.arange(GRID_W)
    cstart = jnp.clip(col - NA_WIN_C // 2, 0, GRID_W - NA_WIN_C)
    col_mask = (col[None, :] >= cstart[:, None]) & (col[None, :] < cstart[:, None] + NA_WIN_C)
    col_idx = jnp.clip(col[None, :] - col[:, None] + NA_WIN_C - 1, 0, 2 * NA_WIN_C - 2)
    rpb_cols = jnp.where(col_mask[None, None], rpb[:, :, col_idx], NEG_BIG)
    roff = jnp.arange(NA_WIN_R)[:, None] + jnp.arange(NA_WIN_R)[None, :]
    t = rpb_cols[:, roff]
    t = jnp.transpose(t, (0, 1, 3, 2, 4)).reshape(NA_HEADS // 2, 2, NA_WIN_R, GRID_W, NA_WIN_R * GRID_W)
    return jnp.transpose(t, (0, 2, 1, 3, 4)).reshape(NA_HEADS // 2, NA_WIN_R, 2 * GRID_W, NA_WIN_R * GRID_W)


def _na_latent(att, s0, kc, vc, tab):
    n = att.shape[1]
    b, p, _ = kc.shape
    nq = NA_WIDTH // LANES
    return pl.pallas_call(
        _na_lat_kernel,
        grid=(b, nq),
        in_specs=[
            pl.BlockSpec((1, n, LANES), lambda bi, j: (s0 + bi, 0, j)),
            pl.BlockSpec((1, n, LANES), lambda bi, j: (s0 + bi, 0, nq + j)),
            pl.BlockSpec((1, n, LANES), lambda bi, j: (s0 + bi, 0, 2 * nq + j)),
            pl.BlockSpec((1, p, LANES), lambda bi, j: (bi, 0, j)),
            pl.BlockSpec((1, p, LANES), lambda bi, j: (bi, 0, j)),
            pl.BlockSpec((1, NA_WIN_R, 2 * GRID_W, NA_WIN_R * GRID_W), lambda bi, j: (j, 0, 0, 0)),
        ],
        out_specs=pl.BlockSpec((1, n, LANES), lambda bi, j: (bi, 0, j)),
        out_shape=jax.ShapeDtypeStruct((b, n, NA_WIDTH), F32),
        scratch_shapes=[pltpu.VMEM((n, LANES), BF16), pltpu.VMEM((n, LANES), BF16)],
        compiler_params=_cparams(("parallel", "parallel")),
        name="na_latent",
    )(att, att, att, kc, vc, tab)


SWA_BLOCKS_PER_ITER = 2


def _swa_lat_kernel(sink_ref, q_ref, k_ref, v_ref, kc_ref, vc_ref, o_ref, kb_ref, vb_ref):
    j = pl.program_id(1)
    n = q_ref.shape[1]
    blk = SWA_WIN
    span = 3 * blk
    kb_ref[...] = _dup_head(k_ref[0], j).astype(BF16)
    vb_ref[...] = _dup_head(v_ref[0], j).astype(BF16)
    kc = _dup_head(kc_ref[0], j).astype(BF16)
    vc = _dup_head(vc_ref[0], j).astype(BF16)
    row = lax.broadcasted_iota(jnp.int32, (2 * blk, 1), 0)
    snk = jnp.where(row < blk, sink_ref[2 * j], sink_ref[2 * j + 1])
    qoff = lax.broadcasted_iota(jnp.int32, (2 * blk, span), 0) % blk
    koff = lax.broadcasted_iota(jnp.int32, (2 * blk, span), 1)

    def q_group(qg, carry):
        nr = SWA_BLOCKS_PER_ITER
        g0 = pl.multiple_of(qg * (nr * blk), nr * blk)
        q2 = _stack_heads3(q_ref[0, pl.ds(g0, nr * blk), :].reshape(nr, blk, LANES)).astype(BF16)
        kws, vws, valids = [], [], []
        for r in range(nr):
            q0 = g0 + r * blk
            w0 = pl.multiple_of(jnp.clip(q0 - blk, 0, n - span), blk)
            kws.append(kb_ref[pl.ds(w0, span), :])
            vws.append(vb_ref[pl.ds(w0, span), :])
            valids.append(jnp.abs((q0 + qoff) - (w0 + koff)) <= SWA_WIN)
        s_loc = jnp.where(jnp.stack(valids), _bmm_nt(q2, jnp.stack(kws)) * ATTN_SCALE, NEG_BIG)
        s_ctx = _dot_nt(q2.reshape(nr * 2 * blk, LANES), kc).reshape(nr, 2 * blk, -1) * ATTN_SCALE
        m = jnp.maximum(jnp.max(s_loc, axis=-1, keepdims=True), jnp.max(s_ctx, axis=-1, keepdims=True))
        m = jnp.maximum(m, snk)
        p_loc = jnp.exp(s_loc - m)
        p_ctx = jnp.exp(s_ctx - m)
        den = (jnp.sum(p_loc, axis=-1, keepdims=True) + jnp.sum(p_ctx, axis=-1, keepdims=True)
               + jnp.exp(snk - m))
        o_ctx = _dot(p_ctx.reshape(nr * 2 * blk, -1).astype(BF16), vc).reshape(nr, 2 * blk, LANES)
        o2 = (_bmm(p_loc, jnp.stack(vws)) + o_ctx) / den
        out = jnp.where(_lane_lo((nr, blk, LANES)), o2[:, :blk], o2[:, blk:])
        o_ref[0, pl.ds(g0, nr * blk), :] = out.reshape(nr * blk, LANES)
        return carry

    lax.fori_loop(0, n // (SWA_BLOCKS_PER_ITER * blk), q_group, 0)


def _swa_latent(att, s0, kc, vc, sink):
    n = att.shape[1]
    b, p, _ = kc.shape
    nq = SWA_WIDTH // LANES
    qb, kb, vb = SWA_Q_BLOCK0, SWA_Q_BLOCK0 + nq, SWA_Q_BLOCK0 + nq + 1
    return pl.pallas_call(
        _swa_lat_kernel,
        grid=(b, nq),
        in_specs=[
            pl.BlockSpec(memory_space=pltpu.SMEM),
            pl.BlockSpec((1, n, LANES), lambda bi, j: (s0 + bi, 0, qb + j)),
            pl.BlockSpec((1, n, LANES), lambda bi, j: (s0 + bi, 0, kb)),
            pl.BlockSpec((1, n, LANES), lambda bi, j: (s0 + bi, 0, vb)),
            pl.BlockSpec((1, p, LANES), lambda bi, j: (bi, 0, 0)),
            pl.BlockSpec((1, p, LANES), lambda bi, j: (bi, 0, 0)),
        ],
        out_specs=pl.BlockSpec((1, n, LANES), lambda bi, j: (bi, 0, j)),
        out_shape=jax.ShapeDtypeStruct((b, n, SWA_WIDTH), F32),
        scratch_shapes=[pltpu.VMEM((n, LANES), BF16), pltpu.VMEM((n, LANES), BF16)],
        compiler_params=_cparams(("parallel", "parallel")),
        name="swa_latent",
    )(sink, att, att, att, kc, vc)


RK_NB = RK_WIDTH // LANES
LORA_BLOCK = 3 * RK_WIDTH // LANES
GATE_BLOCK = LORA_BLOCK + 1
Q_R, Q_V, Q_A, Q_W, Q_K, Q_B = range(6)
Q_DIR = 3
Q_COLS = (6 + Q_DIR) * RK_WIDTH


def _softplus(x):
    return jnp.maximum(x, 0.0) + jnp.log(1.0 + jnp.exp(-jnp.abs(x)))


def _rk_prep_kernel(u_ref, up_ref, un_ref, cw_ref, w0_ref, w2_ref, a0_ref, a2_ref, g2_ref, kk_ref, ka_ref,
                    rk_ref, q_ref, g_ref, bonus_ref, *, n_ctx_tiles, tiles_per_seq):
    def put(slot, val):
        q_ref[:, slot * RK_WIDTH:(slot + 1) * RK_WIDTH] = val

    i = pl.program_id(0)
    li = i - n_ctx_tiles
    is_lat = i >= n_ctx_tiles
    has_prev = jnp.logical_and(is_lat, li % tiles_per_seq != 0)
    has_next = jnp.logical_and(is_lat, li % tiles_per_seq != tiles_per_seq - 1)
    u = u_ref[...]
    tm = u.shape[0]
    prev_row = jnp.where(has_prev, up_ref[7:8, :], 0.0)
    next_row = jnp.where(has_next, un_ref[0:1, :], 0.0)
    row = lax.broadcasted_iota(jnp.int32, u.shape, 0)
    um = jnp.where(row == 0, prev_row, pltpu.roll(u, 1, 0))
    up = jnp.where(row == tm - 1, next_row, pltpu.roll(u, tm - 1, 0))
    u = um * cw_ref[0:1, :] + u * cw_ref[1:2, :] + up * cw_ref[2:3, :]

    r = u[:, 0:RK_WIDTH]
    k = u[:, RK_WIDTH:2 * RK_WIDTH]
    v = u[:, 2 * RK_WIDTH:3 * RK_WIDTH]
    lora = u[:, LORA_BLOCK * LANES:(LORA_BLOCK + 1) * LANES]
    gl = u[:, GATE_BLOCK * LANES:(GATE_BLOCK + 1) * LANES]
    put(Q_R, r)
    put(Q_V, v)
    g_ref[...] = _dot3(jax.nn.sigmoid(gl), g2_ref[...])

    kn = k * kk_ref[...]
    kk = jnp.concatenate(
        [kn[:, c * LANES:(c + 1) * LANES]
         * lax.rsqrt(jnp.maximum(_pair_sum(jnp.square(kn[:, c * LANES:(c + 1) * LANES])), 1e-24))
         for c in range(RK_NB)], axis=1)
    put(Q_A, -kk)

    lora_t = jnp.tanh(lora)
    kd_sum = None
    for d in range(2):
        w = -_softplus(-(w0_ref[d:d + 1, :] + _dot3(lora_t, w2_ref[d]))) - 0.5
        put(Q_W + Q_DIR * d, -jnp.exp(w))
        a = jax.nn.sigmoid(a0_ref[d:d + 1, :] + _dot3(lora, a2_ref[d]))
        kd = k * (1.0 + (a - 1.0) * ka_ref[...])
        put(Q_K + Q_DIR * d, kd)
        put(Q_B + Q_DIR * d, kk * a)
        kd_sum = kd if kd_sum is None else kd_sum + kd

    t = r * kd_sum * rk_ref[...]
    bonus_ref[...] = jnp.concatenate(
        [_pair_sum(t[:, c * LANES:(c + 1) * LANES]) for c in range(RK_NB)], axis=1) * v


def _rk_prep(u, p, n_ctx_tiles, tiles_per_seq):
    n_tok = u.shape[0]
    n_tiles = n_tok // TOK_TILE
    sub = TOK_TILE // 8
    last8 = n_tok // 8 - 1
    tok = lambda i: (i, 0)
    const2 = lambda i: (0, 0)
    const3 = lambda i: (0, 0, 0)
    one = jax.ShapeDtypeStruct((n_tok, RK_WIDTH), F32)
    tok_spec = pl.BlockSpec((TOK_TILE, RK_WIDTH), tok)
    return pl.pallas_call(
        functools.partial(_rk_prep_kernel, n_ctx_tiles=n_ctx_tiles, tiles_per_seq=tiles_per_seq),
        grid=(n_tiles,),
        in_specs=[
            pl.BlockSpec((TOK_TILE, RK_COLS), tok),
            pl.BlockSpec((8, RK_COLS), lambda i: (jnp.maximum(i * sub - 1, 0), 0)),
            pl.BlockSpec((8, RK_COLS), lambda i: (jnp.minimum((i + 1) * sub, last8), 0)),
            pl.BlockSpec((3, RK_COLS), const2),
            pl.BlockSpec((2, RK_WIDTH), const2),
            pl.BlockSpec((2, LANES, RK_WIDTH), const3),
            pl.BlockSpec((2, RK_WIDTH), const2),
            pl.BlockSpec((2, LANES, RK_WIDTH), const3),
            pl.BlockSpec((RK_GATE_LORA, RK_WIDTH), const2),
            pl.BlockSpec((1, RK_WIDTH), const2),
            pl.BlockSpec((1, RK_WIDTH), const2),
            pl.BlockSpec((1, RK_WIDTH), const2),
        ],
        out_specs=[pl.BlockSpec((TOK_TILE, Q_COLS), tok), tok_spec, tok_spec],
        out_shape=[jax.ShapeDtypeStruct((n_tok, Q_COLS), F32), one, one],
        compiler_params=_cparams(("parallel",)),
        name="rk_prep",
    )(u, u, u, p["rk_conv"], p["rk_w0"], p["rk_w2_pad"], p["rk_a0"], p["rk_a2_pad"], p["rk_g2"],
      p["rk_k_k"], p["rk_k_a"], p["rk_r_k"])


RK_CHUNK = 64
PAIR = 2 * HEAD_DIM
STATE_SEQS = 8
RK_STEP_CHUNKS = 4


def _split3_bf16(x):
    hi = x.astype(BF16)
    r1 = x - hi.astype(F32)
    mid = r1.astype(BF16)
    return hi, mid, (r1 - mid.astype(F32)).astype(BF16)


def _pack_pair(m):
    return jnp.concatenate([m[:HEAD_DIM, :HEAD_DIM], m[HEAD_DIM:, HEAD_DIM:]], axis=1)


def _unpack_pairs(m):
    lo = _lane_lo(m.shape)
    return jnp.concatenate([jnp.where(lo, m, 0.0), jnp.where(lo, 0.0, m)], axis=1)


def _rk_chunk_kernel(q_ref, rbar_ref, ybar_ref, phi_ref, psi_ref):
    c = RK_CHUNK
    n = 2 * c
    nd = 2 * RK_NB
    nu = RK_STEP_CHUNKS * nd

    def tiles(slot, per_dir):
        cols = [(slot + (Q_DIR * d if per_dir else 0)) * RK_WIDTH + p * LANES
                for d in range(2) for p in range(RK_NB)]
        return jnp.stack([q_ref[ck * c:(ck + 1) * c, lo:lo + LANES] for ck in range(RK_STEP_CHUNKS) for lo in cols])

    r, v, a = tiles(Q_R, False), tiles(Q_V, False), tiles(Q_A, False)
    lw, k, b = tiles(Q_W, True), tiles(Q_K, True), tiles(Q_B, True)
    unit = lax.broadcasted_iota(jnp.int32, (nu, 1, 1), 0)
    sgn = jnp.ones((nu, 1, 1), jnp.int32)
    for ck in range(RK_STEP_CHUNKS):
        sgn = jnp.where(jnp.logical_and(unit >= ck * nd + RK_NB, unit < (ck + 1) * nd), -1, sgn)
    bwd = sgn < 0
    tdiff = lax.broadcasted_iota(jnp.int32, (1, c, c), 2) - lax.broadcasted_iota(jnp.int32, (1, c, c), 1)
    tri = jnp.where(tdiff * sgn <= 0, 1.0, 0.0)
    cum = sum(_bmm(tri, part) for part in _split3_bf16(lw))
    tot = jnp.where(bwd, cum[:, 0:1], cum[:, c - 1:c])
    a_t = a * jnp.exp(cum - lw)
    r_t = r * jnp.exp(cum)
    e_neg = jnp.exp(-cum)
    e_end = jnp.exp(tot - cum)
    g = _bmm_nt(jnp.concatenate([_stack_heads3(a_t), _stack_heads3(r_t)], axis=1),
                jnp.concatenate([_stack_heads3(b * e_neg), _stack_heads3(k * e_neg)], axis=1))
    r2 = lax.broadcasted_iota(jnp.int32, (1, n, n), 1)
    c2 = lax.broadcasted_iota(jnp.int32, (1, n, n), 2)
    order = (jnp.bitwise_and(c2, c - 1) - jnp.bitwise_and(r2, c - 1)) * sgn
    eye = jnp.where(r2 == c2, 1.0, 0.0)
    l_ab = jnp.where(order < 0, g[:, :n, :n], 0.0)
    l_ak = jnp.where(order < 0, g[:, :n, n:], 0.0)
    m_rb = jnp.where(order <= 0, g[:, n:, :n], 0.0)
    m_rk = jnp.where(order <= 0, g[:, n:, n:], 0.0)
    t_inv = eye + l_ab
    pw = l_ab
    for _ in range(5):
        pw = _bmm(pw, pw)
        t_inv = t_inv + _bmm(t_inv, pw)
    sv = _stack_heads3(v)
    au = _bmm(t_inv, jnp.concatenate([_stack_heads3(a_t), _bmm(l_ak, sv)], axis=2))
    ry = _bmm(m_rb, au) + jnp.concatenate([_stack_heads3(r_t), _bmm(m_rk, sv)], axis=2)
    ry = ry[:, :c] + ry[:, c:]
    bt = jnp.swapaxes(_stack_heads3(b * e_end), 1, 2)
    kt = jnp.swapaxes(_stack_heads3(k * e_end), 1, 2)
    pp = _bmm(bt, au)
    phi = eye * jnp.exp(tot) + pp[:, :, :PAIR]
    psi = pp[:, :, PAIR:] + _bmm(kt, sv)
    for ck in range(RK_STEP_CHUNKS):
        for d in range(2):
            for p in range(RK_NB):
                u = ck * nd + d * RK_NB + p
                rbar_ref[d, ck * c:(ck + 1) * c, p * LANES:(p + 1) * LANES] = ry[u, :, :PAIR]
                ybar_ref[d, ck * c:(ck + 1) * c, p * LANES:(p + 1) * LANES] = ry[u, :, PAIR:]
                phi_ref[d, ck, p] = _pack_pair(phi[u])
                psi_ref[d, ck, p] = _pack_pair(psi[u])


def _rk_chunk(q, tile0, n_seq, t):
    nc = t // RK_CHUNK
    sc = RK_STEP_CHUNKS
    assert nc % sc == 0 and tile0 % sc == 0
    row_sh = jax.ShapeDtypeStruct((2, n_seq, t, RK_WIDTH), F32)
    mat_sh = jax.ShapeDtypeStruct((2, n_seq, nc, RK_NB, HEAD_DIM, PAIR), F32)
    row_spec = pl.BlockSpec((2, None, sc * RK_CHUNK, RK_WIDTH), lambda s, c: (0, s, c, 0))
    mat_spec = pl.BlockSpec((2, None, sc, RK_NB, HEAD_DIM, PAIR), lambda s, c: (0, s, c, 0, 0, 0))
    return pl.pallas_call(
        _rk_chunk_kernel,
        grid=(n_seq, nc // sc),
        in_specs=[pl.BlockSpec((sc * RK_CHUNK, Q_COLS), lambda s, c: ((tile0 + s * nc) // sc + c, 0))],
        out_specs=[row_spec, row_spec, mat_spec, mat_spec],
        out_shape=[row_sh, row_sh, mat_sh, mat_sh],
        compiler_params=_cparams(("parallel", "parallel")),
        name="rk_chunk",
    )(q)


def _rk_state_kernel(rf_ref, rb_ref, yf_ref, yb_ref, phf_ref, phb_ref, psf_ref, psb_ref, s0_ref,
                     of_ref, ob_ref, s_ref):
    @pl.when(pl.program_id(1) == 0)
    def _():
        s_ref[...] = s0_ref[...]

    ns = s_ref.shape[0]
    nd = 2 * RK_NB

    def pair_tiles(ref_f, ref_b):
        tiles = [ref[:, :, p * LANES:(p + 1) * LANES] for ref in (ref_f, ref_b) for p in range(RK_NB)]
        return jnp.stack(tiles, axis=1).reshape(ns * nd, RK_CHUNK, LANES)

    def mats(ref_f, ref_b):
        return _unpack_pairs(jnp.concatenate([ref_f[...], ref_b[...]], axis=1).reshape(ns * nd, HEAD_DIM, PAIR))

    h = s_ref[...].reshape(ns * nd, PAIR, PAIR)
    y = (_bmm3(pair_tiles(rf_ref, rb_ref), h) + pair_tiles(yf_ref, yb_ref)).reshape(ns, nd, RK_CHUNK, LANES)
    s_ref[...] = (_bmm3(mats(phf_ref, phb_ref), h) + mats(psf_ref, psb_ref)).reshape(ns, 2, RK_NB, PAIR, PAIR)
    of_ref[...] = jnp.concatenate([y[:, p] for p in range(RK_NB)], axis=2)
    ob_ref[...] = jnp.concatenate([y[:, RK_NB + p] for p in range(RK_NB)], axis=2)


def _rk_state(rbar, ybar, phi, psi, s0):
    _, n_seq, t, _ = rbar.shape
    nc = t // RK_CHUNK
    sg = STATE_SEQS
    row_blk = (None, sg, RK_CHUNK, RK_WIDTH)
    mat_blk = (None, sg, None, RK_NB, HEAD_DIM, PAIR)
    fwd_row = pl.BlockSpec(row_blk, lambda g, c: (0, g, c, 0))
    bwd_row = pl.BlockSpec(row_blk, lambda g, c: (1, g, nc - 1 - c, 0))
    fwd_mat = pl.BlockSpec(mat_blk, lambda g, c: (0, g, c, 0, 0, 0))
    bwd_mat = pl.BlockSpec(mat_blk, lambda g, c: (1, g, nc - 1 - c, 0, 0, 0))
    st = pl.BlockSpec((sg, 2, RK_NB, PAIR, PAIR), lambda g, c: (g, 0, 0, 0, 0))
    out_sh = jax.ShapeDtypeStruct((n_seq, t, RK_WIDTH), F32)
    return pl.pallas_call(
        _rk_state_kernel,
        grid=(n_seq // sg, nc),
        in_specs=[fwd_row, bwd_row, fwd_row, bwd_row, fwd_mat, bwd_mat, fwd_mat, bwd_mat, st],
        out_specs=[pl.BlockSpec((sg, RK_CHUNK, RK_WIDTH), lambda g, c: (g, c, 0)),
                   pl.BlockSpec((sg, RK_CHUNK, RK_WIDTH), lambda g, c: (g, nc - 1 - c, 0)), st],
        out_shape=[out_sh, out_sh, jax.ShapeDtypeStruct((n_seq, 2, RK_NB, PAIR, PAIR), F32)],
        compiler_params=_cparams(("parallel", "arbitrary")),
        name="rk_state",
    )(rbar, rbar, ybar, ybar, phi, phi, psi, psi, s0)


def _pair_states(s):
    bsz = s.shape[0]
    h = jnp.swapaxes(s, -1, -2).reshape(bsz, 2, RK_NB, 2, HEAD_DIM, HEAD_DIM)
    return jnp.einsum("bdphkv,hg->bdphkgv", h, jnp.eye(2, dtype=F32)).reshape(bsz, 2, RK_NB, PAIR, PAIR)


def _head_states(s):
    bsz = s.shape[0]
    h = jnp.stack([s[..., :HEAD_DIM, :HEAD_DIM], s[..., HEAD_DIM:, HEAD_DIM:]], axis=3)
    return jnp.swapaxes(h.reshape(bsz, 2, RK_HEADS, HEAD_DIM, HEAD_DIM), -1, -2)


def _rwkv_group(q, tile0, n_seq, t, s0):
    rbar, ybar, phi, psi = _rk_chunk(q, tile0, n_seq, t)
    y_f, y_b, s_fin = _rk_state(rbar, ybar, phi, psi, s0)
    return y_f.reshape(n_seq * t, RK_WIDTH), y_b.reshape(n_seq * t, RK_WIDTH), s_fin


def _out_proj_kernel(x_c, x_l, ona_c, ona_l, osw_c, osw_l, yf_c, yf_l, yb_c, yb_l, bonus_ref, g_ref, lng_ref, lnb_ref,
                     w_ref, mod_ref, n2_ref, rw_ref, rb_ref, x1_ref, h2_ref, gate_ref, top_ref, *, n_ctx_tiles):
    is_ctx = pl.program_id(0) < n_ctx_tiles
    pick = lambda c_ref, l_ref: jnp.where(is_ctx, c_ref[...], l_ref[...])
    ona = pick(ona_c, ona_l)
    osw = pick(osw_c, osw_l)
    y = pick(yf_c, yf_l) + pick(yb_c, yb_l)
    parts = []
    for c in range(RK_NB):
        yc = y[:, c * LANES:(c + 1) * LANES]
        dc = yc - _pair_sum(yc) * (1.0 / HEAD_DIM)
        var = _pair_sum(dc * dc) * (1.0 / HEAD_DIM)
        parts.append(dc * lax.rsqrt(var + GN_EPS))
    yn = jnp.concatenate(parts, axis=1) * lng_ref[...] + lnb_ref[...]
    o_rk = (yn + bonus_ref[...]) * g_ref[...]
    o = (_dot(ona.astype(BF16), w_ref[0:NA_WIDTH, :])
         + _dot(osw.astype(BF16), w_ref[NA_WIDTH:NA_WIDTH + SWA_WIDTH, :])
         + _dot(o_rk.astype(BF16), w_ref[NA_WIDTH + SWA_WIDTH:, :]))
    x1 = pick(x_c, x_l) + mod_ref[0, 2:3, :] * o
    x1_ref[...] = x1
    yn2 = x1 * lax.rsqrt(jnp.mean(x1 * x1, axis=-1, keepdims=True) + RMS_EPS)
    h2 = (yn2 * n2_ref[...]) * (1.0 + mod_ref[0, 4:5, :]) + mod_ref[0, 3:4, :]
    h2_ref[...] = h2.astype(BF16)
    lane = lax.broadcasted_iota(jnp.int32, (h2.shape[0], LANES), 1).astype(F32)
    logit = jnp.where(lane < N_EXPERTS, _dot3(h2, rw_ref[...]) + rb_ref[...], -jnp.inf)
    vals, idxs = [], []
    for _ in range(TOP_K):
        best = jnp.max(logit, axis=-1, keepdims=True)
        idx = jnp.min(jnp.where(logit == best, lane, float(LANES)), axis=-1, keepdims=True)
        vals.append(best)
        idxs.append(idx)
        logit = jnp.where(lane == idx, -jnp.inf, logit)
    e = jnp.exp(jnp.concatenate(vals, axis=1) - vals[0])
    gate_ref[...] = e / jnp.sum(e, axis=-1, keepdims=True)
    top_ref[...] = jnp.concatenate(idxs, axis=1).astype(jnp.int32)


def _out_proj(x, o_na, o_sw, y_f, y_b, bonus, g, p, mods, tile_mod, n_ctx_tiles):
    n_tok = x[0].shape[0] + x[1].shape[0]
    tok = lambda i: (i, 0)
    const = lambda i: (0, 0)
    pair = lambda w: _group_tile_specs(w, n_ctx_tiles)
    return pl.pallas_call(
        functools.partial(_out_proj_kernel, n_ctx_tiles=n_ctx_tiles),
        grid=(n_tok // TOK_TILE,),
        in_specs=[
            *pair(D_MODEL), *pair(NA_WIDTH), *pair(SWA_WIDTH), *pair(RK_WIDTH), *pair(RK_WIDTH),
            pl.BlockSpec((TOK_TILE, RK_WIDTH), tok),
            pl.BlockSpec((TOK_TILE, RK_WIDTH), tok),
            pl.BlockSpec((1, RK_WIDTH), const),
            pl.BlockSpec((1, RK_WIDTH), const),
            pl.BlockSpec((D_MODEL, D_MODEL), const),
            pl.BlockSpec((1, 6, D_MODEL), lambda i: (tile_mod(i), 0, 0)),
            pl.BlockSpec((1, D_MODEL), const),
            pl.BlockSpec((D_MODEL, LANES), const),
            pl.BlockSpec((1, LANES), const),
        ],
        out_specs=[
            pl.BlockSpec((TOK_TILE, D_MODEL), tok),
            pl.BlockSpec((TOK_TILE, D_MODEL), tok),
            pl.BlockSpec((TOK_TILE, TOP_K), tok),
            pl.BlockSpec((TOK_TILE, TOP_K), tok),
        ],
        out_shape=[
            jax.ShapeDtypeStruct((n_tok, D_MODEL), F32),
            jax.ShapeDtypeStruct((n_tok, D_MODEL), BF16),
            jax.ShapeDtypeStruct((n_tok, TOP_K), F32),
            jax.ShapeDtypeStruct((n_tok, TOP_K), jnp.int32),
        ],
        compiler_params=_cparams(("parallel",)),
        name="out_proj",
    )(*x, *o_na, *o_sw, *y_f, *y_b, bonus, g, p["rk_ln_g"], p["rk_ln_b"], p["w_out_bf16"], mods, p["norm2_g"],
      p["router_w_pad"], p["router_b_pad"])


H2_PAD_ROWS = 32768
W1_SEL_COLS = 256
MOE_VMEM_LIMIT = 56 * 1024 * 1024


def _moe_kernel(meta_ref, x_ref, w1_ref, b1g_ref, b1l_ref, w2_ref, b2_ref, o_ref, w1g_ref, w1l_ref, w2b_ref):
    i = pl.program_id(0)
    n_blk = meta_ref.shape[0] - 1
    n_used = meta_ref[n_blk]
    d_e = w2_ref.shape[1]
    new_expert = jnp.logical_or(i == 0, meta_ref[i] != meta_ref[jnp.maximum(i - 1, 0)])

    @pl.when(jnp.logical_and(i < n_used, new_expert))
    def _():
        src = lax.broadcasted_iota(jnp.int32, (2 * W1_SEL_COLS, 2 * W1_SEL_COLS), 0)
        dst = lax.broadcasted_iota(jnp.int32, (2 * W1_SEL_COLS, 2 * W1_SEL_COLS), 1)
        pick = jnp.where(dst < W1_SEL_COLS, 2 * dst, 2 * (dst - W1_SEL_COLS) + 1)
        sel = jnp.where(src == pick, 1.0, 0.0).astype(BF16)
        for t in range(d_e // W1_SEL_COLS):
            cols = _dot(w1_ref[0, :, 2 * t * W1_SEL_COLS:2 * (t + 1) * W1_SEL_COLS].astype(BF16), sel)
            w1g_ref[:, t * W1_SEL_COLS:(t + 1) * W1_SEL_COLS] = cols[:, :W1_SEL_COLS].astype(BF16)
            w1l_ref[:, t * W1_SEL_COLS:(t + 1) * W1_SEL_COLS] = cols[:, W1_SEL_COLS:].astype(BF16)
        w2b_ref[...] = w2_ref[0].astype(BF16)

    @pl.when(i < n_used)
    def _():
        x = x_ref[...]
        glu = jnp.minimum(_dot(x, w1g_ref[...]) + b1g_ref[0], SWIGLU_LIMIT)
        lin = jnp.clip(_dot(x, w1l_ref[...]) + b1l_ref[0], -SWIGLU_LIMIT, SWIGLU_LIMIT)
        act = glu * jax.nn.sigmoid(SWIGLU_ALPHA * glu) * (lin + 1.0)
        o_ref[...] = (_dot(act.astype(BF16), w2b_ref[...]) + b2_ref[0]).astype(BF16)

    @pl.when(i >= n_used)
    def _():
        o_ref[...] = jnp.zeros_like(o_ref)


def _moe_blocks(meta, xb, w1, b1g, b1l, w2, b2, layer):
    n_rows = xb.shape[0]
    n_blk = n_rows // MOE_BLK
    d_e = w2.shape[2]
    row = lambda i, m: (i, 0)
    exp3 = lambda i, m: (layer, m[i], 0, 0)
    grid_spec = pltpu.PrefetchScalarGridSpec(
        num_scalar_prefetch=1,
        grid=(n_blk,),
        in_specs=[
            pl.BlockSpec((MOE_BLK, D_MODEL), row),
            pl.BlockSpec((None, 1, D_MODEL, 2 * d_e), exp3),
            pl.BlockSpec((None, 1, 1, d_e), exp3),
            pl.BlockSpec((None, 1, 1, d_e), exp3),
            pl.BlockSpec((None, 1, d_e, D_MODEL), exp3),
            pl.BlockSpec((None, 1, 1, D_MODEL), exp3),
        ],
        out_specs=pl.BlockSpec((MOE_BLK, D_MODEL), row),
        scratch_shapes=[
            pltpu.VMEM((D_MODEL, d_e), BF16),
            pltpu.VMEM((D_MODEL, d_e), BF16),
            pltpu.VMEM((d_e, D_MODEL), BF16),
        ],
    )
    return pl.pallas_call(
        _moe_kernel,
        grid_spec=grid_spec,
        out_shape=jax.ShapeDtypeStruct((n_rows, D_MODEL), BF16),
        compiler_params=pltpu.CompilerParams(dimension_semantics=("arbitrary",),
                                             vmem_limit_bytes=MOE_VMEM_LIMIT),
        name="moe_blocks",
    )(meta, xb, w1, b1g, b1l, w2, b2)


def _route(top_i):
    n_tok = top_i.shape[0]
    e_flat = top_i.reshape(-1)
    n_rows = n_tok * TOP_K
    onehot = (e_flat[:, None] == jnp.arange(N_EXPERTS, dtype=jnp.int32)[None, :]).astype(jnp.int32)
    csum = jnp.cumsum(onehot, axis=0)
    counts = csum[-1]
    starts = jnp.cumsum(counts) - counts
    pcounts = (counts + MOE_BLK - 1) // MOE_BLK * MOE_BLK
    pends = jnp.cumsum(pcounts)
    pstarts = pends - pcounts
    dest = jnp.take_along_axis(csum + (pstarts - 1)[None, :], e_flat[:, None], axis=1)[:, 0]
    n_blk = n_rows // MOE_BLK + N_EXPERTS
    blk_start = jnp.arange(n_blk, dtype=jnp.int32) * MOE_BLK
    blk_exp = jnp.minimum(jnp.sum((blk_start[:, None] >= pends[None, :]).astype(jnp.int32), axis=1), N_EXPERTS - 1)
    order = jnp.argsort(e_flat)
    pos = jnp.arange(n_blk * MOE_BLK, dtype=jnp.int32)
    src = (pos + jnp.repeat((starts - pstarts)[blk_exp], MOE_BLK)) % n_rows
    row_tok = order[src].astype(jnp.int32) // TOP_K
    meta = jnp.concatenate([blk_exp, (pends[-1:] // MOE_BLK).astype(jnp.int32)])
    return meta, row_tok, dest.reshape(n_tok, TOP_K).T.reshape(-1)


def _combine_kernel(x_ref, yg_ref, gate_ref, mod_ref, oc_ref, ol_ref, *, n_ctx_tiles):
    gate = gate_ref[...]
    acc = gate[:, 0:1] * yg_ref[0].astype(F32)
    for j in range(1, TOP_K):
        acc = acc + gate[:, j:j + 1] * yg_ref[j].astype(F32)
    out = x_ref[...] + mod_ref[0, 5:6, :] * acc
    is_ctx = pl.program_id(0) < n_ctx_tiles

    @pl.when(is_ctx)
    def _():
        oc_ref[...] = out

    @pl.when(jnp.logical_not(is_ctx))
    def _():
        ol_ref[...] = out


def _combine(x1, yg, gates, mods, tile_mod, n_ctx_tiles):
    n_tok = x1.shape[0]
    n_ctx = n_ctx_tiles * TOK_TILE
    return pl.pallas_call(
        functools.partial(_combine_kernel, n_ctx_tiles=n_ctx_tiles),
        grid=(n_tok // TOK_TILE,),
        in_specs=[
            pl.BlockSpec((TOK_TILE, D_MODEL), lambda i: (i, 0)),
            pl.BlockSpec((TOP_K, TOK_TILE, D_MODEL), lambda i: (0, i, 0)),
            pl.BlockSpec((TOK_TILE, TOP_K), lambda i: (i, 0)),
            pl.BlockSpec((1, 6, D_MODEL), lambda i: (tile_mod(i), 0, 0)),
        ],
        out_specs=_group_tile_specs(D_MODEL, n_ctx_tiles),
        out_shape=[jax.ShapeDtypeStruct((n_ctx, D_MODEL), F32), jax.ShapeDtypeStruct((n_tok - n_ctx, D_MODEL), F32)],
        compiler_params=_cparams(("arbitrary",)),
        name="moe_combine",
    )(x1, yg, gates, mods)


def kernel(x_prompt, x_sample, c, cache_na_k, cache_na_v, cache_swa_k, cache_swa_v, state_rwkv, c_ctx, w_ada, b_ada, norm1_g, norm2_g, w_in, w_out, na_q_norm, na_k_norm, na_rpb, swa_q_norm, swa_k_norm, swa_sink, rk_conv, rk_w0, rk_w2, rk_a0, rk_a2, rk_g2, rk_k_k, rk_k_a, rk_r_k, rk_ln_g, rk_ln_b, moe_router_w, moe_router_b, moe_w1, moe_b1, moe_w2, moe_b2):
    bc, tc, _ = x_prompt.shape
    bl, tl, _ = x_sample.shape
    depth = w_in.shape[0]
    n_ctx = bc * tc
    n_lat = bl * tl
    assert tc == TOK_TILE and tl % TOK_TILE == 0 and n_ctx % tl == 0
    n_ctx_tiles = n_ctx // TOK_TILE
    tiles_per_seq = tl // TOK_TILE
    past = cache_na_k.shape[2]

    def tile_mod(i):
        return jnp.where(i < n_ctx_tiles, 0, 1 + (i - n_ctx_tiles) // tiles_per_seq)

    def tile_rope(i):
        return jnp.where(i < n_ctx_tiles, tiles_per_seq, (i - n_ctx_tiles) % tiles_per_seq)

    x = (x_prompt.reshape(n_ctx, D_MODEL), x_sample.reshape(n_lat, D_MODEL))

    n_mod = 1 + bl
    mod_rows = -(-n_mod // 8) * 8
    cvecs = jnp.concatenate([c_ctx[None, :], c, jnp.zeros((mod_rows - n_mod, D_MODEL), F32)], axis=0)
    mods_all = _ada_mod(cvecs, w_ada, b_ada).reshape(depth, mod_rows, 6, D_MODEL)
    cos_tab, sin_tab = _rope_tables(tl)
    tile2 = lambda g: jnp.concatenate([g, g])[None, :]
    pad_lanes = lambda z: jnp.pad(z, ((0, 0), (0, LANES - z.shape[1])))
    zeros_lora = jnp.zeros((2, RK_DECAY_LORA, RK_WIDTH), F32)

    na_k_l, na_v_l, sw_k_l, sw_v_l, st_l = [], [], [], [], []
    for l in range(depth):
        mods = mods_all[l]
        qk_gains = jnp.concatenate(
            [tile2(na_q_norm[l]), tile2(na_k_norm[l]), tile2(swa_q_norm[l]), tile2(swa_k_norm[l])], axis=0)
        p = {
            "rk_conv": rk_conv[l], "rk_w0": rk_w0[l], "rk_a0": rk_a0[l], "rk_g2": rk_g2[l],
            "rk_w2_pad": jnp.concatenate([rk_w2[l], zeros_lora], axis=1),
            "rk_a2_pad": jnp.concatenate([zeros_lora, rk_a2[l]], axis=1),
            "rk_k_k": rk_k_k[l][None, :], "rk_k_a": rk_k_a[l][None, :],
            "rk_r_k": rk_r_k[l].reshape(1, RK_WIDTH),
            "rk_ln_g": rk_ln_g[l][None, :], "rk_ln_b": rk_ln_b[l][None, :],
            "w_out_bf16": w_out[l].astype(BF16), "norm2_g": norm2_g[l][None, :],
            "router_w_pad": pad_lanes(moe_router_w[l]), "router_b_pad": pad_lanes(moe_router_b[l][None, :]),
        }

        att, u = _in_proj(x, norm1_g[l][None, :], mods, w_in[l].astype(BF16), qk_gains, cos_tab, sin_tab,
                          tile_mod, tile_rope, n_ctx_tiles)
        q, g, bonus = _rk_prep(u, p, n_ctx_tiles, tiles_per_seq)
        att_c = att[:n_ctx].reshape(bc, tc, ATT_COLS)
        att_by_ctx_len = att.reshape((n_ctx + n_lat) // tc, tc, ATT_COLS)
        att_by_lat_len = att.reshape((n_ctx + n_lat) // tl, tl, ATT_COLS)
        na_k_l.append(att_c[:, :, NA_WIDTH:2 * NA_WIDTH].reshape(bc, tc, NA_HEADS, HEAD_DIM))
        na_v_l.append(att_c[:, :, 2 * NA_WIDTH:NA_COLS].reshape(bc, tc, NA_HEADS, HEAD_DIM))
        sw_k_l.append(att_c[:, :, NA_COLS + SWA_WIDTH:NA_COLS + SWA_WIDTH + SWA_KV_WIDTH]
                      .reshape(bc, tc, SWA_KV_HEADS, HEAD_DIM))
        sw_v_l.append(att_c[:, :, NA_COLS + SWA_WIDTH + SWA_KV_WIDTH:].reshape(bc, tc, SWA_KV_HEADS, HEAD_DIM))

        sink = swa_sink[l]
        o_na = (_ctx_attn(att_by_ctx_len, bc, sink, gqa=False).reshape(n_ctx, NA_WIDTH),
                _na_latent(att_by_lat_len, n_ctx // tl, cache_na_k[:, l].reshape(bl, past, NA_WIDTH),
                           cache_na_v[:, l].reshape(bl, past, NA_WIDTH),
                           _na_bias_tables(na_rpb[l])).reshape(n_lat, NA_WIDTH))
        o_sw = (_ctx_attn(att_by_ctx_len, bc, sink, gqa=True).reshape(n_ctx, SWA_WIDTH),
                _swa_latent(att_by_lat_len, n_ctx // tl, cache_swa_k[:, l].reshape(bl, past, SWA_KV_WIDTH),
                            cache_swa_v[:, l].reshape(bl, past, SWA_KV_WIDTH), sink).reshape(n_lat, SWA_WIDTH))

        yf_c, yb_c, s_fin = _rwkv_group(q, 0, bc, tc, jnp.zeros((bc, 2, RK_NB, PAIR, PAIR), F32))
        yf_l, yb_l, _ = _rwkv_group(q, n_ctx // RK_CHUNK, bl, tl, _pair_states(state_rwkv[:, l]))
        st_l.append(_head_states(s_fin))

        x1, h2, gates, top_i = _out_proj(x, o_na, o_sw, (yf_c, yf_l), (yb_c, yb_l), bonus, g, p, mods, tile_mod,
                                         n_ctx_tiles)
        meta, row_tok, dest = _route(top_i)
        h2 = jnp.concatenate([h2, jnp.zeros((H2_PAD_ROWS - h2.shape[0], D_MODEL), BF16)], axis=0)
        yb = _moe_blocks(meta, h2[row_tok], moe_w1, moe_b1[:, :, None, 0::2], moe_b1[:, :, None, 1::2], moe_w2,
                         moe_b2[:, :, None, :], l)
        x = _combine(x1, yb[dest].reshape(TOP_K, n_ctx + n_lat, D_MODEL), gates, mods, tile_mod, n_ctx_tiles)

    y_p = x[0].reshape(bc, tc, D_MODEL)
    y_s = x[1].reshape(bl, tl, D_MODEL)
    return (y_p, y_s, jnp.stack(na_k_l, axis=1), jnp.stack(na_v_l, axis=1), jnp.stack(sw_k_l, axis=1),
            jnp.stack(sw_v_l, axis=1), jnp.stack(st_l, axis=1))
```

```python
import functools

import jax
import jax.numpy as jnp
from jax import lax
from jax.experimental import pallas as pl
from jax.experimental.pallas import tpu as pltpu

F32 = jnp.float32
BF16 = jnp.bfloat16

D_MODEL = 1024
HEAD_DIM = 64
LANES = 128
GRID_W = 64
NA_HEADS = 6
SWA_HEADS = 4
SWA_KV_HEADS = 2
RK_HEADS = 6
NA_WIDTH = NA_HEADS * HEAD_DIM
SWA_WIDTH = SWA_HEADS * HEAD_DIM
SWA_KV_WIDTH = SWA_KV_HEADS * HEAD_DIM
RK_WIDTH = RK_HEADS * HEAD_DIM
RK_DECAY_LORA = 64
RK_A_LORA = 64
RK_GATE_LORA = 128
RK_COLS = 3 * RK_WIDTH + RK_DECAY_LORA + RK_A_LORA + RK_GATE_LORA
NA_COLS = 3 * NA_WIDTH
SWA_COLS = SWA_WIDTH + 2 * SWA_KV_WIDTH
ATT_COLS = NA_COLS + SWA_COLS
IN_COLS = ATT_COLS + RK_COLS
NA_WIN_R = 8
NA_WIN_C = 16
SWA_WIN = 128
ROPE_THETA = 10000.0
ATTN_SCALE = HEAD_DIM ** -0.5
N_EXPERTS = 32
TOP_K = 4
SWIGLU_LIMIT = 7.0
SWIGLU_ALPHA = 1.702
MOE_BLK = 256
RMS_EPS = 1e-6
GN_EPS = 64e-5
NEG_BIG = -1e30

TOK_TILE = 256
VMEM_LIMIT = 48 * 1024 * 1024


def _cparams(sem):
    return pltpu.CompilerParams(dimension_semantics=sem, vmem_limit_bytes=VMEM_LIMIT)


def _dot(a, b):
    return jnp.dot(a, b, preferred_element_type=F32)


def _dot_nt(a, b):
    return lax.dot_general(a, b, (((1,), (1,)), ((), ())), preferred_element_type=F32)


def _split_bf16(x):
    hi = x.astype(BF16)
    lo = (x - hi.astype(F32)).astype(BF16)
    return hi, lo


def _dot3(a, b):
    ah, al = _split_bf16(a)
    bh, bl = _split_bf16(b)
    return _dot(ah, bh) + (_dot(ah, bl) + _dot(al, bh))


def _bmm_raw(a, b):
    return lax.dot_general(a, b, (((2,), (1,)), ((0,), (0,))), preferred_element_type=F32)


def _bmm(a, b):
    return _bmm_raw(a.astype(BF16), b.astype(BF16))


def _bmm_nt(a, b):
    return lax.dot_general(a.astype(BF16), b.astype(BF16), (((2,), (2,)), ((0,), (0,))),
                           preferred_element_type=F32)


def _bmm3(a, b):
    ah, al = _split_bf16(a)
    bh, bl = _split_bf16(b)
    return _bmm_raw(ah, bh) + (_bmm_raw(ah, bl) + _bmm_raw(al, bh))


def _lane_lo(shape):
    return lax.broadcasted_iota(jnp.int32, shape, len(shape) - 1) < HEAD_DIM


def _pair_sum(x):
    lo = _lane_lo(x.shape)
    s_lo = jnp.sum(jnp.where(lo, x, 0.0), axis=-1, keepdims=True)
    s_hi = jnp.sum(jnp.where(lo, 0.0, x), axis=-1, keepdims=True)
    return jnp.where(lo, s_lo, s_hi)


def _stack_heads(q):
    lo = _lane_lo(q.shape)
    return jnp.concatenate([jnp.where(lo, q, 0.0), jnp.where(lo, 0.0, q)], axis=0)


def _stack_heads3(x):
    lo = _lane_lo(x.shape)
    return jnp.concatenate([jnp.where(lo, x, 0.0), jnp.where(lo, 0.0, x)], axis=1)


def _unstack_heads(o2):
    n = o2.shape[0] // 2
    return jnp.where(_lane_lo((n, LANES)), o2[:n], o2[n:])


def _dup_head(x, j):
    keep = _lane_lo(x.shape) == (j == 0)
    return jnp.where(keep, x, pltpu.roll(x, HEAD_DIM, 1))


def _ada_kernel(c_ref, w_ref, b_ref, o_ref):
    cv = c_ref[...]
    s = cv * jax.nn.sigmoid(cv)
    o_ref[0] = _dot3(s, w_ref[0]) + b_ref[0]


def _ada_mod(cvecs, w_ada, b_ada):
    depth, _, n_out = w_ada.shape
    rows = cvecs.shape[0]
    tn = 1024
    return pl.pallas_call(
        _ada_kernel,
        grid=(depth, n_out // tn),
        in_specs=[
            pl.BlockSpec((rows, D_MODEL), lambda l, j: (0, 0)),
            pl.BlockSpec((1, D_MODEL, tn), lambda l, j: (l, 0, j)),
            pl.BlockSpec((1, 1, tn), lambda l, j: (l, 0, j)),
        ],
        out_specs=pl.BlockSpec((1, rows, tn), lambda l, j: (l, 0, j)),
        out_shape=jax.ShapeDtypeStruct((depth, rows, n_out), F32),
        compiler_params=_cparams(("parallel", "parallel")),
        name="ada_mod",
    )(cvecs, w_ada, b_ada.reshape(depth, 1, n_out))


NA_QK_BLOCKS = 2 * NA_WIDTH // LANES
SWA_Q_BLOCK0 = NA_COLS // LANES
SWA_QK_BLOCKS = (SWA_WIDTH + SWA_KV_WIDTH) // LANES


def _in_proj_kernel(xc_ref, xl_ref, g_ref, mod_ref, w_ref, qkg_ref, cos_ref, sin_ref, att_ref, u_ref, *, n_ctx_tiles):
    x = jnp.where(pl.program_id(0) < n_ctx_tiles, xc_ref[...], xl_ref[...])
    y = x * lax.rsqrt(jnp.mean(x * x, axis=-1, keepdims=True) + RMS_EPS)
    h = (y * g_ref[...]) * (1.0 + mod_ref[0, 1:2, :]) + mod_ref[0, 0:1, :]
    proj = _dot(h.astype(BF16), w_ref[...])
    u_ref[...] = proj[:, ATT_COLS:]

    def qk_norm(blk, gain):
        ms = _pair_sum(blk * blk) * (1.0 / HEAD_DIM)
        return blk * lax.rsqrt(ms + RMS_EPS) * gain

    lane = lax.broadcasted_iota(jnp.int32, (x.shape[0], LANES), 1)
    first = (lane % (HEAD_DIM // 2)) < (HEAD_DIM // 4)
    for cb in range(ATT_COLS // LANES):
        blk = proj[:, cb * LANES:(cb + 1) * LANES]
        if cb < NA_QK_BLOCKS:
            gi = 0 if cb < NA_QK_BLOCKS // 2 else 1
            blk = qk_norm(blk, qkg_ref[gi:gi + 1, :])
        elif SWA_Q_BLOCK0 <= cb < SWA_Q_BLOCK0 + SWA_QK_BLOCKS:
            gi = 2 if cb < SWA_Q_BLOCK0 + SWA_WIDTH // LANES else 3
            blk = qk_norm(blk, qkg_ref[gi:gi + 1, :])
            partner = jnp.where(first, pltpu.roll(blk, LANES - HEAD_DIM // 4, 1),
                                pltpu.roll(blk, HEAD_DIM // 4, 1))
            blk = blk * cos_ref[...] + partner * sin_ref[...]
        att_ref[:, cb * LANES:(cb + 1) * LANES] = blk


def _group_tile_specs(width, n_ctx_tiles):
    return [pl.BlockSpec((TOK_TILE, width), lambda i: (jnp.minimum(i, n_ctx_tiles - 1), 0)),
            pl.BlockSpec((TOK_TILE, width), lambda i: (jnp.maximum(i - n_ctx_tiles, 0), 0))]


def _in_proj(x, norm_g, mods, w_in_bf16, qk_gains, cos_tab, sin_tab, tile_mod, tile_rope, n_ctx_tiles):
    n_tok = x[0].shape[0] + x[1].shape[0]
    return pl.pallas_call(
        functools.partial(_in_proj_kernel, n_ctx_tiles=n_ctx_tiles),
        grid=(n_tok // TOK_TILE,),
        in_specs=[
            *_group_tile_specs(D_MODEL, n_ctx_tiles),
            pl.BlockSpec((1, D_MODEL), lambda i: (0, 0)),
            pl.BlockSpec((1, 6, D_MODEL), lambda i: (tile_mod(i), 0, 0)),
            pl.BlockSpec((D_MODEL, IN_COLS), lambda i: (0, 0)),
            pl.BlockSpec((4, LANES), lambda i: (0, 0)),
            pl.BlockSpec((TOK_TILE, LANES), lambda i: (tile_rope(i), 0)),
            pl.BlockSpec((TOK_TILE, LANES), lambda i: (tile_rope(i), 0)),
        ],
        out_specs=[
            pl.BlockSpec((TOK_TILE, ATT_COLS), lambda i: (i, 0)),
            pl.BlockSpec((TOK_TILE, RK_COLS), lambda i: (i, 0)),
        ],
        out_shape=[
            jax.ShapeDtypeStruct((n_tok, ATT_COLS), F32),
            jax.ShapeDtypeStruct((n_tok, RK_COLS), F32),
        ],
        compiler_params=_cparams(("parallel",)),
        name="in_proj",
    )(*x, norm_g, mods, w_in_bf16, qk_gains, cos_tab, sin_tab)


def _rope_tables(n_lat):
    nf = HEAD_DIM // 4
    t = jnp.arange(n_lat)
    lane = jnp.arange(LANES)
    d = lane % HEAD_DIM
    inv = ROPE_THETA ** (-(d % nf).astype(F32) / nf)
    pos = jnp.where((d // (2 * nf))[None, :] == 0, (t // GRID_W)[:, None], (t % GRID_W)[:, None]).astype(F32)
    ang = pos * inv[None, :]
    sign = jnp.where((d % (2 * nf)) < nf, -1.0, 1.0).astype(F32)
    cos = jnp.concatenate([jnp.cos(ang), jnp.ones((TOK_TILE, LANES), F32)], 0)
    sin = jnp.concatenate([jnp.sin(ang) * sign[None, :], jnp.zeros((TOK_TILE, LANES), F32)], 0)
    return cos, sin


def _ctx_attn_kernel(sink_ref, q_ref, k_ref, v_ref, o_ref, *, gqa):
    j = pl.program_id(1)
    k = k_ref[0]
    v = v_ref[0]
    if gqa:
        k = _dup_head(k, j)
        v = _dup_head(v, j)
    n = k.shape[0]
    q2 = _stack_heads(q_ref[0]).astype(BF16)
    s = _dot_nt(q2, k.astype(BF16)) * ATTN_SCALE
    m = jnp.max(s, axis=-1, keepdims=True)
    if gqa:
        row = lax.broadcasted_iota(jnp.int32, (2 * n, 1), 0)
        snk = jnp.where(row < n, sink_ref[2 * j], sink_ref[2 * j + 1])
        m = jnp.maximum(m, snk)
    p = jnp.exp(s - m)
    den = jnp.sum(p, axis=-1, keepdims=True)
    if gqa:
        den = den + jnp.exp(snk - m)
    o2 = _dot(p.astype(BF16), v.astype(BF16)) / den
    o_ref[0] = _unstack_heads(o2)


def _ctx_attn(att, b, sink, *, gqa):
    t = att.shape[1]
    if gqa:
        nq = SWA_WIDTH // LANES
        qb, kb, vb = SWA_Q_BLOCK0, SWA_Q_BLOCK0 + nq, SWA_Q_BLOCK0 + nq + 1
        kmap = lambda bi, j: (bi, 0, kb)
        vmap = lambda bi, j: (bi, 0, vb)
    else:
        nq = NA_WIDTH // LANES
        qb, kb, vb = 0, nq, 2 * nq
        kmap = lambda bi, j: (bi, 0, kb + j)
        vmap = lambda bi, j: (bi, 0, vb + j)
    return pl.pallas_call(
        functools.partial(_ctx_attn_kernel, gqa=gqa),
        grid=(b, nq),
        in_specs=[
            pl.BlockSpec(memory_space=pltpu.SMEM),
            pl.BlockSpec((1, t, LANES), lambda bi, j: (bi, 0, qb + j)),
            pl.BlockSpec((1, t, LANES), kmap),
            pl.BlockSpec((1, t, LANES), vmap),
        ],
        out_specs=pl.BlockSpec((1, t, LANES), lambda bi, j: (bi, 0, j)),
        out_shape=jax.ShapeDtypeStruct((b, t, nq * LANES), F32),
        compiler_params=_cparams(("parallel", "parallel")),
        name="ctx_attn_swa" if gqa else "ctx_attn_na",
    )(sink, att, att, att)


NA_ROWS_PER_ITER = 8


def _na_lat_kernel(q_ref, k_ref, v_ref, kc_ref, vc_ref, tab_ref, o_ref, kb_ref, vb_ref):
    n = q_ref.shape[1]
    rows = n // GRID_W
    win = NA_WIN_R * GRID_W
    kb_ref[...] = k_ref[0].astype(BF16)
    vb_ref[...] = v_ref[0].astype(BF16)
    kc = kc_ref[0].astype(BF16)
    vc = vc_ref[0].astype(BF16)

    def row_group(ig, carry):
        nr = NA_ROWS_PER_ITER
        g0 = pl.multiple_of(ig * (nr * GRID_W), nr * GRID_W)
        q2 = _stack_heads3(q_ref[0, pl.ds(g0, nr * GRID_W), :].reshape(nr, GRID_W, LANES)).astype(BF16)
        kws, vws, biases = [], [], []
        for r in range(nr):
            i = ig * nr + r
            start = jnp.clip(i - NA_WIN_R // 2, 0, rows - NA_WIN_R)
            k0 = pl.multiple_of(start * GRID_W, GRID_W)
            kws.append(kb_ref[pl.ds(k0, win), :])
            vws.append(vb_ref[pl.ds(k0, win), :])
            biases.append(tab_ref[0, start - i + (NA_WIN_R - 1)])
        s_loc = _bmm_nt(q2, jnp.stack(kws)) * ATTN_SCALE + jnp.stack(biases)
        s_ctx = _dot_nt(q2.reshape(nr * 2 * GRID_W, LANES), kc).reshape(nr, 2 * GRID_W, -1) * ATTN_SCALE
        m = jnp.maximum(jnp.max(s_loc, axis=-1, keepdims=True), jnp.max(s_ctx, axis=-1, keepdims=True))
        p_loc = jnp.exp(s_loc - m)
        p_ctx = jnp.exp(s_ctx - m)
        den = jnp.sum(p_loc, axis=-1, keepdims=True) + jnp.sum(p_ctx, axis=-1, keepdims=True)
        o_ctx = _dot(p_ctx.reshape(nr * 2 * GRID_W, -1).astype(BF16), vc).reshape(nr, 2 * GRID_W, LANES)
        o2 = (_bmm(p_loc, jnp.stack(vws)) + o_ctx) / den
        out = jnp.where(_lane_lo((nr, GRID_W, LANES)), o2[:, :GRID_W], o2[:, GRID_W:])
        o_ref[0, pl.ds(g0, nr * GRID_W), :] = out.reshape(nr * GRID_W, LANES)
        return carry

    lax.fori_loop(0, rows // NA_ROWS_PER_ITER, row_group, 0)


def _na_bias_tables(rpb):
    col = jnp.arange(GRID_W)
    cstart = jnp.clip(col - NA_WIN_C // 2, 0, GRID_W - NA_WIN_C)
    col_mask = (col[None, :] >= cstart[:, None]) & (col[None, :] < cstart[:, None] + NA_WIN_C)
    col_idx = jnp.clip(col[None, :] - col[:, None] + NA_WIN_C - 1, 0, 2 * NA_WIN_C - 2)
    rpb_cols = jnp.where(col_mask[None, None], rpb[:, :, col_idx], NEG_BIG)
    roff = jnp.arange(NA_WIN_R)[:, None] + jnp.arange(NA_WIN_R)[None, :]
    t = rpb_cols[:, roff]
    t = jnp.transpose(t, (0, 1, 3, 2, 4)).reshape(NA_HEADS // 2, 2, NA_WIN_R, GRID_W, NA_WIN_R * GRID_W)
    return jnp.transpose(t, (0, 2, 1, 3, 4)).reshape(NA_HEADS // 2, NA_WIN_R, 2 * GRID_W, NA_WIN_R * GRID_W)


def _na_latent(att, s0, kc, vc, tab):
    n = att.shape[1]
    b, p, _ = kc.shape
    nq = NA_WIDTH // LANES
    return pl.pallas_call(
        _na_lat_kernel,
        grid=(b, nq),
        in_specs=[
            pl.BlockSpec((1, n, LANES), lambda bi, j: (s0 + bi, 0, j)),
            pl.BlockSpec((1, n, LANES), lambda bi, j: (s0 + bi, 0, nq + j)),
            pl.BlockSpec((1, n, LANES), lambda bi, j: (s0 + bi, 0, 2 * nq + j)),
            pl.BlockSpec((1, p, LANES), lambda bi, j: (bi, 0, j)),
            pl.BlockSpec((1, p, LANES), lambda bi, j: (bi, 0, j)),
            pl.BlockSpec((1, NA_WIN_R, 2 * GRID_W, NA_WIN_R * GRID_W), lambda bi, j: (j, 0, 0, 0)),
        ],
        out_specs=pl.BlockSpec((1, n, LANES), lambda bi, j: (bi, 0, j)),
        out_shape=jax.ShapeDtypeStruct((b, n, NA_WIDTH), F32),
        scratch_shapes=[pltpu.VMEM((n, LANES), BF16), pltpu.VMEM((n, LANES), BF16)],
        compiler_params=_cparams(("parallel", "parallel")),
        name="na_latent",
    )(att, att, att, kc, vc, tab)


SWA_BLOCKS_PER_ITER = 4


def _swa_lat_kernel(sink_ref, q_ref, k_ref, v_ref, kc_ref, vc_ref, o_ref, kb_ref, vb_ref):
    j = pl.program_id(1)
    n = q_ref.shape[1]
    blk = SWA_WIN
    span = 3 * blk
    kb_ref[...] = _dup_head(k_ref[0], j).astype(BF16)
    vb_ref[...] = _dup_head(v_ref[0], j).astype(BF16)
    kc = _dup_head(kc_ref[0], j).astype(BF16)
    vc = _dup_head(vc_ref[0], j).astype(BF16)
    row = lax.broadcasted_iota(jnp.int32, (2 * blk, 1), 0)
    snk = jnp.where(row < blk, sink_ref[2 * j], sink_ref[2 * j + 1])
    qoff = lax.broadcasted_iota(jnp.int32, (2 * blk, span), 0) % blk
    koff = lax.broadcasted_iota(jnp.int32, (2 * blk, span), 1)

    def q_group(qg, carry):
        nr = SWA_BLOCKS_PER_ITER
        g0 = pl.multiple_of(qg * (nr * blk), nr * blk)
        q2 = _stack_heads3(q_ref[0, pl.ds(g0, nr * blk), :].reshape(nr, blk, LANES)).astype(BF16)
        kws, vws, valids = [], [], []
        for r in range(nr):
            q0 = g0 + r * blk
            w0 = pl.multiple_of(jnp.clip(q0 - blk, 0, n - span), blk)
            kws.append(kb_ref[pl.ds(w0, span), :])
            vws.append(vb_ref[pl.ds(w0, span), :])
            valids.append(jnp.abs((q0 + qoff) - (w0 + koff)) <= SWA_WIN)
        s_loc = jnp.where(jnp.stack(valids), _bmm_nt(q2, jnp.stack(kws)) * ATTN_SCALE, NEG_BIG)
        s_ctx = _dot_nt(q2.reshape(nr * 2 * blk, LANES), kc).reshape(nr, 2 * blk, -1) * ATTN_SCALE
        m = jnp.maximum(jnp.max(s_loc, axis=-1, keepdims=True), jnp.max(s_ctx, axis=-1, keepdims=True))
        m = jnp.maximum(m, snk)
        p_loc = jnp.exp(s_loc - m)
        p_ctx = jnp.exp(s_ctx - m)
        den = (jnp.sum(p_loc, axis=-1, keepdims=True) + jnp.sum(p_ctx, axis=-1, keepdims=True)
               + jnp.exp(snk - m))
        o_ctx = _dot(p_ctx.reshape(nr * 2 * blk, -1).astype(BF16), vc).reshape(nr, 2 * blk, LANES)
        o2 = (_bmm(p_loc, jnp.stack(vws)) + o_ctx) / den
        out = jnp.where(_lane_lo((nr, blk, LANES)), o2[:, :blk], o2[:, blk:])
        o_ref[0, pl.ds(g0, nr * blk), :] = out.reshape(nr * blk, LANES)
        return carry

    lax.fori_loop(0, n // (SWA_BLOCKS_PER_ITER * blk), q_group, 0)


def _swa_latent(att, s0, kc, vc, sink):
    n = att.shape[1]
    b, p, _ = kc.shape
    nq = SWA_WIDTH // LANES
    qb, kb, vb = SWA_Q_BLOCK0, SWA_Q_BLOCK0 + nq, SWA_Q_BLOCK0 + nq + 1
    return pl.pallas_call(
        _swa_lat_kernel,
        grid=(b, nq),
        in_specs=[
            pl.BlockSpec(memory_space=pltpu.SMEM),
            pl.BlockSpec((1, n, LANES), lambda bi, j: (s0 + bi, 0, qb + j)),
            pl.BlockSpec((1, n, LANES), lambda bi, j: (s0 + bi, 0, kb)),
            pl.BlockSpec((1, n, LANES), lambda bi, j: (s0 + bi, 0, vb)),
            pl.BlockSpec((1, p, LANES), lambda bi, j: (bi, 0, 0)),
            pl.BlockSpec((1, p, LANES), lambda bi, j: (bi, 0, 0)),
        ],
        out_specs=pl.BlockSpec((1, n, LANES), lambda bi, j: (bi, 0, j)),
        out_shape=jax.ShapeDtypeStruct((b, n, SWA_WIDTH), F32),
        scratch_shapes=[pltpu.VMEM((n, LANES), BF16), pltpu.VMEM((n, LANES), BF16)],
        compiler_params=_cparams(("parallel", "parallel")),
        name="swa_latent",
    )(sink, att, att, att, kc, vc)


RK_NB = RK_WIDTH // LANES
LORA_BLOCK = 3 * RK_WIDTH // LANES
GATE_BLOCK = LORA_BLOCK + 1
Q_R, Q_V, Q_A, Q_W, Q_K, Q_B = range(6)
Q_DIR = 3
Q_COLS = (6 + Q_DIR) * RK_WIDTH


def _softplus(x):
    return jnp.maximum(x, 0.0) + jnp.log(1.0 + jnp.exp(-jnp.abs(x)))


def _rk_prep_kernel(u_ref, up_ref, un_ref, cw_ref, w0_ref, w2_ref, a0_ref, a2_ref, g2_ref, kk_ref, ka_ref,
                    rk_ref, q_ref, g_ref, bonus_ref, *, n_ctx_tiles, tiles_per_seq):
    def put(slot, val):
        q_ref[:, slot * RK_WIDTH:(slot + 1) * RK_WIDTH] = val

    i = pl.program_id(0)
    li = i - n_ctx_tiles
    is_lat = i >= n_ctx_tiles
    has_prev = jnp.logical_and(is_lat, li % tiles_per_seq != 0)
    has_next = jnp.logical_and(is_lat, li % tiles_per_seq != tiles_per_seq - 1)
    u = u_ref[...]
    tm = u.shape[0]
    prev_row = jnp.where(has_prev, up_ref[7:8, :], 0.0)
    next_row = jnp.where(has_next, un_ref[0:1, :], 0.0)
    row = lax.broadcasted_iota(jnp.int32, u.shape, 0)
    um = jnp.where(row == 0, prev_row, pltpu.roll(u, 1, 0))
    up = jnp.where(row == tm - 1, next_row, pltpu.roll(u, tm - 1, 0))
    u = um * cw_ref[0:1, :] + u * cw_ref[1:2, :] + up * cw_ref[2:3, :]

    r = u[:, 0:RK_WIDTH]
    k = u[:, RK_WIDTH:2 * RK_WIDTH]
    v = u[:, 2 * RK_WIDTH:3 * RK_WIDTH]
    lora = u[:, LORA_BLOCK * LANES:(LORA_BLOCK + 1) * LANES]
    gl = u[:, GATE_BLOCK * LANES:(GATE_BLOCK + 1) * LANES]
    put(Q_R, r)
    put(Q_V, v)
    g_ref[...] = _dot3(jax.nn.sigmoid(gl), g2_ref[...])

    kn = k * kk_ref[...]
    kk = jnp.concatenate(
        [kn[:, c * LANES:(c + 1) * LANES]
         * lax.rsqrt(jnp.maximum(_pair_sum(jnp.square(kn[:, c * LANES:(c + 1) * LANES])), 1e-24))
         for c in range(RK_NB)], axis=1)
    put(Q_A, -kk)

    lora_t = jnp.tanh(lora)
    kd_sum = None
    for d in range(2):
        w = -_softplus(-(w0_ref[d:d + 1, :] + _dot3(lora_t, w2_ref[d]))) - 0.5
        put(Q_W + Q_DIR * d, -jnp.exp(w))
        a = jax.nn.sigmoid(a0_ref[d:d + 1, :] + _dot3(lora, a2_ref[d]))
        kd = k * (1.0 + (a - 1.0) * ka_ref[...])
        put(Q_K + Q_DIR * d, kd)
        put(Q_B + Q_DIR * d, kk * a)
        kd_sum = kd if kd_sum is None else kd_sum + kd

    t = r * kd_sum * rk_ref[...]
    bonus_ref[...] = jnp.concatenate(
        [_pair_sum(t[:, c * LANES:(c + 1) * LANES]) for c in range(RK_NB)], axis=1) * v


def _rk_prep(u, p, n_ctx_tiles, tiles_per_seq):
    n_tok = u.shape[0]
    n_tiles = n_tok // TOK_TILE
    sub = TOK_TILE // 8
    last8 = n_tok // 8 - 1
    tok = lambda i: (i, 0)
    const2 = lambda i: (0, 0)
    const3 = lambda i: (0, 0, 0)
    one = jax.ShapeDtypeStruct((n_tok, RK_WIDTH), F32)
    tok_spec = pl.BlockSpec((TOK_TILE, RK_WIDTH), tok)
    return pl.pallas_call(
        functools.partial(_rk_prep_kernel, n_ctx_tiles=n_ctx_tiles, tiles_per_seq=tiles_per_seq),
        grid=(n_tiles,),
        in_specs=[
            pl.BlockSpec((TOK_TILE, RK_COLS), tok),
            pl.BlockSpec((8, RK_COLS), lambda i: (jnp.maximum(i * sub - 1, 0), 0)),
            pl.BlockSpec((8, RK_COLS), lambda i: (jnp.minimum((i + 1) * sub, last8), 0)),
            pl.BlockSpec((3, RK_COLS), const2),
            pl.BlockSpec((2, RK_WIDTH), const2),
            pl.BlockSpec((2, LANES, RK_WIDTH), const3),
            pl.BlockSpec((2, RK_WIDTH), const2),
            pl.BlockSpec((2, LANES, RK_WIDTH), const3),
            pl.BlockSpec((RK_GATE_LORA, RK_WIDTH), const2),
            pl.BlockSpec((1, RK_WIDTH), const2),
            pl.BlockSpec((1, RK_WIDTH), const2),
            pl.BlockSpec((1, RK_WIDTH), const2),
        ],
        out_specs=[pl.BlockSpec((TOK_TILE, Q_COLS), tok), tok_spec, tok_spec],
        out_shape=[jax.ShapeDtypeStruct((n_tok, Q_COLS), F32), one, one],
        compiler_params=_cparams(("parallel",)),
        name="rk_prep",
    )(u, u, u, p["rk_conv"], p["rk_w0"], p["rk_w2_pad"], p["rk_a0"], p["rk_a2_pad"], p["rk_g2"],
      p["rk_k_k"], p["rk_k_a"], p["rk_r_k"])


RK_CHUNK = 64
PAIR = 2 * HEAD_DIM
STATE_SEQS = 8
RK_STEP_CHUNKS = 4


def _split3_bf16(x):
    hi = x.astype(BF16)
    r1 = x - hi.astype(F32)
    mid = r1.astype(BF16)
    return hi, mid, (r1 - mid.astype(F32)).astype(BF16)


def _pack_pair(m):
    return jnp.concatenate([m[:HEAD_DIM, :HEAD_DIM], m[HEAD_DIM:, HEAD_DIM:]], axis=1)


def _unpack_pairs(m):
    lo = _lane_lo(m.shape)
    return jnp.concatenate([jnp.where(lo, m, 0.0), jnp.where(lo, 0.0, m)], axis=1)


def _rk_chunk_kernel(q_ref, rbar_ref, ybar_ref, phi_ref, psi_ref):
    c = RK_CHUNK
    n = 2 * c
    nd = 2 * RK_NB
    nu = RK_STEP_CHUNKS * nd

    def tiles(slot, per_dir):
        cols = [(slot + (Q_DIR * d if per_dir else 0)) * RK_WIDTH + p * LANES
                for d in range(2) for p in range(RK_NB)]
        return jnp.stack([q_ref[ck * c:(ck + 1) * c, lo:lo + LANES] for ck in range(RK_STEP_CHUNKS) for lo in cols])

    r, v, a = tiles(Q_R, False), tiles(Q_V, False), tiles(Q_A, False)
    lw, k, b = tiles(Q_W, True), tiles(Q_K, True), tiles(Q_B, True)
    unit = lax.broadcasted_iota(jnp.int32, (nu, 1, 1), 0)
    sgn = jnp.ones((nu, 1, 1), jnp.int32)
    for ck in range(RK_STEP_CHUNKS):
        sgn = jnp.where(jnp.logical_and(unit >= ck * nd + RK_NB, unit < (ck + 1) * nd), -1, sgn)
    bwd = sgn < 0
    tdiff = lax.broadcasted_iota(jnp.int32, (1, c, c), 2) - lax.broadcasted_iota(jnp.int32, (1, c, c), 1)
    tri = jnp.where(tdiff * sgn <= 0, 1.0, 0.0)
    cum = sum(_bmm(tri, part) for part in _split3_bf16(lw))
    tot = jnp.where(bwd, cum[:, 0:1], cum[:, c - 1:c])
    a_t = a * jnp.exp(cum - lw)
    r_t = r * jnp.exp(cum)
    e_neg = jnp.exp(-cum)
    e_end = jnp.exp(tot - cum)
    g = _bmm_nt(jnp.concatenate([_stack_heads3(a_t), _stack_heads3(r_t)], axis=1),
                jnp.concatenate([_stack_heads3(b * e_neg), _stack_heads3(k * e_neg)], axis=1))
    r2 = lax.broadcasted_iota(jnp.int32, (1, n, n), 1)
    c2 = lax.broadcasted_iota(jnp.int32, (1, n, n), 2)
    order = (jnp.bitwise_and(c2, c - 1) - jnp.bitwise_and(r2, c - 1)) * sgn
    eye = jnp.where(r2 == c2, 1.0, 0.0)
    l_ab = jnp.where(order < 0, g[:, :n, :n], 0.0)
    l_ak = jnp.where(order < 0, g[:, :n, n:], 0.0)
    m_rb = jnp.where(order <= 0, g[:, n:, :n], 0.0)
    m_rk = jnp.where(order <= 0, g[:, n:, n:], 0.0)
    t_inv = eye + l_ab
    pw = l_ab
    for _ in range(5):
        pw = _bmm(pw, pw)
        t_inv = t_inv + _bmm(t_inv, pw)
    sv = _stack_heads3(v)
    au = _bmm(t_inv, jnp.concatenate([_stack_heads3(a_t), _bmm(l_ak, sv)], axis=2))
    ry = _bmm(m_rb, au) + jnp.concatenate([_stack_heads3(r_t), _bmm(m_rk, sv)], axis=2)
    ry = ry[:, :c] + ry[:, c:]
    bt = jnp.swapaxes(_stack_heads3(b * e_end), 1, 2)
    kt = jnp.swapaxes(_stack_heads3(k * e_end), 1, 2)
    pp = _bmm(bt, au)
    phi = eye * jnp.exp(tot) + pp[:, :, :PAIR]
    psi = pp[:, :, PAIR:] + _bmm(kt, sv)
    for ck in range(RK_STEP_CHUNKS):
        for d in range(2):
            for p in range(RK_NB):
                u = ck * nd + d * RK_NB + p
                rbar_ref[d, ck * c:(ck + 1) * c, p * LANES:(p + 1) * LANES] = ry[u, :, :PAIR]
                ybar_ref[d, ck * c:(ck + 1) * c, p * LANES:(p + 1) * LANES] = ry[u, :, PAIR:]
                phi_ref[d, ck, p] = _pack_pair(phi[u])
                psi_ref[d, ck, p] = _pack_pair(psi[u])


def _rk_chunk(q, tile0, n_seq, t):
    nc = t // RK_CHUNK
    sc = RK_STEP_CHUNKS
    assert nc % sc == 0 and tile0 % sc == 0
    row_sh = jax.ShapeDtypeStruct((2, n_seq, t, RK_WIDTH), F32)
    mat_sh = jax.ShapeDtypeStruct((2, n_seq, nc, RK_NB, HEAD_DIM, PAIR), F32)
    row_spec = pl.BlockSpec((2, None, sc * RK_CHUNK, RK_WIDTH), lambda s, c: (0, s, c, 0))
    mat_spec = pl.BlockSpec((2, None, sc, RK_NB, HEAD_DIM, PAIR), lambda s, c: (0, s, c, 0, 0, 0))
    return pl.pallas_call(
        _rk_chunk_kernel,
        grid=(n_seq, nc // sc),
        in_specs=[pl.BlockSpec((sc * RK_CHUNK, Q_COLS), lambda s, c: ((tile0 + s * nc) // sc + c, 0))],
        out_specs=[row_spec, row_spec, mat_spec, mat_spec],
        out_shape=[row_sh, row_sh, mat_sh, mat_sh],
        compiler_params=_cparams(("parallel", "parallel")),
        name="rk_chunk",
    )(q)


def _rk_state_kernel(rf_ref, rb_ref, yf_ref, yb_ref, phf_ref, phb_ref, psf_ref, psb_ref, s0_ref,
                     of_ref, ob_ref, s_ref):
    @pl.when(pl.program_id(1) == 0)
    def _():
        s_ref[...] = s0_ref[...]

    ns = s_ref.shape[0]
    nd = 2 * RK_NB

    def pair_tiles(ref_f, ref_b):
        tiles = [ref[:, :, p * LANES:(p + 1) * LANES] for ref in (ref_f, ref_b) for p in range(RK_NB)]
        return jnp.stack(tiles, axis=1).reshape(ns * nd, RK_CHUNK, LANES)

    def mats(ref_f, ref_b):
        return _unpack_pairs(jnp.concatenate([ref_f[...], ref_b[...]], axis=1).reshape(ns * nd, HEAD_DIM, PAIR))

    h = s_ref[...].reshape(ns * nd, PAIR, PAIR)
    y = (_bmm3(pair_tiles(rf_ref, rb_ref), h) + pair_tiles(yf_ref, yb_ref)).reshape(ns, nd, RK_CHUNK, LANES)
    s_ref[...] = (_bmm3(mats(phf_ref, phb_ref), h) + mats(psf_ref, psb_ref)).reshape(ns, 2, RK_NB, PAIR, PAIR)
    of_ref[...] = jnp.concatenate([y[:, p] for p in range(RK_NB)], axis=2)
    ob_ref[...] = jnp.concatenate([y[:, RK_NB + p] for p in range(RK_NB)], axis=2)


def _rk_state(rbar, ybar, phi, psi, s0):
    _, n_seq, t, _ = rbar.shape
    nc = t // RK_CHUNK
    sg = STATE_SEQS
    row_blk = (None, sg, RK_CHUNK, RK_WIDTH)
    mat_blk = (None, sg, None, RK_NB, HEAD_DIM, PAIR)
    fwd_row = pl.BlockSpec(row_blk, lambda g, c: (0, g, c, 0))
    bwd_row = pl.BlockSpec(row_blk, lambda g, c: (1, g, nc - 1 - c, 0))
    fwd_mat = pl.BlockSpec(mat_blk, lambda g, c: (0, g, c, 0, 0, 0))
    bwd_mat = pl.BlockSpec(mat_blk, lambda g, c: (1, g, nc - 1 - c, 0, 0, 0))
    st = pl.BlockSpec((sg, 2, RK_NB, PAIR, PAIR), lambda g, c: (g, 0, 0, 0, 0))
    out_sh = jax.ShapeDtypeStruct((n_seq, t, RK_WIDTH), F32)
    return pl.pallas_call(
        _rk_state_kernel,
        grid=(n_seq // sg, nc),
        in_specs=[fwd_row, bwd_row, fwd_row, bwd_row, fwd_mat, bwd_mat, fwd_mat, bwd_mat, st],
        out_specs=[pl.BlockSpec((sg, RK_CHUNK, RK_WIDTH), lambda g, c: (g, c, 0)),
                   pl.BlockSpec((sg, RK_CHUNK, RK_WIDTH), lambda g, c: (g, nc - 1 - c, 0)), st],
        out_shape=[out_sh, out_sh, jax.ShapeDtypeStruct((n_seq, 2, RK_NB, PAIR, PAIR), F32)],
        compiler_params=_cparams(("parallel", "arbitrary")),
        name="rk_state",
    )(rbar, rbar, ybar, ybar, phi, phi, psi, psi, s0)


def _pair_states(s):
    bsz = s.shape[0]
    h = jnp.swapaxes(s, -1, -2).reshape(bsz, 2, RK_NB, 2, HEAD_DIM, HEAD_DIM)
    return jnp.einsum("bdphkv,hg->bdphkgv", h, jnp.eye(2, dtype=F32)).reshape(bsz, 2, RK_NB, PAIR, PAIR)


def _head_states(s):
    bsz = s.shape[0]
    h = jnp.stack([s[..., :HEAD_DIM, :HEAD_DIM], s[..., HEAD_DIM:, HEAD_DIM:]], axis=3)
    return jnp.swapaxes(h.reshape(bsz, 2, RK_HEADS, HEAD_DIM, HEAD_DIM), -1, -2)


def _rwkv_group(q, tile0, n_seq, t, s0):
    rbar, ybar, phi, psi = _rk_chunk(q, tile0, n_seq, t)
    y_f, y_b, s_fin = _rk_state(rbar, ybar, phi, psi, s0)
    return y_f.reshape(n_seq * t, RK_WIDTH), y_b.reshape(n_seq * t, RK_WIDTH), s_fin


def _out_proj_kernel(x_c, x_l, ona_c, ona_l, osw_c, osw_l, yf_c, yf_l, yb_c, yb_l, bonus_ref, g_ref, lng_ref, lnb_ref,
                     w_ref, mod_ref, n2_ref, rw_ref, rb_ref, x1_ref, h2_ref, gate_ref, top_ref, *, n_ctx_tiles):
    is_ctx = pl.program_id(0) < n_ctx_tiles
    pick = lambda c_ref, l_ref: jnp.where(is_ctx, c_ref[...], l_ref[...])
    ona = pick(ona_c, ona_l)
    osw = pick(osw_c, osw_l)
    y = pick(yf_c, yf_l) + pick(yb_c, yb_l)
    parts = []
    for c in range(RK_NB):
        yc = y[:, c * LANES:(c + 1) * LANES]
        dc = yc - _pair_sum(yc) * (1.0 / HEAD_DIM)
        var = _pair_sum(dc * dc) * (1.0 / HEAD_DIM)
        parts.append(dc * lax.rsqrt(var + GN_EPS))
    yn = jnp.concatenate(parts, axis=1) * lng_ref[...] + lnb_ref[...]
    o_rk = (yn + bonus_ref[...]) * g_ref[...]
    o = (_dot(ona.astype(BF16), w_ref[0:NA_WIDTH, :])
         + _dot(osw.astype(BF16), w_ref[NA_WIDTH:NA_WIDTH + SWA_WIDTH, :])
         + _dot(o_rk.astype(BF16), w_ref[NA_WIDTH + SWA_WIDTH:, :]))
    x1 = pick(x_c, x_l) + mod_ref[0, 2:3, :] * o
    x1_ref[...] = x1
    yn2 = x1 * lax.rsqrt(jnp.mean(x1 * x1, axis=-1, keepdims=True) + RMS_EPS)
    h2 = (yn2 * n2_ref[...]) * (1.0 + mod_ref[0, 4:5, :]) + mod_ref[0, 3:4, :]
    h2_ref[...] = h2.astype(BF16)
    lane = lax.broadcasted_iota(jnp.int32, (h2.shape[0], LANES), 1).astype(F32)
    logit = jnp.where(lane < N_EXPERTS, _dot3(h2, rw_ref[...]) + rb_ref[...], -jnp.inf)
    vals, idxs = [], []
    for _ in range(TOP_K):
        best = jnp.max(logit, axis=-1, keepdims=True)
        idx = jnp.min(jnp.where(logit == best, lane, float(LANES)), axis=-1, keepdims=True)
        vals.append(best)
        idxs.append(idx)
        logit = jnp.where(lane == idx, -jnp.inf, logit)
    e = jnp.exp(jnp.concatenate(vals, axis=1) - vals[0])
    gate_ref[...] = e / jnp.sum(e, axis=-1, keepdims=True)
    top_ref[...] = jnp.concatenate(idxs, axis=1).astype(jnp.int32)


def _out_proj(x, o_na, o_sw, y_f, y_b, bonus, g, p, mods, tile_mod, n_ctx_tiles):
    n_tok = x[0].shape[0] + x[1].shape[0]
    tok = lambda i: (i, 0)
    const = lambda i: (0, 0)
    pair = lambda w: _group_tile_specs(w, n_ctx_tiles)
    return pl.pallas_call(
        functools.partial(_out_proj_kernel, n_ctx_tiles=n_ctx_tiles),
        grid=(n_tok // TOK_TILE,),
        in_specs=[
            *pair(D_MODEL), *pair(NA_WIDTH), *pair(SWA_WIDTH), *pair(RK_WIDTH), *pair(RK_WIDTH),
            pl.BlockSpec((TOK_TILE, RK_WIDTH), tok),
            pl.BlockSpec((TOK_TILE, RK_WIDTH), tok),
            pl.BlockSpec((1, RK_WIDTH), const),
            pl.BlockSpec((1, RK_WIDTH), const),
            pl.BlockSpec((D_MODEL, D_MODEL), const),
            pl.BlockSpec((1, 6, D_MODEL), lambda i: (tile_mod(i), 0, 0)),
            pl.BlockSpec((1, D_MODEL), const),
            pl.BlockSpec((D_MODEL, LANES), const),
            pl.BlockSpec((1, LANES), const),
        ],
        out_specs=[
            pl.BlockSpec((TOK_TILE, D_MODEL), tok),
            pl.BlockSpec((TOK_TILE, D_MODEL), tok),
            pl.BlockSpec((TOK_TILE, TOP_K), tok),
            pl.BlockSpec((TOK_TILE, TOP_K), tok),
        ],
        out_shape=[
            jax.ShapeDtypeStruct((n_tok, D_MODEL), F32),
            jax.ShapeDtypeStruct((n_tok, D_MODEL), BF16),
            jax.ShapeDtypeStruct((n_tok, TOP_K), F32),
            jax.ShapeDtypeStruct((n_tok, TOP_K), jnp.int32),
        ],
        compiler_params=_cparams(("parallel",)),
        name="out_proj",
    )(*x, *o_na, *o_sw, *y_f, *y_b, bonus, g, p["rk_ln_g"], p["rk_ln_b"], p["w_out_bf16"], mods, p["norm2_g"],
      p["router_w_pad"], p["router_b_pad"])


H2_PAD_ROWS = 32768
W1_SEL_COLS = 256
MOE_VMEM_LIMIT = 56 * 1024 * 1024


def _moe_kernel(meta_ref, x_ref, w1_ref, b1g_ref, b1l_ref, w2_ref, b2_ref, o_ref, w1g_ref, w1l_ref, w2b_ref):
    i = pl.program_id(0)
    n_blk = meta_ref.shape[0] - 1
    n_used = meta_ref[n_blk]
    d_e = w2_ref.shape[1]
    new_expert = jnp.logical_or(i == 0, meta_ref[i] != meta_ref[jnp.maximum(i - 1, 0)])

    @pl.when(jnp.logical_and(i < n_used, new_expert))
    def _():
        src = lax.broadcasted_iota(jnp.int32, (2 * W1_SEL_COLS, 2 * W1_SEL_COLS), 0)
        dst = lax.broadcasted_iota(jnp.int32, (2 * W1_SEL_COLS, 2 * W1_SEL_COLS), 1)
        pick = jnp.where(dst < W1_SEL_COLS, 2 * dst, 2 * (dst - W1_SEL_COLS) + 1)
        sel = jnp.where(src == pick, 1.0, 0.0).astype(BF16)
        for t in range(d_e // W1_SEL_COLS):
            cols = _dot(w1_ref[0, :, 2 * t * W1_SEL_COLS:2 * (t + 1) * W1_SEL_COLS].astype(BF16), sel)
            w1g_ref[:, t * W1_SEL_COLS:(t + 1) * W1_SEL_COLS] = cols[:, :W1_SEL_COLS].astype(BF16)
            w1l_ref[:, t * W1_SEL_COLS:(t + 1) * W1_SEL_COLS] = cols[:, W1_SEL_COLS:].astype(BF16)
        w2b_ref[...] = w2_ref[0].astype(BF16)

    @pl.when(i < n_used)
    def _():
        x = x_ref[...]
        glu = jnp.minimum(_dot(x, w1g_ref[...]) + b1g_ref[0], SWIGLU_LIMIT)
        lin = jnp.clip(_dot(x, w1l_ref[...]) + b1l_ref[0], -SWIGLU_LIMIT, SWIGLU_LIMIT)
        act = glu * jax.nn.sigmoid(SWIGLU_ALPHA * glu) * (lin + 1.0)
        o_ref[...] = (_dot(act.astype(BF16), w2b_ref[...]) + b2_ref[0]).astype(BF16)

    @pl.when(i >= n_used)
    def _():
        o_ref[...] = jnp.zeros_like(o_ref)


def _moe_blocks(meta, xb, w1, b1g, b1l, w2, b2, layer):
    n_rows = xb.shape[0]
    n_blk = n_rows // MOE_BLK
    d_e = w2.shape[2]
    row = lambda i, m: (i, 0)
    exp3 = lambda i, m: (layer, m[i], 0, 0)
    grid_spec = pltpu.PrefetchScalarGridSpec(
        num_scalar_prefetch=1,
        grid=(n_blk,),
        in_specs=[
            pl.BlockSpec((MOE_BLK, D_MODEL), row),
            pl.BlockSpec((None, 1, D_MODEL, 2 * d_e), exp3),
            pl.BlockSpec((None, 1, 1, d_e), exp3),
            pl.BlockSpec((None, 1, 1, d_e), exp3),
            pl.BlockSpec((None, 1, d_e, D_MODEL), exp3),
            pl.BlockSpec((None, 1, 1, D_MODEL), exp3),
        ],
        out_specs=pl.BlockSpec((MOE_BLK, D_MODEL), row),
        scratch_shapes=[
            pltpu.VMEM((D_MODEL, d_e), BF16),
            pltpu.VMEM((D_MODEL, d_e), BF16),
            pltpu.VMEM((d_e, D_MODEL), BF16),
        ],
    )
    return pl.pallas_call(
        _moe_kernel,
        grid_spec=grid_spec,
        out_shape=jax.ShapeDtypeStruct((n_rows, D_MODEL), BF16),
        compiler_params=pltpu.CompilerParams(dimension_semantics=("arbitrary",),
                                             vmem_limit_bytes=MOE_VMEM_LIMIT),
        name="moe_blocks",
    )(meta, xb, w1, b1g, b1l, w2, b2)


def _route(top_i):
    n_tok = top_i.shape[0]
    e_flat = top_i.reshape(-1)
    n_rows = n_tok * TOP_K
    onehot = (e_flat[:, None] == jnp.arange(N_EXPERTS, dtype=jnp.int32)[None, :]).astype(jnp.int32)
    csum = jnp.cumsum(onehot, axis=0)
    counts = csum[-1]
    starts = jnp.cumsum(counts) - counts
    pcounts = (counts + MOE_BLK - 1) // MOE_BLK * MOE_BLK
    pends = jnp.cumsum(pcounts)
    pstarts = pends - pcounts
    dest = jnp.take_along_axis(csum + (pstarts - 1)[None, :], e_flat[:, None], axis=1)[:, 0]
    n_blk = n_rows // MOE_BLK + N_EXPERTS
    blk_start = jnp.arange(n_blk, dtype=jnp.int32) * MOE_BLK
    blk_exp = jnp.minimum(jnp.sum((blk_start[:, None] >= pends[None, :]).astype(jnp.int32), axis=1), N_EXPERTS - 1)
    order = jnp.argsort(e_flat)
    pos = jnp.arange(n_blk * MOE_BLK, dtype=jnp.int32)
    src = (pos + jnp.repeat((starts - pstarts)[blk_exp], MOE_BLK)) % n_rows
    row_tok = order[src].astype(jnp.int32) // TOP_K
    meta = jnp.concatenate([blk_exp, (pends[-1:] // MOE_BLK).astype(jnp.int32)])
    return meta, row_tok, dest.reshape(n_tok, TOP_K).T.reshape(-1)


def _combine_kernel(x_ref, yg_ref, gate_ref, mod_ref, oc_ref, ol_ref, *, n_ctx_tiles):
    gate = gate_ref[...]
    acc = gate[:, 0:1] * yg_ref[0].astype(F32)
    for j in range(1, TOP_K):
        acc = acc + gate[:, j:j + 1] * yg_ref[j].astype(F32)
    out = x_ref[...] + mod_ref[0, 5:6, :] * acc
    is_ctx = pl.program_id(0) < n_ctx_tiles

    @pl.when(is_ctx)
    def _():
        oc_ref[...] = out

    @pl.when(jnp.logical_not(is_ctx))
    def _():
        ol_ref[...] = out


def _combine(x1, yg, gates, mods, tile_mod, n_ctx_tiles):
    n_tok = x1.shape[0]
    n_ctx = n_ctx_tiles * TOK_TILE
    return pl.pallas_call(
        functools.partial(_combine_kernel, n_ctx_tiles=n_ctx_tiles),
        grid=(n_tok // TOK_TILE,),
        in_specs=[
            pl.BlockSpec((TOK_TILE, D_MODEL), lambda i: (i, 0)),
            pl.BlockSpec((TOP_K, TOK_TILE, D_MODEL), lambda i: (0, i, 0)),
            pl.BlockSpec((TOK_TILE, TOP_K), lambda i: (i, 0)),
            pl.BlockSpec((1, 6, D_MODEL), lambda i: (tile_mod(i), 0, 0)),
        ],
        out_specs=_group_tile_specs(D_MODEL, n_ctx_tiles),
        out_shape=[jax.ShapeDtypeStruct((n_ctx, D_MODEL), F32), jax.ShapeDtypeStruct((n_tok - n_ctx, D_MODEL), F32)],
        compiler_params=_cparams(("arbitrary",)),
        name="moe_combine",
    )(x1, yg, gates, mods)


def kernel(x_prompt, x_sample, c, cache_na_k, cache_na_v, cache_swa_k, cache_swa_v, state_rwkv, c_ctx, w_ada, b_ada, norm1_g, norm2_g, w_in, w_out, na_q_norm, na_k_norm, na_rpb, swa_q_norm, swa_k_norm, swa_sink, rk_conv, rk_w0, rk_w2, rk_a0, rk_a2, rk_g2, rk_k_k, rk_k_a, rk_r_k, rk_ln_g, rk_ln_b, moe_router_w, moe_router_b, moe_w1, moe_b1, moe_w2, moe_b2):
    bc, tc, _ = x_prompt.shape
    bl, tl, _ = x_sample.shape
    depth = w_in.shape[0]
    n_ctx = bc * tc
    n_lat = bl * tl
    assert tc == TOK_TILE and tl % TOK_TILE == 0 and n_ctx % tl == 0
    n_ctx_tiles = n_ctx // TOK_TILE
    tiles_per_seq = tl // TOK_TILE
    past = cache_na_k.shape[2]

    def tile_mod(i):
        return jnp.where(i < n_ctx_tiles, 0, 1 + (i - n_ctx_tiles) // tiles_per_seq)

    def tile_rope(i):
        return jnp.where(i < n_ctx_tiles, tiles_per_seq, (i - n_ctx_tiles) % tiles_per_seq)

    x = (x_prompt.reshape(n_ctx, D_MODEL), x_sample.reshape(n_lat, D_MODEL))

    n_mod = 1 + bl
    mod_rows = -(-n_mod // 8) * 8
    cvecs = jnp.concatenate([c_ctx[None, :], c, jnp.zeros((mod_rows - n_mod, D_MODEL), F32)], axis=0)
    mods_all = _ada_mod(cvecs, w_ada, b_ada).reshape(depth, mod_rows, 6, D_MODEL)
    cos_tab, sin_tab = _rope_tables(tl)
    tile2 = lambda g: jnp.concatenate([g, g])[None, :]
    pad_lanes = lambda z: jnp.pad(z, ((0, 0), (0, LANES - z.shape[1])))
    zeros_lora = jnp.zeros((2, RK_DECAY_LORA, RK_WIDTH), F32)

    na_k_l, na_v_l, sw_k_l, sw_v_l, st_l = [], [], [], [], []
    for l in range(depth):
        mods = mods_all[l]
        qk_gains = jnp.concatenate(
            [tile2(na_q_norm[l]), tile2(na_k_norm[l]), tile2(swa_q_norm[l]), tile2(swa_k_norm[l])], axis=0)
        p = {
            "rk_conv": rk_conv[l], "rk_w0": rk_w0[l], "rk_a0": rk_a0[l], "rk_g2": rk_g2[l],
            "rk_w2_pad": jnp.concatenate([rk_w2[l], zeros_lora], axis=1),
            "rk_a2_pad": jnp.concatenate([zeros_lora, rk_a2[l]], axis=1),
            "rk_k_k": rk_k_k[l][None, :], "rk_k_a": rk_k_a[l][None, :],
            "rk_r_k": rk_r_k[l].reshape(1, RK_WIDTH),
            "rk_ln_g": rk_ln_g[l][None, :], "rk_ln_b": rk_ln_b[l][None, :],
            "w_out_bf16": w_out[l].astype(BF16), "norm2_g": norm2_g[l][None, :],
            "router_w_pad": pad_lanes(moe_router_w[l]), "router_b_pad": pad_lanes(moe_router_b[l][None, :]),
        }

        att, u = _in_proj(x, norm1_g[l][None, :], mods, w_in[l].astype(BF16), qk_gains, cos_tab, sin_tab,
                          tile_mod, tile_rope, n_ctx_tiles)
        q, g, bonus = _rk_prep(u, p, n_ctx_tiles, tiles_per_seq)
        att_c = att[:n_ctx].reshape(bc, tc, ATT_COLS)
        att_by_ctx_len = att.reshape((n_ctx + n_lat) // tc, tc, ATT_COLS)
        att_by_lat_len = att.reshape((n_ctx + n_lat) // tl, tl, ATT_COLS)
        na_k_l.append(att_c[:, :, NA_WIDTH:2 * NA_WIDTH].reshape(bc, tc, NA_HEADS, HEAD_DIM))
        na_v_l.append(att_c[:, :, 2 * NA_WIDTH:NA_COLS].reshape(bc, tc, NA_HEADS, HEAD_DIM))
        sw_k_l.append(att_c[:, :, NA_COLS + SWA_WIDTH:NA_COLS + SWA_WIDTH + SWA_KV_WIDTH]
                      .reshape(bc, tc, SWA_KV_HEADS, HEAD_DIM))
        sw_v_l.append(att_c[:, :, NA_COLS + SWA_WIDTH + SWA_KV_WIDTH:].reshape(bc, tc, SWA_KV_HEADS, HEAD_DIM))

        sink = swa_sink[l]
        o_na = (_ctx_attn(att_by_ctx_len, bc, sink, gqa=False).reshape(n_ctx, NA_WIDTH),
                _na_latent(att_by_lat_len, n_ctx // tl, cache_na_k[:, l].reshape(bl, past, NA_WIDTH),
                           cache_na_v[:, l].reshape(bl, past, NA_WIDTH),
                           _na_bias_tables(na_rpb[l])).reshape(n_lat, NA_WIDTH))
        o_sw = (_ctx_attn(att_by_ctx_len, bc, sink, gqa=True).reshape(n_ctx, SWA_WIDTH),
                _swa_latent(att_by_lat_len, n_ctx // tl, cache_swa_k[:, l].reshape(bl, past, SWA_KV_WIDTH),
                            cache_swa_v[:, l].reshape(bl, past, SWA_KV_WIDTH), sink).reshape(n_lat, SWA_WIDTH))

        yf_c, yb_c, s_fin = _rwkv_group(q, 0, bc, tc, jnp.zeros((bc, 2, RK_NB, PAIR, PAIR), F32))
        yf_l, yb_l, _ = _rwkv_group(q, n_ctx // RK_CHUNK, bl, tl, _pair_states(state_rwkv[:, l]))
        st_l.append(_head_states(s_fin))

        x1, h2, gates, top_i = _out_proj(x, o_na, o_sw, (yf_c, yf_l), (yb_c, yb_l), bonus, g, p, mods, tile_mod,
                                         n_ctx_tiles)
        meta, row_tok, dest = _route(top_i)
        h2 = jnp.concatenate([h2, jnp.zeros((H2_PAD_ROWS - h2.shape[0], D_MODEL), BF16)], axis=0)
        yb = _moe_blocks(meta, h2[row_tok], moe_w1, moe_b1[:, :, None, 0::2], moe_b1[:, :, None, 1::2], moe_w2,
                         moe_b2[:, :, None, :], l)
        x = _combine(x1, yb[dest].reshape(TOP_K, n_ctx + n_lat, D_MODEL), gates, mods, tile_mod, n_ctx_tiles)

    y_p = x[0].reshape(bc, tc, D_MODEL)
    y_s = x[1].reshape(bl, tl, D_MODEL)
    return (y_p, y_s, jnp.stack(na_k_l, axis=1), jnp.stack(na_v_l, axis=1), jnp.stack(sw_k_l, axis=1),
            jnp.stack(sw_v_l, axis=1), jnp.stack(st_l, axis=1))
```

```python
import functools

import jax
import jax.numpy as jnp
from jax import lax
from jax.experimental import pallas as pl
from jax.experimental.pallas import tpu as pltpu

F32 = jnp.float32
BF16 = jnp.bfloat16

D_MODEL = 1024
HEAD_DIM = 64
LANES = 128
GRID_W = 64
NA_HEADS = 6
SWA_HEADS = 4
SWA_KV_HEADS = 2
RK_HEADS = 6
NA_WIDTH = NA_HEADS * HEAD_DIM
SWA_WIDTH = SWA_HEADS * HEAD_DIM
SWA_KV_WIDTH = SWA_KV_HEADS * HEAD_DIM
RK_WIDTH = RK_HEADS * HEAD_DIM
RK_DECAY_LORA = 64
RK_A_LORA = 64
RK_GATE_LORA = 128
RK_COLS = 3 * RK_WIDTH + RK_DECAY_LORA + RK_A_LORA + RK_GATE_LORA
NA_COLS = 3 * NA_WIDTH
SWA_COLS = SWA_WIDTH + 2 * SWA_KV_WIDTH
ATT_COLS = NA_COLS + SWA_COLS
IN_COLS = ATT_COLS + RK_COLS
NA_WIN_R = 8
NA_WIN_C = 16
SWA_WIN = 128
ROPE_THETA = 10000.0
ATTN_SCALE = HEAD_DIM ** -0.5
N_EXPERTS = 32
TOP_K = 4
SWIGLU_LIMIT = 7.0
SWIGLU_ALPHA = 1.702
MOE_BLK = 256
RMS_EPS = 1e-6
GN_EPS = 64e-5
NEG_BIG = -1e30

TOK_TILE = 256
VMEM_LIMIT = 48 * 1024 * 1024


def _cparams(sem):
    return pltpu.CompilerParams(dimension_semantics=sem, vmem_limit_bytes=VMEM_LIMIT)


def _dot(a, b):
    return jnp.dot(a, b, preferred_element_type=F32)


def _dot_nt(a, b):
    return lax.dot_general(a, b, (((1,), (1,)), ((), ())), preferred_element_type=F32)


def _split_bf16(x):
    hi = x.astype(BF16)
    lo = (x - hi.astype(F32)).astype(BF16)
    return hi, lo


def _dot3(a, b):
    ah, al = _split_bf16(a)
    bh, bl = _split_bf16(b)
    return _dot(ah, bh) + (_dot(ah, bl) + _dot(al, bh))


def _bmm_raw(a, b):
    return lax.dot_general(a, b, (((2,), (1,)), ((0,), (0,))), preferred_element_type=F32)


def _bmm(a, b):
    return _bmm_raw(a.astype(BF16), b.astype(BF16))


def _bmm_nt(a, b):
    return lax.dot_general(a.astype(BF16), b.astype(BF16), (((2,), (2,)), ((0,), (0,))),
                           preferred_element_type=F32)


def _bmm3(a, b):
    ah, al = _split_bf16(a)
    bh, bl = _split_bf16(b)
    return _bmm_raw(ah, bh) + (_bmm_raw(ah, bl) + _bmm_raw(al, bh))


def _lane_lo(shape):
    return lax.broadcasted_iota(jnp.int32, shape, len(shape) - 1) < HEAD_DIM


def _pair_sum(x):
    lo = _lane_lo(x.shape)
    s_lo = jnp.sum(jnp.where(lo, x, 0.0), axis=-1, keepdims=True)
    s_hi = jnp.sum(jnp.where(lo, 0.0, x), axis=-1, keepdims=True)
    return jnp.where(lo, s_lo, s_hi)


def _stack_heads(q):
    lo = _lane_lo(q.shape)
    return jnp.concatenate([jnp.where(lo, q, 0.0), jnp.where(lo, 0.0, q)], axis=0)


def _stack_heads3(x):
    lo = _lane_lo(x.shape)
    return jnp.concatenate([jnp.where(lo, x, 0.0), jnp.where(lo, 0.0, x)], axis=1)


def _unstack_heads(o2):
    n = o2.shape[0] // 2
    return jnp.where(_lane_lo((n, LANES)), o2[:n], o2[n:])


def _dup_head(x, j):
    keep = _lane_lo(x.shape) == (j == 0)
    return jnp.where(keep, x, pltpu.roll(x, HEAD_DIM, 1))


def _ada_kernel(c_ref, w_ref, b_ref, o_ref):
    cv = c_ref[...]
    s = cv * jax.nn.sigmoid(cv)
    o_ref[0] = _dot3(s, w_ref[0]) + b_ref[0]


def _ada_mod(cvecs, w_ada, b_ada):
    depth, _, n_out = w_ada.shape
    rows = cvecs.shape[0]
    tn = 1024
    return pl.pallas_call(
        _ada_kernel,
        grid=(depth, n_out // tn),
        in_specs=[
            pl.BlockSpec((rows, D_MODEL), lambda l, j: (0, 0)),
            pl.BlockSpec((1, D_MODEL, tn), lambda l, j: (l, 0, j)),
            pl.BlockSpec((1, 1, tn), lambda l, j: (l, 0, j)),
        ],
        out_specs=pl.BlockSpec((1, rows, tn), lambda l, j: (l, 0, j)),
        out_shape=jax.ShapeDtypeStruct((depth, rows, n_out), F32),
        compiler_params=_cparams(("parallel", "parallel")),
        name="ada_mod",
    )(cvecs, w_ada, b_ada.reshape(depth, 1, n_out))


NA_QK_BLOCKS = 2 * NA_WIDTH // LANES
SWA_Q_BLOCK0 = NA_COLS // LANES
SWA_QK_BLOCKS = (SWA_WIDTH + SWA_KV_WIDTH) // LANES


def _in_proj_kernel(xc_ref, xl_ref, g_ref, mod_ref, w_ref, qkg_ref, cos_ref, sin_ref, att_ref, u_ref, *, n_ctx_tiles):
    x = jnp.where(pl.program_id(0) < n_ctx_tiles, xc_ref[...], xl_ref[...])
    y = x * lax.rsqrt(jnp.mean(x * x, axis=-1, keepdims=True) + RMS_EPS)
    h = (y * g_ref[...]) * (1.0 + mod_ref[0, 1:2, :]) + mod_ref[0, 0:1, :]
    proj = _dot(h.astype(BF16), w_ref[...])
    u_ref[...] = proj[:, ATT_COLS:]

    def qk_norm(blk, gain):
        ms = _pair_sum(blk * blk) * (1.0 / HEAD_DIM)
        return blk * lax.rsqrt(ms + RMS_EPS) * gain

    lane = lax.broadcasted_iota(jnp.int32, (x.shape[0], LANES), 1)
    first = (lane % (HEAD_DIM // 2)) < (HEAD_DIM // 4)
    for cb in range(ATT_COLS // LANES):
        blk = proj[:, cb * LANES:(cb + 1) * LANES]
        if cb < NA_QK_BLOCKS:
            gi = 0 if cb < NA_QK_BLOCKS // 2 else 1
            blk = qk_norm(blk, qkg_ref[gi:gi + 1, :])
        elif SWA_Q_BLOCK0 <= cb < SWA_Q_BLOCK0 + SWA_QK_BLOCKS:
            gi = 2 if cb < SWA_Q_BLOCK0 + SWA_WIDTH // LANES else 3
            blk = qk_norm(blk, qkg_ref[gi:gi + 1, :])
            partner = jnp.where(first, pltpu.roll(blk, LANES - HEAD_DIM // 4, 1),
                                pltpu.roll(blk, HEAD_DIM // 4, 1))
            blk = blk * cos_ref[...] + partner * sin_ref[...]
        att_ref[:, cb * LANES:(cb + 1) * LANES] = blk


def _group_tile_specs(width, n_ctx_tiles):
    return [pl.BlockSpec((TOK_TILE, width), lambda i: (jnp.minimum(i, n_ctx_tiles - 1), 0)),
            pl.BlockSpec((TOK_TILE, width), lambda i: (jnp.maximum(i - n_ctx_tiles, 0), 0))]


def _in_proj(x, norm_g, mods, w_in_bf16, qk_gains, cos_tab, sin_tab, tile_mod, tile_rope, n_ctx_tiles):
    n_tok = x[0].shape[0] + x[1].shape[0]
    return pl.pallas_call(
        functools.partial(_in_proj_kernel, n_ctx_tiles=n_ctx_tiles),
        grid=(n_tok // TOK_TILE,),
        in_specs=[
            *_group_tile_specs(D_MODEL, n_ctx_tiles),
            pl.BlockSpec((1, D_MODEL), lambda i: (0, 0)),
            pl.BlockSpec((1, 6, D_MODEL), lambda i: (tile_mod(i), 0, 0)),
            pl.BlockSpec((D_MODEL, IN_COLS), lambda i: (0, 0)),
            pl.BlockSpec((4, LANES), lambda i: (0, 0)),
            pl.BlockSpec((TOK_TILE, LANES), lambda i: (tile_rope(i), 0)),
            pl.BlockSpec((TOK_TILE, LANES), lambda i: (tile_rope(i), 0)),
        ],
        out_specs=[
            pl.BlockSpec((TOK_TILE, ATT_COLS), lambda i: (i, 0)),
            pl.BlockSpec((TOK_TILE, RK_COLS), lambda i: (i, 0)),
        ],
        out_shape=[
            jax.ShapeDtypeStruct((n_tok, ATT_COLS), F32),
            jax.ShapeDtypeStruct((n_tok, RK_COLS), F32),
        ],
        compiler_params=_cparams(("parallel",)),
        name="in_proj",
    )(*x, norm_g, mods, w_in_bf16, qk_gains, cos_tab, sin_tab)


def _rope_tables(n_lat):
    nf = HEAD_DIM // 4
    t = jnp.arange(n_lat)
    lane = jnp.arange(LANES)
    d = lane % HEAD_DIM
    inv = ROPE_THETA ** (-(d % nf).astype(F32) / nf)
    pos = jnp.where((d // (2 * nf))[None, :] == 0, (t // GRID_W)[:, None], (t % GRID_W)[:, None]).astype(F32)
    ang = pos * inv[None, :]
    sign = jnp.where((d % (2 * nf)) < nf, -1.0, 1.0).astype(F32)
    cos = jnp.concatenate([jnp.cos(ang), jnp.ones((TOK_TILE, LANES), F32)], 0)
    sin = jnp.concatenate([jnp.sin(ang) * sign[None, :], jnp.zeros((TOK_TILE, LANES), F32)], 0)
    return cos, sin


def _ctx_attn_kernel(sink_ref, q_ref, k_ref, v_ref, o_ref, *, gqa):
    j = pl.program_id(1)
    k = k_ref[0]
    v = v_ref[0]
    if gqa:
        k = _dup_head(k, j)
        v = _dup_head(v, j)
    n = k.shape[0]
    q2 = _stack_heads(q_ref[0]).astype(BF16)
    s = _dot_nt(q2, k.astype(BF16)) * ATTN_SCALE
    m = jnp.max(s, axis=-1, keepdims=True)
    if gqa:
        row = lax.broadcasted_iota(jnp.int32, (2 * n, 1), 0)
        snk = jnp.where(row < n, sink_ref[2 * j], sink_ref[2 * j + 1])
        m = jnp.maximum(m, snk)
    p = jnp.exp(s - m)
    den = jnp.sum(p, axis=-1, keepdims=True)
    if gqa:
        den = den + jnp.exp(snk - m)
    o2 = _dot(p.astype(BF16), v.astype(BF16)) / den
    o_ref[0] = _unstack_heads(o2)


def _ctx_attn(att, b, sink, *, gqa):
    t = att.shape[1]
    if gqa:
        nq = SWA_WIDTH // LANES
        qb, kb, vb = SWA_Q_BLOCK0, SWA_Q_BLOCK0 + nq, SWA_Q_BLOCK0 + nq + 1
        kmap = lambda bi, j: (bi, 0, kb)
        vmap = lambda bi, j: (bi, 0, vb)
    else:
        nq = NA_WIDTH // LANES
        qb, kb, vb = 0, nq, 2 * nq
        kmap = lambda bi, j: (bi, 0, kb + j)
        vmap = lambda bi, j: (bi, 0, vb + j)
    return pl.pallas_call(
        functools.partial(_ctx_attn_kernel, gqa=gqa),
        grid=(b, nq),
        in_specs=[
            pl.BlockSpec(memory_space=pltpu.SMEM),
            pl.BlockSpec((1, t, LANES), lambda bi, j: (bi, 0, qb + j)),
            pl.BlockSpec((1, t, LANES), kmap),
            pl.BlockSpec((1, t, LANES), vmap),
        ],
        out_specs=pl.BlockSpec((1, t, LANES), lambda bi, j: (bi, 0, j)),
        out_shape=jax.ShapeDtypeStruct((b, t, nq * LANES), F32),
        compiler_params=_cparams(("parallel", "parallel")),
        name="ctx_attn_swa" if gqa else "ctx_attn_na",
    )(sink, att, att, att)


NA_ROWS_PER_ITER = 8


def _na_lat_kernel(q_ref, k_ref, v_ref, kc_ref, vc_ref, tab_ref, o_ref, kb_ref, vb_ref):
    n = q_ref.shape[1]
    rows = n // GRID_W
    win = NA_WIN_R * GRID_W
    kb_ref[...] = k_ref[0].astype(BF16)
    vb_ref[...] = v_ref[0].astype(BF16)
    kc = kc_ref[0].astype(BF16)
    vc = vc_ref[0].astype(BF16)

    def row_group(ig, carry):
        nr = NA_ROWS_PER_ITER
        g0 = pl.multiple_of(ig * (nr * GRID_W), nr * GRID_W)
        q2 = _stack_heads3(q_ref[0, pl.ds(g0, nr * GRID_W), :].reshape(nr, GRID_W, LANES)).astype(BF16)
        kws, vws, biases = [], [], []
        for r in range(nr):
            i = ig * nr + r
            start = jnp.clip(i - NA_WIN_R // 2, 0, rows - NA_WIN_R)
            k0 = pl.multiple_of(start * GRID_W, GRID_W)
            kws.append(kb_ref[pl.ds(k0, win), :])
            vws.append(vb_ref[pl.ds(k0, win), :])
            biases.append(tab_ref[0, start - i + (NA_WIN_R - 1)])
        s_loc = _bmm_nt(q2, jnp.stack(kws)) * ATTN_SCALE + jnp.stack(biases)
        s_ctx = _dot_nt(q2.reshape(nr * 2 * GRID_W, LANES), kc).reshape(nr, 2 * GRID_W, -1) * ATTN_SCALE
        m = jnp.maximum(jnp.max(s_loc, axis=-1, keepdims=True), jnp.max(s_ctx, axis=-1, keepdims=True))
        p_loc = jnp.exp(s_loc - m)
        p_ctx = jnp.exp(s_ctx - m)
        den = jnp.sum(p_loc, axis=-1, keepdims=True) + jnp.sum(p_ctx, axis=-1, keepdims=True)
        o_ctx = _dot(p_ctx.reshape(nr * 2 * GRID_W, -1).astype(BF16), vc).reshape(nr, 2 * GRID_W, LANES)
        o2 = (_bmm(p_loc, jnp.stack(vws)) + o_ctx) / den
        out = jnp.where(_lane_lo((nr, GRID_W, LANES)), o2[:, :GRID_W], o2[:, GRID_W:])
        o_ref[0, pl.ds(g0, nr * GRID_W), :] = out.reshape(nr * GRID_W, LANES)
        return carry

    lax.fori_loop(0, rows // NA_ROWS_PER_ITER, row_group, 0)


def _na_bias_tables(rpb):
    col = jnp.arange(GRID_W)
    cstart = jnp.clip(col - NA_WIN_C // 2, 0, GRID_W - NA_WIN_C)
    col_mask = (col[None, :] >= cstart[:, None]) & (col[None, :] < cstart[:, None] + NA_WIN_C)
    col_idx = jnp.clip(col[None, :] - col[:, None] + NA_WIN_C - 1, 0, 2 * NA_WIN_C - 2)
    rpb_cols = jnp.where(col_mask[None, None], rpb[:, :, col_idx], NEG_BIG)
    roff = jnp.arange(NA_WIN_R)[:, None] + jnp.arange(NA_WIN_R)[None, :]
    t = rpb_cols[:, roff]
    t = jnp.transpose(t, (0, 1, 3, 2, 4)).reshape(NA_HEADS // 2, 2, NA_WIN_R, GRID_W, NA_WIN_R * GRID_W)
    return jnp.transpose(t, (0, 2, 1, 3, 4)).reshape(NA_HEADS // 2, NA_WIN_R, 2 * GRID_W, NA_WIN_R * GRID_W)


def _na_latent(att, s0, kc, vc, tab):
    n = att.shape[1]
    b, p, _ = kc.shape
    nq = NA_WIDTH // LANES
    return pl.pallas_call(
        _na_lat_kernel,
        grid=(b, nq),
        in_specs=[
            pl.BlockSpec((1, n, LANES), lambda bi, j: (s0 + bi, 0, j)),
            pl.BlockSpec((1, n, LANES), lambda bi, j: (s0 + bi, 0, nq + j)),
            pl.BlockSpec((1, n, LANES), lambda bi, j: (s0 + bi, 0, 2 * nq + j)),
            pl.BlockSpec((1, p, LANES), lambda bi, j: (bi, 0, j)),
            pl.BlockSpec((1, p, LANES), lambda bi, j: (bi, 0, j)),
            pl.BlockSpec((1, NA_WIN_R, 2 * GRID_W, NA_WIN_R * GRID_W), lambda bi, j: (j, 0, 0, 0)),
        ],
        out_specs=pl.BlockSpec((1, n, LANES), lambda bi, j: (bi, 0, j)),
        out_shape=jax.ShapeDtypeStruct((b, n, NA_WIDTH), F32),
        scratch_shapes=[pltpu.VMEM((n, LANES), BF16), pltpu.VMEM((n, LANES), BF16)],
        compiler_params=_cparams(("parallel", "parallel")),
        name="na_latent",
    )(att, att, att, kc, vc, tab)


SWA_BLOCKS_PER_ITER = 4


def _swa_lat_kernel(sink_ref, q_ref, k_ref, v_ref, kc_ref, vc_ref, o_ref, kb_ref, vb_ref):
    j = pl.program_id(1)
    n = q_ref.shape[1]
    blk = SWA_WIN
    span = 3 * blk
    kb_ref[...] = _dup_head(k_ref[0], j).astype(BF16)
    vb_ref[...] = _dup_head(v_ref[0], j).astype(BF16)
    kc = _dup_head(kc_ref[0], j).astype(BF16)
    vc = _dup_head(vc_ref[0], j).astype(BF16)
    row = lax.broadcasted_iota(jnp.int32, (2 * blk, 1), 0)
    snk = jnp.where(row < blk, sink_ref[2 * j], sink_ref[2 * j + 1])
    qoff = lax.broadcasted_iota(jnp.int32, (2 * blk, span), 0) % blk
    koff = lax.broadcasted_iota(jnp.int32, (2 * blk, span), 1)

    def q_group(qg, carry):
        nr = SWA_BLOCKS_PER_ITER
        g0 = pl.multiple_of(qg * (nr * blk), nr * blk)
        q2 = _stack_heads3(q_ref[0, pl.ds(g0, nr * blk), :].reshape(nr, blk, LANES)).astype(BF16)
        kws, vws, valids = [], [], []
        for r in range(nr):
            q0 = g0 + r * blk
            w0 = pl.multiple_of(jnp.clip(q0 - blk, 0, n - span), blk)
            kws.append(kb_ref[pl.ds(w0, span), :])
            vws.append(vb_ref[pl.ds(w0, span), :])
            valids.append(jnp.abs((q0 + qoff) - (w0 + koff)) <= SWA_WIN)
        s_loc = jnp.where(jnp.stack(valids), _bmm_nt(q2, jnp.stack(kws)) * ATTN_SCALE, NEG_BIG)
        s_ctx = _dot_nt(q2.reshape(nr * 2 * blk, LANES), kc).reshape(nr, 2 * blk, -1) * ATTN_SCALE
        m = jnp.maximum(jnp.max(s_loc, axis=-1, keepdims=True), jnp.max(s_ctx, axis=-1, keepdims=True))
        m = jnp.maximum(m, snk)
        p_loc = jnp.exp(s_loc - m)
        p_ctx = jnp.exp(s_ctx - m)
        den = (jnp.sum(p_loc, axis=-1, keepdims=True) + jnp.sum(p_ctx, axis=-1, keepdims=True)
               + jnp.exp(snk - m))
        o_ctx = _dot(p_ctx.reshape(nr * 2 * blk, -1).astype(BF16), vc).reshape(nr, 2 * blk, LANES)
        o2 = (_bmm(p_loc, jnp.stack(vws)) + o_ctx) / den
        out = jnp.where(_lane_lo((nr, blk, LANES)), o2[:, :blk], o2[:, blk:])
        o_ref[0, pl.ds(g0, nr * blk), :] = out.reshape(nr * blk, LANES)
        return carry

    lax.fori_loop(0, n // (SWA_BLOCKS_PER_ITER * blk), q_group, 0)


def _swa_latent(att, s0, kc, vc, sink):
    n = att.shape[1]
    b, p, _ = kc.shape
    nq = SWA_WIDTH // LANES
    qb, kb, vb = SWA_Q_BLOCK0, SWA_Q_BLOCK0 + nq, SWA_Q_BLOCK0 + nq + 1
    return pl.pallas_call(
        _swa_lat_kernel,
        grid=(b, nq),
        in_specs=[
            pl.BlockSpec(memory_space=pltpu.SMEM),
            pl.BlockSpec((1, n, LANES), lambda bi, j: (s0 + bi, 0, qb + j)),
            pl.BlockSpec((1, n, LANES), lambda bi, j: (s0 + bi, 0, kb)),
            pl.BlockSpec((1, n, LANES), lambda bi, j: (s0 + bi, 0, vb)),
            pl.BlockSpec((1, p, LANES), lambda bi, j: (bi, 0, 0)),
            pl.BlockSpec((1, p, LANES), lambda bi, j: (bi, 0, 0)),
        ],
        out_specs=pl.BlockSpec((1, n, LANES), lambda bi, j: (bi, 0, j)),
        out_shape=jax.ShapeDtypeStruct((b, n, SWA_WIDTH), F32),
        scratch_shapes=[pltpu.VMEM((n, LANES), BF16), pltpu.VMEM((n, LANES), BF16)],
        compiler_params=_cparams(("parallel", "parallel")),
        name="swa_latent",
    )(sink, att, att, att, kc, vc)


RK_NB = RK_WIDTH // LANES
LORA_BLOCK = 3 * RK_WIDTH // LANES
GATE_BLOCK = LORA_BLOCK + 1
Q_R, Q_V, Q_A, Q_W, Q_K, Q_B = range(6)
Q_DIR = 3
Q_COLS = (6 + Q_DIR) * RK_WIDTH


def _softplus(x):
    return jnp.maximum(x, 0.0) + jnp.log(1.0 + jnp.exp(-jnp.abs(x)))


def _rk_prep_kernel(u_ref, up_ref, un_ref, cw_ref, w0_ref, w2_ref, a0_ref, a2_ref, g2_ref, kk_ref, ka_ref,
                    rk_ref, q_ref, g_ref, bonus_ref, *, n_ctx_tiles, tiles_per_seq):
    def put(slot, val):
        q_ref[:, slot * RK_WIDTH:(slot + 1) * RK_WIDTH] = val

    i = pl.program_id(0)
    li = i - n_ctx_tiles
    is_lat = i >= n_ctx_tiles
    has_prev = jnp.logical_and(is_lat, li % tiles_per_seq != 0)
    has_next = jnp.logical_and(is_lat, li % tiles_per_seq != tiles_per_seq - 1)
    u = u_ref[...]
    tm = u.shape[0]
    prev_row = jnp.where(has_prev, up_ref[7:8, :], 0.0)
    next_row = jnp.where(has_next, un_ref[0:1, :], 0.0)
    row = lax.broadcasted_iota(jnp.int32, u.shape, 0)
    um = jnp.where(row == 0, prev_row, pltpu.roll(u, 1, 0))
    up = jnp.where(row == tm - 1, next_row, pltpu.roll(u, tm - 1, 0))
    u = um * cw_ref[0:1, :] + u * cw_ref[1:2, :] + up * cw_ref[2:3, :]

    r = u[:, 0:RK_WIDTH]
    k = u[:, RK_WIDTH:2 * RK_WIDTH]
    v = u[:, 2 * RK_WIDTH:3 * RK_WIDTH]
    lora = u[:, LORA_BLOCK * LANES:(LORA_BLOCK + 1) * LANES]
    gl = u[:, GATE_BLOCK * LANES:(GATE_BLOCK + 1) * LANES]
    put(Q_R, r)
    put(Q_V, v)
    g_ref[...] = _dot3(jax.nn.sigmoid(gl), g2_ref[...])

    kn = k * kk_ref[...]
    kk = jnp.concatenate(
        [kn[:, c * LANES:(c + 1) * LANES]
         * lax.rsqrt(jnp.maximum(_pair_sum(jnp.square(kn[:, c * LANES:(c + 1) * LANES])), 1e-24))
         for c in range(RK_NB)], axis=1)
    put(Q_A, -kk)

    lora_t = jnp.tanh(lora)
    kd_sum = None
    for d in range(2):
        w = -_softplus(-(w0_ref[d:d + 1, :] + _dot3(lora_t, w2_ref[d]))) - 0.5
        put(Q_W + Q_DIR * d, -jnp.exp(w))
        a = jax.nn.sigmoid(a0_ref[d:d + 1, :] + _dot3(lora, a2_ref[d]))
        kd = k * (1.0 + (a - 1.0) * ka_ref[...])
        put(Q_K + Q_DIR * d, kd)
        put(Q_B + Q_DIR * d, kk * a)
        kd_sum = kd if kd_sum is None else kd_sum + kd

    t = r * kd_sum * rk_ref[...]
    bonus_ref[...] = jnp.concatenate(
        [_pair_sum(t[:, c * LANES:(c + 1) * LANES]) for c in range(RK_NB)], axis=1) * v


def _rk_prep(u, p, n_ctx_tiles, tiles_per_seq):
    n_tok = u.shape[0]
    n_tiles = n_tok // TOK_TILE
    sub = TOK_TILE // 8
    last8 = n_tok // 8 - 1
    tok = lambda i: (i, 0)
    const2 = lambda i: (0, 0)
    const3 = lambda i: (0, 0, 0)
    one = jax.ShapeDtypeStruct((n_tok, RK_WIDTH), F32)
    tok_spec = pl.BlockSpec((TOK_TILE, RK_WIDTH), tok)
    return pl.pallas_call(
        functools.partial(_rk_prep_kernel, n_ctx_tiles=n_ctx_tiles, tiles_per_seq=tiles_per_seq),
        grid=(n_tiles,),
        in_specs=[
            pl.BlockSpec((TOK_TILE, RK_COLS), tok),
            pl.BlockSpec((8, RK_COLS), lambda i: (jnp.maximum(i * sub - 1, 0), 0)),
            pl.BlockSpec((8, RK_COLS), lambda i: (jnp.minimum((i + 1) * sub, last8), 0)),
            pl.BlockSpec((3, RK_COLS), const2),
            pl.BlockSpec((2, RK_WIDTH), const2),
            pl.BlockSpec((2, LANES, RK_WIDTH), const3),
            pl.BlockSpec((2, RK_WIDTH), const2),
            pl.BlockSpec((2, LANES, RK_WIDTH), const3),
            pl.BlockSpec((RK_GATE_LORA, RK_WIDTH), const2),
            pl.BlockSpec((1, RK_WIDTH), const2),
            pl.BlockSpec((1, RK_WIDTH), const2),
            pl.BlockSpec((1, RK_WIDTH), const2),
        ],
        out_specs=[pl.BlockSpec((TOK_TILE, Q_COLS), tok), tok_spec, tok_spec],
        out_shape=[jax.ShapeDtypeStruct((n_tok, Q_COLS), F32), one, one],
        compiler_params=_cparams(("parallel",)),
        name="rk_prep",
    )(u, u, u, p["rk_conv"], p["rk_w0"], p["rk_w2_pad"], p["rk_a0"], p["rk_a2_pad"], p["rk_g2"],
      p["rk_k_k"], p["rk_k_a"], p["rk_r_k"])


RK_CHUNK = 64
PAIR = 2 * HEAD_DIM
STATE_SEQS = 8
RK_STEP_CHUNKS = 4


def _split3_bf16(x):
    hi = x.astype(BF16)
    r1 = x - hi.astype(F32)
    mid = r1.astype(BF16)
    return hi, mid, (r1 - mid.astype(F32)).astype(BF16)


def _pack_pair(m):
    return jnp.concatenate([m[:HEAD_DIM, :HEAD_DIM], m[HEAD_DIM:, HEAD_DIM:]], axis=1)


def _unpack_pairs(m):
    lo = _lane_lo(m.shape)
    return jnp.concatenate([jnp.where(lo, m, 0.0), jnp.where(lo, 0.0, m)], axis=1)


def _rk_chunk_kernel(q_ref, rbar_ref, ybar_ref, phi_ref, psi_ref):
    c = RK_CHUNK
    n = 2 * c
    nd = 2 * RK_NB
    nu = RK_STEP_CHUNKS * nd

    def tiles(slot, per_dir):
        cols = [(slot + (Q_DIR * d if per_dir else 0)) * RK_WIDTH + p * LANES
                for d in range(2) for p in range(RK_NB)]
        return jnp.stack([q_ref[ck * c:(ck + 1) * c, lo:lo + LANES] for ck in range(RK_STEP_CHUNKS) for lo in cols])

    r, v, a = tiles(Q_R, False), tiles(Q_V, False), tiles(Q_A, False)
    lw, k, b = tiles(Q_W, True), tiles(Q_K, True), tiles(Q_B, True)
    unit = lax.broadcasted_iota(jnp.int32, (nu, 1, 1), 0)
    sgn = jnp.ones((nu, 1, 1), jnp.int32)
    for ck in range(RK_STEP_CHUNKS):
        sgn = jnp.where(jnp.logical_and(unit >= ck * nd + RK_NB, unit < (ck + 1) * nd), -1, sgn)
    bwd = sgn < 0
    tdiff = lax.broadcasted_iota(jnp.int32, (1, c, c), 2) - lax.broadcasted_iota(jnp.int32, (1, c, c), 1)
    tri = jnp.where(tdiff * sgn <= 0, 1.0, 0.0)
    cum = sum(_bmm(tri, part) for part in _split3_bf16(lw))
    tot = jnp.where(bwd, cum[:, 0:1], cum[:, c - 1:c])
    a_t = a * jnp.exp(cum - lw)
    r_t = r * jnp.exp(cum)
    e_neg = jnp.exp(-cum)
    e_end = jnp.exp(tot - cum)
    g = _bmm_nt(jnp.concatenate([_stack_heads3(a_t), _stack_heads3(r_t)], axis=1),
                jnp.concatenate([_stack_heads3(b * e_neg), _stack_heads3(k * e_neg)], axis=1))
    r2 = lax.broadcasted_iota(jnp.int32, (1, n, n), 1)
    c2 = lax.broadcasted_iota(jnp.int32, (1, n, n), 2)
    order = (jnp.bitwise_and(c2, c - 1) - jnp.bitwise_and(r2, c - 1)) * sgn
    eye = jnp.where(r2 == c2, 1.0, 0.0)
    l_ab = jnp.where(order < 0, g[:, :n, :n], 0.0)
    l_ak = jnp.where(order < 0, g[:, :n, n:], 0.0)
    m_rb = jnp.where(order <= 0, g[:, n:, :n], 0.0)
    m_rk = jnp.where(order <= 0, g[:, n:, n:], 0.0)
    t_inv = eye + l_ab
    pw = l_ab
    for _ in range(5):
        pw = _bmm(pw, pw)
        t_inv = t_inv + _bmm(t_inv, pw)
    sv = _stack_heads3(v)
    au = _bmm(t_inv, jnp.concatenate([_stack_heads3(a_t), _bmm(l_ak, sv)], axis=2))
    ry = _bmm(m_rb, au) + jnp.concatenate([_stack_heads3(r_t), _bmm(m_rk, sv)], axis=2)
    ry = ry[:, :c] + ry[:, c:]
    bt = jnp.swapaxes(_stack_heads3(b * e_end), 1, 2)
    kt = jnp.swapaxes(_stack_heads3(k * e_end), 1, 2)
    pp = _bmm(bt, au)
    phi = eye * jnp.exp(tot) + pp[:, :, :PAIR]
    psi = pp[:, :, PAIR:] + _bmm(kt, sv)
    for ck in range(RK_STEP_CHUNKS):
        for d in range(2):
            for p in range(RK_NB):
                u = ck * nd + d * RK_NB + p
                rbar_ref[d, ck * c:(ck + 1) * c, p * LANES:(p + 1) * LANES] = ry[u, :, :PAIR]
                ybar_ref[d, ck * c:(ck + 1) * c, p * LANES:(p + 1) * LANES] = ry[u, :, PAIR:]
                phi_ref[d, ck, p] = _pack_pair(phi[u])
                psi_ref[d, ck, p] = _pack_pair(psi[u])


def _rk_chunk(q, tile0, n_seq, t):
    nc = t // RK_CHUNK
    sc = RK_STEP_CHUNKS
    assert nc % sc == 0 and tile0 % sc == 0
    row_sh = jax.ShapeDtypeStruct((2, n_seq, t, RK_WIDTH), F32)
    mat_sh = jax.ShapeDtypeStruct((2, n_seq, nc, RK_NB, HEAD_DIM, PAIR), F32)
    row_spec = pl.BlockSpec((2, None, sc * RK_CHUNK, RK_WIDTH), lambda s, c: (0, s, c, 0))
    mat_spec = pl.BlockSpec((2, None, sc, RK_NB, HEAD_DIM, PAIR), lambda s, c: (0, s, c, 0, 0, 0))
    return pl.pallas_call(
        _rk_chunk_kernel,
        grid=(n_seq, nc // sc),
        in_specs=[pl.BlockSpec((sc * RK_CHUNK, Q_COLS), lambda s, c: ((tile0 + s * nc) // sc + c, 0))],
        out_specs=[row_spec, row_spec, mat_spec, mat_spec],
        out_shape=[row_sh, row_sh, mat_sh, mat_sh],
        compiler_params=_cparams(("parallel", "parallel")),
        name="rk_chunk",
    )(q)


def _rk_state_kernel(rf_ref, rb_ref, yf_ref, yb_ref, phf_ref, phb_ref, psf_ref, psb_ref, s0_ref,
                     of_ref, ob_ref, s_ref):
    @pl.when(pl.program_id(1) == 0)
    def _():
        s_ref[...] = s0_ref[...]

    ns = s_ref.shape[0]
    nd = 2 * RK_NB

    def pair_tiles(ref_f, ref_b):
        tiles = [ref[:, :, p * LANES:(p + 1) * LANES] for ref in (ref_f, ref_b) for p in range(RK_NB)]
        return jnp.stack(tiles, axis=1).reshape(ns * nd, RK_CHUNK, LANES)

    def mats(ref_f, ref_b):
        return _unpack_pairs(jnp.concatenate([ref_f[...], ref_b[...]], axis=1).reshape(ns * nd, HEAD_DIM, PAIR))

    h = s_ref[...].reshape(ns * nd, PAIR, PAIR)
    y = (_bmm3(pair_tiles(rf_ref, rb_ref), h) + pair_tiles(yf_ref, yb_ref)).reshape(ns, nd, RK_CHUNK, LANES)
    s_ref[...] = (_bmm3(mats(phf_ref, phb_ref), h) + mats(psf_ref, psb_ref)).reshape(ns, 2, RK_NB, PAIR, PAIR)
    of_ref[...] = jnp.concatenate([y[:, p] for p in range(RK_NB)], axis=2)
    ob_ref[...] = jnp.concatenate([y[:, RK_NB + p] for p in range(RK_NB)], axis=2)


def _rk_state(rbar, ybar, phi, psi, s0):
    _, n_seq, t, _ = rbar.shape
    nc = t // RK_CHUNK
    sg = STATE_SEQS
    row_blk = (None, sg, RK_CHUNK, RK_WIDTH)
    mat_blk = (None, sg, None, RK_NB, HEAD_DIM, PAIR)
    fwd_row = pl.BlockSpec(row_blk, lambda g, c: (0, g, c, 0))
    bwd_row = pl.BlockSpec(row_blk, lambda g, c: (1, g, nc - 1 - c, 0))
    fwd_mat = pl.BlockSpec(mat_blk, lambda g, c: (0, g, c, 0, 0, 0))
    bwd_mat = pl.BlockSpec(mat_blk, lambda g, c: (1, g, nc - 1 - c, 0, 0, 0))
    st = pl.BlockSpec((sg, 2, RK_NB, PAIR, PAIR), lambda g, c: (g, 0, 0, 0, 0))
    out_sh = jax.ShapeDtypeStruct((n_seq, t, RK_WIDTH), F32)
    return pl.pallas_call(
        _rk_state_kernel,
        grid=(n_seq // sg, nc),
        in_specs=[fwd_row, bwd_row, fwd_row, bwd_row, fwd_mat, bwd_mat, fwd_mat, bwd_mat, st],
        out_specs=[pl.BlockSpec((sg, RK_CHUNK, RK_WIDTH), lambda g, c: (g, c, 0)),
                   pl.BlockSpec((sg, RK_CHUNK, RK_WIDTH), lambda g, c: (g, nc - 1 - c, 0)), st],
        out_shape=[out_sh, out_sh, jax.ShapeDtypeStruct((n_seq, 2, RK_NB, PAIR, PAIR), F32)],
        compiler_params=_cparams(("parallel", "arbitrary")),
        name="rk_state",
    )(rbar, rbar, ybar, ybar, phi, phi, psi, psi, s0)


def _pair_states(s):
    bsz = s.shape[0]
    h = jnp.swapaxes(s, -1, -2).reshape(bsz, 2, RK_NB, 2, HEAD_DIM, HEAD_DIM)
    return jnp.einsum("bdphkv,hg->bdphkgv", h, jnp.eye(2, dtype=F32)).reshape(bsz, 2, RK_NB, PAIR, PAIR)


def _head_states(s):
    bsz = s.shape[0]
    h = jnp.stack([s[..., :HEAD_DIM, :HEAD_DIM], s[..., HEAD_DIM:, HEAD_DIM:]], axis=3)
    return jnp.swapaxes(h.reshape(bsz, 2, RK_HEADS, HEAD_DIM, HEAD_DIM), -1, -2)


def _rwkv_group(q, tile0, n_seq, t, s0):
    rbar, ybar, phi, psi = _rk_chunk(q, tile0, n_seq, t)
    y_f, y_b, s_fin = _rk_state(rbar, ybar, phi, psi, s0)
    return y_f.reshape(n_seq * t, RK_WIDTH), y_b.reshape(n_seq * t, RK_WIDTH), s_fin


def _out_proj_kernel(x_c, x_l, ona_c, ona_l, osw_c, osw_l, yf_c, yf_l, yb_c, yb_l, bonus_ref, g_ref, lng_ref, lnb_ref,
                     w_ref, mod_ref, n2_ref, rw_ref, rb_ref, x1_ref, h2_ref, gate_ref, top_ref, *, n_ctx_tiles):
    is_ctx = pl.program_id(0) < n_ctx_tiles
    pick = lambda c_ref, l_ref: jnp.where(is_ctx, c_ref[...], l_ref[...])
    ona = pick(ona_c, ona_l)
    osw = pick(osw_c, osw_l)
    y = pick(yf_c, yf_l) + pick(yb_c, yb_l)
    parts = []
    for c in range(RK_NB):
        yc = y[:, c * LANES:(c + 1) * LANES]
        dc = yc - _pair_sum(yc) * (1.0 / HEAD_DIM)
        var = _pair_sum(dc * dc) * (1.0 / HEAD_DIM)
        parts.append(dc * lax.rsqrt(var + GN_EPS))
    yn = jnp.concatenate(parts, axis=1) * lng_ref[...] + lnb_ref[...]
    o_rk = (yn + bonus_ref[...]) * g_ref[...]
    o = (_dot(ona.astype(BF16), w_ref[0:NA_WIDTH, :])
         + _dot(osw.astype(BF16), w_ref[NA_WIDTH:NA_WIDTH + SWA_WIDTH, :])
         + _dot(o_rk.astype(BF16), w_ref[NA_WIDTH + SWA_WIDTH:, :]))
    x1 = pick(x_c, x_l) + mod_ref[0, 2:3, :] * o
    x1_ref[...] = x1
    yn2 = x1 * lax.rsqrt(jnp.mean(x1 * x1, axis=-1, keepdims=True) + RMS_EPS)
    h2 = (yn2 * n2_ref[...]) * (1.0 + mod_ref[0, 4:5, :]) + mod_ref[0, 3:4, :]
    h2_ref[...] = h2.astype(BF16)
    lane = lax.broadcasted_iota(jnp.int32, (h2.shape[0], LANES), 1).astype(F32)
    logit = jnp.where(lane < N_EXPERTS, _dot3(h2, rw_ref[...]) + rb_ref[...], -jnp.inf)
    vals, idxs = [], []
    for _ in range(TOP_K):
        best = jnp.max(logit, axis=-1, keepdims=True)
        idx = jnp.min(jnp.where(logit == best, lane, float(LANES)), axis=-1, keepdims=True)
        vals.append(best)
        idxs.append(idx)
        logit = jnp.where(lane == idx, -jnp.inf, logit)
    e = jnp.exp(jnp.concatenate(vals, axis=1) - vals[0])
    gate_ref[...] = e / jnp.sum(e, axis=-1, keepdims=True)
    top_ref[...] = jnp.concatenate(idxs, axis=1).astype(jnp.int32)


def _out_proj(x, o_na, o_sw, y_f, y_b, bonus, g, p, mods, tile_mod, n_ctx_tiles):
    n_tok = x[0].shape[0] + x[1].shape[0]
    tok = lambda i: (i, 0)
    const = lambda i: (0, 0)
    pair = lambda w: _group_tile_specs(w, n_ctx_tiles)
    return pl.pallas_call(
        functools.partial(_out_proj_kernel, n_ctx_tiles=n_ctx_tiles),
        grid=(n_tok // TOK_TILE,),
        in_specs=[
            *pair(D_MODEL), *pair(NA_WIDTH), *pair(SWA_WIDTH), *pair(RK_WIDTH), *pair(RK_WIDTH),
            pl.BlockSpec((TOK_TILE, RK_WIDTH), tok),
            pl.BlockSpec((TOK_TILE, RK_WIDTH), tok),
            pl.BlockSpec((1, RK_WIDTH), const),
            pl.BlockSpec((1, RK_WIDTH), const),
            pl.BlockSpec((D_MODEL, D_MODEL), const),
            pl.BlockSpec((1, 6, D_MODEL), lambda i: (tile_mod(i), 0, 0)),
            pl.BlockSpec((1, D_MODEL), const),
            pl.BlockSpec((D_MODEL, LANES), const),
            pl.BlockSpec((1, LANES), const),
        ],
        out_specs=[
            pl.BlockSpec((TOK_TILE, D_MODEL), tok),
            pl.BlockSpec((TOK_TILE, D_MODEL), tok),
            pl.BlockSpec((TOK_TILE, TOP_K), tok),
            pl.BlockSpec((TOK_TILE, TOP_K), tok),
        ],
        out_shape=[
            jax.ShapeDtypeStruct((n_tok, D_MODEL), F32),
            jax.ShapeDtypeStruct((n_tok, D_MODEL), BF16),
            jax.ShapeDtypeStruct((n_tok, TOP_K), F32),
            jax.ShapeDtypeStruct((n_tok, TOP_K), jnp.int32),
        ],
        compiler_params=_cparams(("parallel",)),
        name="out_proj",
    )(*x, *o_na, *o_sw, *y_f, *y_b, bonus, g, p["rk_ln_g"], p["rk_ln_b"], p["w_out_bf16"], mods, p["norm2_g"],
      p["router_w_pad"], p["router_b_pad"])


H2_PAD_ROWS = 32768
W1_SEL_COLS = 256
MOE_VMEM_LIMIT = 56 * 1024 * 1024


def _moe_kernel(meta_ref, x_ref, w1_ref, b1g_ref, b1l_ref, w2_ref, b2_ref, o_ref, w1g_ref, w1l_ref, w2b_ref):
    i = pl.program_id(0)
    n_blk = meta_ref.shape[0] - 1
    n_used = meta_ref[n_blk]
    d_e = w2_ref.shape[1]
    new_expert = jnp.logical_or(i == 0, meta_ref[i] != meta_ref[jnp.maximum(i - 1, 0)])

    @pl.when(jnp.logical_and(i < n_used, new_expert))
    def _():
        src = lax.broadcasted_iota(jnp.int32, (2 * W1_SEL_COLS, 2 * W1_SEL_COLS), 0)
        dst = lax.broadcasted_iota(jnp.int32, (2 * W1_SEL_COLS, 2 * W1_SEL_COLS), 1)
        pick = jnp.where(dst < W1_SEL_COLS, 2 * dst, 2 * (dst - W1_SEL_COLS) + 1)
        sel = jnp.where(src == pick, 1.0, 0.0).astype(BF16)
        for t in range(d_e // W1_SEL_COLS):
            cols = _dot(w1_ref[0, :, 2 * t * W1_SEL_COLS:2 * (t + 1) * W1_SEL_COLS].astype(BF16), sel)
            w1g_ref[:, t * W1_SEL_COLS:(t + 1) * W1_SEL_COLS] = cols[:, :W1_SEL_COLS].astype(BF16)
            w1l_ref[:, t * W1_SEL_COLS:(t + 1) * W1_SEL_COLS] = cols[:, W1_SEL_COLS:].astype(BF16)
        w2b_ref[...] = w2_ref[0].astype(BF16)

    @pl.when(i < n_used)
    def _():
        x = x_ref[...]
        glu = jnp.minimum(_dot(x, w1g_ref[...]) + b1g_ref[0], SWIGLU_LIMIT)
        lin = jnp.clip(_dot(x, w1l_ref[...]) + b1l_ref[0], -SWIGLU_LIMIT, SWIGLU_LIMIT)
        act = glu * jax.nn.sigmoid(SWIGLU_ALPHA * glu) * (lin + 1.0)
        o_ref[...] = (_dot(act.astype(BF16), w2b_ref[...]) + b2_ref[0]).astype(BF16)

    @pl.when(i >= n_used)
    def _():
        o_ref[...] = jnp.zeros_like(o_ref)


def _moe_blocks(meta, xb, w1, b1g, b1l, w2, b2, layer):
    n_rows = xb.shape[0]
    n_blk = n_rows // MOE_BLK
    d_e = w2.shape[2]
    row = lambda i, m: (i, 0)
    exp3 = lambda i, m: (layer, m[i], 0, 0)
    grid_spec = pltpu.PrefetchScalarGridSpec(
        num_scalar_prefetch=1,
        grid=(n_blk,),
        in_specs=[
            pl.BlockSpec((MOE_BLK, D_MODEL), row),
            pl.BlockSpec((None, 1, D_MODEL, 2 * d_e), exp3),
            pl.BlockSpec((None, 1, 1, d_e), exp3),
            pl.BlockSpec((None, 1, 1, d_e), exp3),
            pl.BlockSpec((None, 1, d_e, D_MODEL), exp3),
            pl.BlockSpec((None, 1, 1, D_MODEL), exp3),
        ],
        out_specs=pl.BlockSpec((MOE_BLK, D_MODEL), row),
        scratch_shapes=[
            pltpu.VMEM((D_MODEL, d_e), BF16),
            pltpu.VMEM((D_MODEL, d_e), BF16),
            pltpu.VMEM((d_e, D_MODEL), BF16),
        ],
    )
    return pl.pallas_call(
        _moe_kernel,
        grid_spec=grid_spec,
        out_shape=jax.ShapeDtypeStruct((n_rows, D_MODEL), BF16),
        compiler_params=pltpu.CompilerParams(dimension_semantics=("arbitrary",),
                                             vmem_limit_bytes=MOE_VMEM_LIMIT),
        name="moe_blocks",
    )(meta, xb, w1, b1g, b1l, w2, b2)


def _route(top_i):
    n_tok = top_i.shape[0]
    e_flat = top_i.reshape(-1)
    n_rows = n_tok * TOP_K
    onehot = (e_flat[:, None] == jnp.arange(N_EXPERTS, dtype=jnp.int32)[None, :]).astype(jnp.int32)
    csum = jnp.cumsum(onehot, axis=0)
    counts = jnp.sum(onehot, axis=0)
    starts = jnp.cumsum(counts) - counts
    pcounts = (counts + MOE_BLK - 1) // MOE_BLK * MOE_BLK
    pends = jnp.cumsum(pcounts)
    pstarts = pends - pcounts
    dest = jnp.take_along_axis(csum + (pstarts - 1)[None, :], e_flat[:, None], axis=1)[:, 0]
    n_blk = n_rows // MOE_BLK + N_EXPERTS
    blk_start = jnp.arange(n_blk, dtype=jnp.int32) * MOE_BLK
    blk_exp = jnp.minimum(jnp.sum((blk_start[:, None] >= pends[None, :]).astype(jnp.int32), axis=1), N_EXPERTS - 1)
    order = jnp.argsort(e_flat)
    pos = jnp.arange(n_blk * MOE_BLK, dtype=jnp.int32)
    src = (pos + jnp.repeat((starts - pstarts)[blk_exp], MOE_BLK)) % n_rows
    row_tok = order[src].astype(jnp.int32) // TOP_K
    meta = jnp.concatenate([blk_exp, (pends[-1:] // MOE_BLK).astype(jnp.int32)])
    return meta, row_tok, dest.reshape(n_tok, TOP_K).T.reshape(-1)


def _combine_kernel(x_ref, yg_ref, gate_ref, mod_ref, oc_ref, ol_ref, *, n_ctx_tiles):
    gate = gate_ref[...]
    acc = gate[:, 0:1] * yg_ref[0].astype(F32)
    for j in range(1, TOP_K):
        acc = acc + gate[:, j:j + 1] * yg_ref[j].astype(F32)
    out = x_ref[...] + mod_ref[0, 5:6, :] * acc
    is_ctx = pl.program_id(0) < n_ctx_tiles

    @pl.when(is_ctx)
    def _():
        oc_ref[...] = out

    @pl.when(jnp.logical_not(is_ctx))
    def _():
        ol_ref[...] = out


def _combine(x1, yg, gates, mods, tile_mod, n_ctx_tiles):
    n_tok = x1.shape[0]
    n_ctx = n_ctx_tiles * TOK_TILE
    return pl.pallas_call(
        functools.partial(_combine_kernel, n_ctx_tiles=n_ctx_tiles),
        grid=(n_tok // TOK_TILE,),
        in_specs=[
            pl.BlockSpec((TOK_TILE, D_MODEL), lambda i: (i, 0)),
            pl.BlockSpec((TOP_K, TOK_TILE, D_MODEL), lambda i: (0, i, 0)),
            pl.BlockSpec((TOK_TILE, TOP_K), lambda i: (i, 0)),
            pl.BlockSpec((1, 6, D_MODEL), lambda i: (tile_mod(i), 0, 0)),
        ],
        out_specs=_group_tile_specs(D_MODEL, n_ctx_tiles),
        out_shape=[jax.ShapeDtypeStruct((n_ctx, D_MODEL), F32), jax.ShapeDtypeStruct((n_tok - n_ctx, D_MODEL), F32)],
        compiler_params=_cparams(("arbitrary",)),
        name="moe_combine",
    )(x1, yg, gates, mods)


def kernel(x_prompt, x_sample, c, cache_na_k, cache_na_v, cache_swa_k, cache_swa_v, state_rwkv, c_ctx, w_ada, b_ada, norm1_g, norm2_g, w_in, w_out, na_q_norm, na_k_norm, na_rpb, swa_q_norm, swa_k_norm, swa_sink, rk_conv, rk_w0, rk_w2, rk_a0, rk_a2, rk_g2, rk_k_k, rk_k_a, rk_r_k, rk_ln_g, rk_ln_b, moe_router_w, moe_router_b, moe_w1, moe_b1, moe_w2, moe_b2):
    bc, tc, _ = x_prompt.shape
    bl, tl, _ = x_sample.shape
    depth = w_in.shape[0]
    n_ctx = bc * tc
    n_lat = bl * tl
    assert tc == TOK_TILE and tl % TOK_TILE == 0 and n_ctx % tl == 0
    n_ctx_tiles = n_ctx // TOK_TILE
    tiles_per_seq = tl // TOK_TILE
    past = cache_na_k.shape[2]

    def tile_mod(i):
        return jnp.where(i < n_ctx_tiles, 0, 1 + (i - n_ctx_tiles) // tiles_per_seq)

    def tile_rope(i):
        return jnp.where(i < n_ctx_tiles, tiles_per_seq, (i - n_ctx_tiles) % tiles_per_seq)

    x = (x_prompt.reshape(n_ctx, D_MODEL), x_sample.reshape(n_lat, D_MODEL))

    n_mod = 1 + bl
    mod_rows = -(-n_mod // 8) * 8
    cvecs = jnp.concatenate([c_ctx[None, :], c, jnp.zeros((mod_rows - n_mod, D_MODEL), F32)], axis=0)
    mods_all = _ada_mod(cvecs, w_ada, b_ada).reshape(depth, mod_rows, 6, D_MODEL)
    cos_tab, sin_tab = _rope_tables(tl)
    tile2 = lambda g: jnp.concatenate([g, g])[None, :]
    pad_lanes = lambda z: jnp.pad(z, ((0, 0), (0, LANES - z.shape[1])))
    zeros_lora = jnp.zeros((2, RK_DECAY_LORA, RK_WIDTH), F32)

    na_k_l, na_v_l, sw_k_l, sw_v_l, st_l = [], [], [], [], []
    for l in range(depth):
        mods = mods_all[l]
        qk_gains = jnp.concatenate(
            [tile2(na_q_norm[l]), tile2(na_k_norm[l]), tile2(swa_q_norm[l]), tile2(swa_k_norm[l])], axis=0)
        p = {
            "rk_conv": rk_conv[l], "rk_w0": rk_w0[l], "rk_a0": rk_a0[l], "rk_g2": rk_g2[l],
            "rk_w2_pad": jnp.concatenate([rk_w2[l], zeros_lora], axis=1),
            "rk_a2_pad": jnp.concatenate([zeros_lora, rk_a2[l]], axis=1),
            "rk_k_k": rk_k_k[l][None, :], "rk_k_a": rk_k_a[l][None, :],
            "rk_r_k": rk_r_k[l].reshape(1, RK_WIDTH),
            "rk_ln_g": rk_ln_g[l][None, :], "rk_ln_b": rk_ln_b[l][None, :],
            "w_out_bf16": w_out[l].astype(BF16), "norm2_g": norm2_g[l][None, :],
            "router_w_pad": pad_lanes(moe_router_w[l]), "router_b_pad": pad_lanes(moe_router_b[l][None, :]),
        }

        att, u = _in_proj(x, norm1_g[l][None, :], mods, w_in[l].astype(BF16), qk_gains, cos_tab, sin_tab,
                          tile_mod, tile_rope, n_ctx_tiles)
        q, g, bonus = _rk_prep(u, p, n_ctx_tiles, tiles_per_seq)
        att_c = att[:n_ctx].reshape(bc, tc, ATT_COLS)
        att_by_ctx_len = att.reshape((n_ctx + n_lat) // tc, tc, ATT_COLS)
        att_by_lat_len = att.reshape((n_ctx + n_lat) // tl, tl, ATT_COLS)
        na_k_l.append(att_c[:, :, NA_WIDTH:2 * NA_WIDTH].reshape(bc, tc, NA_HEADS, HEAD_DIM))
        na_v_l.append(att_c[:, :, 2 * NA_WIDTH:NA_COLS].reshape(bc, tc, NA_HEADS, HEAD_DIM))
        sw_k_l.append(att_c[:, :, NA_COLS + SWA_WIDTH:NA_COLS + SWA_WIDTH + SWA_KV_WIDTH]
                      .reshape(bc, tc, SWA_KV_HEADS, HEAD_DIM))
        sw_v_l.append(att_c[:, :, NA_COLS + SWA_WIDTH + SWA_KV_WIDTH:].reshape(bc, tc, SWA_KV_HEADS, HEAD_DIM))

        sink = swa_sink[l]
        o_na = (_ctx_attn(att_by_ctx_len, bc, sink, gqa=False).reshape(n_ctx, NA_WIDTH),
                _na_latent(att_by_lat_len, n_ctx // tl, cache_na_k[:, l].reshape(bl, past, NA_WIDTH),
                           cache_na_v[:, l].reshape(bl, past, NA_WIDTH),
                           _na_bias_tables(na_rpb[l])).reshape(n_lat, NA_WIDTH))
        o_sw = (_ctx_attn(att_by_ctx_len, bc, sink, gqa=True).reshape(n_ctx, SWA_WIDTH),
                _swa_latent(att_by_lat_len, n_ctx // tl, cache_swa_k[:, l].reshape(bl, past, SWA_KV_WIDTH),
                            cache_swa_v[:, l].reshape(bl, past, SWA_KV_WIDTH), sink).reshape(n_lat, SWA_WIDTH))

        yf_c, yb_c, s_fin = _rwkv_group(q, 0, bc, tc, jnp.zeros((bc, 2, RK_NB, PAIR, PAIR), F32))
        yf_l, yb_l, _ = _rwkv_group(q, n_ctx // RK_CHUNK, bl, tl, _pair_states(state_rwkv[:, l]))
        st_l.append(_head_states(s_fin))

        x1, h2, gates, top_i = _out_proj(x, o_na, o_sw, (yf_c, yf_l), (yb_c, yb_l), bonus, g, p, mods, tile_mod,
                                         n_ctx_tiles)
        meta, row_tok, dest = _route(top_i)
        h2 = jnp.concatenate([h2, jnp.zeros((H2_PAD_ROWS - h2.shape[0], D_MODEL), BF16)], axis=0)
        yb = _moe_blocks(meta, h2[row_tok], moe_w1, moe_b1[:, :, None, 0::2], moe_b1[:, :, None, 1::2], moe_w2,
                         moe_b2[:, :, None, :], l)
        x = _combine(x1, yb[dest].reshape(TOP_K, n_ctx + n_lat, D_MODEL), gates, mods, tile_mod, n_ctx_tiles)

    y_p = x[0].reshape(bc, tc, D_MODEL)
    y_s = x[1].reshape(bl, tl, D_MODEL)
    return (y_p, y_s, jnp.stack(na_k_l, axis=1), jnp.stack(na_v_l, axis=1), jnp.stack(sw_k_l, axis=1),
            jnp.stack(sw_v_l, axis=1), jnp.stack(st_l, axis=1))
```

```python
import functools

import jax
import jax.numpy as jnp
from jax import lax
from jax.experimental import pallas as pl
from jax.experimental.pallas import tpu as pltpu

F32 = jnp.float32
BF16 = jnp.bfloat16

D_MODEL = 1024
HEAD_DIM = 64
LANES = 128
GRID_W = 64
NA_HEADS = 6
SWA_HEADS = 4
SWA_KV_HEADS = 2
RK_HEADS = 6
NA_WIDTH = NA_HEADS * HEAD_DIM
SWA_WIDTH = SWA_HEADS * HEAD_DIM
SWA_KV_WIDTH = SWA_KV_HEADS * HEAD_DIM
RK_WIDTH = RK_HEADS * HEAD_DIM
RK_DECAY_LORA = 64
RK_A_LORA = 64
RK_GATE_LORA = 128
RK_COLS = 3 * RK_WIDTH + RK_DECAY_LORA + RK_A_LORA + RK_GATE_LORA
NA_COLS = 3 * NA_WIDTH
SWA_COLS = SWA_WIDTH + 2 * SWA_KV_WIDTH
ATT_COLS = NA_COLS + SWA_COLS
IN_COLS = ATT_COLS + RK_COLS
NA_WIN_R = 8
NA_WIN_C = 16
SWA_WIN = 128
ROPE_THETA = 10000.0
ATTN_SCALE = HEAD_DIM ** -0.5
N_EXPERTS = 32
TOP_K = 4
SWIGLU_LIMIT = 7.0
SWIGLU_ALPHA = 1.702
MOE_BLK = 256
RMS_EPS = 1e-6
GN_EPS = 64e-5
NEG_BIG = -1e30

TOK_TILE = 256
VMEM_LIMIT = 48 * 1024 * 1024


def _cparams(sem):
    return pltpu.CompilerParams(dimension_semantics=sem, vmem_limit_bytes=VMEM_LIMIT)


def _dot(a, b):
    return jnp.dot(a, b, preferred_element_type=F32)


def _dot_nt(a, b):
    return lax.dot_general(a, b, (((1,), (1,)), ((), ())), preferred_element_type=F32)


def _split_bf16(x):
    hi = x.astype(BF16)
    lo = (x - hi.astype(F32)).astype(BF16)
    return hi, lo


def _dot3(a, b):
    ah, al = _split_bf16(a)
    bh, bl = _split_bf16(b)
    return _dot(ah, bh) + (_dot(ah, bl) + _dot(al, bh))


def _bmm_raw(a, b):
    return lax.dot_general(a, b, (((2,), (1,)), ((0,), (0,))), preferred_element_type=F32)


def _bmm(a, b):
    return _bmm_raw(a.astype(BF16), b.astype(BF16))


def _bmm_nt(a, b):
    return lax.dot_general(a.astype(BF16), b.astype(BF16), (((2,), (2,)), ((0,), (0,))),
                           preferred_element_type=F32)


def _bmm3(a, b):
    ah, al = _split_bf16(a)
    bh, bl = _split_bf16(b)
    return _bmm_raw(ah, bh) + (_bmm_raw(ah, bl) + _bmm_raw(al, bh))


def _lane_lo(shape):
    return lax.broadcasted_iota(jnp.int32, shape, len(shape) - 1) < HEAD_DIM


def _pair_sum(x):
    lo = _lane_lo(x.shape)
    s_lo = jnp.sum(jnp.where(lo, x, 0.0), axis=-1, keepdims=True)
    s_hi = jnp.sum(jnp.where(lo, 0.0, x), axis=-1, keepdims=True)
    return jnp.where(lo, s_lo, s_hi)


def _stack_heads(q):
    lo = _lane_lo(q.shape)
    return jnp.concatenate([jnp.where(lo, q, 0.0), jnp.where(lo, 0.0, q)], axis=0)


def _stack_heads3(x):
    lo = _lane_lo(x.shape)
    return jnp.concatenate([jnp.where(lo, x, 0.0), jnp.where(lo, 0.0, x)], axis=1)


def _unstack_heads(o2):
    n = o2.shape[0] // 2
    return jnp.where(_lane_lo((n, LANES)), o2[:n], o2[n:])


def _dup_head(x, j):
    keep = _lane_lo(x.shape) == (j == 0)
    return jnp.where(keep, x, pltpu.roll(x, HEAD_DIM, 1))


def _ada_kernel(c_ref, w_ref, b_ref, o_ref):
    cv = c_ref[...]
    s = cv * jax.nn.sigmoid(cv)
    o_ref[0] = _dot3(s, w_ref[0]) + b_ref[0]


def _ada_mod(cvecs, w_ada, b_ada):
    depth, _, n_out = w_ada.shape
    rows = cvecs.shape[0]
    tn = 1024
    return pl.pallas_call(
        _ada_kernel,
        grid=(depth, n_out // tn),
        in_specs=[
            pl.BlockSpec((rows, D_MODEL), lambda l, j: (0, 0)),
            pl.BlockSpec((1, D_MODEL, tn), lambda l, j: (l, 0, j)),
            pl.BlockSpec((1, 1, tn), lambda l, j: (l, 0, j)),
        ],
        out_specs=pl.BlockSpec((1, rows, tn), lambda l, j: (l, 0, j)),
        out_shape=jax.ShapeDtypeStruct((depth, rows, n_out), F32),
        compiler_params=_cparams(("parallel", "parallel")),
        name="ada_mod",
    )(cvecs, w_ada, b_ada.reshape(depth, 1, n_out))


NA_QK_BLOCKS = 2 * NA_WIDTH // LANES
SWA_Q_BLOCK0 = NA_COLS // LANES
SWA_QK_BLOCKS = (SWA_WIDTH + SWA_KV_WIDTH) // LANES


def _in_proj_kernel(xc_ref, xl_ref, g_ref, mod_ref, w_ref, qkg_ref, cos_ref, sin_ref, att_ref, u_ref, *, n_ctx_tiles):
    x = jnp.where(pl.program_id(0) < n_ctx_tiles, xc_ref[...], xl_ref[...])
    y = x * lax.rsqrt(jnp.mean(x * x, axis=-1, keepdims=True) + RMS_EPS)
    h = (y * g_ref[...]) * (1.0 + mod_ref[0, 1:2, :]) + mod_ref[0, 0:1, :]
    proj = _dot(h.astype(BF16), w_ref[...])
    u_ref[...] = proj[:, ATT_COLS:]

    def qk_norm(blk, gain):
        ms = _pair_sum(blk * blk) * (1.0 / HEAD_DIM)
        return blk * lax.rsqrt(ms + RMS_EPS) * gain

    lane = lax.broadcasted_iota(jnp.int32, (x.shape[0], LANES), 1)
    first = (lane % (HEAD_DIM // 2)) < (HEAD_DIM // 4)
    for cb in range(ATT_COLS // LANES):
        blk = proj[:, cb * LANES:(cb + 1) * LANES]
        if cb < NA_QK_BLOCKS:
            gi = 0 if cb < NA_QK_BLOCKS // 2 else 1
            blk = qk_norm(blk, qkg_ref[gi:gi + 1, :])
        elif SWA_Q_BLOCK0 <= cb < SWA_Q_BLOCK0 + SWA_QK_BLOCKS:
            gi = 2 if cb < SWA_Q_BLOCK0 + SWA_WIDTH // LANES else 3
            blk = qk_norm(blk, qkg_ref[gi:gi + 1, :])
            partner = jnp.where(first, pltpu.roll(blk, LANES - HEAD_DIM // 4, 1),
                                pltpu.roll(blk, HEAD_DIM // 4, 1))
            blk = blk * cos_ref[...] + partner * sin_ref[...]
        att_ref[:, cb * LANES:(cb + 1) * LANES] = blk


def _group_tile_specs(width, n_ctx_tiles):
    return [pl.BlockSpec((TOK_TILE, width), lambda i: (jnp.minimum(i, n_ctx_tiles - 1), 0)),
            pl.BlockSpec((TOK_TILE, width), lambda i: (jnp.maximum(i - n_ctx_tiles, 0), 0))]


def _in_proj(x, norm_g, mods, w_in_bf16, qk_gains, cos_tab, sin_tab, tile_mod, tile_rope, n_ctx_tiles):
    n_tok = x[0].shape[0] + x[1].shape[0]
    return pl.pallas_call(
        functools.partial(_in_proj_kernel, n_ctx_tiles=n_ctx_tiles),
        grid=(n_tok // TOK_TILE,),
        in_specs=[
            *_group_tile_specs(D_MODEL, n_ctx_tiles),
            pl.BlockSpec((1, D_MODEL), lambda i: (0, 0)),
            pl.BlockSpec((1, 6, D_MODEL), lambda i: (tile_mod(i), 0, 0)),
            pl.BlockSpec((D_MODEL, IN_COLS), lambda i: (0, 0)),
            pl.BlockSpec((4, LANES), lambda i: (0, 0)),
            pl.BlockSpec((TOK_TILE, LANES), lambda i: (tile_rope(i), 0)),
            pl.BlockSpec((TOK_TILE, LANES), lambda i: (tile_rope(i), 0)),
        ],
        out_specs=[
            pl.BlockSpec((TOK_TILE, ATT_COLS), lambda i: (i, 0)),
            pl.BlockSpec((TOK_TILE, RK_COLS), lambda i: (i, 0)),
        ],
        out_shape=[
            jax.ShapeDtypeStruct((n_tok, ATT_COLS), F32),
            jax.ShapeDtypeStruct((n_tok, RK_COLS), F32),
        ],
        compiler_params=_cparams(("parallel",)),
        name="in_proj",
    )(*x, norm_g, mods, w_in_bf16, qk_gains, cos_tab, sin_tab)


def _rope_tables(n_lat):
    nf = HEAD_DIM // 4
    t = jnp.arange(n_lat)
    lane = jnp.arange(LANES)
    d = lane % HEAD_DIM
    inv = ROPE_THETA ** (-(d % nf).astype(F32) / nf)
    pos = jnp.where((d // (2 * nf))[None, :] == 0, (t // GRID_W)[:, None], (t % GRID_W)[:, None]).astype(F32)
    ang = pos * inv[None, :]
    sign = jnp.where((d % (2 * nf)) < nf, -1.0, 1.0).astype(F32)
    cos = jnp.concatenate([jnp.cos(ang), jnp.ones((TOK_TILE, LANES), F32)], 0)
    sin = jnp.concatenate([jnp.sin(ang) * sign[None, :], jnp.zeros((TOK_TILE, LANES), F32)], 0)
    return cos, sin


def _ctx_attn_kernel(sink_ref, q_ref, k_ref, v_ref, o_ref, *, gqa):
    j = pl.program_id(1)
    k = k_ref[0]
    v = v_ref[0]
    if gqa:
        k = _dup_head(k, j)
        v = _dup_head(v, j)
    n = k.shape[0]
    q2 = _stack_heads(q_ref[0]).astype(BF16)
    s = _dot_nt(q2, k.astype(BF16)) * ATTN_SCALE
    m = jnp.max(s, axis=-1, keepdims=True)
    if gqa:
        row = lax.broadcasted_iota(jnp.int32, (2 * n, 1), 0)
        snk = jnp.where(row < n, sink_ref[2 * j], sink_ref[2 * j + 1])
        m = jnp.maximum(m, snk)
    p = jnp.exp(s - m)
    den = jnp.sum(p, axis=-1, keepdims=True)
    if gqa:
        den = den + jnp.exp(snk - m)
    o2 = _dot(p.astype(BF16), v.astype(BF16)) / den
    o_ref[0] = _unstack_heads(o2).astype(o_ref.dtype)


def _ctx_attn(att, b, sink, *, gqa):
    t = att.shape[1]
    if gqa:
        nq = SWA_WIDTH // LANES
        qb, kb, vb = SWA_Q_BLOCK0, SWA_Q_BLOCK0 + nq, SWA_Q_BLOCK0 + nq + 1
        kmap = lambda bi, j: (bi, 0, kb)
        vmap = lambda bi, j: (bi, 0, vb)
    else:
        nq = NA_WIDTH // LANES
        qb, kb, vb = 0, nq, 2 * nq
        kmap = lambda bi, j: (bi, 0, kb + j)
        vmap = lambda bi, j: (bi, 0, vb + j)
    return pl.pallas_call(
        functools.partial(_ctx_attn_kernel, gqa=gqa),
        grid=(b, nq),
        in_specs=[
            pl.BlockSpec(memory_space=pltpu.SMEM),
            pl.BlockSpec((1, t, LANES), lambda bi, j: (bi, 0, qb + j)),
            pl.BlockSpec((1, t, LANES), kmap),
            pl.BlockSpec((1, t, LANES), vmap),
        ],
        out_specs=pl.BlockSpec((1, t, LANES), lambda bi, j: (bi, 0, j)),
        out_shape=jax.ShapeDtypeStruct((b, t, nq * LANES), BF16),
        compiler_params=_cparams(("parallel", "parallel")),
        name="ctx_attn_swa" if gqa else "ctx_attn_na",
    )(sink, att, att, att)


NA_ROWS_PER_ITER = 8


def _na_lat_kernel(q_ref, k_ref, v_ref, kc_ref, vc_ref, tab_ref, o_ref, kb_ref, vb_ref):
    n = q_ref.shape[1]
    rows = n // GRID_W
    win = NA_WIN_R * GRID_W
    kb_ref[...] = k_ref[0].astype(BF16)
    vb_ref[...] = v_ref[0].astype(BF16)
    kc = kc_ref[0].astype(BF16)
    vc = vc_ref[0].astype(BF16)

    def row_group(ig, carry):
        nr = NA_ROWS_PER_ITER
        g0 = pl.multiple_of(ig * (nr * GRID_W), nr * GRID_W)
        q2 = _stack_heads3(q_ref[0, pl.ds(g0, nr * GRID_W), :].reshape(nr, GRID_W, LANES)).astype(BF16)
        kws, vws, biases = [], [], []
        for r in range(nr):
            i = ig * nr + r
            start = jnp.clip(i - NA_WIN_R // 2, 0, rows - NA_WIN_R)
            k0 = pl.multiple_of(start * GRID_W, GRID_W)
            kws.append(kb_ref[pl.ds(k0, win), :])
            vws.append(vb_ref[pl.ds(k0, win), :])
            biases.append(tab_ref[0, start - i + (NA_WIN_R - 1)])
        s_loc = _bmm_nt(q2, jnp.stack(kws)) * ATTN_SCALE + jnp.stack(biases)
        s_ctx = _dot_nt(q2.reshape(nr * 2 * GRID_W, LANES), kc).reshape(nr, 2 * GRID_W, -1) * ATTN_SCALE
        m = jnp.maximum(jnp.max(s_loc, axis=-1, keepdims=True), jnp.max(s_ctx, axis=-1, keepdims=True))
        p_loc = jnp.exp(s_loc - m)
        p_ctx = jnp.exp(s_ctx - m)
        den = jnp.sum(p_loc, axis=-1, keepdims=True) + jnp.sum(p_ctx, axis=-1, keepdims=True)
        o_ctx = _dot(p_ctx.reshape(nr * 2 * GRID_W, -1).astype(BF16), vc).reshape(nr, 2 * GRID_W, LANES)
        o2 = (_bmm(p_loc, jnp.stack(vws)) + o_ctx) / den
        out = jnp.where(_lane_lo((nr, GRID_W, LANES)), o2[:, :GRID_W], o2[:, GRID_W:])
        o_ref[0, pl.ds(g0, nr * GRID_W), :] = out.reshape(nr * GRID_W, LANES).astype(o_ref.dtype)
        return carry

    lax.fori_loop(0, rows // NA_ROWS_PER_ITER, row_group, 0)


def _na_bias_tables(rpb):
    col = jnp.arange(GRID_W)
    cstart = jnp.clip(col - NA_WIN_C // 2, 0, GRID_W - NA_WIN_C)
    col_mask = (col[None, :] >= cstart[:, None]) & (col[None, :] < cstart[:, None] + NA_WIN_C)
    col_idx = jnp.clip(col[None, :] - col[:, None] + NA_WIN_C - 1, 0, 2 * NA_WIN_C - 2)
    rpb_cols = jnp.where(col_mask[None, None], rpb[:, :, col_idx], NEG_BIG)
    roff = jnp.arange(NA_WIN_R)[:, None] + jnp.arange(NA_WIN_R)[None, :]
    t = rpb_cols[:, roff]
    t = jnp.transpose(t, (0, 1, 3, 2, 4)).reshape(NA_HEADS // 2, 2, NA_WIN_R, GRID_W, NA_WIN_R * GRID_W)
    return jnp.transpose(t, (0, 2, 1, 3, 4)).reshape(NA_HEADS // 2, NA_WIN_R, 2 * GRID_W, NA_WIN_R * GRID_W)


def _na_latent(att, s0, kc, vc, tab):
    n = att.shape[1]
    b, p, _ = kc.shape
    nq = NA_WIDTH // LANES
    return pl.pallas_call(
        _na_lat_kernel,
        grid=(b, nq),
        in_specs=[
            pl.BlockSpec((1, n, LANES), lambda bi, j: (s0 + bi, 0, j)),
            pl.BlockSpec((1, n, LANES), lambda bi, j: (s0 + bi, 0, nq + j)),
            pl.BlockSpec((1, n, LANES), lambda bi, j: (s0 + bi, 0, 2 * nq + j)),
            pl.BlockSpec((1, p, LANES), lambda bi, j: (bi, 0, j)),
            pl.BlockSpec((1, p, LANES), lambda bi, j: (bi, 0, j)),
            pl.BlockSpec((1, NA_WIN_R, 2 * GRID_W, NA_WIN_R * GRID_W), lambda bi, j: (j, 0, 0, 0)),
        ],
        out_specs=pl.BlockSpec((1, n, LANES), lambda bi, j: (bi, 0, j)),
        out_shape=jax.ShapeDtypeStruct((b, n, NA_WIDTH), BF16),
        scratch_shapes=[pltpu.VMEM((n, LANES), BF16), pltpu.VMEM((n, LANES), BF16)],
        compiler_params=_cparams(("parallel", "parallel")),
        name="na_latent",
    )(att, att, att, kc, vc, tab)


SWA_BLOCKS_PER_ITER = 4


def _swa_lat_kernel(sink_ref, q_ref, k_ref, v_ref, kc_ref, vc_ref, o_ref, kb_ref, vb_ref):
    j = pl.program_id(1)
    n = q_ref.shape[1]
    blk = SWA_WIN
    span = 3 * blk
    kb_ref[...] = _dup_head(k_ref[0], j).astype(BF16)
    vb_ref[...] = _dup_head(v_ref[0], j).astype(BF16)
    kc = _dup_head(kc_ref[0], j).astype(BF16)
    vc = _dup_head(vc_ref[0], j).astype(BF16)
    row = lax.broadcasted_iota(jnp.int32, (2 * blk, 1), 0)
    snk = jnp.where(row < blk, sink_ref[2 * j], sink_ref[2 * j + 1])
    qoff = lax.broadcasted_iota(jnp.int32, (2 * blk, span), 0) % blk
    koff = lax.broadcasted_iota(jnp.int32, (2 * blk, span), 1)

    def q_group(qg, carry):
        nr = SWA_BLOCKS_PER_ITER
        g0 = pl.multiple_of(qg * (nr * blk), nr * blk)
        q2 = _stack_heads3(q_ref[0, pl.ds(g0, nr * blk), :].reshape(nr, blk, LANES)).astype(BF16)
        kws, vws, valids = [], [], []
        for r in range(nr):
            q0 = g0 + r * blk
            w0 = pl.multiple_of(jnp.clip(q0 - blk, 0, n - span), blk)
            kws.append(kb_ref[pl.ds(w0, span), :])
            vws.append(vb_ref[pl.ds(w0, span), :])
            valids.append(jnp.abs((q0 + qoff) - (w0 + koff)) <= SWA_WIN)
        s_loc = jnp.where(jnp.stack(valids), _bmm_nt(q2, jnp.stack(kws)) * ATTN_SCALE, NEG_BIG)
        s_ctx = _dot_nt(q2.reshape(nr * 2 * blk, LANES), kc).reshape(nr, 2 * blk, -1) * ATTN_SCALE
        m = jnp.maximum(jnp.max(s_loc, axis=-1, keepdims=True), jnp.max(s_ctx, axis=-1, keepdims=True))
        m = jnp.maximum(m, snk)
        p_loc = jnp.exp(s_loc - m)
        p_ctx = jnp.exp(s_ctx - m)
        den = (jnp.sum(p_loc, axis=-1, keepdims=True) + jnp.sum(p_ctx, axis=-1, keepdims=True)
               + jnp.exp(snk - m))
        o_ctx = _dot(p_ctx.reshape(nr * 2 * blk, -1).astype(BF16), vc).reshape(nr, 2 * blk, LANES)
        o2 = (_bmm(p_loc, jnp.stack(vws)) + o_ctx) / den
        out = jnp.where(_lane_lo((nr, blk, LANES)), o2[:, :blk], o2[:, blk:])
        o_ref[0, pl.ds(g0, nr * blk), :] = out.reshape(nr * blk, LANES).astype(o_ref.dtype)
        return carry

    lax.fori_loop(0, n // (SWA_BLOCKS_PER_ITER * blk), q_group, 0)


def _swa_latent(att, s0, kc, vc, sink):
    n = att.shape[1]
    b, p, _ = kc.shape
    nq = SWA_WIDTH // LANES
    qb, kb, vb = SWA_Q_BLOCK0, SWA_Q_BLOCK0 + nq, SWA_Q_BLOCK0 + nq + 1
    return pl.pallas_call(
        _swa_lat_kernel,
        grid=(b, nq),
        in_specs=[
            pl.BlockSpec(memory_space=pltpu.SMEM),
            pl.BlockSpec((1, n, LANES), lambda bi, j: (s0 + bi, 0, qb + j)),
            pl.BlockSpec((1, n, LANES), lambda bi, j: (s0 + bi, 0, kb)),
            pl.BlockSpec((1, n, LANES), lambda bi, j: (s0 + bi, 0, vb)),
            pl.BlockSpec((1, p, LANES), lambda bi, j: (bi, 0, 0)),
            pl.BlockSpec((1, p, LANES), lambda bi, j: (bi, 0, 0)),
        ],
        out_specs=pl.BlockSpec((1, n, LANES), lambda bi, j: (bi, 0, j)),
        out_shape=jax.ShapeDtypeStruct((b, n, SWA_WIDTH), BF16),
        scratch_shapes=[pltpu.VMEM((n, LANES), BF16), pltpu.VMEM((n, LANES), BF16)],
        compiler_params=_cparams(("parallel", "parallel")),
        name="swa_latent",
    )(sink, att, att, att, kc, vc)


RK_NB = RK_WIDTH // LANES
LORA_BLOCK = 3 * RK_WIDTH // LANES
GATE_BLOCK = LORA_BLOCK + 1
Q_R, Q_V, Q_A, Q_W, Q_K, Q_B = range(6)
Q_DIR = 3
Q_COLS = (6 + Q_DIR) * RK_WIDTH


def _softplus(x):
    return jnp.maximum(x, 0.0) + jnp.log(1.0 + jnp.exp(-jnp.abs(x)))


def _rk_prep_kernel(u_ref, up_ref, un_ref, cw_ref, w0_ref, w2_ref, a0_ref, a2_ref, g2_ref, kk_ref, ka_ref,
                    rk_ref, q_ref, g_ref, bonus_ref, *, n_ctx_tiles, tiles_per_seq):
    def put(slot, val):
        q_ref[:, slot * RK_WIDTH:(slot + 1) * RK_WIDTH] = val

    i = pl.program_id(0)
    li = i - n_ctx_tiles
    is_lat = i >= n_ctx_tiles
    has_prev = jnp.logical_and(is_lat, li % tiles_per_seq != 0)
    has_next = jnp.logical_and(is_lat, li % tiles_per_seq != tiles_per_seq - 1)
    u = u_ref[...]
    tm = u.shape[0]
    prev_row = jnp.where(has_prev, up_ref[7:8, :], 0.0)
    next_row = jnp.where(has_next, un_ref[0:1, :], 0.0)
    row = lax.broadcasted_iota(jnp.int32, u.shape, 0)
    um = jnp.where(row == 0, prev_row, pltpu.roll(u, 1, 0))
    up = jnp.where(row == tm - 1, next_row, pltpu.roll(u, tm - 1, 0))
    u = um * cw_ref[0:1, :] + u * cw_ref[1:2, :] + up * cw_ref[2:3, :]

    r = u[:, 0:RK_WIDTH]
    k = u[:, RK_WIDTH:2 * RK_WIDTH]
    v = u[:, 2 * RK_WIDTH:3 * RK_WIDTH]
    lora = u[:, LORA_BLOCK * LANES:(LORA_BLOCK + 1) * LANES]
    gl = u[:, GATE_BLOCK * LANES:(GATE_BLOCK + 1) * LANES]
    put(Q_R, r)
    put(Q_V, v)
    g_ref[...] = _dot3(jax.nn.sigmoid(gl), g2_ref[...])

    kn = k * kk_ref[...]
    kk = jnp.concatenate(
        [kn[:, c * LANES:(c + 1) * LANES]
         * lax.rsqrt(jnp.maximum(_pair_sum(jnp.square(kn[:, c * LANES:(c + 1) * LANES])), 1e-24))
         for c in range(RK_NB)], axis=1)
    put(Q_A, -kk)

    lora_t = jnp.tanh(lora)
    kd_sum = None
    for d in range(2):
        w = -_softplus(-(w0_ref[d:d + 1, :] + _dot3(lora_t, w2_ref[d]))) - 0.5
        put(Q_W + Q_DIR * d, -jnp.exp(w))
        a = jax.nn.sigmoid(a0_ref[d:d + 1, :] + _dot3(lora, a2_ref[d]))
        kd = k * (1.0 + (a - 1.0) * ka_ref[...])
        put(Q_K + Q_DIR * d, kd)
        put(Q_B + Q_DIR * d, kk * a)
        kd_sum = kd if kd_sum is None else kd_sum + kd

    t = r * kd_sum * rk_ref[...]
    bonus_ref[...] = jnp.concatenate(
        [_pair_sum(t[:, c * LANES:(c + 1) * LANES]) for c in range(RK_NB)], axis=1) * v


def _rk_prep(u, p, n_ctx_tiles, tiles_per_seq):
    n_tok = u.shape[0]
    n_tiles = n_tok // TOK_TILE
    sub = TOK_TILE // 8
    last8 = n_tok // 8 - 1
    tok = lambda i: (i, 0)
    const2 = lambda i: (0, 0)
    const3 = lambda i: (0, 0, 0)
    one = jax.ShapeDtypeStruct((n_tok, RK_WIDTH), F32)
    tok_spec = pl.BlockSpec((TOK_TILE, RK_WIDTH), tok)
    return pl.pallas_call(
        functools.partial(_rk_prep_kernel, n_ctx_tiles=n_ctx_tiles, tiles_per_seq=tiles_per_seq),
        grid=(n_tiles,),
        in_specs=[
            pl.BlockSpec((TOK_TILE, RK_COLS), tok),
            pl.BlockSpec((8, RK_COLS), lambda i: (jnp.maximum(i * sub - 1, 0), 0)),
            pl.BlockSpec((8, RK_COLS), lambda i: (jnp.minimum((i + 1) * sub, last8), 0)),
            pl.BlockSpec((3, RK_COLS), const2),
            pl.BlockSpec((2, RK_WIDTH), const2),
            pl.BlockSpec((2, LANES, RK_WIDTH), const3),
            pl.BlockSpec((2, RK_WIDTH), const2),
            pl.BlockSpec((2, LANES, RK_WIDTH), const3),
            pl.BlockSpec((RK_GATE_LORA, RK_WIDTH), const2),
            pl.BlockSpec((1, RK_WIDTH), const2),
            pl.BlockSpec((1, RK_WIDTH), const2),
            pl.BlockSpec((1, RK_WIDTH), const2),
        ],
        out_specs=[pl.BlockSpec((TOK_TILE, Q_COLS), tok), tok_spec, tok_spec],
        out_shape=[jax.ShapeDtypeStruct((n_tok, Q_COLS), F32), one, one],
        compiler_params=_cparams(("parallel",)),
        name="rk_prep",
    )(u, u, u, p["rk_conv"], p["rk_w0"], p["rk_w2_pad"], p["rk_a0"], p["rk_a2_pad"], p["rk_g2"],
      p["rk_k_k"], p["rk_k_a"], p["rk_r_k"])


RK_CHUNK = 64
PAIR = 2 * HEAD_DIM
STATE_SEQS = 8
RK_STEP_CHUNKS = 4


def _split3_bf16(x):
    hi = x.astype(BF16)
    r1 = x - hi.astype(F32)
    mid = r1.astype(BF16)
    return hi, mid, (r1 - mid.astype(F32)).astype(BF16)


def _pack_pair(m):
    return jnp.concatenate([m[:HEAD_DIM, :HEAD_DIM], m[HEAD_DIM:, HEAD_DIM:]], axis=1)


def _unpack_pairs(m):
    lo = _lane_lo(m.shape)
    return jnp.concatenate([jnp.where(lo, m, 0.0), jnp.where(lo, 0.0, m)], axis=1)


def _rk_chunk_kernel(q_ref, rbar_ref, ybar_ref, phi_ref, psi_ref):
    c = RK_CHUNK
    n = 2 * c
    nd = 2 * RK_NB
    nu = RK_STEP_CHUNKS * nd

    def tiles(slot, per_dir):
        cols = [(slot + (Q_DIR * d if per_dir else 0)) * RK_WIDTH + p * LANES
                for d in range(2) for p in range(RK_NB)]
        return jnp.stack([q_ref[ck * c:(ck + 1) * c, lo:lo + LANES] for ck in range(RK_STEP_CHUNKS) for lo in cols])

    r, v, a = tiles(Q_R, False), tiles(Q_V, False), tiles(Q_A, False)
    lw, k, b = tiles(Q_W, True), tiles(Q_K, True), tiles(Q_B, True)
    unit = lax.broadcasted_iota(jnp.int32, (nu, 1, 1), 0)
    sgn = jnp.ones((nu, 1, 1), jnp.int32)
    for ck in range(RK_STEP_CHUNKS):
        sgn = jnp.where(jnp.logical_and(unit >= ck * nd + RK_NB, unit < (ck + 1) * nd), -1, sgn)
    bwd = sgn < 0
    tdiff = lax.broadcasted_iota(jnp.int32, (1, c, c), 2) - lax.broadcasted_iota(jnp.int32, (1, c, c), 1)
    tri = jnp.where(tdiff * sgn <= 0, 1.0, 0.0)
    cum = sum(_bmm(tri, part) for part in _split3_bf16(lw))
    tot = jnp.where(bwd, cum[:, 0:1], cum[:, c - 1:c])
    a_t = a * jnp.exp(cum - lw)
    r_t = r * jnp.exp(cum)
    e_neg = jnp.exp(-cum)
    e_end = jnp.exp(tot - cum)
    g = _bmm_nt(jnp.concatenate([_stack_heads3(a_t), _stack_heads3(r_t)], axis=1),
                jnp.concatenate([_stack_heads3(b * e_neg), _stack_heads3(k * e_neg)], axis=1))
    r2 = lax.broadcasted_iota(jnp.int32, (1, n, n), 1)
    c2 = lax.broadcasted_iota(jnp.int32, (1, n, n), 2)
    order = (jnp.bitwise_and(c2, c - 1) - jnp.bitwise_and(r2, c - 1)) * sgn
    eye = jnp.where(r2 == c2, 1.0, 0.0)
    l_ab = jnp.where(order < 0, g[:, :n, :n], 0.0)
    l_ak = jnp.where(order < 0, g[:, :n, n:], 0.0)
    m_rb = jnp.where(order <= 0, g[:, n:, :n], 0.0)
    m_rk = jnp.where(order <= 0, g[:, n:, n:], 0.0)
    t_inv = eye + l_ab
    pw = l_ab
    for _ in range(5):
        pw = _bmm(pw, pw)
        t_inv = t_inv + _bmm(t_inv, pw)
    sv = _stack_heads3(v)
    au = _bmm(t_inv, jnp.concatenate([_stack_heads3(a_t), _bmm(l_ak, sv)], axis=2))
    ry = _bmm(m_rb, au) + jnp.concatenate([_stack_heads3(r_t), _bmm(m_rk, sv)], axis=2)
    ry = ry[:, :c] + ry[:, c:]
    bt = jnp.swapaxes(_stack_heads3(b * e_end), 1, 2)
    kt = jnp.swapaxes(_stack_heads3(k * e_end), 1, 2)
    pp = _bmm(bt, au)
    phi = eye * jnp.exp(tot) + pp[:, :, :PAIR]
    psi = pp[:, :, PAIR:] + _bmm(kt, sv)
    for ck in range(RK_STEP_CHUNKS):
        for d in range(2):
            for p in range(RK_NB):
                u = ck * nd + d * RK_NB + p
                rbar_ref[d, ck * c:(ck + 1) * c, p * LANES:(p + 1) * LANES] = ry[u, :, :PAIR]
                ybar_ref[d, ck * c:(ck + 1) * c, p * LANES:(p + 1) * LANES] = ry[u, :, PAIR:]
                phi_ref[d, ck, p] = _pack_pair(phi[u])
                psi_ref[d, ck, p] = _pack_pair(psi[u])


def _rk_chunk(q, tile0, n_seq, t):
    nc = t // RK_CHUNK
    sc = RK_STEP_CHUNKS
    assert nc % sc == 0 and tile0 % sc == 0
    row_sh = jax.ShapeDtypeStruct((2, n_seq, t, RK_WIDTH), F32)
    mat_sh = jax.ShapeDtypeStruct((2, n_seq, nc, RK_NB, HEAD_DIM, PAIR), F32)
    row_spec = pl.BlockSpec((2, None, sc * RK_CHUNK, RK_WIDTH), lambda s, c: (0, s, c, 0))
    mat_spec = pl.BlockSpec((2, None, sc, RK_NB, HEAD_DIM, PAIR), lambda s, c: (0, s, c, 0, 0, 0))
    return pl.pallas_call(
        _rk_chunk_kernel,
        grid=(n_seq, nc // sc),
        in_specs=[pl.BlockSpec((sc * RK_CHUNK, Q_COLS), lambda s, c: ((tile0 + s * nc) // sc + c, 0))],
        out_specs=[row_spec, row_spec, mat_spec, mat_spec],
        out_shape=[row_sh, row_sh, mat_sh, mat_sh],
        compiler_params=_cparams(("parallel", "parallel")),
        name="rk_chunk",
    )(q)


def _rk_state_kernel(rf_ref, rb_ref, yf_ref, yb_ref, phf_ref, phb_ref, psf_ref, psb_ref, s0_ref,
                     of_ref, ob_ref, s_ref):
    @pl.when(pl.program_id(1) == 0)
    def _():
        s_ref[...] = s0_ref[...]

    ns = s_ref.shape[0]
    nd = 2 * RK_NB

    def pair_tiles(ref_f, ref_b):
        tiles = [ref[:, :, p * LANES:(p + 1) * LANES] for ref in (ref_f, ref_b) for p in range(RK_NB)]
        return jnp.stack(tiles, axis=1).reshape(ns * nd, RK_CHUNK, LANES)

    def mats(ref_f, ref_b):
        return _unpack_pairs(jnp.concatenate([ref_f[...], ref_b[...]], axis=1).reshape(ns * nd, HEAD_DIM, PAIR))

    h = s_ref[...].reshape(ns * nd, PAIR, PAIR)
    y = (_bmm3(pair_tiles(rf_ref, rb_ref), h) + pair_tiles(yf_ref, yb_ref)).reshape(ns, nd, RK_CHUNK, LANES)
    s_ref[...] = (_bmm3(mats(phf_ref, phb_ref), h) + mats(psf_ref, psb_ref)).reshape(ns, 2, RK_NB, PAIR, PAIR)
    of_ref[...] = jnp.concatenate([y[:, p] for p in range(RK_NB)], axis=2)
    ob_ref[...] = jnp.concatenate([y[:, RK_NB + p] for p in range(RK_NB)], axis=2)


def _rk_state(rbar, ybar, phi, psi, s0):
    _, n_seq, t, _ = rbar.shape
    nc = t // RK_CHUNK
    sg = STATE_SEQS
    row_blk = (None, sg, RK_CHUNK, RK_WIDTH)
    mat_blk = (None, sg, None, RK_NB, HEAD_DIM, PAIR)
    fwd_row = pl.BlockSpec(row_blk, lambda g, c: (0, g, c, 0))
    bwd_row = pl.BlockSpec(row_blk, lambda g, c: (1, g, nc - 1 - c, 0))
    fwd_mat = pl.BlockSpec(mat_blk, lambda g, c: (0, g, c, 0, 0, 0))
    bwd_mat = pl.BlockSpec(mat_blk, lambda g, c: (1, g, nc - 1 - c, 0, 0, 0))
    st = pl.BlockSpec((sg, 2, RK_NB, PAIR, PAIR), lambda g, c: (g, 0, 0, 0, 0))
    out_sh = jax.ShapeDtypeStruct((n_seq, t, RK_WIDTH), F32)
    return pl.pallas_call(
        _rk_state_kernel,
        grid=(n_seq // sg, nc),
        in_specs=[fwd_row, bwd_row, fwd_row, bwd_row, fwd_mat, bwd_mat, fwd_mat, bwd_mat, st],
        out_specs=[pl.BlockSpec((sg, RK_CHUNK, RK_WIDTH), lambda g, c: (g, c, 0)),
                   pl.BlockSpec((sg, RK_CHUNK, RK_WIDTH), lambda g, c: (g, nc - 1 - c, 0)), st],
        out_shape=[out_sh, out_sh, jax.ShapeDtypeStruct((n_seq, 2, RK_NB, PAIR, PAIR), F32)],
        compiler_params=_cparams(("parallel", "arbitrary")),
        name="rk_state",
    )(rbar, rbar, ybar, ybar, phi, phi, psi, psi, s0)


def _pair_states(s):
    bsz = s.shape[0]
    h = jnp.swapaxes(s, -1, -2).reshape(bsz, 2, RK_NB, 2, HEAD_DIM, HEAD_DIM)
    return jnp.einsum("bdphkv,hg->bdphkgv", h, jnp.eye(2, dtype=F32)).reshape(bsz, 2, RK_NB, PAIR, PAIR)


def _head_states(s):
    bsz = s.shape[0]
    h = jnp.stack([s[..., :HEAD_DIM, :HEAD_DIM], s[..., HEAD_DIM:, HEAD_DIM:]], axis=3)
    return jnp.swapaxes(h.reshape(bsz, 2, RK_HEADS, HEAD_DIM, HEAD_DIM), -1, -2)


def _rwkv_group(q, tile0, n_seq, t, s0):
    rbar, ybar, phi, psi = _rk_chunk(q, tile0, n_seq, t)
    y_f, y_b, s_fin = _rk_state(rbar, ybar, phi, psi, s0)
    return y_f.reshape(n_seq * t, RK_WIDTH), y_b.reshape(n_seq * t, RK_WIDTH), s_fin


def _out_proj_kernel(x_c, x_l, ona_c, ona_l, osw_c, osw_l, yf_c, yf_l, yb_c, yb_l, bonus_ref, g_ref, lng_ref, lnb_ref,
                     w_ref, mod_ref, n2_ref, rw_ref, rb_ref, x1_ref, h2_ref, gate_ref, top_ref, *, n_ctx_tiles):
    is_ctx = pl.program_id(0) < n_ctx_tiles
    pick = lambda c_ref, l_ref: jnp.where(is_ctx, c_ref[...], l_ref[...])
    ona = pick(ona_c, ona_l)
    osw = pick(osw_c, osw_l)
    y = pick(yf_c, yf_l) + pick(yb_c, yb_l)
    parts = []
    for c in range(RK_NB):
        yc = y[:, c * LANES:(c + 1) * LANES]
        dc = yc - _pair_sum(yc) * (1.0 / HEAD_DIM)
        var = _pair_sum(dc * dc) * (1.0 / HEAD_DIM)
        parts.append(dc * lax.rsqrt(var + GN_EPS))
    yn = jnp.concatenate(parts, axis=1) * lng_ref[...] + lnb_ref[...]
    o_rk = (yn + bonus_ref[...]) * g_ref[...]
    o = (_dot(ona.astype(BF16), w_ref[0:NA_WIDTH, :])
         + _dot(osw.astype(BF16), w_ref[NA_WIDTH:NA_WIDTH + SWA_WIDTH, :])
         + _dot(o_rk.astype(BF16), w_ref[NA_WIDTH + SWA_WIDTH:, :]))
    x1 = pick(x_c, x_l) + mod_ref[0, 2:3, :] * o
    x1_ref[...] = x1
    yn2 = x1 * lax.rsqrt(jnp.mean(x1 * x1, axis=-1, keepdims=True) + RMS_EPS)
    h2 = (yn2 * n2_ref[...]) * (1.0 + mod_ref[0, 4:5, :]) + mod_ref[0, 3:4, :]
    h2_ref[...] = h2.astype(BF16)
    lane = lax.broadcasted_iota(jnp.int32, (h2.shape[0], LANES), 1).astype(F32)
    logit = jnp.where(lane < N_EXPERTS, _dot3(h2, rw_ref[...]) + rb_ref[...], -jnp.inf)
    vals, idxs = [], []
    for _ in range(TOP_K):
        best = jnp.max(logit, axis=-1, keepdims=True)
        idx = jnp.min(jnp.where(logit == best, lane, float(LANES)), axis=-1, keepdims=True)
        vals.append(best)
        idxs.append(idx)
        logit = jnp.where(lane == idx, -jnp.inf, logit)
    e = jnp.exp(jnp.concatenate(vals, axis=1) - vals[0])
    gate_ref[...] = e / jnp.sum(e, axis=-1, keepdims=True)
    top_ref[...] = jnp.concatenate(idxs, axis=1).astype(jnp.int32)


def _out_proj(x, o_na, o_sw, y_f, y_b, bonus, g, p, mods, tile_mod, n_ctx_tiles):
    n_tok = x[0].shape[0] + x[1].shape[0]
    tok = lambda i: (i, 0)
    const = lambda i: (0, 0)
    pair = lambda w: _group_tile_specs(w, n_ctx_tiles)
    return pl.pallas_call(
        functools.partial(_out_proj_kernel, n_ctx_tiles=n_ctx_tiles),
        grid=(n_tok // TOK_TILE,),
        in_specs=[
            *pair(D_MODEL), *pair(NA_WIDTH), *pair(SWA_WIDTH), *pair(RK_WIDTH), *pair(RK_WIDTH),
            pl.BlockSpec((TOK_TILE, RK_WIDTH), tok),
            pl.BlockSpec((TOK_TILE, RK_WIDTH), tok),
            pl.BlockSpec((1, RK_WIDTH), const),
            pl.BlockSpec((1, RK_WIDTH), const),
            pl.BlockSpec((D_MODEL, D_MODEL), const),
            pl.BlockSpec((1, 6, D_MODEL), lambda i: (tile_mod(i), 0, 0)),
            pl.BlockSpec((1, D_MODEL), const),
            pl.BlockSpec((D_MODEL, LANES), const),
            pl.BlockSpec((1, LANES), const),
        ],
        out_specs=[
            pl.BlockSpec((TOK_TILE, D_MODEL), tok),
            pl.BlockSpec((TOK_TILE, D_MODEL), tok),
            pl.BlockSpec((TOK_TILE, TOP_K), tok),
            pl.BlockSpec((TOK_TILE, TOP_K), tok),
        ],
        out_shape=[
            jax.ShapeDtypeStruct((n_tok, D_MODEL), F32),
            jax.ShapeDtypeStruct((n_tok, D_MODEL), BF16),
            jax.ShapeDtypeStruct((n_tok, TOP_K), F32),
            jax.ShapeDtypeStruct((n_tok, TOP_K), jnp.int32),
        ],
        compiler_params=_cparams(("parallel",)),
        name="out_proj",
    )(*x, *o_na, *o_sw, *y_f, *y_b, bonus, g, p["rk_ln_g"], p["rk_ln_b"], p["w_out_bf16"], mods, p["norm2_g"],
      p["router_w_pad"], p["router_b_pad"])


H2_PAD_ROWS = 32768
W1_SEL_COLS = 256
MOE_VMEM_LIMIT = 56 * 1024 * 1024


def _moe_kernel(meta_ref, x_ref, w1_ref, b1g_ref, b1l_ref, w2_ref, b2_ref, o_ref, w1g_ref, w1l_ref, w2b_ref):
    i = pl.program_id(0)
    n_blk = meta_ref.shape[0] - 1
    n_used = meta_ref[n_blk]
    d_e = w2_ref.shape[1]
    new_expert = jnp.logical_or(i == 0, meta_ref[i] != meta_ref[jnp.maximum(i - 1, 0)])

    @pl.when(jnp.logical_and(i < n_used, new_expert))
    def _():
        src = lax.broadcasted_iota(jnp.int32, (2 * W1_SEL_COLS, 2 * W1_SEL_COLS), 0)
        dst = lax.broadcasted_iota(jnp.int32, (2 * W1_SEL_COLS, 2 * W1_SEL_COLS), 1)
        pick = jnp.where(dst < W1_SEL_COLS, 2 * dst, 2 * (dst - W1_SEL_COLS) + 1)
        sel = jnp.where(src == pick, 1.0, 0.0).astype(BF16)
        for t in range(d_e // W1_SEL_COLS):
            cols = _dot(w1_ref[0, :, 2 * t * W1_SEL_COLS:2 * (t + 1) * W1_SEL_COLS].astype(BF16), sel)
            w1g_ref[:, t * W1_SEL_COLS:(t + 1) * W1_SEL_COLS] = cols[:, :W1_SEL_COLS].astype(BF16)
            w1l_ref[:, t * W1_SEL_COLS:(t + 1) * W1_SEL_COLS] = cols[:, W1_SEL_COLS:].astype(BF16)
        w2b_ref[...] = w2_ref[0].astype(BF16)

    @pl.when(i < n_used)
    def _():
        x = x_ref[...]
        glu = jnp.minimum(_dot(x, w1g_ref[...]) + b1g_ref[0], SWIGLU_LIMIT)
        lin = jnp.clip(_dot(x, w1l_ref[...]) + b1l_ref[0], -SWIGLU_LIMIT, SWIGLU_LIMIT)
        act = glu * jax.nn.sigmoid(SWIGLU_ALPHA * glu) * (lin + 1.0)
        o_ref[...] = (_dot(act.astype(BF16), w2b_ref[...]) + b2_ref[0]).astype(BF16)

    @pl.when(i >= n_used)
    def _():
        o_ref[...] = jnp.zeros_like(o_ref)


def _moe_blocks(meta, xb, w1, b1g, b1l, w2, b2, layer):
    n_rows = xb.shape[0]
    n_blk = n_rows // MOE_BLK
    d_e = w2.shape[2]
    row = lambda i, m: (i, 0)
    exp3 = lambda i, m: (layer, m[i], 0, 0)
    grid_spec = pltpu.PrefetchScalarGridSpec(
        num_scalar_prefetch=1,
        grid=(n_blk,),
        in_specs=[
            pl.BlockSpec((MOE_BLK, D_MODEL), row),
            pl.BlockSpec((None, 1, D_MODEL, 2 * d_e), exp3),
            pl.BlockSpec((None, 1, 1, d_e), exp3),
            pl.BlockSpec((None, 1, 1, d_e), exp3),
            pl.BlockSpec((None, 1, d_e, D_MODEL), exp3),
            pl.BlockSpec((None, 1, 1, D_MODEL), exp3),
        ],
        out_specs=pl.BlockSpec((MOE_BLK, D_MODEL), row),
        scratch_shapes=[
            pltpu.VMEM((D_MODEL, d_e), BF16),
            pltpu.VMEM((D_MODEL, d_e), BF16),
            pltpu.VMEM((d_e, D_MODEL), BF16),
        ],
    )
    return pl.pallas_call(
        _moe_kernel,
        grid_spec=grid_spec,
        out_shape=jax.ShapeDtypeStruct((n_rows, D_MODEL), BF16),
        compiler_params=pltpu.CompilerParams(dimension_semantics=("arbitrary",),
                                             vmem_limit_bytes=MOE_VMEM_LIMIT),
        name="moe_blocks",
    )(meta, xb, w1, b1g, b1l, w2, b2)


def _route(top_i):
    n_tok = top_i.shape[0]
    e_flat = top_i.reshape(-1)
    n_rows = n_tok * TOP_K
    onehot = (e_flat[:, None] == jnp.arange(N_EXPERTS, dtype=jnp.int32)[None, :]).astype(jnp.int32)
    csum = jnp.cumsum(onehot, axis=0)
    counts = jnp.sum(onehot, axis=0)
    starts = jnp.cumsum(counts) - counts
    pcounts = (counts + MOE_BLK - 1) // MOE_BLK * MOE_BLK
    pends = jnp.cumsum(pcounts)
    pstarts = pends - pcounts
    dest = jnp.take_along_axis(csum + (pstarts - 1)[None, :], e_flat[:, None], axis=1)[:, 0]
    n_blk = n_rows // MOE_BLK + N_EXPERTS
    blk_start = jnp.arange(n_blk, dtype=jnp.int32) * MOE_BLK
    blk_exp = jnp.minimum(jnp.sum((blk_start[:, None] >= pends[None, :]).astype(jnp.int32), axis=1), N_EXPERTS - 1)
    order = jnp.argsort(e_flat)
    pos = jnp.arange(n_blk * MOE_BLK, dtype=jnp.int32)
    src = (pos + jnp.repeat((starts - pstarts)[blk_exp], MOE_BLK)) % n_rows
    row_tok = order[src].astype(jnp.int32) // TOP_K
    meta = jnp.concatenate([blk_exp, (pends[-1:] // MOE_BLK).astype(jnp.int32)])
    return meta, row_tok, dest.reshape(n_tok, TOP_K).T.reshape(-1)


def _combine_kernel(x_ref, yg_ref, gate_ref, mod_ref, oc_ref, ol_ref, *, n_ctx_tiles):
    gate = gate_ref[...]
    acc = gate[:, 0:1] * yg_ref[0].astype(F32)
    for j in range(1, TOP_K):
        acc = acc + gate[:, j:j + 1] * yg_ref[j].astype(F32)
    out = x_ref[...] + mod_ref[0, 5:6, :] * acc
    is_ctx = pl.program_id(0) < n_ctx_tiles

    @pl.when(is_ctx)
    def _():
        oc_ref[...] = out

    @pl.when(jnp.logical_not(is_ctx))
    def _():
        ol_ref[...] = out


def _combine(x1, yg, gates, mods, tile_mod, n_ctx_tiles):
    n_tok = x1.shape[0]
    n_ctx = n_ctx_tiles * TOK_TILE
    return pl.pallas_call(
        functools.partial(_combine_kernel, n_ctx_tiles=n_ctx_tiles),
        grid=(n_tok // TOK_TILE,),
        in_specs=[
            pl.BlockSpec((TOK_TILE, D_MODEL), lambda i: (i, 0)),
            pl.BlockSpec((TOP_K, TOK_TILE, D_MODEL), lambda i: (0, i, 0)),
            pl.BlockSpec((TOK_TILE, TOP_K), lambda i: (i, 0)),
            pl.BlockSpec((1, 6, D_MODEL), lambda i: (tile_mod(i), 0, 0)),
        ],
        out_specs=_group_tile_specs(D_MODEL, n_ctx_tiles),
        out_shape=[jax.ShapeDtypeStruct((n_ctx, D_MODEL), F32), jax.ShapeDtypeStruct((n_tok - n_ctx, D_MODEL), F32)],
        compiler_params=_cparams(("arbitrary",)),
        name="moe_combine",
    )(x1, yg, gates, mods)


def kernel(x_prompt, x_sample, c, cache_na_k, cache_na_v, cache_swa_k, cache_swa_v, state_rwkv, c_ctx, w_ada, b_ada, norm1_g, norm2_g, w_in, w_out, na_q_norm, na_k_norm, na_rpb, swa_q_norm, swa_k_norm, swa_sink, rk_conv, rk_w0, rk_w2, rk_a0, rk_a2, rk_g2, rk_k_k, rk_k_a, rk_r_k, rk_ln_g, rk_ln_b, moe_router_w, moe_router_b, moe_w1, moe_b1, moe_w2, moe_b2):
    bc, tc, _ = x_prompt.shape
    bl, tl, _ = x_sample.shape
    depth = w_in.shape[0]
    n_ctx = bc * tc
    n_lat = bl * tl
    assert tc == TOK_TILE and tl % TOK_TILE == 0 and n_ctx % tl == 0
    n_ctx_tiles = n_ctx // TOK_TILE
    tiles_per_seq = tl // TOK_TILE
    past = cache_na_k.shape[2]

    def tile_mod(i):
        return jnp.where(i < n_ctx_tiles, 0, 1 + (i - n_ctx_tiles) // tiles_per_seq)

    def tile_rope(i):
        return jnp.where(i < n_ctx_tiles, tiles_per_seq, (i - n_ctx_tiles) % tiles_per_seq)

    x = (x_prompt.reshape(n_ctx, D_MODEL), x_sample.reshape(n_lat, D_MODEL))

    n_mod = 1 + bl
    mod_rows = -(-n_mod // 8) * 8
    cvecs = jnp.concatenate([c_ctx[None, :], c, jnp.zeros((mod_rows - n_mod, D_MODEL), F32)], axis=0)
    mods_all = _ada_mod(cvecs, w_ada, b_ada).reshape(depth, mod_rows, 6, D_MODEL)
    cos_tab, sin_tab = _rope_tables(tl)
    tile2 = lambda g: jnp.concatenate([g, g])[None, :]
    pad_lanes = lambda z: jnp.pad(z, ((0, 0), (0, LANES - z.shape[1])))
    zeros_lora = jnp.zeros((2, RK_DECAY_LORA, RK_WIDTH), F32)

    na_k_l, na_v_l, sw_k_l, sw_v_l, st_l = [], [], [], [], []
    for l in range(depth):
        mods = mods_all[l]
        qk_gains = jnp.concatenate(
            [tile2(na_q_norm[l]), tile2(na_k_norm[l]), tile2(swa_q_norm[l]), tile2(swa_k_norm[l])], axis=0)
        p = {
            "rk_conv": rk_conv[l], "rk_w0": rk_w0[l], "rk_a0": rk_a0[l], "rk_g2": rk_g2[l],
            "rk_w2_pad": jnp.concatenate([rk_w2[l], zeros_lora], axis=1),
            "rk_a2_pad": jnp.concatenate([zeros_lora, rk_a2[l]], axis=1),
            "rk_k_k": rk_k_k[l][None, :], "rk_k_a": rk_k_a[l][None, :],
            "rk_r_k": rk_r_k[l].reshape(1, RK_WIDTH),
            "rk_ln_g": rk_ln_g[l][None, :], "rk_ln_b": rk_ln_b[l][None, :],
            "w_out_bf16": w_out[l].astype(BF16), "norm2_g": norm2_g[l][None, :],
            "router_w_pad": pad_lanes(moe_router_w[l]), "router_b_pad": pad_lanes(moe_router_b[l][None, :]),
        }

        att, u = _in_proj(x, norm1_g[l][None, :], mods, w_in[l].astype(BF16), qk_gains, cos_tab, sin_tab,
                          tile_mod, tile_rope, n_ctx_tiles)
        q, g, bonus = _rk_prep(u, p, n_ctx_tiles, tiles_per_seq)
        att_c = att[:n_ctx].reshape(bc, tc, ATT_COLS)
        att_by_ctx_len = att.reshape((n_ctx + n_lat) // tc, tc, ATT_COLS)
        att_by_lat_len = att.reshape((n_ctx + n_lat) // tl, tl, ATT_COLS)
        na_k_l.append(att_c[:, :, NA_WIDTH:2 * NA_WIDTH].reshape(bc, tc, NA_HEADS, HEAD_DIM))
        na_v_l.append(att_c[:, :, 2 * NA_WIDTH:NA_COLS].reshape(bc, tc, NA_HEADS, HEAD_DIM))
        sw_k_l.append(att_c[:, :, NA_COLS + SWA_WIDTH:NA_COLS + SWA_WIDTH + SWA_KV_WIDTH]
                      .reshape(bc, tc, SWA_KV_HEADS, HEAD_DIM))
        sw_v_l.append(att_c[:, :, NA_COLS + SWA_WIDTH + SWA_KV_WIDTH:].reshape(bc, tc, SWA_KV_HEADS, HEAD_DIM))

        sink = swa_sink[l]
        o_na = (_ctx_attn(att_by_ctx_len, bc, sink, gqa=False).reshape(n_ctx, NA_WIDTH),
                _na_latent(att_by_lat_len, n_ctx // tl, cache_na_k[:, l].reshape(bl, past, NA_WIDTH),
                           cache_na_v[:, l].reshape(bl, past, NA_WIDTH),
                           _na_bias_tables(na_rpb[l])).reshape(n_lat, NA_WIDTH))
        o_sw = (_ctx_attn(att_by_ctx_len, bc, sink, gqa=True).reshape(n_ctx, SWA_WIDTH),
                _swa_latent(att_by_lat_len, n_ctx // tl, cache_swa_k[:, l].reshape(bl, past, SWA_KV_WIDTH),
                            cache_swa_v[:, l].reshape(bl, past, SWA_KV_WIDTH), sink).reshape(n_lat, SWA_WIDTH))

        yf_c, yb_c, s_fin = _rwkv_group(q, 0, bc, tc, jnp.zeros((bc, 2, RK_NB, PAIR, PAIR), F32))
        yf_l, yb_l, _ = _rwkv_group(q, n_ctx // RK_CHUNK, bl, tl, _pair_states(state_rwkv[:, l]))
        st_l.append(_head_states(s_fin))

        x1, h2, gates, top_i = _out_proj(x, o_na, o_sw, (yf_c, yf_l), (yb_c, yb_l), bonus, g, p, mods, tile_mod,
                                         n_ctx_tiles)
        meta, row_tok, dest = _route(top_i)
        h2 = jnp.concatenate([h2, jnp.zeros((H2_PAD_ROWS - h2.shape[0], D_MODEL), BF16)], axis=0)
        yb = _moe_blocks(meta, h2[row_tok], moe_w1, moe_b1[:, :, None, 0::2], moe_b1[:, :, None, 1::2], moe_w2,
                         moe_b2[:, :, None, :], l)
        x = _combine(x1, yb[dest].reshape(TOP_K, n_ctx + n_lat, D_MODEL), gates, mods, tile_mod, n_ctx_tiles)

    y_p = x[0].reshape(bc, tc, D_MODEL)
    y_s = x[1].reshape(bl, tl, D_MODEL)
    return (y_p, y_s, jnp.stack(na_k_l, axis=1), jnp.stack(na_v_l, axis=1), jnp.stack(sw_k_l, axis=1),
            jnp.stack(sw_v_l, axis=1), jnp.stack(st_l, axis=1))
```

```python
import functools

import jax
import jax.numpy as jnp
from jax import lax
from jax.experimental import pallas as pl
from jax.experimental.pallas import tpu as pltpu

F32 = jnp.float32
BF16 = jnp.bfloat16

D_MODEL = 1024
HEAD_DIM = 64
LANES = 128
GRID_W = 64
NA_HEADS = 6
SWA_HEADS = 4
SWA_KV_HEADS = 2
RK_HEADS = 6
NA_WIDTH = NA_HEADS * HEAD_DIM
SWA_WIDTH = SWA_HEADS * HEAD_DIM
SWA_KV_WIDTH = SWA_KV_HEADS * HEAD_DIM
RK_WIDTH = RK_HEADS * HEAD_DIM
RK_DECAY_LORA = 64
RK_A_LORA = 64
RK_GATE_LORA = 128
RK_COLS = 3 * RK_WIDTH + RK_DECAY_LORA + RK_A_LORA + RK_GATE_LORA
NA_COLS = 3 * NA_WIDTH
SWA_COLS = SWA_WIDTH + 2 * SWA_KV_WIDTH
ATT_COLS = NA_COLS + SWA_COLS
IN_COLS = ATT_COLS + RK_COLS
NA_WIN_R = 8
NA_WIN_C = 16
SWA_WIN = 128
ROPE_THETA = 10000.0
ATTN_SCALE = HEAD_DIM ** -0.5
N_EXPERTS = 32
TOP_K = 4
SWIGLU_LIMIT = 7.0
SWIGLU_ALPHA = 1.702
MOE_BLK = 256
RMS_EPS = 1e-6
GN_EPS = 64e-5
NEG_BIG = -1e30

TOK_TILE = 256
VMEM_LIMIT = 48 * 1024 * 1024


def _cparams(sem):
    return pltpu.CompilerParams(dimension_semantics=sem, vmem_limit_bytes=VMEM_LIMIT)


def _dot(a, b):
    return jnp.dot(a, b, preferred_element_type=F32)


def _dot_nt(a, b):
    return lax.dot_general(a, b, (((1,), (1,)), ((), ())), preferred_element_type=F32)


def _split_bf16(x):
    hi = x.astype(BF16)
    lo = (x - hi.astype(F32)).astype(BF16)
    return hi, lo


def _dot3(a, b):
    ah, al = _split_bf16(a)
    bh, bl = _split_bf16(b)
    return _dot(ah, bh) + (_dot(ah, bl) + _dot(al, bh))


def _bmm_raw(a, b):
    return lax.dot_general(a, b, (((2,), (1,)), ((0,), (0,))), preferred_element_type=F32)


def _bmm(a, b):
    return _bmm_raw(a.astype(BF16), b.astype(BF16))


def _bmm_nt(a, b):
    return lax.dot_general(a.astype(BF16), b.astype(BF16), (((2,), (2,)), ((0,), (0,))),
                           preferred_element_type=F32)


def _bmm3(a, b):
    ah, al = _split_bf16(a)
    bh, bl = _split_bf16(b)
    return _bmm_raw(ah, bh) + (_bmm_raw(ah, bl) + _bmm_raw(al, bh))


def _lane_lo(shape):
    return lax.broadcasted_iota(jnp.int32, shape, len(shape) - 1) < HEAD_DIM


def _pair_sum(x):
    lo = _lane_lo(x.shape)
    s_lo = jnp.sum(jnp.where(lo, x, 0.0), axis=-1, keepdims=True)
    s_hi = jnp.sum(jnp.where(lo, 0.0, x), axis=-1, keepdims=True)
    return jnp.where(lo, s_lo, s_hi)


def _stack_heads(q):
    lo = _lane_lo(q.shape)
    return jnp.concatenate([jnp.where(lo, q, 0.0), jnp.where(lo, 0.0, q)], axis=0)


def _stack_heads3(x):
    lo = _lane_lo(x.shape)
    return jnp.concatenate([jnp.where(lo, x, 0.0), jnp.where(lo, 0.0, x)], axis=1)


def _unstack_heads(o2):
    n = o2.shape[0] // 2
    return jnp.where(_lane_lo((n, LANES)), o2[:n], o2[n:])


def _dup_head(x, j):
    keep = _lane_lo(x.shape) == (j == 0)
    return jnp.where(keep, x, pltpu.roll(x, HEAD_DIM, 1))


def _ada_kernel(c_ref, w_ref, b_ref, o_ref):
    cv = c_ref[...]
    s = cv * jax.nn.sigmoid(cv)
    o_ref[0] = _dot3(s, w_ref[0]) + b_ref[0]


def _ada_mod(cvecs, w_ada, b_ada):
    depth, _, n_out = w_ada.shape
    rows = cvecs.shape[0]
    tn = 1024
    return pl.pallas_call(
        _ada_kernel,
        grid=(depth, n_out // tn),
        in_specs=[
            pl.BlockSpec((rows, D_MODEL), lambda l, j: (0, 0)),
            pl.BlockSpec((1, D_MODEL, tn), lambda l, j: (l, 0, j)),
            pl.BlockSpec((1, 1, tn), lambda l, j: (l, 0, j)),
        ],
        out_specs=pl.BlockSpec((1, rows, tn), lambda l, j: (l, 0, j)),
        out_shape=jax.ShapeDtypeStruct((depth, rows, n_out), F32),
        compiler_params=_cparams(("parallel", "parallel")),
        name="ada_mod",
    )(cvecs, w_ada, b_ada.reshape(depth, 1, n_out))


NA_QK_BLOCKS = 2 * NA_WIDTH // LANES
SWA_Q_BLOCK0 = NA_COLS // LANES
SWA_QK_BLOCKS = (SWA_WIDTH + SWA_KV_WIDTH) // LANES


def _in_proj_kernel(xc_ref, xl_ref, g_ref, mod_ref, w_ref, qkg_ref, cos_ref, sin_ref, att_ref, u_ref, *, n_ctx_tiles):
    x = jnp.where(pl.program_id(0) < n_ctx_tiles, xc_ref[...], xl_ref[...])
    y = x * lax.rsqrt(jnp.mean(x * x, axis=-1, keepdims=True) + RMS_EPS)
    h = (y * g_ref[...]) * (1.0 + mod_ref[0, 1:2, :]) + mod_ref[0, 0:1, :]
    proj = _dot(h.astype(BF16), w_ref[...])
    u_ref[...] = proj[:, ATT_COLS:]

    def qk_norm(blk, gain):
        ms = _pair_sum(blk * blk) * (1.0 / HEAD_DIM)
        return blk * lax.rsqrt(ms + RMS_EPS) * gain

    lane = lax.broadcasted_iota(jnp.int32, (x.shape[0], LANES), 1)
    first = (lane % (HEAD_DIM // 2)) < (HEAD_DIM // 4)
    for cb in range(ATT_COLS // LANES):
        blk = proj[:, cb * LANES:(cb + 1) * LANES]
        if cb < NA_QK_BLOCKS:
            gi = 0 if cb < NA_QK_BLOCKS // 2 else 1
            blk = qk_norm(blk, qkg_ref[gi:gi + 1, :])
        elif SWA_Q_BLOCK0 <= cb < SWA_Q_BLOCK0 + SWA_QK_BLOCKS:
            gi = 2 if cb < SWA_Q_BLOCK0 + SWA_WIDTH // LANES else 3
            blk = qk_norm(blk, qkg_ref[gi:gi + 1, :])
            partner = jnp.where(first, pltpu.roll(blk, LANES - HEAD_DIM // 4, 1),
                                pltpu.roll(blk, HEAD_DIM // 4, 1))
            blk = blk * cos_ref[...] + partner * sin_ref[...]
        att_ref[:, cb * LANES:(cb + 1) * LANES] = blk


def _group_tile_specs(width, n_ctx_tiles):
    return [pl.BlockSpec((TOK_TILE, width), lambda i: (jnp.minimum(i, n_ctx_tiles - 1), 0)),
            pl.BlockSpec((TOK_TILE, width), lambda i: (jnp.maximum(i - n_ctx_tiles, 0), 0))]


def _in_proj(x, norm_g, mods, w_in_bf16, qk_gains, cos_tab, sin_tab, tile_mod, tile_rope, n_ctx_tiles):
    n_tok = x[0].shape[0] + x[1].shape[0]
    return pl.pallas_call(
        functools.partial(_in_proj_kernel, n_ctx_tiles=n_ctx_tiles),
        grid=(n_tok // TOK_TILE,),
        in_specs=[
            *_group_tile_specs(D_MODEL, n_ctx_tiles),
            pl.BlockSpec((1, D_MODEL), lambda i: (0, 0)),
            pl.BlockSpec((1, 6, D_MODEL), lambda i: (tile_mod(i), 0, 0)),
            pl.BlockSpec((D_MODEL, IN_COLS), lambda i: (0, 0)),
            pl.BlockSpec((4, LANES), lambda i: (0, 0)),
            pl.BlockSpec((TOK_TILE, LANES), lambda i: (tile_rope(i), 0)),
            pl.BlockSpec((TOK_TILE, LANES), lambda i: (tile_rope(i), 0)),
        ],
        out_specs=[
            pl.BlockSpec((TOK_TILE, ATT_COLS), lambda i: (i, 0)),
            pl.BlockSpec((TOK_TILE, RK_COLS), lambda i: (i, 0)),
        ],
        out_shape=[
            jax.ShapeDtypeStruct((n_tok, ATT_COLS), F32),
            jax.ShapeDtypeStruct((n_tok, RK_COLS), F32),
        ],
        compiler_params=_cparams(("parallel",)),
        name="in_proj",
    )(*x, norm_g, mods, w_in_bf16, qk_gains, cos_tab, sin_tab)


def _rope_tables(n_lat):
    nf = HEAD_DIM // 4
    t = jnp.arange(n_lat)
    lane = jnp.arange(LANES)
    d = lane % HEAD_DIM
    inv = ROPE_THETA ** (-(d % nf).astype(F32) / nf)
    pos = jnp.where((d // (2 * nf))[None, :] == 0, (t // GRID_W)[:, None], (t % GRID_W)[:, None]).astype(F32)
    ang = pos * inv[None, :]
    sign = jnp.where((d % (2 * nf)) < nf, -1.0, 1.0).astype(F32)
    cos = jnp.concatenate([jnp.cos(ang), jnp.ones((TOK_TILE, LANES), F32)], 0)
    sin = jnp.concatenate([jnp.sin(ang) * sign[None, :], jnp.zeros((TOK_TILE, LANES), F32)], 0)
    return cos, sin


def _ctx_attn_kernel(sink_ref, q_ref, k_ref, v_ref, o_ref, *, gqa):
    j = pl.program_id(1)
    k = k_ref[0]
    v = v_ref[0]
    if gqa:
        k = _dup_head(k, j)
        v = _dup_head(v, j)
    n = k.shape[0]
    q2 = _stack_heads(q_ref[0]).astype(BF16)
    s = _dot_nt(q2, k.astype(BF16)) * ATTN_SCALE
    m = jnp.max(s, axis=-1, keepdims=True)
    if gqa:
        row = lax.broadcasted_iota(jnp.int32, (2 * n, 1), 0)
        snk = jnp.where(row < n, sink_ref[2 * j], sink_ref[2 * j + 1])
        m = jnp.maximum(m, snk)
    p = jnp.exp(s - m)
    den = jnp.sum(p, axis=-1, keepdims=True)
    if gqa:
        den = den + jnp.exp(snk - m)
    o2 = _dot(p.astype(BF16), v.astype(BF16)) / den
    o_ref[0] = _unstack_heads(o2).astype(o_ref.dtype)


def _ctx_attn(att, b, sink, *, gqa):
    t = att.shape[1]
    if gqa:
        nq = SWA_WIDTH // LANES
        qb, kb, vb = SWA_Q_BLOCK0, SWA_Q_BLOCK0 + nq, SWA_Q_BLOCK0 + nq + 1
        kmap = lambda bi, j: (bi, 0, kb)
        vmap = lambda bi, j: (bi, 0, vb)
    else:
        nq = NA_WIDTH // LANES
        qb, kb, vb = 0, nq, 2 * nq
        kmap = lambda bi, j: (bi, 0, kb + j)
        vmap = lambda bi, j: (bi, 0, vb + j)
    return pl.pallas_call(
        functools.partial(_ctx_attn_kernel, gqa=gqa),
        grid=(b, nq),
        in_specs=[
            pl.BlockSpec(memory_space=pltpu.SMEM),
            pl.BlockSpec((1, t, LANES), lambda bi, j: (bi, 0, qb + j)),
            pl.BlockSpec((1, t, LANES), kmap),
            pl.BlockSpec((1, t, LANES), vmap),
        ],
        out_specs=pl.BlockSpec((1, t, LANES), lambda bi, j: (bi, 0, j)),
        out_shape=jax.ShapeDtypeStruct((b, t, nq * LANES), BF16),
        compiler_params=_cparams(("parallel", "parallel")),
        name="ctx_attn_swa" if gqa else "ctx_attn_na",
    )(sink, att, att, att)


NA_ROWS_PER_ITER = 8


def _na_lat_kernel(q_ref, k_ref, v_ref, kc_ref, vc_ref, tab_ref, o_ref, kb_ref, vb_ref):
    n = q_ref.shape[1]
    rows = n // GRID_W
    win = NA_WIN_R * GRID_W
    kb_ref[...] = k_ref[0].astype(BF16)
    vb_ref[...] = v_ref[0].astype(BF16)
    kc = kc_ref[0].astype(BF16)
    vc = vc_ref[0].astype(BF16)

    def row_group(ig, carry):
        nr = NA_ROWS_PER_ITER
        g0 = pl.multiple_of(ig * (nr * GRID_W), nr * GRID_W)
        q2 = _stack_heads3(q_ref[0, pl.ds(g0, nr * GRID_W), :].reshape(nr, GRID_W, LANES)).astype(BF16)
        kws, vws, biases = [], [], []
        for r in range(nr):
            i = ig * nr + r
            start = jnp.clip(i - NA_WIN_R // 2, 0, rows - NA_WIN_R)
            k0 = pl.multiple_of(start * GRID_W, GRID_W)
            kws.append(kb_ref[pl.ds(k0, win), :])
            vws.append(vb_ref[pl.ds(k0, win), :])
            biases.append(tab_ref[0, start - i + (NA_WIN_R - 1)])
        s_loc = _bmm_nt(q2, jnp.stack(kws)) * ATTN_SCALE + jnp.stack(biases)
        s_ctx = _dot_nt(q2.reshape(nr * 2 * GRID_W, LANES), kc).reshape(nr, 2 * GRID_W, -1) * ATTN_SCALE
        m = jnp.maximum(jnp.max(s_loc, axis=-1, keepdims=True), jnp.max(s_ctx, axis=-1, keepdims=True))
        p_loc = jnp.exp(s_loc - m)
        p_ctx = jnp.exp(s_ctx - m)
        den = jnp.sum(p_loc, axis=-1, keepdims=True) + jnp.sum(p_ctx, axis=-1, keepdims=True)
        o_ctx = _dot(p_ctx.reshape(nr * 2 * GRID_W, -1).astype(BF16), vc).reshape(nr, 2 * GRID_W, LANES)
        o2 = (_bmm(p_loc, jnp.stack(vws)) + o_ctx) / den
        out = jnp.where(_lane_lo((nr, GRID_W, LANES)), o2[:, :GRID_W], o2[:, GRID_W:])
        o_ref[0, pl.ds(g0, nr * GRID_W), :] = out.reshape(nr * GRID_W, LANES).astype(o_ref.dtype)
        return carry

    lax.fori_loop(0, rows // NA_ROWS_PER_ITER, row_group, 0)


def _na_bias_tables(rpb):
    col = jnp.arange(GRID_W)
    cstart = jnp.clip(col - NA_WIN_C // 2, 0, GRID_W - NA_WIN_C)
    col_mask = (col[None, :] >= cstart[:, None]) & (col[None, :] < cstart[:, None] + NA_WIN_C)
    col_idx = jnp.clip(col[None, :] - col[:, None] + NA_WIN_C - 1, 0, 2 * NA_WIN_C - 2)
    rpb_cols = jnp.where(col_mask[None, None], rpb[:, :, col_idx], NEG_BIG)
    roff = jnp.arange(NA_WIN_R)[:, None] + jnp.arange(NA_WIN_R)[None, :]
    t = rpb_cols[:, roff]
    t = jnp.transpose(t, (0, 1, 3, 2, 4)).reshape(NA_HEADS // 2, 2, NA_WIN_R, GRID_W, NA_WIN_R * GRID_W)
    return jnp.transpose(t, (0, 2, 1, 3, 4)).reshape(NA_HEADS // 2, NA_WIN_R, 2 * GRID_W, NA_WIN_R * GRID_W)


def _na_latent(att, s0, kc, vc, tab):
    n = att.shape[1]
    b, p, _ = kc.shape
    nq = NA_WIDTH // LANES
    return pl.pallas_call(
        _na_lat_kernel,
        grid=(b, nq),
        in_specs=[
            pl.BlockSpec((1, n, LANES), lambda bi, j: (s0 + bi, 0, j)),
            pl.BlockSpec((1, n, LANES), lambda bi, j: (s0 + bi, 0, nq + j)),
            pl.BlockSpec((1, n, LANES), lambda bi, j: (s0 + bi, 0, 2 * nq + j)),
            pl.BlockSpec((1, p, LANES), lambda bi, j: (bi, 0, j)),
            pl.BlockSpec((1, p, LANES), lambda bi, j: (bi, 0, j)),
            pl.BlockSpec((1, NA_WIN_R, 2 * GRID_W, NA_WIN_R * GRID_W), lambda bi, j: (j, 0, 0, 0)),
        ],
        out_specs=pl.BlockSpec((1, n, LANES), lambda bi, j: (bi, 0, j)),
        out_shape=jax.ShapeDtypeStruct((b, n, NA_WIDTH), BF16),
        scratch_shapes=[pltpu.VMEM((n, LANES), BF16), pltpu.VMEM((n, LANES), BF16)],
        compiler_params=_cparams(("parallel", "parallel")),
        name="na_latent",
    )(att, att, att, kc, vc, tab)


SWA_BLOCKS_PER_ITER = 4


def _swa_lat_kernel(sink_ref, q_ref, k_ref, v_ref, kc_ref, vc_ref, o_ref, kb_ref, vb_ref):
    j = pl.program_id(1)
    n = q_ref.shape[1]
    blk = SWA_WIN
    span = 3 * blk
    kb_ref[...] = _dup_head(k_ref[0], j).astype(BF16)
    vb_ref[...] = _dup_head(v_ref[0], j).astype(BF16)
    kc = _dup_head(kc_ref[0], j).astype(BF16)
    vc = _dup_head(vc_ref[0], j).astype(BF16)
    row = lax.broadcasted_iota(jnp.int32, (2 * blk, 1), 0)
    snk = jnp.where(row < blk, sink_ref[2 * j], sink_ref[2 * j + 1])
    qoff = lax.broadcasted_iota(jnp.int32, (2 * blk, span), 0) % blk
    koff = lax.broadcasted_iota(jnp.int32, (2 * blk, span), 1)

    def q_group(qg, carry):
        nr = SWA_BLOCKS_PER_ITER
        g0 = pl.multiple_of(qg * (nr * blk), nr * blk)
        q2 = _stack_heads3(q_ref[0, pl.ds(g0, nr * blk), :].reshape(nr, blk, LANES)).astype(BF16)
        kws, vws, valids = [], [], []
        for r in range(nr):
            q0 = g0 + r * blk
            w0 = pl.multiple_of(jnp.clip(q0 - blk, 0, n - span), blk)
            kws.append(kb_ref[pl.ds(w0, span), :])
            vws.append(vb_ref[pl.ds(w0, span), :])
            valids.append(jnp.abs((q0 + qoff) - (w0 + koff)) <= SWA_WIN)
        s_loc = jnp.where(jnp.stack(valids), _bmm_nt(q2, jnp.stack(kws)) * ATTN_SCALE, NEG_BIG)
        s_ctx = _dot_nt(q2.reshape(nr * 2 * blk, LANES), kc).reshape(nr, 2 * blk, -1) * ATTN_SCALE
        m = jnp.maximum(jnp.max(s_loc, axis=-1, keepdims=True), jnp.max(s_ctx, axis=-1, keepdims=True))
        m = jnp.maximum(m, snk)
        p_loc = jnp.exp(s_loc - m)
        p_ctx = jnp.exp(s_ctx - m)
        den = (jnp.sum(p_loc, axis=-1, keepdims=True) + jnp.sum(p_ctx, axis=-1, keepdims=True)
               + jnp.exp(snk - m))
        o_ctx = _dot(p_ctx.reshape(nr * 2 * blk, -1).astype(BF16), vc).reshape(nr, 2 * blk, LANES)
        o2 = (_bmm(p_loc, jnp.stack(vws)) + o_ctx) / den
        out = jnp.where(_lane_lo((nr, blk, LANES)), o2[:, :blk], o2[:, blk:])
        o_ref[0, pl.ds(g0, nr * blk), :] = out.reshape(nr * blk, LANES).astype(o_ref.dtype)
        return carry

    lax.fori_loop(0, n // (SWA_BLOCKS_PER_ITER * blk), q_group, 0)


def _swa_latent(att, s0, kc, vc, sink):
    n = att.shape[1]
    b, p, _ = kc.shape
    nq = SWA_WIDTH // LANES
    qb, kb, vb = SWA_Q_BLOCK0, SWA_Q_BLOCK0 + nq, SWA_Q_BLOCK0 + nq + 1
    return pl.pallas_call(
        _swa_lat_kernel,
        grid=(b, nq),
        in_specs=[
            pl.BlockSpec(memory_space=pltpu.SMEM),
            pl.BlockSpec((1, n, LANES), lambda bi, j: (s0 + bi, 0, qb + j)),
            pl.BlockSpec((1, n, LANES), lambda bi, j: (s0 + bi, 0, kb)),
            pl.BlockSpec((1, n, LANES), lambda bi, j: (s0 + bi, 0, vb)),
            pl.BlockSpec((1, p, LANES), lambda bi, j: (bi, 0, 0)),
            pl.BlockSpec((1, p, LANES), lambda bi, j: (bi, 0, 0)),
        ],
        out_specs=pl.BlockSpec((1, n, LANES), lambda bi, j: (bi, 0, j)),
        out_shape=jax.ShapeDtypeStruct((b, n, SWA_WIDTH), BF16),
        scratch_shapes=[pltpu.VMEM((n, LANES), BF16), pltpu.VMEM((n, LANES), BF16)],
        compiler_params=_cparams(("parallel", "parallel")),
        name="swa_latent",
    )(sink, att, att, att, kc, vc)


RK_NB = RK_WIDTH // LANES
LORA_BLOCK = 3 * RK_WIDTH // LANES
GATE_BLOCK = LORA_BLOCK + 1
Q_R, Q_V, Q_A, Q_W, Q_K, Q_B = range(6)
Q_DIR = 3
Q_COLS = (6 + Q_DIR) * RK_WIDTH


def _softplus(x):
    return jnp.maximum(x, 0.0) + jnp.log(1.0 + jnp.exp(-jnp.abs(x)))


def _rk_prep_kernel(u_ref, up_ref, un_ref, cw_ref, w0_ref, w2_ref, a0_ref, a2_ref, g2_ref, kk_ref, ka_ref,
                    rk_ref, q_ref, g_ref, bonus_ref, *, n_ctx_tiles, tiles_per_seq):
    def put(slot, val):
        q_ref[:, slot * RK_WIDTH:(slot + 1) * RK_WIDTH] = val

    i = pl.program_id(0)
    li = i - n_ctx_tiles
    is_lat = i >= n_ctx_tiles
    has_prev = jnp.logical_and(is_lat, li % tiles_per_seq != 0)
    has_next = jnp.logical_and(is_lat, li % tiles_per_seq != tiles_per_seq - 1)
    u = u_ref[...]
    tm = u.shape[0]
    prev_row = jnp.where(has_prev, up_ref[7:8, :], 0.0)
    next_row = jnp.where(has_next, un_ref[0:1, :], 0.0)
    row = lax.broadcasted_iota(jnp.int32, u.shape, 0)
    um = jnp.where(row == 0, prev_row, pltpu.roll(u, 1, 0))
    up = jnp.where(row == tm - 1, next_row, pltpu.roll(u, tm - 1, 0))
    u = um * cw_ref[0:1, :] + u * cw_ref[1:2, :] + up * cw_ref[2:3, :]

    r = u[:, 0:RK_WIDTH]
    k = u[:, RK_WIDTH:2 * RK_WIDTH]
    v = u[:, 2 * RK_WIDTH:3 * RK_WIDTH]
    lora = u[:, LORA_BLOCK * LANES:(LORA_BLOCK + 1) * LANES]
    gl = u[:, GATE_BLOCK * LANES:(GATE_BLOCK + 1) * LANES]
    put(Q_R, r)
    put(Q_V, v)
    g_ref[...] = _dot3(jax.nn.sigmoid(gl), g2_ref[...])

    kn = k * kk_ref[...]
    kk = jnp.concatenate(
        [kn[:, c * LANES:(c + 1) * LANES]
         * lax.rsqrt(jnp.maximum(_pair_sum(jnp.square(kn[:, c * LANES:(c + 1) * LANES])), 1e-24))
         for c in range(RK_NB)], axis=1)
    put(Q_A, -kk)

    lora_t = jnp.tanh(lora)
    kd_sum = None
    for d in range(2):
        w = -_softplus(-(w0_ref[d:d + 1, :] + _dot3(lora_t, w2_ref[d]))) - 0.5
        put(Q_W + Q_DIR * d, -jnp.exp(w))
        a = jax.nn.sigmoid(a0_ref[d:d + 1, :] + _dot3(lora, a2_ref[d]))
        kd = k * (1.0 + (a - 1.0) * ka_ref[...])
        put(Q_K + Q_DIR * d, kd)
        put(Q_B + Q_DIR * d, kk * a)
        kd_sum = kd if kd_sum is None else kd_sum + kd

    t = r * kd_sum * rk_ref[...]
    bonus_ref[...] = jnp.concatenate(
        [_pair_sum(t[:, c * LANES:(c + 1) * LANES]) for c in range(RK_NB)], axis=1) * v


def _rk_prep(u, p, n_ctx_tiles, tiles_per_seq):
    n_tok = u.shape[0]
    n_tiles = n_tok // TOK_TILE
    sub = TOK_TILE // 8
    last8 = n_tok // 8 - 1
    tok = lambda i: (i, 0)
    const2 = lambda i: (0, 0)
    const3 = lambda i: (0, 0, 0)
    one = jax.ShapeDtypeStruct((n_tok, RK_WIDTH), F32)
    tok_spec = pl.BlockSpec((TOK_TILE, RK_WIDTH), tok)
    return pl.pallas_call(
        functools.partial(_rk_prep_kernel, n_ctx_tiles=n_ctx_tiles, tiles_per_seq=tiles_per_seq),
        grid=(n_tiles,),
        in_specs=[
            pl.BlockSpec((TOK_TILE, RK_COLS), tok),
            pl.BlockSpec((8, RK_COLS), lambda i: (jnp.maximum(i * sub - 1, 0), 0)),
            pl.BlockSpec((8, RK_COLS), lambda i: (jnp.minimum((i + 1) * sub, last8), 0)),
            pl.BlockSpec((3, RK_COLS), const2),
            pl.BlockSpec((2, RK_WIDTH), const2),
            pl.BlockSpec((2, LANES, RK_WIDTH), const3),
            pl.BlockSpec((2, RK_WIDTH), const2),
            pl.BlockSpec((2, LANES, RK_WIDTH), const3),
            pl.BlockSpec((RK_GATE_LORA, RK_WIDTH), const2),
            pl.BlockSpec((1, RK_WIDTH), const2),
            pl.BlockSpec((1, RK_WIDTH), const2),
            pl.BlockSpec((1, RK_WIDTH), const2),
        ],
        out_specs=[pl.BlockSpec((TOK_TILE, Q_COLS), tok), tok_spec, tok_spec],
        out_shape=[jax.ShapeDtypeStruct((n_tok, Q_COLS), F32), one, one],
        compiler_params=_cparams(("parallel",)),
        name="rk_prep",
    )(u, u, u, p["rk_conv"], p["rk_w0"], p["rk_w2_pad"], p["rk_a0"], p["rk_a2_pad"], p["rk_g2"],
      p["rk_k_k"], p["rk_k_a"], p["rk_r_k"])


RK_CHUNK = 64
PAIR = 2 * HEAD_DIM
STATE_SEQS = 8
RK_STEP_CHUNKS = 4


def _split3_bf16(x):
    hi = x.astype(BF16)
    r1 = x - hi.astype(F32)
    mid = r1.astype(BF16)
    return hi, mid, (r1 - mid.astype(F32)).astype(BF16)


def _pack_pair(m):
    return jnp.concatenate([m[:HEAD_DIM, :HEAD_DIM], m[HEAD_DIM:, HEAD_DIM:]], axis=1)


def _unpack_pairs(m):
    lo = _lane_lo(m.shape)
    return jnp.concatenate([jnp.where(lo, m, 0.0), jnp.where(lo, 0.0, m)], axis=1)


def _rk_chunk_kernel(q_ref, rbar_ref, ybar_ref, phi_ref, psi_ref):
    c = RK_CHUNK
    n = 2 * c
    nd = 2 * RK_NB
    nu = RK_STEP_CHUNKS * nd

    def tiles(slot, per_dir):
        cols = [(slot + (Q_DIR * d if per_dir else 0)) * RK_WIDTH + p * LANES
                for d in range(2) for p in range(RK_NB)]
        return jnp.stack([q_ref[ck * c:(ck + 1) * c, lo:lo + LANES] for ck in range(RK_STEP_CHUNKS) for lo in cols])

    r, v, a = tiles(Q_R, False), tiles(Q_V, False), tiles(Q_A, False)
    lw, k, b = tiles(Q_W, True), tiles(Q_K, True), tiles(Q_B, True)
    unit = lax.broadcasted_iota(jnp.int32, (nu, 1, 1), 0)
    sgn = jnp.ones((nu, 1, 1), jnp.int32)
    for ck in range(RK_STEP_CHUNKS):
        sgn = jnp.where(jnp.logical_and(unit >= ck * nd + RK_NB, unit < (ck + 1) * nd), -1, sgn)
    bwd = sgn < 0
    tdiff = lax.broadcasted_iota(jnp.int32, (1, c, c), 2) - lax.broadcasted_iota(jnp.int32, (1, c, c), 1)
    tri = jnp.where(tdiff * sgn <= 0, 1.0, 0.0)
    cum = sum(_bmm(tri, part) for part in _split3_bf16(lw))
    tot = jnp.where(bwd, cum[:, 0:1], cum[:, c - 1:c])
    a_t = a * jnp.exp(cum - lw)
    r_t = r * jnp.exp(cum)
    e_neg = jnp.exp(-cum)
    e_end = jnp.exp(tot - cum)
    g = _bmm_nt(jnp.concatenate([_stack_heads3(a_t), _stack_heads3(r_t)], axis=1),
                jnp.concatenate([_stack_heads3(b * e_neg), _stack_heads3(k * e_neg)], axis=1))
    r2 = lax.broadcasted_iota(jnp.int32, (1, n, n), 1)
    c2 = lax.broadcasted_iota(jnp.int32, (1, n, n), 2)
    order = (jnp.bitwise_and(c2, c - 1) - jnp.bitwise_and(r2, c - 1)) * sgn
    eye = jnp.where(r2 == c2, 1.0, 0.0)
    l_ab = jnp.where(order < 0, g[:, :n, :n], 0.0)
    l_ak = jnp.where(order < 0, g[:, :n, n:], 0.0)
    m_rb = jnp.where(order <= 0, g[:, n:, :n], 0.0)
    m_rk = jnp.where(order <= 0, g[:, n:, n:], 0.0)
    t_inv = eye + l_ab
    pw = l_ab
    for _ in range(5):
        pw = _bmm(pw, pw)
        t_inv = t_inv + _bmm(t_inv, pw)
    sv = _stack_heads3(v)
    au = _bmm(t_inv, jnp.concatenate([_stack_heads3(a_t), _bmm(l_ak, sv)], axis=2))
    ry = _bmm(m_rb, au) + jnp.concatenate([_stack_heads3(r_t), _bmm(m_rk, sv)], axis=2)
    ry = ry[:, :c] + ry[:, c:]
    bt = jnp.swapaxes(_stack_heads3(b * e_end), 1, 2)
    kt = jnp.swapaxes(_stack_heads3(k * e_end), 1, 2)
    pp = _bmm(bt, au)
    phi = eye * jnp.exp(tot) + pp[:, :, :PAIR]
    psi = pp[:, :, PAIR:] + _bmm(kt, sv)
    for ck in range(RK_STEP_CHUNKS):
        for d in range(2):
            for p in range(RK_NB):
                u = ck * nd + d * RK_NB + p
                rbar_ref[d, ck * c:(ck + 1) * c, p * LANES:(p + 1) * LANES] = ry[u, :, :PAIR]
                ybar_ref[d, ck * c:(ck + 1) * c, p * LANES:(p + 1) * LANES] = ry[u, :, PAIR:]
                phi_ref[d, ck, p] = _pack_pair(phi[u])
                psi_ref[d, ck, p] = _pack_pair(psi[u])


def _rk_chunk(q, tile0, n_seq, t):
    nc = t // RK_CHUNK
    sc = RK_STEP_CHUNKS
    assert nc % sc == 0 and tile0 % sc == 0
    row_sh = jax.ShapeDtypeStruct((2, n_seq, t, RK_WIDTH), F32)
    mat_sh = jax.ShapeDtypeStruct((2, n_seq, nc, RK_NB, HEAD_DIM, PAIR), F32)
    row_spec = pl.BlockSpec((2, None, sc * RK_CHUNK, RK_WIDTH), lambda s, c: (0, s, c, 0))
    mat_spec = pl.BlockSpec((2, None, sc, RK_NB, HEAD_DIM, PAIR), lambda s, c: (0, s, c, 0, 0, 0))
    return pl.pallas_call(
        _rk_chunk_kernel,
        grid=(n_seq, nc // sc),
        in_specs=[pl.BlockSpec((sc * RK_CHUNK, Q_COLS), lambda s, c: ((tile0 + s * nc) // sc + c, 0))],
        out_specs=[row_spec, row_spec, mat_spec, mat_spec],
        out_shape=[row_sh, row_sh, mat_sh, mat_sh],
        compiler_params=_cparams(("parallel", "parallel")),
        name="rk_chunk",
    )(q)


def _rk_state_kernel(rf_ref, rb_ref, yf_ref, yb_ref, phf_ref, phb_ref, psf_ref, psb_ref, s0_ref,
                     of_ref, ob_ref, s_ref):
    @pl.when(pl.program_id(1) == 0)
    def _():
        s_ref[...] = s0_ref[...]

    ns = s_ref.shape[0]
    nd = 2 * RK_NB

    def pair_tiles(ref_f, ref_b):
        tiles = [ref[:, :, p * LANES:(p + 1) * LANES] for ref in (ref_f, ref_b) for p in range(RK_NB)]
        return jnp.stack(tiles, axis=1).reshape(ns * nd, RK_CHUNK, LANES)

    def mats(ref_f, ref_b):
        return _unpack_pairs(jnp.concatenate([ref_f[...], ref_b[...]], axis=1).reshape(ns * nd, HEAD_DIM, PAIR))

    h = s_ref[...].reshape(ns * nd, PAIR, PAIR)
    y = (_bmm3(pair_tiles(rf_ref, rb_ref), h) + pair_tiles(yf_ref, yb_ref)).reshape(ns, nd, RK_CHUNK, LANES)
    s_ref[...] = (_bmm3(mats(phf_ref, phb_ref), h) + mats(psf_ref, psb_ref)).reshape(ns, 2, RK_NB, PAIR, PAIR)
    of_ref[...] = jnp.concatenate([y[:, p] for p in range(RK_NB)], axis=2)
    ob_ref[...] = jnp.concatenate([y[:, RK_NB + p] for p in range(RK_NB)], axis=2)


def _rk_state(rbar, ybar, phi, psi, s0):
    _, n_seq, t, _ = rbar.shape
    nc = t // RK_CHUNK
    sg = STATE_SEQS
    row_blk = (None, sg, RK_CHUNK, RK_WIDTH)
    mat_blk = (None, sg, None, RK_NB, HEAD_DIM, PAIR)
    fwd_row = pl.BlockSpec(row_blk, lambda g, c: (0, g, c, 0))
    bwd_row = pl.BlockSpec(row_blk, lambda g, c: (1, g, nc - 1 - c, 0))
    fwd_mat = pl.BlockSpec(mat_blk, lambda g, c: (0, g, c, 0, 0, 0))
    bwd_mat = pl.BlockSpec(mat_blk, lambda g, c: (1, g, nc - 1 - c, 0, 0, 0))
    st = pl.BlockSpec((sg, 2, RK_NB, PAIR, PAIR), lambda g, c: (g, 0, 0, 0, 0))
    out_sh = jax.ShapeDtypeStruct((n_seq, t, RK_WIDTH), F32)
    return pl.pallas_call(
        _rk_state_kernel,
        grid=(n_seq // sg, nc),
        in_specs=[fwd_row, bwd_row, fwd_row, bwd_row, fwd_mat, bwd_mat, fwd_mat, bwd_mat, st],
        out_specs=[pl.BlockSpec((sg, RK_CHUNK, RK_WIDTH), lambda g, c: (g, c, 0)),
                   pl.BlockSpec((sg, RK_CHUNK, RK_WIDTH), lambda g, c: (g, nc - 1 - c, 0)), st],
        out_shape=[out_sh, out_sh, jax.ShapeDtypeStruct((n_seq, 2, RK_NB, PAIR, PAIR), F32)],
        compiler_params=_cparams(("parallel", "arbitrary")),
        name="rk_state",
    )(rbar, rbar, ybar, ybar, phi, phi, psi, psi, s0)


def _pair_states(s):
    bsz = s.shape[0]
    h = jnp.swapaxes(s, -1, -2).reshape(bsz, 2, RK_NB, 2, HEAD_DIM, HEAD_DIM)
    return jnp.einsum("bdphkv,hg->bdphkgv", h, jnp.eye(2, dtype=F32)).reshape(bsz, 2, RK_NB, PAIR, PAIR)


def _head_states(s):
    bsz = s.shape[0]
    h = jnp.stack([s[..., :HEAD_DIM, :HEAD_DIM], s[..., HEAD_DIM:, HEAD_DIM:]], axis=3)
    return jnp.swapaxes(h.reshape(bsz, 2, RK_HEADS, HEAD_DIM, HEAD_DIM), -1, -2)


def _rwkv_group(q, tile0, n_seq, t, s0):
    rbar, ybar, phi, psi = _rk_chunk(q, tile0, n_seq, t)
    y_f, y_b, s_fin = _rk_state(rbar, ybar, phi, psi, s0)
    return y_f.reshape(n_seq * t, RK_WIDTH), y_b.reshape(n_seq * t, RK_WIDTH), s_fin


def _out_proj_kernel(x_c, x_l, ona_c, ona_l, osw_c, osw_l, yf_c, yf_l, yb_c, yb_l, bonus_ref, g_ref, lng_ref, lnb_ref,
                     w_ref, mod_ref, n2_ref, rw_ref, rb_ref, x1_ref, h2_ref, gate_ref, top_ref, *, n_ctx_tiles):
    is_ctx = pl.program_id(0) < n_ctx_tiles
    pick = lambda c_ref, l_ref: jnp.where(is_ctx, c_ref[...], l_ref[...])
    ona = pick(ona_c, ona_l)
    osw = pick(osw_c, osw_l)
    y = pick(yf_c, yf_l) + pick(yb_c, yb_l)
    parts = []
    for c in range(RK_NB):
        yc = y[:, c * LANES:(c + 1) * LANES]
        dc = yc - _pair_sum(yc) * (1.0 / HEAD_DIM)
        var = _pair_sum(dc * dc) * (1.0 / HEAD_DIM)
        parts.append(dc * lax.rsqrt(var + GN_EPS))
    yn = jnp.concatenate(parts, axis=1) * lng_ref[...] + lnb_ref[...]
    o_rk = (yn + bonus_ref[...]) * g_ref[...]
    o = (_dot(ona.astype(BF16), w_ref[0:NA_WIDTH, :])
         + _dot(osw.astype(BF16), w_ref[NA_WIDTH:NA_WIDTH + SWA_WIDTH, :])
         + _dot(o_rk.astype(BF16), w_ref[NA_WIDTH + SWA_WIDTH:, :]))
    x1 = pick(x_c, x_l) + mod_ref[0, 2:3, :] * o
    x1_ref[...] = x1
    yn2 = x1 * lax.rsqrt(jnp.mean(x1 * x1, axis=-1, keepdims=True) + RMS_EPS)
    h2 = (yn2 * n2_ref[...]) * (1.0 + mod_ref[0, 4:5, :]) + mod_ref[0, 3:4, :]
    h2_ref[...] = h2.astype(BF16)
    lane = lax.broadcasted_iota(jnp.int32, (h2.shape[0], LANES), 1).astype(F32)
    logit = jnp.where(lane < N_EXPERTS, _dot3(h2, rw_ref[...]) + rb_ref[...], -jnp.inf)
    vals, idxs = [], []
    for _ in range(TOP_K):
        best = jnp.max(logit, axis=-1, keepdims=True)
        idx = jnp.min(jnp.where(logit == best, lane, float(LANES)), axis=-1, keepdims=True)
        vals.append(best)
        idxs.append(idx)
        logit = jnp.where(lane == idx, -jnp.inf, logit)
    e = jnp.exp(jnp.concatenate(vals, axis=1) - vals[0])
    gate_ref[...] = e / jnp.sum(e, axis=-1, keepdims=True)
    top_ref[...] = jnp.concatenate(idxs, axis=1).astype(jnp.int32)


def _out_proj(x, o_na, o_sw, y_f, y_b, bonus, g, p, mods, tile_mod, n_ctx_tiles):
    n_tok = x[0].shape[0] + x[1].shape[0]
    tok = lambda i: (i, 0)
    const = lambda i: (0, 0)
    pair = lambda w: _group_tile_specs(w, n_ctx_tiles)
    return pl.pallas_call(
        functools.partial(_out_proj_kernel, n_ctx_tiles=n_ctx_tiles),
        grid=(n_tok // TOK_TILE,),
        in_specs=[
            *pair(D_MODEL), *pair(NA_WIDTH), *pair(SWA_WIDTH), *pair(RK_WIDTH), *pair(RK_WIDTH),
            pl.BlockSpec((TOK_TILE, RK_WIDTH), tok),
            pl.BlockSpec((TOK_TILE, RK_WIDTH), tok),
            pl.BlockSpec((1, RK_WIDTH), const),
            pl.BlockSpec((1, RK_WIDTH), const),
            pl.BlockSpec((D_MODEL, D_MODEL), const),
            pl.BlockSpec((1, 6, D_MODEL), lambda i: (tile_mod(i), 0, 0)),
            pl.BlockSpec((1, D_MODEL), const),
            pl.BlockSpec((D_MODEL, LANES), const),
            pl.BlockSpec((1, LANES), const),
        ],
        out_specs=[
            pl.BlockSpec((TOK_TILE, D_MODEL), tok),
            pl.BlockSpec((TOK_TILE, D_MODEL), tok),
            pl.BlockSpec((TOK_TILE, TOP_K), tok),
            pl.BlockSpec((TOK_TILE, TOP_K), tok),
        ],
        out_shape=[
            jax.ShapeDtypeStruct((n_tok, D_MODEL), F32),
            jax.ShapeDtypeStruct((n_tok, D_MODEL), BF16),
            jax.ShapeDtypeStruct((n_tok, TOP_K), F32),
            jax.ShapeDtypeStruct((n_tok, TOP_K), jnp.int32),
        ],
        compiler_params=_cparams(("parallel",)),
        name="out_proj",
    )(*x, *o_na, *o_sw, *y_f, *y_b, bonus, g, p["rk_ln_g"], p["rk_ln_b"], p["w_out_bf16"], mods, p["norm2_g"],
      p["router_w_pad"], p["router_b_pad"])


H2_PAD_ROWS = 32768
W1_SEL_COLS = 256
MOE_VMEM_LIMIT = 56 * 1024 * 1024


def _moe_kernel(meta_ref, x_ref, w1_ref, b1g_ref, b1l_ref, w2_ref, b2_ref, o_ref, w1g_ref, w1l_ref, w2b_ref):
    i = pl.program_id(0)
    n_blk = (meta_ref.shape[0] - 1) // 2
    n_used = meta_ref[n_blk]
    d_e = w2_ref.shape[1]
    new_expert = jnp.logical_or(i == 0, meta_ref[i] != meta_ref[jnp.maximum(i - 1, 0)])

    @pl.when(jnp.logical_and(i < n_used, new_expert))
    def _():
        src = lax.broadcasted_iota(jnp.int32, (2 * W1_SEL_COLS, 2 * W1_SEL_COLS), 0)
        dst = lax.broadcasted_iota(jnp.int32, (2 * W1_SEL_COLS, 2 * W1_SEL_COLS), 1)
        pick = jnp.where(dst < W1_SEL_COLS, 2 * dst, 2 * (dst - W1_SEL_COLS) + 1)
        sel = jnp.where(src == pick, 1.0, 0.0).astype(BF16)
        for t in range(d_e // W1_SEL_COLS):
            cols = _dot(w1_ref[0, :, 2 * t * W1_SEL_COLS:2 * (t + 1) * W1_SEL_COLS].astype(BF16), sel)
            w1g_ref[:, t * W1_SEL_COLS:(t + 1) * W1_SEL_COLS] = cols[:, :W1_SEL_COLS].astype(BF16)
            w1l_ref[:, t * W1_SEL_COLS:(t + 1) * W1_SEL_COLS] = cols[:, W1_SEL_COLS:].astype(BF16)
        w2b_ref[...] = w2_ref[0].astype(BF16)

    @pl.when(i < n_used)
    def _():
        x = x_ref[...]
        glu = jnp.minimum(_dot(x, w1g_ref[...]) + b1g_ref[0], SWIGLU_LIMIT)
        lin = jnp.clip(_dot(x, w1l_ref[...]) + b1l_ref[0], -SWIGLU_LIMIT, SWIGLU_LIMIT)
        act = glu * jax.nn.sigmoid(SWIGLU_ALPHA * glu) * (lin + 1.0)
        o_ref[...] = (_dot(act.astype(BF16), w2b_ref[...]) + b2_ref[0]).astype(BF16)

    @pl.when(i >= n_used)
    def _():
        o_ref[...] = jnp.zeros_like(o_ref)


def _moe_blocks(meta, xb, w1, b1g, b1l, w2, b2, layer):
    n_rows = xb.shape[0]
    n_blk = n_rows // MOE_BLK
    d_e = w2.shape[2]
    row = lambda i, m: (i, 0)
    exp3 = lambda i, m: (layer, m[i], 0, 0)
    wexp = lambda i, m: (layer, m[n_blk + 1 + i], 0, 0)
    grid_spec = pltpu.PrefetchScalarGridSpec(
        num_scalar_prefetch=1,
        grid=(n_blk,),
        in_specs=[
            pl.BlockSpec((MOE_BLK, D_MODEL), row),
            pl.BlockSpec((None, 1, D_MODEL, 2 * d_e), wexp),
            pl.BlockSpec((None, 1, 1, d_e), exp3),
            pl.BlockSpec((None, 1, 1, d_e), exp3),
            pl.BlockSpec((None, 1, d_e, D_MODEL), wexp),
            pl.BlockSpec((None, 1, 1, D_MODEL), exp3),
        ],
        out_specs=pl.BlockSpec((MOE_BLK, D_MODEL), row),
        scratch_shapes=[
            pltpu.VMEM((D_MODEL, d_e), BF16),
            pltpu.VMEM((D_MODEL, d_e), BF16),
            pltpu.VMEM((d_e, D_MODEL), BF16),
        ],
    )
    return pl.pallas_call(
        _moe_kernel,
        grid_spec=grid_spec,
        out_shape=jax.ShapeDtypeStruct((n_rows, D_MODEL), BF16),
        compiler_params=pltpu.CompilerParams(dimension_semantics=("arbitrary",),
                                             vmem_limit_bytes=MOE_VMEM_LIMIT),
        name="moe_blocks",
    )(meta, xb, w1, b1g, b1l, w2, b2)


def _route(top_i):
    n_tok = top_i.shape[0]
    e_flat = top_i.reshape(-1)
    n_rows = n_tok * TOP_K
    onehot = (e_flat[:, None] == jnp.arange(N_EXPERTS, dtype=jnp.int32)[None, :]).astype(jnp.int32)
    csum = jnp.cumsum(onehot, axis=0)
    counts = jnp.sum(onehot, axis=0)
    starts = jnp.cumsum(counts) - counts
    pcounts = (counts + MOE_BLK - 1) // MOE_BLK * MOE_BLK
    pends = jnp.cumsum(pcounts)
    pstarts = pends - pcounts
    dest = jnp.take_along_axis(csum + (pstarts - 1)[None, :], e_flat[:, None], axis=1)[:, 0]
    n_blk = n_rows // MOE_BLK + N_EXPERTS
    blk_start = jnp.arange(n_blk, dtype=jnp.int32) * MOE_BLK
    blk_exp = jnp.minimum(jnp.sum((blk_start[:, None] >= pends[None, :]).astype(jnp.int32), axis=1), N_EXPERTS - 1)
    order = jnp.argsort(e_flat)
    pos = jnp.arange(n_blk * MOE_BLK, dtype=jnp.int32)
    src = (pos + jnp.repeat((starts - pstarts)[blk_exp], MOE_BLK)) % n_rows
    row_tok = order[src].astype(jnp.int32) // TOP_K
    first = jnp.concatenate([jnp.ones((1,), bool), blk_exp[1:] != blk_exp[:-1]])
    seg_end = jnp.sum((blk_exp[None, :] <= blk_exp[:, None]).astype(jnp.int32), axis=1)
    fetch_exp = jnp.where(first, blk_exp, blk_exp[jnp.minimum(seg_end, n_blk - 1)])
    meta = jnp.concatenate([blk_exp, (pends[-1:] // MOE_BLK).astype(jnp.int32), fetch_exp])
    return meta, row_tok, dest.reshape(n_tok, TOP_K).T.reshape(-1)


def _combine_kernel(x_ref, yg_ref, gate_ref, mod_ref, oc_ref, ol_ref, *, n_ctx_tiles):
    gate = gate_ref[...]
    acc = gate[:, 0:1] * yg_ref[0].astype(F32)
    for j in range(1, TOP_K):
        acc = acc + gate[:, j:j + 1] * yg_ref[j].astype(F32)
    out = x_ref[...] + mod_ref[0, 5:6, :] * acc
    is_ctx = pl.program_id(0) < n_ctx_tiles

    @pl.when(is_ctx)
    def _():
        oc_ref[...] = out

    @pl.when(jnp.logical_not(is_ctx))
    def _():
        ol_ref[...] = out


def _combine(x1, yg, gates, mods, tile_mod, n_ctx_tiles):
    n_tok = x1.shape[0]
    n_ctx = n_ctx_tiles * TOK_TILE
    return pl.pallas_call(
        functools.partial(_combine_kernel, n_ctx_tiles=n_ctx_tiles),
        grid=(n_tok // TOK_TILE,),
        in_specs=[
            pl.BlockSpec((TOK_TILE, D_MODEL), lambda i: (i, 0)),
            pl.BlockSpec((TOP_K, TOK_TILE, D_MODEL), lambda i: (0, i, 0)),
            pl.BlockSpec((TOK_TILE, TOP_K), lambda i: (i, 0)),
            pl.BlockSpec((1, 6, D_MODEL), lambda i: (tile_mod(i), 0, 0)),
        ],
        out_specs=_group_tile_specs(D_MODEL, n_ctx_tiles),
        out_shape=[jax.ShapeDtypeStruct((n_ctx, D_MODEL), F32), jax.ShapeDtypeStruct((n_tok - n_ctx, D_MODEL), F32)],
        compiler_params=_cparams(("arbitrary",)),
        name="moe_combine",
    )(x1, yg, gates, mods)


def kernel(x_prompt, x_sample, c, cache_na_k, cache_na_v, cache_swa_k, cache_swa_v, state_rwkv, c_ctx, w_ada, b_ada, norm1_g, norm2_g, w_in, w_out, na_q_norm, na_k_norm, na_rpb, swa_q_norm, swa_k_norm, swa_sink, rk_conv, rk_w0, rk_w2, rk_a0, rk_a2, rk_g2, rk_k_k, rk_k_a, rk_r_k, rk_ln_g, rk_ln_b, moe_router_w, moe_router_b, moe_w1, moe_b1, moe_w2, moe_b2):
    bc, tc, _ = x_prompt.shape
    bl, tl, _ = x_sample.shape
    depth = w_in.shape[0]
    n_ctx = bc * tc
    n_lat = bl * tl
    assert tc == TOK_TILE and tl % TOK_TILE == 0 and n_ctx % tl == 0
    n_ctx_tiles = n_ctx // TOK_TILE
    tiles_per_seq = tl // TOK_TILE
    past = cache_na_k.shape[2]

    def tile_mod(i):
        return jnp.where(i < n_ctx_tiles, 0, 1 + (i - n_ctx_tiles) // tiles_per_seq)

    def tile_rope(i):
        return jnp.where(i < n_ctx_tiles, tiles_per_seq, (i - n_ctx_tiles) % tiles_per_seq)

    x = (x_prompt.reshape(n_ctx, D_MODEL), x_sample.reshape(n_lat, D_MODEL))

    n_mod = 1 + bl
    mod_rows = -(-n_mod // 8) * 8
    cvecs = jnp.concatenate([c_ctx[None, :], c, jnp.zeros((mod_rows - n_mod, D_MODEL), F32)], axis=0)
    mods_all = _ada_mod(cvecs, w_ada, b_ada).reshape(depth, mod_rows, 6, D_MODEL)
    cos_tab, sin_tab = _rope_tables(tl)
    tile2 = lambda g: jnp.concatenate([g, g])[None, :]
    pad_lanes = lambda z: jnp.pad(z, ((0, 0), (0, LANES - z.shape[1])))
    zeros_lora = jnp.zeros((2, RK_DECAY_LORA, RK_WIDTH), F32)

    na_k_l, na_v_l, sw_k_l, sw_v_l, st_l = [], [], [], [], []
    for l in range(depth):
        mods = mods_all[l]
        qk_gains = jnp.concatenate(
            [tile2(na_q_norm[l]), tile2(na_k_norm[l]), tile2(swa_q_norm[l]), tile2(swa_k_norm[l])], axis=0)
        p = {
            "rk_conv": rk_conv[l], "rk_w0": rk_w0[l], "rk_a0": rk_a0[l], "rk_g2": rk_g2[l],
            "rk_w2_pad": jnp.concatenate([rk_w2[l], zeros_lora], axis=1),
            "rk_a2_pad": jnp.concatenate([zeros_lora, rk_a2[l]], axis=1),
            "rk_k_k": rk_k_k[l][None, :], "rk_k_a": rk_k_a[l][None, :],
            "rk_r_k": rk_r_k[l].reshape(1, RK_WIDTH),
            "rk_ln_g": rk_ln_g[l][None, :], "rk_ln_b": rk_ln_b[l][None, :],
            "w_out_bf16": w_out[l].astype(BF16), "norm2_g": norm2_g[l][None, :],
            "router_w_pad": pad_lanes(moe_router_w[l]), "router_b_pad": pad_lanes(moe_router_b[l][None, :]),
        }

        att, u = _in_proj(x, norm1_g[l][None, :], mods, w_in[l].astype(BF16), qk_gains, cos_tab, sin_tab,
                          tile_mod, tile_rope, n_ctx_tiles)
        q, g, bonus = _rk_prep(u, p, n_ctx_tiles, tiles_per_seq)
        att_c = att[:n_ctx].reshape(bc, tc, ATT_COLS)
        att_by_ctx_len = att.reshape((n_ctx + n_lat) // tc, tc, ATT_COLS)
        att_by_lat_len = att.reshape((n_ctx + n_lat) // tl, tl, ATT_COLS)
        na_k_l.append(att_c[:, :, NA_WIDTH:2 * NA_WIDTH].reshape(bc, tc, NA_HEADS, HEAD_DIM))
        na_v_l.append(att_c[:, :, 2 * NA_WIDTH:NA_COLS].reshape(bc, tc, NA_HEADS, HEAD_DIM))
        sw_k_l.append(att_c[:, :, NA_COLS + SWA_WIDTH:NA_COLS + SWA_WIDTH + SWA_KV_WIDTH]
                      .reshape(bc, tc, SWA_KV_HEADS, HEAD_DIM))
        sw_v_l.append(att_c[:, :, NA_COLS + SWA_WIDTH + SWA_KV_WIDTH:].reshape(bc, tc, SWA_KV_HEADS, HEAD_DIM))

        sink = swa_sink[l]
        o_na = (_ctx_attn(att_by_ctx_len, bc, sink, gqa=False).reshape(n_ctx, NA_WIDTH),
                _na_latent(att_by_lat_len, n_ctx // tl, cache_na_k[:, l].reshape(bl, past, NA_WIDTH),
                           cache_na_v[:, l].reshape(bl, past, NA_WIDTH),
                           _na_bias_tables(na_rpb[l])).reshape(n_lat, NA_WIDTH))
        o_sw = (_ctx_attn(att_by_ctx_len, bc, sink, gqa=True).reshape(n_ctx, SWA_WIDTH),
                _swa_latent(att_by_lat_len, n_ctx // tl, cache_swa_k[:, l].reshape(bl, past, SWA_KV_WIDTH),
                            cache_swa_v[:, l].reshape(bl, past, SWA_KV_WIDTH), sink).reshape(n_lat, SWA_WIDTH))

        yf_c, yb_c, s_fin = _rwkv_group(q, 0, bc, tc, jnp.zeros((bc, 2, RK_NB, PAIR, PAIR), F32))
        yf_l, yb_l, _ = _rwkv_group(q, n_ctx // RK_CHUNK, bl, tl, _pair_states(state_rwkv[:, l]))
        st_l.append(_head_states(s_fin))

        x1, h2, gates, top_i = _out_proj(x, o_na, o_sw, (yf_c, yf_l), (yb_c, yb_l), bonus, g, p, mods, tile_mod,
                                         n_ctx_tiles)
        meta, row_tok, dest = _route(top_i)
        h2 = jnp.concatenate([h2, jnp.zeros((H2_PAD_ROWS - h2.shape[0], D_MODEL), BF16)], axis=0)
        yb = _moe_blocks(meta, h2[row_tok], moe_w1, moe_b1[:, :, None, 0::2], moe_b1[:, :, None, 1::2], moe_w2,
                         moe_b2[:, :, None, :], l)
        x = _combine(x1, yb[dest].reshape(TOP_K, n_ctx + n_lat, D_MODEL), gates, mods, tile_mod, n_ctx_tiles)

    y_p = x[0].reshape(bc, tc, D_MODEL)
    y_s = x[1].reshape(bl, tl, D_MODEL)
    return (y_p, y_s, jnp.stack(na_k_l, axis=1), jnp.stack(na_v_l, axis=1), jnp.stack(sw_k_l, axis=1),
            jnp.stack(sw_v_l, axis=1), jnp.stack(st_l, axis=1))
```
